```python
import math
import jax, jax.numpy as jnp
from jax import lax
import numpy as np

D_MODEL = 1024
BATCH = 2
SEQ = 8192
DEPTH = 1

GRID_W = 64
CTX_LEN = 256
N_DA_HEADS = 4
DA_HEAD_DIM = 64
DA_V_DIM = 2 * DA_HEAD_DIM
QK_WIDTH = N_DA_HEADS * 2 * DA_HEAD_DIM
ATTN_WIDTH = N_DA_HEADS * DA_V_DIM
N_FOURIER_GROUPS = 4
FOURIER_GROUP_DIM = 128
FOURIER_WIDTH = N_FOURIER_GROUPS * FOURIER_GROUP_DIM
MIX_WIDTH = ATTN_WIDTH + FOURIER_WIDTH
IN_PROJ_WIDTH = 2 * QK_WIDTH + ATTN_WIDTH + FOURIER_WIDTH
ROPE_THETA = 10000.0
ROPE_AXIS_DIM = DA_HEAD_DIM // 2
Q_BLOCK = 128
N_GROUPS = 4
EXPERTS_PER_GROUP = 8
N_EXPERTS = N_GROUPS * EXPERTS_PER_GROUP
TOP_K = 2
D_EXPERT = 512
MOE_BLOCK = 128
N_MOD = 6
EPS = 1e-6

kernel_name = 'hybrid_diffattn_fnet_hmoe_dit_block'


def rms_norm(x, g):
    xf = x.astype(jnp.float32)
    y = xf * lax.rsqrt(jnp.mean(xf * xf, axis=-1, keepdims=True) + EPS)
    return (y * g.astype(jnp.float32)).astype(x.dtype)


def modulate(h, shift, scale):
    return h * (1 + scale) + shift


def axial_rope_tables(rows, dtype):
    r, col = jnp.meshgrid(jnp.arange(rows), jnp.arange(GRID_W), indexing='ij')
    pos = jnp.stack([r.reshape(-1), col.reshape(-1)], axis=-1).astype(jnp.float32)
    inv_freq = ROPE_THETA ** (-jnp.arange(0, ROPE_AXIS_DIM, 2, dtype=jnp.float32) / ROPE_AXIS_DIM)
    ang = pos[:, :, None] * inv_freq
    ang = jnp.concatenate([ang, ang], axis=-1)
    return jnp.cos(ang).astype(dtype), jnp.sin(ang).astype(dtype)


def apply_axial_rope(x, cos, sin):
    xr = x.reshape(*x.shape[:-1], 2, ROPE_AXIS_DIM)
    x1, x2 = jnp.split(xr, 2, axis=-1)
    rot = jnp.concatenate([-x2, x1], axis=-1)
    c = cos[None, :, None, None]
    s = sin[None, :, None, None]
    return (xr * c + rot * s).reshape(x.shape)


def diff_lambda(lq1, lk1, lq2, lk2, lambda_init):
    f = lambda a: a.astype(jnp.float32)
    return jnp.exp(jnp.sum(f(lq1) * f(lk1))) - jnp.exp(jnp.sum(f(lq2) * f(lk2))) + lambda_init


def diff_attention(q, k, v, lam, g_subln, lambda_init):
    b, n = q.shape[:2]
    nb = n // Q_BLOCK
    qb = jnp.moveaxis(q.reshape(b, nb, Q_BLOCK, *q.shape[2:]), 1, 0)
    scale = DA_HEAD_DIM ** -0.5

    def block(qi):
        s = jnp.einsum('bqhmd,bkhmd->bhmqk', qi, k, preferred_element_type=jnp.float32) * scale
        p = jax.nn.softmax(s, axis=-1)
        a = p[:, :, 0] - lam * p[:, :, 1]
        return jnp.einsum('bhqk,bkhv->bqhv', a.astype(v.dtype), v)

    o = lax.map(block, qb)
    o = jnp.moveaxis(o, 0, 1).reshape(b, n, N_DA_HEADS, DA_V_DIM)
    o = rms_norm(o, g_subln) * (1.0 - lambda_init)
    return o.reshape(b, n, ATTN_WIDTH)


def fourier_mix(f, w_fourier):
    b, n, _ = f.shape
    fg = f.reshape(b, n, N_FOURIER_GROUPS, FOURIER_GROUP_DIM).astype(jnp.float32)
    fr = jnp.fft.fft2(fg, axes=(1, 3), norm='ortho').real.astype(f.dtype)
    return jnp.einsum('bngc,gcd->bngd', fr, w_fourier).reshape(b, n, FOURIER_WIDTH)


def hier_moe(h, w_rg, b_rg, w_re, b_re, w_gate, w_up, w_down):
    t, d = h.shape
    hf = h.astype(jnp.float32)
    g_logits = hf @ w_rg.astype(jnp.float32) + b_rg.astype(jnp.float32)
    grp = jnp.argmax(g_logits, axis=-1)
    p_grp = jnp.take_along_axis(jax.nn.softmax(g_logits, axis=-1), grp[:, None], axis=-1)
    e_logits = (hf @ w_re.astype(jnp.float32) + b_re.astype(jnp.float32)).reshape(t, N_GROUPS, EXPERTS_PER_GROUP)
    e_in = jnp.take_along_axis(e_logits, grp[:, None, None], axis=1)[:, 0]
    top_val, top_idx = lax.top_k(e_in, TOP_K)
    gate = p_grp * jax.nn.softmax(top_val, axis=-1)
    expert = grp[:, None] * EXPERTS_PER_GROUP + top_idx

    n_assign = t * TOP_K
    e_flat = expert.reshape(n_assign).astype(jnp.int32)
    tok_flat = jnp.repeat(jnp.arange(t, dtype=jnp.int32), TOP_K)
    w_flat = gate.reshape(n_assign)
    order = jnp.argsort(e_flat)
    e_sorted = e_flat[order]
    counts = jnp.bincount(e_flat, length=N_EXPERTS).astype(jnp.int32)
    starts = jnp.cumsum(counts) - counts
    padded = (counts + MOE_BLOCK - 1) // MOE_BLOCK * MOE_BLOCK
    pad_ends = jnp.cumsum(padded)
    pad_starts = pad_ends - padded
    dest = pad_starts[e_sorted] + jnp.arange(n_assign, dtype=jnp.int32) - starts[e_sorted]
    n_blocks = -(-n_assign // MOE_BLOCK) + N_EXPERTS
    buf_tok = jnp.full((n_blocks * MOE_BLOCK,), t, jnp.int32).at[dest].set(tok_flat[order])
    buf_w = jnp.zeros((n_blocks * MOE_BLOCK,), h.dtype).at[dest].set(w_flat[order].astype(h.dtype))
    block_expert = jnp.minimum(
        jnp.searchsorted(pad_ends, jnp.arange(n_blocks, dtype=jnp.int32) * MOE_BLOCK, side='right'),
        N_EXPERTS - 1)
    h_pad = jnp.concatenate([h, jnp.zeros((1, d), h.dtype)], axis=0)

    def expert_block(args):
        tok, wt, e = args
        xb = h_pad[tok]
        y = (jax.nn.silu(xb @ w_gate[e]) * (xb @ w_up[e])) @ w_down[e]
        return y * wt[:, None]

    ys = lax.map(expert_block, (buf_tok.reshape(n_blocks, MOE_BLOCK),
                                buf_w.reshape(n_blocks, MOE_BLOCK), block_expert))
    out = jnp.zeros((t + 1, d), ys.dtype).at[buf_tok].add(ys.reshape(-1, d))
    return out[:t]


def setup_inputs(seed: int = 0) -> dict:
    key = jax.random.key(seed)
    ks = jax.random.split(key, 24)
    f32 = jnp.float32
    nrm = lambda k, shape, std: jax.random.normal(k, shape, f32) * std
    L = DEPTH
    return {
        'x': nrm(ks[0], (BATCH, SEQ, D_MODEL), 1.0),
        'c': nrm(ks[1], (BATCH, D_MODEL), 1.0),
        'ctx': nrm(ks[2], (BATCH, CTX_LEN, D_MODEL), 1.0),
        'c_ctx': nrm(ks[3], (D_MODEL,), 1.0),
        'w_ada': nrm(ks[4], (L, D_MODEL, N_MOD * D_MODEL), 0.5 * D_MODEL ** -0.5),
        'b_ada': nrm(ks[5], (L, N_MOD * D_MODEL), 0.02),
        'g_mix_norm': 1.0 + nrm(ks[6], (L, D_MODEL), 0.02),
        'g_ffn_norm': 1.0 + nrm(ks[7], (L, D_MODEL), 0.02),
        'w_in': nrm(ks[8], (L, D_MODEL, IN_PROJ_WIDTH), D_MODEL ** -0.5),
        'lambda_q1': nrm(ks[9], (L, DA_HEAD_DIM), 0.1),
        'lambda_k1': nrm(ks[10], (L, DA_HEAD_DIM), 0.1),
        'lambda_q2': nrm(ks[11], (L, DA_HEAD_DIM), 0.1),
        'lambda_k2': nrm(ks[12], (L, DA_HEAD_DIM), 0.1),
        'g_subln': 1.0 + nrm(ks[13], (L, DA_V_DIM), 0.02),
        'w_fourier': nrm(ks[14], (L, N_FOURIER_GROUPS, FOURIER_GROUP_DIM, FOURIER_GROUP_DIM), FOURIER_GROUP_DIM ** -0.5),
        'w_out': nrm(ks[15], (L, MIX_WIDTH, D_MODEL), MIX_WIDTH ** -0.5),
        'w_router_group': nrm(ks[16], (L, D_MODEL, N_GROUPS), D_MODEL ** -0.5),
        'b_router_group': nrm(ks[17], (L, N_GROUPS), 0.01),
        'w_router_expert': nrm(ks[18], (L, D_MODEL, N_EXPERTS), D_MODEL ** -0.5),
        'b_router_expert': nrm(ks[19], (L, N_EXPERTS), 0.01),
        'w_gate': nrm(ks[20], (L, N_EXPERTS, D_MODEL, D_EXPERT), D_MODEL ** -0.5),
        'w_up': nrm(ks[21], (L, N_EXPERTS, D_MODEL, D_EXPERT), D_MODEL ** -0.5),
        'w_down': nrm(ks[22], (L, N_EXPERTS, D_EXPERT, D_MODEL), D_EXPERT ** -0.5),
        'g_final': 1.0 + nrm(ks[23], (D_MODEL,), 0.02),
    }


def reference(x, c, ctx, c_ctx, w_ada, b_ada, g_mix_norm, g_ffn_norm, w_in,
              lambda_q1, lambda_k1, lambda_q2, lambda_k2, g_subln, w_fourier, w_out,
              w_router_group, b_router_group, w_router_expert, b_router_expert,
              w_gate, w_up, w_down, g_final):
    b, s, d = x.shape
    n_ctx = ctx.shape[1]
    ROWS = s // GRID_W
    cos, sin = axial_rope_tables(ROWS, x.dtype)
    o_k = QK_WIDTH
    o_v = 2 * QK_WIDTH
    o_f = 2 * QK_WIDTH + ATTN_WIDTH
    xl, xc = x, ctx
    for l in range(DEPTH):
        last = l == DEPTH - 1
        lambda_init = 0.8 - 0.6 * math.exp(-0.3 * l)
        lam = diff_lambda(lambda_q1[l], lambda_k1[l], lambda_q2[l], lambda_k2[l], lambda_init)
        mod_l = (jax.nn.silu(c) @ w_ada[l] + b_ada[l]).reshape(b, N_MOD, 1, d)
        mod_c = (jax.nn.silu(c_ctx) @ w_ada[l] + b_ada[l]).reshape(N_MOD, d)
        sh1, sc1, gt1, sh2, sc2, gt2 = (mod_l[:, i] for i in range(N_MOD))
        csh1, csc1, cgt1, csh2, csc2, cgt2 = (mod_c[i] for i in range(N_MOD))
        wl = w_in[l]
        moe_args = (w_router_group[l], b_router_group[l], w_router_expert[l], b_router_expert[l],
                    w_gate[l], w_up[l], w_down[l])

        hl = modulate(rms_norm(xl, g_mix_norm[l]), sh1, sc1)
        pl = hl @ wl
        ql = apply_axial_rope(pl[..., :o_k].reshape(b, s, N_DA_HEADS, 2, DA_HEAD_DIM), cos, sin)
        kl = apply_axial_rope(pl[..., o_k:o_v].reshape(b, s, N_DA_HEADS, 2, DA_HEAD_DIM), cos, sin)
        vl = pl[..., o_v:o_f].reshape(b, s, N_DA_HEADS, DA_V_DIM)
        fl = pl[..., o_f:]

        hc = modulate(rms_norm(xc, g_mix_norm[l]), csh1, csc1)
        pc_kv = hc @ wl[:, o_k:o_f]
        kc = pc_kv[..., :QK_WIDTH].reshape(b, n_ctx, N_DA_HEADS, 2, DA_HEAD_DIM)
        vc = pc_kv[..., QK_WIDTH:].reshape(b, n_ctx, N_DA_HEADS, DA_V_DIM)

        k_all = jnp.concatenate([kc, kl], axis=1)
        v_all = jnp.concatenate([vc, vl], axis=1)
        mix_l = jnp.concatenate([diff_attention(ql, k_all, v_all, lam, g_subln[l], lambda_init),
                                 fourier_mix(fl, w_fourier[l])], axis=-1)
        xl = xl + gt1 * (mix_l @ w_out[l])
        h2l = modulate(rms_norm(xl, g_ffn_norm[l]), sh2, sc2)

        if last:
            xl = xl + gt2 * hier_moe(h2l.reshape(b * s, d), *moe_args).reshape(b, s, d)
        else:
            qc = (hc @ wl[:, :o_k]).reshape(b, n_ctx, N_DA_HEADS, 2, DA_HEAD_DIM)
            fc = hc @ wl[:, o_f:]
            mix_c = jnp.concatenate([diff_attention(qc, kc, vc, lam, g_subln[l], lambda_init),
                                     fourier_mix(fc, w_fourier[l])], axis=-1)
            xc = xc + cgt1 * (mix_c @ w_out[l])
            h2c = modulate(rms_norm(xc, g_ffn_norm[l]), csh2, csc2)
            y = hier_moe(jnp.concatenate([h2l.reshape(b * s, d), h2c.reshape(b * n_ctx, d)], axis=0), *moe_args)
            xl = xl + gt2 * y[:b * s].reshape(b, s, d)
            xc = xc + cgt2 * y[b * s:].reshape(b, n_ctx, d)
    return rms_norm(xl, g_final)
```

```python
import functools
import math

import numpy as np
import jax
import jax.numpy as jnp
from jax import lax
from jax.experimental import pallas as pl
from jax.experimental.pallas import tpu as pltpu

F32 = jnp.float32
BF16 = jnp.bfloat16

D_MODEL = 1024
GRID_W = 64
N_HEADS = 4
HEAD_DIM = 64
HEAD_W = 2 * HEAD_DIM
QK_W = N_HEADS * HEAD_W
N_FGROUPS = 4
FGROUP_DIM = 128
F_W = N_FGROUPS * FGROUP_DIM
ROPE_THETA = 10000.0
ROPE_AXIS = HEAD_DIM // 2
ROPE_HALF = ROPE_AXIS // 2
N_GROUPS = 4
EXPERTS_PER_GROUP = 8
N_EXPERTS = N_GROUPS * EXPERTS_PER_GROUP
TOP_K = 2
D_EXPERT = 512
N_MOD = 6
EPS = 1e-6
LAMBDA_INIT = 0.8 - 0.6 * math.exp(-0.3 * 0)

DFT_N1 = 64
DFT_N2 = 128

TM_PROJ = 512
TQ = 256
TK = 512
F1_NB = 8
F2_KB = 8
MOE_BLK = 256
TD = 256
VMEM_LIMIT = 48 * 1024 * 1024


def _cparams(*sem):
    return pltpu.CompilerParams(dimension_semantics=sem, vmem_limit_bytes=VMEM_LIMIT)


def _adaln_kernel(c_ref, w_ref, b_ref, o_ref):
    cc = c_ref[...]
    s = cc * jax.nn.sigmoid(cc)
    o_ref[...] = jnp.dot(s, w_ref[...], preferred_element_type=F32,
                         precision=lax.Precision.HIGHEST) + b_ref[...]


def _adaln(cc, w_ada, b_ada):
    n = w_ada.shape[1]
    tn = 1536
    return pl.pallas_call(
        _adaln_kernel,
        grid=(n // tn,),
        in_specs=[pl.BlockSpec((8, D_MODEL), lambda j: (0, 0)),
                  pl.BlockSpec((D_MODEL, tn), lambda j: (0, j)),
                  pl.BlockSpec((1, tn), lambda j: (0, j))],
        out_specs=pl.BlockSpec((8, tn), lambda j: (0, j)),
        out_shape=jax.ShapeDtypeStruct((8, n), F32),
        compiler_params=_cparams("arbitrary"),
        name="adaln",
    )(cc, w_ada, b_ada.reshape(1, n))


def _wfold_kernel(c_ref, s_ref, w_ref, o_ref):
    w = w_ref[0]
    o_ref[0, :, :FGROUP_DIM] = jnp.dot(c_ref[...], w, preferred_element_type=F32,
                                       precision=lax.Precision.HIGHEST).astype(BF16)
    o_ref[0, :, FGROUP_DIM:] = jnp.dot(s_ref[...], w, preferred_element_type=F32,
                                       precision=lax.Precision.HIGHEST).astype(BF16)


def _wfold(cmat, smat, w_fourier):
    return pl.pallas_call(
        _wfold_kernel,
        grid=(N_FGROUPS,),
        in_specs=[pl.BlockSpec((FGROUP_DIM, FGROUP_DIM), lambda g: (0, 0)),
                  pl.BlockSpec((FGROUP_DIM, FGROUP_DIM), lambda g: (0, 0)),
                  pl.BlockSpec((1, FGROUP_DIM, FGROUP_DIM), lambda g: (g, 0, 0))],
        out_specs=pl.BlockSpec((1, FGROUP_DIM, 2 * FGROUP_DIM), lambda g: (g, 0, 0)),
        out_shape=jax.ShapeDtypeStruct((N_FGROUPS, FGROUP_DIM, 2 * FGROUP_DIM), BF16),
        compiler_params=_cparams("arbitrary"),
        name="wfold",
    )(cmat, smat, w_fourier)


def _norm_mod(x, g, shift, scale):
    ms = jnp.mean(x * x, axis=-1, keepdims=True)
    y = x * lax.rsqrt(ms + EPS) * g
    return y * (1.0 + scale) + shift


def _rope_slab(p, cos, sa, sb):
    return (p * cos + pltpu.roll(p, ROPE_HALF, 1) * sa
            + pltpu.roll(p, HEAD_W - ROPE_HALF, 1) * sb)


def _inproj_kernel(x_ref, mod_ref, g_ref, w_ref, wf_ref, cos_ref, sa_ref, sb_ref,
                   q_ref, k_ref, v_ref, y_ref):
    m = mod_ref[0]
    h = _norm_mod(x_ref[0], g_ref[...], m[0:1], m[1:2]).astype(BF16)
    cos, sa, sb = cos_ref[...], sa_ref[...], sb_ref[...]
    scale = HEAD_DIM ** -0.5
    pq = jnp.dot(h, w_ref[:, 0:QK_W], preferred_element_type=F32)
    for hh in range(N_HEADS):
        sl = slice(hh * HEAD_W, (hh + 1) * HEAD_W)
        q_ref[0, :, sl] = (_rope_slab(pq[:, sl], cos, sa, sb) * scale).astype(BF16)
    pk = jnp.dot(h, w_ref[:, QK_W:2 * QK_W], preferred_element_type=F32)
    for hh in range(N_HEADS):
        sl = slice(hh * HEAD_W, (hh + 1) * HEAD_W)
        k_ref[0, :, sl] = _rope_slab(pk[:, sl], cos, sa, sb).astype(BF16)
    v_ref[0] = jnp.dot(h, w_ref[:, 2 * QK_W:3 * QK_W], preferred_element_type=F32).astype(BF16)
    pf = jnp.dot(h, w_ref[:, 3 * QK_W:], preferred_element_type=F32).astype(BF16)
    for g in range(N_FGROUPS):
        yy = jnp.dot(pf[:, g * FGROUP_DIM:(g + 1) * FGROUP_DIM], wf_ref[g],
                     preferred_element_type=F32)
        y_ref[0, :, g * FGROUP_DIM:(g + 1) * FGROUP_DIM] = yy[:, :FGROUP_DIM].astype(BF16)
        y_ref[0, :, F_W + g * FGROUP_DIM:F_W + (g + 1) * FGROUP_DIM] = yy[:, FGROUP_DIM:].astype(BF16)


def _inproj(x, mods, g_mix, w_in, wf, cos_t, sa_t, sb_t):
    b, s, d = x.shape
    tm = TM_PROJ
    tok = lambda bi, i: (bi, i, 0)
    return pl.pallas_call(
        _inproj_kernel,
        grid=(b, s // tm),
        in_specs=[pl.BlockSpec((1, tm, d), tok),
                  pl.BlockSpec((1, N_MOD, d), lambda bi, i: (bi, 0, 0)),
                  pl.BlockSpec((1, d), lambda bi, i: (0, 0)),
                  pl.BlockSpec(w_in.shape, lambda bi, i: (0, 0)),
                  pl.BlockSpec(wf.shape, lambda bi, i: (0, 0, 0)),
                  pl.BlockSpec((tm, HEAD_W), lambda bi, i: (i, 0)),
                  pl.BlockSpec((tm, HEAD_W), lambda bi, i: (i, 0)),
                  pl.BlockSpec((tm, HEAD_W), lambda bi, i: (i, 0))],
        out_specs=[pl.BlockSpec((1, tm, QK_W), tok),
                   pl.BlockSpec((1, tm, QK_W), tok),
                   pl.BlockSpec((1, tm, QK_W), tok),
                   pl.BlockSpec((1, tm, 2 * F_W), tok)],
        out_shape=[jax.ShapeDtypeStruct((b, s, QK_W), BF16),
                   jax.ShapeDtypeStruct((b, s, QK_W), BF16),
                   jax.ShapeDtypeStruct((b, s, QK_W), BF16),
                   jax.ShapeDtypeStruct((b, s, 2 * F_W), BF16)],
        compiler_params=_cparams("parallel", "arbitrary"),
        name="inproj",
    )(x, mods, g_mix, w_in, wf, cos_t, sa_t, sb_t)


def _ctxproj_kernel(x_ref, mod_ref, g_ref, w_ref, k_ref, v_ref):
    m = mod_ref[0]
    h = _norm_mod(x_ref[0], g_ref[...], m[0:1], m[1:2]).astype(BF16)
    k_ref[0] = jnp.dot(h, w_ref[:, QK_W:2 * QK_W], preferred_element_type=F32).astype(BF16)
    v_ref[0] = jnp.dot(h, w_ref[:, 2 * QK_W:3 * QK_W], preferred_element_type=F32).astype(BF16)


def _ctxproj(ctx, mods, g_mix, w_in):
    b, n, d = ctx.shape
    return pl.pallas_call(
        _ctxproj_kernel,
        grid=(b,),
        in_specs=[pl.BlockSpec((1, n, d), lambda bi: (bi, 0, 0)),
                  pl.BlockSpec((1, N_MOD, d), lambda bi: (2, 0, 0)),
                  pl.BlockSpec((1, d), lambda bi: (0, 0)),
                  pl.BlockSpec(w_in.shape, lambda bi: (0, 0))],
        out_specs=[pl.BlockSpec((1, n, QK_W), lambda bi: (bi, 0, 0)),
                   pl.BlockSpec((1, n, QK_W), lambda bi: (bi, 0, 0))],
        out_shape=[jax.ShapeDtypeStruct((b, n, QK_W), BF16),
                   jax.ShapeDtypeStruct((b, n, QK_W), BF16)],
        compiler_params=_cparams("arbitrary"),
        name="ctxproj",
    )(ctx, mods, g_mix, w_in)


def _attn_kernel(q_ref, kc_ref, vc_ref, kl_ref, vl_ref, lam_ref, g_ref, o_ref):
    q = q_ref[0]
    lane = lax.broadcasted_iota(jnp.int32, q.shape, 1)
    zero = jnp.zeros_like(q)
    q0 = jnp.where(lane < HEAD_DIM, q, zero)
    q1 = jnp.where(lane >= HEAD_DIM, q, zero)
    contract_last = (((1,), (1,)), ((), ()))

    def one_map(qm, kb, vb, m, l, acc):
        s = lax.dot_general(qm, kb, contract_last, preferred_element_type=F32)
        m_new = jnp.maximum(m, jnp.max(s, axis=-1, keepdims=True))
        alpha = jnp.exp(m - m_new)
        p = jnp.exp(s - m_new)
        l_new = alpha * l + jnp.sum(p, axis=-1, keepdims=True)
        acc_new = alpha * acc + jnp.dot(p.astype(BF16), vb, preferred_element_type=F32)
        return m_new, l_new, acc_new

    def step(kb, vb, carry):
        m0, l0, a0, m1, l1, a1 = carry
        m0, l0, a0 = one_map(q0, kb, vb, m0, l0, a0)
        m1, l1, a1 = one_map(q1, kb, vb, m1, l1, a1)
        return m0, l0, a0, m1, l1, a1

    tq = q.shape[0]
    neg = jnp.full((tq, 1), -1e30, F32)
    zl = jnp.zeros((tq, 1), F32)
    za = jnp.zeros((tq, HEAD_W), F32)
    carry = step(kc_ref[0], vc_ref[0], (neg, zl, za, neg, zl, za))

    def body(i, carry):
        off = pl.multiple_of(i * TK, TK)
        return step(kl_ref[0, pl.ds(off, TK), :], vl_ref[0, pl.ds(off, TK), :], carry)

    m0, l0, a0, m1, l1, a1 = lax.fori_loop(0, kl_ref.shape[1] // TK, body, carry)

    lp = lam_ref[...]
    t1 = jnp.sum(lp[0:1] * lp[1:2], axis=-1, keepdims=True)
    t2 = jnp.sum(lp[2:3] * lp[3:4], axis=-1, keepdims=True)
    lam = jnp.exp(t1) - jnp.exp(t2) + LAMBDA_INIT
    o = a0 / l0 - lam * (a1 / l1)
    ms = jnp.mean(o * o, axis=-1, keepdims=True)
    o = o * lax.rsqrt(ms + EPS) * g_ref[...] * (1.0 - LAMBDA_INIT)
    o_ref[0] = o.astype(BF16)


def _attention(q, kc, vc, kl, vl, lam_p, g_subln):
    b, s, _ = q.shape
    n_ctx = kc.shape[1]
    return pl.pallas_call(
        _attn_kernel,
        grid=(b, N_HEADS, s // TQ),
        in_specs=[pl.BlockSpec((1, TQ, HEAD_W), lambda bi, h, i: (bi, i, h)),
                  pl.BlockSpec((1, n_ctx, HEAD_W), lambda bi, h, i: (bi, 0, h)),
                  pl.BlockSpec((1, n_ctx, HEAD_W), lambda bi, h, i: (bi, 0, h)),
                  pl.BlockSpec((1, s, HEAD_W), lambda bi, h, i: (bi, 0, h)),
                  pl.BlockSpec((1, s, HEAD_W), lambda bi, h, i: (bi, 0, h)),
                  pl.BlockSpec((4, HEAD_DIM), lambda bi, h, i: (0, 0)),
                  pl.BlockSpec((1, HEAD_W), lambda bi, h, i: (0, 0))],
        out_specs=pl.BlockSpec((1, TQ, HEAD_W), lambda bi, h, i: (bi, i, h)),
        out_shape=jax.ShapeDtypeStruct((b, s, QK_W), BF16),
        compiler_params=_cparams("parallel", "parallel", "arbitrary"),
        name="diffattn",
    )(q, kc, vc, kl, vl, lam_p, g_subln)


def _dft1_kernel(y_ref, g_ref, br_ref, bi_ref):
    for j in range(F1_NB):
        yb = y_ref[0, :, j * 2 * F_W:(j + 1) * 2 * F_W]
        p = jnp.dot(g_ref[j], yb, preferred_element_type=F32)
        top, bot = p[:DFT_N1], p[DFT_N1:]
        br_ref[0, :, j * F_W:(j + 1) * F_W] = (top[:, :F_W] - bot[:, F_W:]).astype(BF16)
        bi_ref[0, :, j * F_W:(j + 1) * F_W] = (-top[:, F_W:] - bot[:, :F_W]).astype(BF16)


def _dft1(y2d, gmat):
    b = y2d.shape[0]
    return pl.pallas_call(
        _dft1_kernel,
        grid=(b, DFT_N2 // F1_NB),
        in_specs=[pl.BlockSpec((1, DFT_N1, F1_NB * 2 * F_W), lambda bi, j: (bi, 0, j)),
                  pl.BlockSpec((F1_NB, 2 * DFT_N1, DFT_N1), lambda bi, j: (j, 0, 0))],
        out_specs=[pl.BlockSpec((1, DFT_N1, F1_NB * F_W), lambda bi, j: (bi, 0, j)),
                   pl.BlockSpec((1, DFT_N1, F1_NB * F_W), lambda bi, j: (bi, 0, j))],
        out_shape=[jax.ShapeDtypeStruct((b, DFT_N1, DFT_N2 * F_W), BF16),
                   jax.ShapeDtypeStruct((b, DFT_N1, DFT_N2 * F_W), BF16)],
        compiler_params=_cparams("parallel", "arbitrary"),
        name="dft1",
    )(y2d, gmat)


def _dft2_kernel(br_ref, bi_ref, c_ref, s_ref, o_ref):
    for j in range(F2_KB):
        r = (jnp.dot(c_ref[...], br_ref[0, j], preferred_element_type=F32)
             + jnp.dot(s_ref[...], bi_ref[0, j], preferred_element_type=F32))
        o_ref[0, :, j * F_W:(j + 1) * F_W] = r.astype(BF16)


def _dft2(br4, bi4, c2, s2):
    b = br4.shape[0]
    blk = (1, F2_KB, DFT_N2, F_W)
    return pl.pallas_call(
        _dft2_kernel,
        grid=(b, DFT_N1 // F2_KB),
        in_specs=[pl.BlockSpec(blk, lambda bi, j: (bi, j, 0, 0)),
                  pl.BlockSpec(blk, lambda bi, j: (bi, j, 0, 0)),
                  pl.BlockSpec((DFT_N2, DFT_N2), lambda bi, j: (0, 0)),
                  pl.BlockSpec((DFT_N2, DFT_N2), lambda bi, j: (0, 0))],
        out_specs=pl.BlockSpec((1, DFT_N2, F2_KB * F_W), lambda bi, j: (bi, 0, j)),
        out_shape=jax.ShapeDtypeStruct((b, DFT_N2, DFT_N1 * F_W), BF16),
        compiler_params=_cparams("parallel", "arbitrary"),
        name="dft2",
    )(br4, bi4, c2, s2)


def _outproj_kernel(o_ref, f_ref, x_ref, mod_ref, g_ref, wo_ref, wr_ref, br_ref,
                    x1_ref, h2_ref, eid_ref, gate_ref):
    m = mod_ref[0]
    mix = (jnp.dot(o_ref[0], wo_ref[0:QK_W, :], preferred_element_type=F32)
           + jnp.dot(f_ref[0], wo_ref[QK_W:, :], preferred_element_type=F32))
    x1 = x_ref[0] + m[2:3] * mix
    x1_ref[0] = x1
    h2 = _norm_mod(x1, g_ref[...], m[3:4], m[4:5])
    h2_ref[0] = h2
    lg = jnp.dot(h2, wr_ref[...], preferred_element_type=F32,
                 precision=lax.Precision.HIGHEST) + br_ref[...]
    lane = lax.broadcasted_iota(jnp.int32, lg.shape, 1)
    ninf = jnp.float32(-jnp.inf)
    big = jnp.int32(lg.shape[1])
    gl = jnp.where(lane < N_GROUPS, lg, ninf)
    gmax = jnp.max(gl, axis=-1, keepdims=True)
    grp = jnp.min(jnp.where(gl == gmax, lane, big), axis=-1, keepdims=True)
    pg = 1.0 / jnp.sum(jnp.exp(gl - gmax), axis=-1, keepdims=True)
    e_lane = lane - N_GROUPS
    emask = (e_lane >= 0) & (e_lane < N_EXPERTS) & ((e_lane >> 3) == grp)
    el = jnp.where(emask, lg, ninf)
    t1 = jnp.max(el, axis=-1, keepdims=True)
    i1 = jnp.min(jnp.where(el == t1, lane, big), axis=-1, keepdims=True)
    el2 = jnp.where(lane == i1, ninf, el)
    t2 = jnp.max(el2, axis=-1, keepdims=True)
    i2 = jnp.min(jnp.where(el2 == t2, lane, big), axis=-1, keepdims=True)
    dd = jnp.exp(t2 - t1)
    w1 = pg / (1.0 + dd)
    w2 = pg * dd / (1.0 + dd)
    eid_ref[...] = jnp.where(lane == 0, i1 - N_GROUPS, jnp.where(lane == 1, i2 - N_GROUPS, 0))
    gate_ref[...] = jnp.where(lane == 0, w1, jnp.where(lane == 1, w2, 0.0))


def _outproj(attn_o, four, x, mods, g_ffn, w_out, w_r, b_r):
    b, s, d = x.shape
    tm = TM_PROJ
    nt = s // tm
    tok = lambda bi, i: (bi, i, 0)
    flat = lambda bi, i: (bi * nt + i, 0)
    return pl.pallas_call(
        _outproj_kernel,
        grid=(b, nt),
        in_specs=[pl.BlockSpec((1, tm, QK_W), tok),
                  pl.BlockSpec((1, tm, F_W), tok),
                  pl.BlockSpec((1, tm, d), tok),
                  pl.BlockSpec((1, N_MOD, d), lambda bi, i: (bi, 0, 0)),
                  pl.BlockSpec((1, d), lambda bi, i: (0, 0)),
                  pl.BlockSpec(w_out.shape, lambda bi, i: (0, 0)),
                  pl.BlockSpec(w_r.shape, lambda bi, i: (0, 0)),
                  pl.BlockSpec(b_r.shape, lambda bi, i: (0, 0))],
        out_specs=[pl.BlockSpec((1, tm, d), tok),
                   pl.BlockSpec((1, tm, d), tok),
                   pl.BlockSpec((tm, 128), flat),
                   pl.BlockSpec((tm, 128), flat)],
        out_shape=[jax.ShapeDtypeStruct((b, s, d), F32),
                   jax.ShapeDtypeStruct((b, s, d), F32),
                   jax.ShapeDtypeStruct((b * s, 128), jnp.int32),
                   jax.ShapeDtypeStruct((b * s, 128), F32)],
        compiler_params=_cparams("parallel", "arbitrary"),
        name="outproj",
    )(attn_o, four, x, mods, g_ffn, w_out, w_r, b_r)


def _row_copy(src_ref, src_row, dst_ref, dst_row, sem):
    return pltpu.make_async_copy(src_ref.at[pl.ds(src_row, 1)], dst_ref.at[pl.ds(dst_row, 1)], sem)


def _dispatch_kernel(dest_ref, h_ref, zeros_ref, xs_ref, sem):
    del zeros_ref
    n = 2 * TD

    def issue(a, c):
        _row_copy(h_ref, a >> 1, xs_ref, dest_ref[0, 0, a], sem).start()
        return c

    lax.fori_loop(0, n, issue, 0)

    def drain(a, c):
        _row_copy(h_ref, 0, xs_ref, 0, sem).wait()
        return c

    lax.fori_loop(0, n, drain, 0)


def _dispatch(dest3, h2_flat, xs_zeros):
    t, d = h2_flat.shape
    return pl.pallas_call(
        _dispatch_kernel,
        grid=(t // TD,),
        in_specs=[pl.BlockSpec((1, 1, 2 * TD), lambda i: (i, 0, 0), memory_space=pltpu.SMEM),
                  pl.BlockSpec((TD, d), lambda i: (i, 0)),
                  pl.BlockSpec(memory_space=pl.ANY)],
        out_specs=pl.BlockSpec(memory_space=pl.ANY),
        out_shape=jax.ShapeDtypeStruct(xs_zeros.shape, xs_zeros.dtype),
        scratch_shapes=[pltpu.SemaphoreType.DMA(())],
        input_output_aliases={2: 0},
        compiler_params=_cparams("arbitrary"),
        name="dispatch",
    )(dest3, h2_flat, xs_zeros)


def _experts_kernel(be_ref, nu_ref, xs_ref, wg_ref, wu_ref, wd_ref, ys_ref):
    @pl.when(pl.program_id(0) < nu_ref[0])
    def _():
        xb = xs_ref[...].astype(BF16)
        gate = jnp.dot(xb, wg_ref[0], preferred_element_type=F32)
        up = jnp.dot(xb, wu_ref[0], preferred_element_type=F32)
        hid = (gate * jax.nn.sigmoid(gate) * up).astype(BF16)
        ys_ref[...] = jnp.dot(hid, wd_ref[0], preferred_element_type=F32)

    @pl.when(pl.program_id(0) >= nu_ref[0])
    def _():
        ys_ref[...] = jnp.zeros_like(ys_ref)


def _experts(blk_expert, n_used, xs, wg, wu, wd):
    rows, d = xs.shape
    nb = rows // MOE_BLK
    row_blk = lambda j, be, nu: (jnp.minimum(j, nu[0] - 1), 0)
    out_blk = lambda j, be, nu: (j, 0)
    w_blk = lambda j, be, nu: (be[jnp.minimum(j, nu[0] - 1)], 0, 0)
    return pl.pallas_call(
        _experts_kernel,
        grid_spec=pltpu.PrefetchScalarGridSpec(
            num_scalar_prefetch=2,
            grid=(nb,),
            in_specs=[pl.BlockSpec((MOE_BLK, d), row_blk),
                      pl.BlockSpec((1, d, D_EXPERT), w_blk),
                      pl.BlockSpec((1, d, D_EXPERT), w_blk),
                      pl.BlockSpec((1, D_EXPERT, d), w_blk)],
            out_specs=pl.BlockSpec((MOE_BLK, d), out_blk)),
        out_shape=jax.ShapeDtypeStruct((rows, d), F32),
        compiler_params=_cparams("arbitrary"),
        name="experts",
    )(blk_expert, n_used, xs, wg, wu, wd)


def _combine_kernel(dest_ref, ys_ref, x1_ref, gate_ref, mod_ref, g_ref, o_ref, ya, yb, sem):
    def issue(r, c):
        _row_copy(ys_ref, dest_ref[0, 0, 2 * r], ya, r, sem).start()
        _row_copy(ys_ref, dest_ref[0, 0, 2 * r + 1], yb, r, sem).start()
        return c

    lax.fori_loop(0, TD, issue, 0)

    def drain(r, c):
        _row_copy(ys_ref, 0, ya, 0, sem).wait()
        _row_copy(ys_ref, 0, yb, 0, sem).wait()
        return c

    lax.fori_loop(0, TD, drain, 0)
    gt = gate_ref[...]
    moe = gt[:, 0:1] * ya[...] + gt[:, 1:2] * yb[...]
    x2 = x1_ref[0] + mod_ref[0][5:6] * moe
    ms = jnp.mean(x2 * x2, axis=-1, keepdims=True)
    o_ref[0] = x2 * lax.rsqrt(ms + EPS) * g_ref[...]


def _combine(dest3, ys, x1, gates, mods, g_final):
    b, s, d = x1.shape
    nt = s // TD
    return pl.pallas_call(
        _combine_kernel,
        grid=(b, nt),
        in_specs=[pl.BlockSpec((1, 1, 2 * TD), lambda bi, i: (bi * nt + i, 0, 0),
                               memory_space=pltpu.SMEM),
                  pl.BlockSpec(memory_space=pl.ANY),
                  pl.BlockSpec((1, TD, d), lambda bi, i: (bi, i, 0)),
                  pl.BlockSpec((TD, 128), lambda bi, i: (bi * nt + i, 0)),
                  pl.BlockSpec((1, N_MOD, d), lambda bi, i: (bi, 0, 0)),
                  pl.BlockSpec((1, d), lambda bi, i: (0, 0))],
        out_specs=pl.BlockSpec((1, TD, d), lambda bi, i: (bi, i, 0)),
        out_shape=jax.ShapeDtypeStruct((b, s, d), F32),
        scratch_shapes=[pltpu.VMEM((TD, d), F32), pltpu.VMEM((TD, d), F32),
                        pltpu.SemaphoreType.DMA(())],
        compiler_params=_cparams("arbitrary", "arbitrary"),
        name="combine",
    )(dest3, ys, x1, gates, mods, g_final)


def _rope_tables(rows):
    r, col = jnp.meshgrid(jnp.arange(rows), jnp.arange(GRID_W), indexing='ij')
    pos = jnp.stack([r.reshape(-1), col.reshape(-1)], axis=-1).astype(F32)
    inv_freq = ROPE_THETA ** (-jnp.arange(0, ROPE_AXIS, 2, dtype=F32) / ROPE_AXIS)
    ang = pos[:, :, None] * inv_freq
    ang = jnp.concatenate([ang, ang], axis=-1)
    n = ang.shape[0]
    cos = jnp.tile(jnp.cos(ang).reshape(n, HEAD_DIM), (1, 2))
    sin = jnp.tile(jnp.sin(ang).reshape(n, HEAD_DIM), (1, 2))
    upper = (jnp.arange(HEAD_W) % ROPE_AXIS) >= ROPE_HALF
    sa = jnp.where(upper, sin, 0.0)
    sb = jnp.where(upper, 0.0, -sin)
    return cos, sa, sb


def _dft_constants(n_pos):
    c = np.arange(FGROUP_DIM)
    ang_c = 2.0 * np.pi * ((c[:, None] * c[None, :]) % FGROUP_DIM) / FGROUP_DIM
    norm = 1.0 / math.sqrt(n_pos * FGROUP_DIM)
    cmat = (np.cos(ang_c) * norm).astype(np.float32)
    smat = (np.sin(ang_c) * norm).astype(np.float32)
    k1 = np.arange(DFT_N1)[None, :, None]
    n1 = np.arange(DFT_N1)[None, None, :]
    n2 = np.arange(DFT_N2)[:, None, None]
    ang_g = 2.0 * np.pi * ((k1 * (DFT_N2 * n1 + n2)) % n_pos) / n_pos
    gmat = np.concatenate([np.cos(ang_g), np.sin(ang_g)], axis=1).astype(np.float32)
    k2 = np.arange(DFT_N2)
    ang_2 = 2.0 * np.pi * ((k2[:, None] * k2[None, :]) % DFT_N2) / DFT_N2
    c2 = np.cos(ang_2).astype(np.float32)
    s2 = np.sin(ang_2).astype(np.float32)
    return cmat, smat, gmat, c2, s2


def kernel(x, c, ctx, c_ctx, w_ada, b_ada, g_mix_norm, g_ffn_norm, w_in, lambda_q1, lambda_k1, lambda_q2, lambda_k2, g_subln, w_fourier, w_out, w_router_group, b_router_group, w_router_expert, b_router_expert, w_gate, w_up, w_down, g_final):
    b, s, d = x.shape
    t = b * s
    assert d == D_MODEL and s == DFT_N1 * DFT_N2 and s % GRID_W == 0 and b == 2

    cc = jnp.concatenate([c, c_ctx[None, :], jnp.zeros((8 - b - 1, d), F32)], axis=0)
    mods = _adaln(cc, w_ada[0], b_ada[0]).reshape(8, N_MOD, d)

    cmat, smat, gmat, c2, s2 = _dft_constants(s)
    wf = _wfold(jnp.asarray(cmat), jnp.asarray(smat), w_fourier[0])
    cos_t, sa_t, sb_t = _rope_tables(s // GRID_W)

    w_in_b = w_in[0].astype(BF16)
    g_mix = g_mix_norm[0].reshape(1, d)
    q, kl, vl, y = _inproj(x, mods, g_mix, w_in_b, wf, cos_t, sa_t, sb_t)
    kc, vc = _ctxproj(ctx, mods, g_mix, w_in_b)

    lam_p = jnp.stack([lambda_q1[0], lambda_k1[0], lambda_q2[0], lambda_k2[0]], axis=0)
    attn_o = _attention(q, kc, vc, kl, vl, lam_p, g_subln[0].reshape(1, HEAD_W))

    br, bi = _dft1(y.reshape(b, DFT_N1, DFT_N2 * 2 * F_W), jnp.asarray(gmat).astype(BF16))
    four = _dft2(br.reshape(b, DFT_N1, DFT_N2, F_W), bi.reshape(b, DFT_N1, DFT_N2, F_W),
                 jnp.asarray(c2).astype(BF16), jnp.asarray(s2).astype(BF16)).reshape(b, s, F_W)

    n_r = N_GROUPS + N_EXPERTS
    w_r = jnp.concatenate([w_router_group[0], w_router_expert[0],
                           jnp.zeros((d, 128 - n_r), F32)], axis=1)
    b_r = jnp.concatenate([b_router_group[0], b_router_expert[0],
                           jnp.zeros((128 - n_r,), F32)]).reshape(1, 128)
    x1, h2, eid, gates = _outproj(attn_o, four, x, mods, g_ffn_norm[0].reshape(1, d),
                                  w_out[0].astype(BF16), w_r, b_r)

    e_flat = eid[:, :TOP_K].reshape(t * TOP_K)
    onehot = (e_flat[:, None] == jnp.arange(N_EXPERTS, dtype=jnp.int32)[None, :]).astype(jnp.int32)
    csum = jnp.cumsum(onehot, axis=0)
    rank = jnp.sum(csum * onehot, axis=1) - 1
    counts = csum[-1]
    nblk = (counts + MOE_BLK - 1) // MOE_BLK
    blk_end = jnp.cumsum(nblk)
    blk_start = blk_end - nblk
    dest = (jnp.sum(onehot * (blk_start * MOE_BLK)[None, :], axis=1) + rank).astype(jnp.int32)
    n_blocks = t * TOP_K // MOE_BLK + N_EXPERTS
    blk_expert = jnp.minimum(
        jnp.searchsorted(blk_end, jnp.arange(n_blocks, dtype=jnp.int32), side='right'),
        N_EXPERTS - 1).astype(jnp.int32)
    n_used = blk_end[-1:].astype(jnp.int32)
    dest3 = dest.reshape(t // TD, 1, 2 * TD)

    xs = _dispatch(dest3, h2.reshape(t, d), jnp.zeros((n_blocks * MOE_BLK, d), F32))
    ys = _experts(blk_expert, n_used, xs, w_gate[0].astype(BF16), w_up[0].astype(BF16),
                  w_down[0].astype(BF16))
    return _combine(dest3, ys, x1, gates, mods, g_final.reshape(1, d))
```

```python
import functools
import math

import numpy as np
import jax
import jax.numpy as jnp
from jax import lax
from jax.experimental import pallas as pl
from jax.experimental.pallas import tpu as pltpu

F32 = jnp.float32
BF16 = jnp.bfloat16

D_MODEL = 1024
GRID_W = 64
N_HEADS = 4
HEAD_DIM = 64
HEAD_W = 2 * HEAD_DIM
QK_W = N_HEADS * HEAD_W
N_FGROUPS = 4
FGROUP_DIM = 128
F_W = N_FGROUPS * FGROUP_DIM
ROPE_THETA = 10000.0
ROPE_AXIS = HEAD_DIM // 2
ROPE_HALF = ROPE_AXIS // 2
N_GROUPS = 4
EXPERTS_PER_GROUP = 8
N_EXPERTS = N_GROUPS * EXPERTS_PER_GROUP
TOP_K = 2
D_EXPERT = 512
N_MOD = 6
EPS = 1e-6
LAMBDA_INIT = 0.8 - 0.6 * math.exp(-0.3 * 0)
LOG2_E = 1.4426950408889634

DFT_N1 = 64
DFT_N2 = 128

TM_PROJ = 512
TQ = 1024
TK = 1024
F1_NB = 8
F2_KB = 8
MOE_BLK = 256
TD = 256
VMEM_LIMIT = 48 * 1024 * 1024


def _cparams(*sem):
    return pltpu.CompilerParams(dimension_semantics=sem, vmem_limit_bytes=VMEM_LIMIT)


def _adaln_kernel(c_ref, w_ref, b_ref, o_ref):
    cc = c_ref[...]
    s = cc * jax.nn.sigmoid(cc)
    o_ref[...] = jnp.dot(s, w_ref[...], preferred_element_type=F32,
                         precision=lax.Precision.HIGHEST) + b_ref[...]


def _adaln(cc, w_ada, b_ada):
    n = w_ada.shape[1]
    tn = 1536
    return pl.pallas_call(
        _adaln_kernel,
        grid=(n // tn,),
        in_specs=[pl.BlockSpec((8, D_MODEL), lambda j: (0, 0)),
                  pl.BlockSpec((D_MODEL, tn), lambda j: (0, j)),
                  pl.BlockSpec((1, tn), lambda j: (0, j))],
        out_specs=pl.BlockSpec((8, tn), lambda j: (0, j)),
        out_shape=jax.ShapeDtypeStruct((8, n), F32),
        compiler_params=_cparams("arbitrary"),
        name="adaln",
    )(cc, w_ada, b_ada.reshape(1, n))


def _wfold_kernel(c_ref, s_ref, w_ref, o_ref):
    w = w_ref[0]
    o_ref[0, :, :FGROUP_DIM] = jnp.dot(c_ref[...], w, preferred_element_type=F32,
                                       precision=lax.Precision.HIGHEST).astype(BF16)
    o_ref[0, :, FGROUP_DIM:] = jnp.dot(s_ref[...], w, preferred_element_type=F32,
                                       precision=lax.Precision.HIGHEST).astype(BF16)


def _wfold(cmat, smat, w_fourier):
    return pl.pallas_call(
        _wfold_kernel,
        grid=(N_FGROUPS,),
        in_specs=[pl.BlockSpec((FGROUP_DIM, FGROUP_DIM), lambda g: (0, 0)),
                  pl.BlockSpec((FGROUP_DIM, FGROUP_DIM), lambda g: (0, 0)),
                  pl.BlockSpec((1, FGROUP_DIM, FGROUP_DIM), lambda g: (g, 0, 0))],
        out_specs=pl.BlockSpec((1, FGROUP_DIM, 2 * FGROUP_DIM), lambda g: (g, 0, 0)),
        out_shape=jax.ShapeDtypeStruct((N_FGROUPS, FGROUP_DIM, 2 * FGROUP_DIM), BF16),
        compiler_params=_cparams("arbitrary"),
        name="wfold",
    )(cmat, smat, w_fourier)


def _norm_mod(x, g, shift, scale):
    ms = jnp.mean(x * x, axis=-1, keepdims=True)
    y = x * lax.rsqrt(ms + EPS) * g
    return y * (1.0 + scale) + shift


def _rope_slab(p, cos, sa, sb):
    return (p * cos + pltpu.roll(p, ROPE_HALF, 1) * sa
            + pltpu.roll(p, HEAD_W - ROPE_HALF, 1) * sb)


def _inproj_kernel(x_ref, mod_ref, g_ref, w_ref, wf_ref, cos_ref, sa_ref, sb_ref,
                   q_ref, k_ref, v_ref, y_ref):
    m = mod_ref[0]
    h = _norm_mod(x_ref[0], g_ref[...], m[0:1], m[1:2]).astype(BF16)
    cos, sa, sb = cos_ref[...], sa_ref[...], sb_ref[...]
    scale = HEAD_DIM ** -0.5 * LOG2_E
    pq = jnp.dot(h, w_ref[:, 0:QK_W], preferred_element_type=F32)
    for hh in range(N_HEADS):
        sl = slice(hh * HEAD_W, (hh + 1) * HEAD_W)
        q_ref[0, :, sl] = (_rope_slab(pq[:, sl], cos, sa, sb) * scale).astype(BF16)
    pk = jnp.dot(h, w_ref[:, QK_W:2 * QK_W], preferred_element_type=F32)
    for hh in range(N_HEADS):
        sl = slice(hh * HEAD_W, (hh + 1) * HEAD_W)
        k_ref[0, :, sl] = _rope_slab(pk[:, sl], cos, sa, sb).astype(BF16)
    v_ref[0] = jnp.dot(h, w_ref[:, 2 * QK_W:3 * QK_W], preferred_element_type=F32).astype(BF16)
    pf = jnp.dot(h, w_ref[:, 3 * QK_W:], preferred_element_type=F32).astype(BF16)
    for g in range(N_FGROUPS):
        yy = jnp.dot(pf[:, g * FGROUP_DIM:(g + 1) * FGROUP_DIM], wf_ref[g],
                     preferred_element_type=F32)
        y_ref[0, :, g * FGROUP_DIM:(g + 1) * FGROUP_DIM] = yy[:, :FGROUP_DIM].astype(BF16)
        y_ref[0, :, F_W + g * FGROUP_DIM:F_W + (g + 1) * FGROUP_DIM] = yy[:, FGROUP_DIM:].astype(BF16)


def _inproj(x, mods, g_mix, w_in, wf, cos_t, sa_t, sb_t):
    b, s, d = x.shape
    tm = TM_PROJ
    tok = lambda bi, i: (bi, i, 0)
    return pl.pallas_call(
        _inproj_kernel,
        grid=(b, s // tm),
        in_specs=[pl.BlockSpec((1, tm, d), tok),
                  pl.BlockSpec((1, N_MOD, d), lambda bi, i: (bi, 0, 0)),
                  pl.BlockSpec((1, d), lambda bi, i: (0, 0)),
                  pl.BlockSpec(w_in.shape, lambda bi, i: (0, 0)),
                  pl.BlockSpec(wf.shape, lambda bi, i: (0, 0, 0)),
                  pl.BlockSpec((tm, HEAD_W), lambda bi, i: (i, 0)),
                  pl.BlockSpec((tm, HEAD_W), lambda bi, i: (i, 0)),
                  pl.BlockSpec((tm, HEAD_W), lambda bi, i: (i, 0))],
        out_specs=[pl.BlockSpec((1, tm, QK_W), tok),
                   pl.BlockSpec((1, tm, QK_W), tok),
                   pl.BlockSpec((1, tm, QK_W), tok),
                   pl.BlockSpec((1, tm, 2 * F_W), tok)],
        out_shape=[jax.ShapeDtypeStruct((b, s, QK_W), BF16),
                   jax.ShapeDtypeStruct((b, s, QK_W), BF16),
                   jax.ShapeDtypeStruct((b, s, QK_W), BF16),
                   jax.ShapeDtypeStruct((b, s, 2 * F_W), BF16)],
        compiler_params=_cparams("parallel", "arbitrary"),
        name="inproj",
    )(x, mods, g_mix, w_in, wf, cos_t, sa_t, sb_t)


def _ctxproj_kernel(x_ref, mod_ref, g_ref, w_ref, k_ref, v_ref):
    m = mod_ref[0]
    h = _norm_mod(x_ref[0], g_ref[...], m[0:1], m[1:2]).astype(BF16)
    k_ref[0] = jnp.dot(h, w_ref[:, QK_W:2 * QK_W], preferred_element_type=F32).astype(BF16)
    v_ref[0] = jnp.dot(h, w_ref[:, 2 * QK_W:3 * QK_W], preferred_element_type=F32).astype(BF16)


def _ctxproj(ctx, mods, g_mix, w_in):
    b, n, d = ctx.shape
    return pl.pallas_call(
        _ctxproj_kernel,
        grid=(b,),
        in_specs=[pl.BlockSpec((1, n, d), lambda bi: (bi, 0, 0)),
                  pl.BlockSpec((1, N_MOD, d), lambda bi: (2, 0, 0)),
                  pl.BlockSpec((1, d), lambda bi: (0, 0)),
                  pl.BlockSpec(w_in.shape, lambda bi: (0, 0))],
        out_specs=[pl.BlockSpec((1, n, QK_W), lambda bi: (bi, 0, 0)),
                   pl.BlockSpec((1, n, QK_W), lambda bi: (bi, 0, 0))],
        out_shape=[jax.ShapeDtypeStruct((b, n, QK_W), BF16),
                   jax.ShapeDtypeStruct((b, n, QK_W), BF16)],
        compiler_params=_cparams("arbitrary"),
        name="ctxproj",
    )(ctx, mods, g_mix, w_in)


def _attn_kernel(q_ref, kc_ref, vc_ref, kl_ref, vl_ref, lam_ref, g_ref, o_ref,
                 m_ref, l_ref, acc_ref):
    q = q_ref[0]
    lane = lax.broadcasted_iota(jnp.int32, q.shape, 1)
    zero = jnp.zeros_like(q)
    q0 = jnp.where(lane < HEAD_DIM, q, zero)
    q1 = jnp.where(lane >= HEAD_DIM, q, zero)
    qs = (q0, q1)
    contract_last = (((1,), (1,)), ((), ()))
    m_ref[...] = jnp.full(m_ref.shape, -1e30, F32)
    l_ref[...] = jnp.zeros(l_ref.shape, F32)
    acc_ref[...] = jnp.zeros(acc_ref.shape, F32)

    def step(kb, vb):
        nk = kb.shape[0] // HEAD_W
        for mi in range(2):
            s = lax.dot_general(qs[mi], kb, contract_last, preferred_element_type=F32)
            m_old = m_ref[mi]
            m_new = jnp.maximum(m_old, jnp.max(s, axis=-1, keepdims=True))
            alpha = jnp.exp2(m_old - m_new)
            p = jnp.exp2(s - jnp.concatenate([m_new] * nk, axis=1))
            psum = p[:, 0:HEAD_W]
            for cblk in range(1, nk):
                psum = psum + p[:, cblk * HEAD_W:(cblk + 1) * HEAD_W]
            l_ref[mi] = alpha * l_ref[mi] + psum
            acc_ref[mi] = alpha * acc_ref[mi] + jnp.dot(p.astype(BF16), vb,
                                                       preferred_element_type=F32)
            m_ref[mi] = m_new

    step(kc_ref[0], vc_ref[0])

    def body(i, c):
        off = pl.multiple_of(i * TK, TK)
        step(kl_ref[0, pl.ds(off, TK), :], vl_ref[0, pl.ds(off, TK), :])
        return c

    lax.fori_loop(0, kl_ref.shape[1] // TK, body, 0)

    lp = lam_ref[...]
    t1 = jnp.sum(lp[0:1] * lp[1:2], axis=-1, keepdims=True)
    t2 = jnp.sum(lp[2:3] * lp[3:4], axis=-1, keepdims=True)
    lam = jnp.exp(t1) - jnp.exp(t2) + LAMBDA_INIT
    l0 = jnp.sum(l_ref[0], axis=-1, keepdims=True)
    l1 = jnp.sum(l_ref[1], axis=-1, keepdims=True)
    o = acc_ref[0] / l0 - lam * (acc_ref[1] / l1)
    ms = jnp.mean(o * o, axis=-1, keepdims=True)
    o = o * lax.rsqrt(ms + EPS) * g_ref[...] * (1.0 - LAMBDA_INIT)
    o_ref[0] = o.astype(BF16)


def _attention(q, kc, vc, kl, vl, lam_p, g_subln):
    b, s, _ = q.shape
    n_ctx = kc.shape[1]
    return pl.pallas_call(
        _attn_kernel,
        grid=(b, N_HEADS, s // TQ),
        in_specs=[pl.BlockSpec((1, TQ, HEAD_W), lambda bi, h, i: (bi, i, h)),
                  pl.BlockSpec((1, n_ctx, HEAD_W), lambda bi, h, i: (bi, 0, h)),
                  pl.BlockSpec((1, n_ctx, HEAD_W), lambda bi, h, i: (bi, 0, h)),
                  pl.BlockSpec((1, s, HEAD_W), lambda bi, h, i: (bi, 0, h)),
                  pl.BlockSpec((1, s, HEAD_W), lambda bi, h, i: (bi, 0, h)),
                  pl.BlockSpec((4, HEAD_DIM), lambda bi, h, i: (0, 0)),
                  pl.BlockSpec((1, HEAD_W), lambda bi, h, i: (0, 0))],
        out_specs=pl.BlockSpec((1, TQ, HEAD_W), lambda bi, h, i: (bi, i, h)),
        out_shape=jax.ShapeDtypeStruct((b, s, QK_W), BF16),
        scratch_shapes=[pltpu.VMEM((2, TQ, HEAD_W), F32)] * 3,
        compiler_params=_cparams("parallel", "parallel", "arbitrary"),
        name="diffattn",
    )(q, kc, vc, kl, vl, lam_p, g_subln)


def _dft1_kernel(y_ref, g_ref, br_ref, bi_ref):
    for j in range(F1_NB):
        yb = y_ref[0, :, j * 2 * F_W:(j + 1) * 2 * F_W]
        p = jnp.dot(g_ref[j], yb, preferred_element_type=F32)
        top, bot = p[:DFT_N1], p[DFT_N1:]
        br_ref[0, :, j * F_W:(j + 1) * F_W] = (top[:, :F_W] - bot[:, F_W:]).astype(BF16)
        bi_ref[0, :, j * F_W:(j + 1) * F_W] = (-top[:, F_W:] - bot[:, :F_W]).astype(BF16)


def _dft1(y2d, gmat):
    b = y2d.shape[0]
    return pl.pallas_call(
        _dft1_kernel,
        grid=(b, DFT_N2 // F1_NB),
        in_specs=[pl.BlockSpec((1, DFT_N1, F1_NB * 2 * F_W), lambda bi, j: (bi, 0, j)),
                  pl.BlockSpec((F1_NB, 2 * DFT_N1, DFT_N1), lambda bi, j: (j, 0, 0))],
        out_specs=[pl.BlockSpec((1, DFT_N1, F1_NB * F_W), lambda bi, j: (bi, 0, j)),
                   pl.BlockSpec((1, DFT_N1, F1_NB * F_W), lambda bi, j: (bi, 0, j))],
        out_shape=[jax.ShapeDtypeStruct((b, DFT_N1, DFT_N2 * F_W), BF16),
                   jax.ShapeDtypeStruct((b, DFT_N1, DFT_N2 * F_W), BF16)],
        compiler_params=_cparams("parallel", "arbitrary"),
        name="dft1",
    )(y2d, gmat)


def _dft2_kernel(br_ref, bi_ref, c_ref, s_ref, o_ref):
    for j in range(F2_KB):
        r = (jnp.dot(c_ref[...], br_ref[0, j], preferred_element_type=F32)
             + jnp.dot(s_ref[...], bi_ref[0, j], preferred_element_type=F32))
        o_ref[0, :, j * F_W:(j + 1) * F_W] = r.astype(BF16)


def _dft2(br4, bi4, c2, s2):
    b = br4.shape[0]
    blk = (1, F2_KB, DFT_N2, F_W)
    return pl.pallas_call(
        _dft2_kernel,
        grid=(b, DFT_N1 // F2_KB),
        in_specs=[pl.BlockSpec(blk, lambda bi, j: (bi, j, 0, 0)),
                  pl.BlockSpec(blk, lambda bi, j: (bi, j, 0, 0)),
                  pl.BlockSpec((DFT_N2, DFT_N2), lambda bi, j: (0, 0)),
                  pl.BlockSpec((DFT_N2, DFT_N2), lambda bi, j: (0, 0))],
        out_specs=pl.BlockSpec((1, DFT_N2, F2_KB * F_W), lambda bi, j: (bi, 0, j)),
        out_shape=jax.ShapeDtypeStruct((b, DFT_N2, DFT_N1 * F_W), BF16),
        compiler_params=_cparams("parallel", "arbitrary"),
        name="dft2",
    )(br4, bi4, c2, s2)


def _outproj_kernel(o_ref, f_ref, x_ref, mod_ref, g_ref, wo_ref, wr_ref, br_ref,
                    x1_ref, h2_ref, eid_ref, gate_ref):
    m = mod_ref[0]
    mix = (jnp.dot(o_ref[0], wo_ref[0:QK_W, :], preferred_element_type=F32)
           + jnp.dot(f_ref[0], wo_ref[QK_W:, :], preferred_element_type=F32))
    x1 = x_ref[0] + m[2:3] * mix
    x1_ref[0] = x1
    h2 = _norm_mod(x1, g_ref[...], m[3:4], m[4:5])
    h2_ref[0] = h2
    h_hi = h2.astype(BF16)
    h_lo = (h2 - h_hi.astype(F32)).astype(BF16)
    lg = (jnp.dot(h_hi, wr_ref[0], preferred_element_type=F32)
          + jnp.dot(h_lo, wr_ref[0], preferred_element_type=F32)
          + jnp.dot(h_hi, wr_ref[1], preferred_element_type=F32)
          + br_ref[...])
    lane = lax.broadcasted_iota(jnp.int32, lg.shape, 1)
    ninf = jnp.float32(-jnp.inf)
    big = jnp.int32(lg.shape[1])
    gl = jnp.where(lane < N_GROUPS, lg, ninf)
    gmax = jnp.max(gl, axis=-1, keepdims=True)
    grp = jnp.min(jnp.where(gl == gmax, lane, big), axis=-1, keepdims=True)
    pg = 1.0 / jnp.sum(jnp.exp(gl - gmax), axis=-1, keepdims=True)
    e_lane = lane - N_GROUPS
    emask = (e_lane >= 0) & (e_lane < N_EXPERTS) & ((e_lane >> 3) == grp)
    el = jnp.where(emask, lg, ninf)
    t1 = jnp.max(el, axis=-1, keepdims=True)
    i1 = jnp.min(jnp.where(el == t1, lane, big), axis=-1, keepdims=True)
    el2 = jnp.where(lane == i1, ninf, el)
    t2 = jnp.max(el2, axis=-1, keepdims=True)
    i2 = jnp.min(jnp.where(el2 == t2, lane, big), axis=-1, keepdims=True)
    dd = jnp.exp(t2 - t1)
    w1 = pg / (1.0 + dd)
    w2 = pg * dd / (1.0 + dd)
    eid_ref[...] = jnp.where(lane == 0, i1 - N_GROUPS, jnp.where(lane == 1, i2 - N_GROUPS, 0))
    gate_ref[...] = jnp.where(lane == 0, w1, jnp.where(lane == 1, w2, 0.0))


def _outproj(attn_o, four, x, mods, g_ffn, w_out, w_r, b_r):
    b, s, d = x.shape
    tm = TM_PROJ
    nt = s // tm
    tok = lambda bi, i: (bi, i, 0)
    flat = lambda bi, i: (bi * nt + i, 0)
    return pl.pallas_call(
        _outproj_kernel,
        grid=(b, nt),
        in_specs=[pl.BlockSpec((1, tm, QK_W), tok),
                  pl.BlockSpec((1, tm, F_W), tok),
                  pl.BlockSpec((1, tm, d), tok),
                  pl.BlockSpec((1, N_MOD, d), lambda bi, i: (bi, 0, 0)),
                  pl.BlockSpec((1, d), lambda bi, i: (0, 0)),
                  pl.BlockSpec(w_out.shape, lambda bi, i: (0, 0)),
                  pl.BlockSpec(w_r.shape, lambda bi, i: (0, 0, 0)),
                  pl.BlockSpec(b_r.shape, lambda bi, i: (0, 0))],
        out_specs=[pl.BlockSpec((1, tm, d), tok),
                   pl.BlockSpec((1, tm, d), tok),
                   pl.BlockSpec((tm, 128), flat),
                   pl.BlockSpec((tm, 128), flat)],
        out_shape=[jax.ShapeDtypeStruct((b, s, d), F32),
                   jax.ShapeDtypeStruct((b, s, d), F32),
                   jax.ShapeDtypeStruct((b * s, 128), jnp.int32),
                   jax.ShapeDtypeStruct((b * s, 128), F32)],
        compiler_params=_cparams("parallel", "arbitrary"),
        name="outproj",
    )(attn_o, four, x, mods, g_ffn, w_out, w_r, b_r)


def _row_copy(src_ref, src_row, dst_ref, dst_row, sem):
    return pltpu.make_async_copy(src_ref.at[pl.ds(src_row, 1)], dst_ref.at[pl.ds(dst_row, 1)], sem)


def _dispatch_kernel(dest_ref, h_ref, zeros_ref, xs_ref, sem):
    del zeros_ref
    n = 2 * TD

    def issue(a, c):
        _row_copy(h_ref, a >> 1, xs_ref, dest_ref[0, 0, a], sem).start()
        return c

    lax.fori_loop(0, n, issue, 0)

    def drain(a, c):
        _row_copy(h_ref, 0, xs_ref, 0, sem).wait()
        return c

    lax.fori_loop(0, n, drain, 0)


def _dispatch(dest3, h2_flat, xs_zeros):
    t, d = h2_flat.shape
    return pl.pallas_call(
        _dispatch_kernel,
        grid=(t // TD,),
        in_specs=[pl.BlockSpec((1, 1, 2 * TD), lambda i: (i, 0, 0), memory_space=pltpu.SMEM),
                  pl.BlockSpec((TD, d), lambda i: (i, 0)),
                  pl.BlockSpec(memory_space=pl.ANY)],
        out_specs=pl.BlockSpec(memory_space=pl.ANY),
        out_shape=jax.ShapeDtypeStruct(xs_zeros.shape, xs_zeros.dtype),
        scratch_shapes=[pltpu.SemaphoreType.DMA(())],
        input_output_aliases={2: 0},
        compiler_params=_cparams("arbitrary"),
        name="dispatch",
    )(dest3, h2_flat, xs_zeros)


def _experts_kernel(be_ref, nu_ref, xs_ref, wg_ref, wu_ref, wd_ref, ys_ref):
    @pl.when(pl.program_id(0) < nu_ref[0])
    def _():
        xb = xs_ref[...].astype(BF16)
        gate = jnp.dot(xb, wg_ref[0], preferred_element_type=F32)
        up = jnp.dot(xb, wu_ref[0], preferred_element_type=F32)
        hid = (gate * jax.nn.sigmoid(gate) * up).astype(BF16)
        ys_ref[...] = jnp.dot(hid, wd_ref[0], preferred_element_type=F32)

    @pl.when(pl.program_id(0) >= nu_ref[0])
    def _():
        ys_ref[...] = jnp.zeros_like(ys_ref)


def _experts(blk_expert, n_used, xs, wg, wu, wd):
    rows, d = xs.shape
    nb = rows // MOE_BLK
    row_blk = lambda j, be, nu: (jnp.minimum(j, nu[0] - 1), 0)
    out_blk = lambda j, be, nu: (j, 0)
    w_blk = lambda j, be, nu: (be[jnp.minimum(j, nu[0] - 1)], 0, 0)
    return pl.pallas_call(
        _experts_kernel,
        grid_spec=pltpu.PrefetchScalarGridSpec(
            num_scalar_prefetch=2,
            grid=(nb,),
            in_specs=[pl.BlockSpec((MOE_BLK, d), row_blk),
                      pl.BlockSpec((1, d, D_EXPERT), w_blk),
                      pl.BlockSpec((1, d, D_EXPERT), w_blk),
                      pl.BlockSpec((1, D_EXPERT, d), w_blk)],
            out_specs=pl.BlockSpec((MOE_BLK, d), out_blk)),
        out_shape=jax.ShapeDtypeStruct((rows, d), F32),
        compiler_params=_cparams("arbitrary"),
        name="experts",
    )(blk_expert, n_used, xs, wg, wu, wd)


def _combine_kernel(dest_ref, ys_ref, x1_ref, gate_ref, mod_ref, g_ref, o_ref, ya, yb, sem):
    def issue(r, c):
        _row_copy(ys_ref, dest_ref[0, 0, 2 * r], ya, r, sem).start()
        _row_copy(ys_ref, dest_ref[0, 0, 2 * r + 1], yb, r, sem).start()
        return c

    lax.fori_loop(0, TD, issue, 0)

    def drain(r, c):
        _row_copy(ys_ref, 0, ya, 0, sem).wait()
        _row_copy(ys_ref, 0, yb, 0, sem).wait()
        return c

    lax.fori_loop(0, TD, drain, 0)
    gt = gate_ref[...]
    moe = gt[:, 0:1] * ya[...] + gt[:, 1:2] * yb[...]
    x2 = x1_ref[0] + mod_ref[0][5:6] * moe
    ms = jnp.mean(x2 * x2, axis=-1, keepdims=True)
    o_ref[0] = x2 * lax.rsqrt(ms + EPS) * g_ref[...]


def _combine(dest3, ys, x1, gates, mods, g_final):
    b, s, d = x1.shape
    nt = s // TD
    return pl.pallas_call(
        _combine_kernel,
        grid=(b, nt),
        in_specs=[pl.BlockSpec((1, 1, 2 * TD), lambda bi, i: (bi * nt + i, 0, 0),
                               memory_space=pltpu.SMEM),
                  pl.BlockSpec(memory_space=pl.ANY),
                  pl.BlockSpec((1, TD, d), lambda bi, i: (bi, i, 0)),
                  pl.BlockSpec((TD, 128), lambda bi, i: (bi * nt + i, 0)),
                  pl.BlockSpec((1, N_MOD, d), lambda bi, i: (bi, 0, 0)),
                  pl.BlockSpec((1, d), lambda bi, i: (0, 0))],
        out_specs=pl.BlockSpec((1, TD, d), lambda bi, i: (bi, i, 0)),
        out_shape=jax.ShapeDtypeStruct((b, s, d), F32),
        scratch_shapes=[pltpu.VMEM((TD, d), F32), pltpu.VMEM((TD, d), F32),
                        pltpu.SemaphoreType.DMA(())],
        compiler_params=_cparams("arbitrary", "arbitrary"),
        name="combine",
    )(dest3, ys, x1, gates, mods, g_final)


def _rope_tables(rows):
    r, col = jnp.meshgrid(jnp.arange(rows), jnp.arange(GRID_W), indexing='ij')
    pos = jnp.stack([r.reshape(-1), col.reshape(-1)], axis=-1).astype(F32)
    inv_freq = ROPE_THETA ** (-jnp.arange(0, ROPE_AXIS, 2, dtype=F32) / ROPE_AXIS)
    ang = pos[:, :, None] * inv_freq
    ang = jnp.concatenate([ang, ang], axis=-1)
    n = ang.shape[0]
    cos = jnp.tile(jnp.cos(ang).reshape(n, HEAD_DIM), (1, 2))
    sin = jnp.tile(jnp.sin(ang).reshape(n, HEAD_DIM), (1, 2))
    upper = (jnp.arange(HEAD_W) % ROPE_AXIS) >= ROPE_HALF
    sa = jnp.where(upper, sin, 0.0)
    sb = jnp.where(upper, 0.0, -sin)
    return cos, sa, sb


def _dft_constants(n_pos):
    c = np.arange(FGROUP_DIM)
    ang_c = 2.0 * np.pi * ((c[:, None] * c[None, :]) % FGROUP_DIM) / FGROUP_DIM
    norm = 1.0 / math.sqrt(n_pos * FGROUP_DIM)
    cmat = (np.cos(ang_c) * norm).astype(np.float32)
    smat = (np.sin(ang_c) * norm).astype(np.float32)
    k1 = np.arange(DFT_N1)[None, :, None]
    n1 = np.arange(DFT_N1)[None, None, :]
    n2 = np.arange(DFT_N2)[:, None, None]
    ang_g = 2.0 * np.pi * ((k1 * (DFT_N2 * n1 + n2)) % n_pos) / n_pos
    gmat = np.concatenate([np.cos(ang_g), np.sin(ang_g)], axis=1).astype(np.float32)
    k2 = np.arange(DFT_N2)
    ang_2 = 2.0 * np.pi * ((k2[:, None] * k2[None, :]) % DFT_N2) / DFT_N2
    c2 = np.cos(ang_2).astype(np.float32)
    s2 = np.sin(ang_2).astype(np.float32)
    return cmat, smat, gmat, c2, s2


def kernel(x, c, ctx, c_ctx, w_ada, b_ada, g_mix_norm, g_ffn_norm, w_in, lambda_q1, lambda_k1, lambda_q2, lambda_k2, g_subln, w_fourier, w_out, w_router_group, b_router_group, w_router_expert, b_router_expert, w_gate, w_up, w_down, g_final):
    b, s, d = x.shape
    t = b * s
    assert d == D_MODEL and s == DFT_N1 * DFT_N2 and s % GRID_W == 0 and b == 2

    cc = jnp.concatenate([c, c_ctx[None, :], jnp.zeros((8 - b - 1, d), F32)], axis=0)
    mods = _adaln(cc, w_ada[0], b_ada[0]).reshape(8, N_MOD, d)

    cmat, smat, gmat, c2, s2 = _dft_constants(s)
    wf = _wfold(jnp.asarray(cmat), jnp.asarray(smat), w_fourier[0])
    cos_t, sa_t, sb_t = _rope_tables(s // GRID_W)

    w_in_b = w_in[0].astype(BF16)
    g_mix = g_mix_norm[0].reshape(1, d)
    q, kl, vl, y = _inproj(x, mods, g_mix, w_in_b, wf, cos_t, sa_t, sb_t)
    kc, vc = _ctxproj(ctx, mods, g_mix, w_in_b)

    lam_p = jnp.stack([lambda_q1[0], lambda_k1[0], lambda_q2[0], lambda_k2[0]], axis=0)
    attn_o = _attention(q, kc, vc, kl, vl, lam_p, g_subln[0].reshape(1, HEAD_W))

    br, bi = _dft1(y.reshape(b, DFT_N1, DFT_N2 * 2 * F_W), jnp.asarray(gmat).astype(BF16))
    four = _dft2(br.reshape(b, DFT_N1, DFT_N2, F_W), bi.reshape(b, DFT_N1, DFT_N2, F_W),
                 jnp.asarray(c2).astype(BF16), jnp.asarray(s2).astype(BF16)).reshape(b, s, F_W)

    n_r = N_GROUPS + N_EXPERTS
    w_r = jnp.concatenate([w_router_group[0], w_router_expert[0],
                           jnp.zeros((d, 128 - n_r), F32)], axis=1)
    b_r = jnp.concatenate([b_router_group[0], b_router_expert[0],
                           jnp.zeros((128 - n_r,), F32)]).reshape(1, 128)
    w_r_hi = w_r.astype(BF16)
    w_r_lo = (w_r - w_r_hi.astype(F32)).astype(BF16)
    x1, h2, eid, gates = _outproj(attn_o, four, x, mods, g_ffn_norm[0].reshape(1, d),
                                  w_out[0].astype(BF16), jnp.stack([w_r_hi, w_r_lo]), b_r)

    e_flat = eid[:, :TOP_K].reshape(t * TOP_K)
    onehot = (e_flat[:, None] == jnp.arange(N_EXPERTS, dtype=jnp.int32)[None, :]).astype(jnp.int32)
    csum = jnp.cumsum(onehot, axis=0)
    rank = jnp.sum(csum * onehot, axis=1) - 1
    counts = csum[-1]
    nblk = (counts + MOE_BLK - 1) // MOE_BLK
    blk_end = jnp.cumsum(nblk)
    blk_start = blk_end - nblk
    dest = (jnp.sum(onehot * (blk_start * MOE_BLK)[None, :], axis=1) + rank).astype(jnp.int32)
    n_blocks = t * TOP_K // MOE_BLK + N_EXPERTS
    blk_expert = jnp.minimum(
        jnp.searchsorted(blk_end, jnp.arange(n_blocks, dtype=jnp.int32), side='right'),
        N_EXPERTS - 1).astype(jnp.int32)
    n_used = blk_end[-1:].astype(jnp.int32)
    dest3 = dest.reshape(t // TD, 1, 2 * TD)

    xs = _dispatch(dest3, h2.reshape(t, d), jnp.zeros((n_blocks * MOE_BLK, d), F32))
    ys = _experts(blk_expert, n_used, xs, w_gate[0].astype(BF16), w_up[0].astype(BF16),
                  w_down[0].astype(BF16))
    return _combine(dest3, ys, x1, gates, mods, g_final.reshape(1, d))
```

```python
import functools
import math

import numpy as np
import jax
import jax.numpy as jnp
from jax import lax
from jax.experimental import pallas as pl
from jax.experimental.pallas import tpu as pltpu

F32 = jnp.float32
BF16 = jnp.bfloat16

D_MODEL = 1024
GRID_W = 64
N_HEADS = 4
HEAD_DIM = 64
HEAD_W = 2 * HEAD_DIM
QK_W = N_HEADS * HEAD_W
N_FGROUPS = 4
FGROUP_DIM = 128
F_W = N_FGROUPS * FGROUP_DIM
ROPE_THETA = 10000.0
ROPE_AXIS = HEAD_DIM // 2
ROPE_HALF = ROPE_AXIS // 2
N_GROUPS = 4
EXPERTS_PER_GROUP = 8
N_EXPERTS = N_GROUPS * EXPERTS_PER_GROUP
TOP_K = 2
D_EXPERT = 512
N_MOD = 6
EPS = 1e-6
LAMBDA_INIT = 0.8 - 0.6 * math.exp(-0.3 * 0)
LOG2_E = 1.4426950408889634

DFT_N1 = 64
DFT_N2 = 128

TM_PROJ = 512
TQ = 1024
TK = 1024
F1_NB = 8
F2_KB = 8
MOE_BLK = 256
ROW_TILE = D_MODEL // 128
TD = 256
VMEM_LIMIT = 48 * 1024 * 1024


def _cparams(*sem):
    return pltpu.CompilerParams(dimension_semantics=sem, vmem_limit_bytes=VMEM_LIMIT)


def _adaln_kernel(c_ref, w_ref, b_ref, o_ref):
    cc = c_ref[...]
    s = cc * jax.nn.sigmoid(cc)
    o_ref[...] = jnp.dot(s, w_ref[...], preferred_element_type=F32,
                         precision=lax.Precision.HIGHEST) + b_ref[...]


def _adaln(cc, w_ada, b_ada):
    n = w_ada.shape[1]
    tn = 1536
    return pl.pallas_call(
        _adaln_kernel,
        grid=(n // tn,),
        in_specs=[pl.BlockSpec((8, D_MODEL), lambda j: (0, 0)),
                  pl.BlockSpec((D_MODEL, tn), lambda j: (0, j)),
                  pl.BlockSpec((1, tn), lambda j: (0, j))],
        out_specs=pl.BlockSpec((8, tn), lambda j: (0, j)),
        out_shape=jax.ShapeDtypeStruct((8, n), F32),
        compiler_params=_cparams("arbitrary"),
        name="adaln",
    )(cc, w_ada, b_ada.reshape(1, n))


def _wfold_kernel(c_ref, s_ref, w_ref, o_ref):
    w = w_ref[0]
    o_ref[0, :, :FGROUP_DIM] = jnp.dot(c_ref[...], w, preferred_element_type=F32,
                                       precision=lax.Precision.HIGHEST).astype(BF16)
    o_ref[0, :, FGROUP_DIM:] = jnp.dot(s_ref[...], w, preferred_element_type=F32,
                                       precision=lax.Precision.HIGHEST).astype(BF16)


def _wfold(cmat, smat, w_fourier):
    return pl.pallas_call(
        _wfold_kernel,
        grid=(N_FGROUPS,),
        in_specs=[pl.BlockSpec((FGROUP_DIM, FGROUP_DIM), lambda g: (0, 0)),
                  pl.BlockSpec((FGROUP_DIM, FGROUP_DIM), lambda g: (0, 0)),
                  pl.BlockSpec((1, FGROUP_DIM, FGROUP_DIM), lambda g: (g, 0, 0))],
        out_specs=pl.BlockSpec((1, FGROUP_DIM, 2 * FGROUP_DIM), lambda g: (g, 0, 0)),
        out_shape=jax.ShapeDtypeStruct((N_FGROUPS, FGROUP_DIM, 2 * FGROUP_DIM), BF16),
        compiler_params=_cparams("arbitrary"),
        name="wfold",
    )(cmat, smat, w_fourier)


def _norm_mod(x, g, shift, scale):
    ms = jnp.mean(x * x, axis=-1, keepdims=True)
    y = x * lax.rsqrt(ms + EPS) * g
    return y * (1.0 + scale) + shift


def _rope_slab(p, cos, sa, sb):
    return (p * cos + pltpu.roll(p, ROPE_HALF, 1) * sa
            + pltpu.roll(p, HEAD_W - ROPE_HALF, 1) * sb)


def _inproj_kernel(x_ref, mod_ref, g_ref, w_ref, wf_ref, cos_ref, sa_ref, sb_ref,
                   q_ref, k_ref, v_ref, y_ref):
    m = mod_ref[0]
    h = _norm_mod(x_ref[0], g_ref[...], m[0:1], m[1:2]).astype(BF16)
    cos, sa, sb = cos_ref[...], sa_ref[...], sb_ref[...]
    scale = HEAD_DIM ** -0.5 * LOG2_E
    pq = jnp.dot(h, w_ref[:, 0:QK_W], preferred_element_type=F32)
    for hh in range(N_HEADS):
        sl = slice(hh * HEAD_W, (hh + 1) * HEAD_W)
        q_ref[0, :, sl] = (_rope_slab(pq[:, sl], cos, sa, sb) * scale).astype(BF16)
    pk = jnp.dot(h, w_ref[:, QK_W:2 * QK_W], preferred_element_type=F32)
    for hh in range(N_HEADS):
        sl = slice(hh * HEAD_W, (hh + 1) * HEAD_W)
        k_ref[0, :, sl] = _rope_slab(pk[:, sl], cos, sa, sb).astype(BF16)
    v_ref[0] = jnp.dot(h, w_ref[:, 2 * QK_W:3 * QK_W], preferred_element_type=F32).astype(BF16)
    pf = jnp.dot(h, w_ref[:, 3 * QK_W:], preferred_element_type=F32).astype(BF16)
    for g in range(N_FGROUPS):
        yy = jnp.dot(pf[:, g * FGROUP_DIM:(g + 1) * FGROUP_DIM], wf_ref[g],
                     preferred_element_type=F32)
        y_ref[0, :, g * FGROUP_DIM:(g + 1) * FGROUP_DIM] = yy[:, :FGROUP_DIM].astype(BF16)
        y_ref[0, :, F_W + g * FGROUP_DIM:F_W + (g + 1) * FGROUP_DIM] = yy[:, FGROUP_DIM:].astype(BF16)


def _inproj(x, mods, g_mix, w_in, wf, cos_t, sa_t, sb_t):
    b, s, d = x.shape
    tm = TM_PROJ
    tok = lambda bi, i: (bi, i, 0)
    return pl.pallas_call(
        _inproj_kernel,
        grid=(b, s // tm),
        in_specs=[pl.BlockSpec((1, tm, d), tok),
                  pl.BlockSpec((1, N_MOD, d), lambda bi, i: (bi, 0, 0)),
                  pl.BlockSpec((1, d), lambda bi, i: (0, 0)),
                  pl.BlockSpec(w_in.shape, lambda bi, i: (0, 0)),
                  pl.BlockSpec(wf.shape, lambda bi, i: (0, 0, 0)),
                  pl.BlockSpec((tm, HEAD_W), lambda bi, i: (i, 0)),
                  pl.BlockSpec((tm, HEAD_W), lambda bi, i: (i, 0)),
                  pl.BlockSpec((tm, HEAD_W), lambda bi, i: (i, 0))],
        out_specs=[pl.BlockSpec((1, tm, QK_W), tok),
                   pl.BlockSpec((1, tm, QK_W), tok),
                   pl.BlockSpec((1, tm, QK_W), tok),
                   pl.BlockSpec((1, tm, 2 * F_W), tok)],
        out_shape=[jax.ShapeDtypeStruct((b, s, QK_W), BF16),
                   jax.ShapeDtypeStruct((b, s, QK_W), BF16),
                   jax.ShapeDtypeStruct((b, s, QK_W), BF16),
                   jax.ShapeDtypeStruct((b, s, 2 * F_W), BF16)],
        compiler_params=_cparams("parallel", "arbitrary"),
        name="inproj",
    )(x, mods, g_mix, w_in, wf, cos_t, sa_t, sb_t)


def _ctxproj_kernel(x_ref, mod_ref, g_ref, w_ref, k_ref, v_ref):
    m = mod_ref[0]
    h = _norm_mod(x_ref[0], g_ref[...], m[0:1], m[1:2]).astype(BF16)
    k_ref[0] = jnp.dot(h, w_ref[:, QK_W:2 * QK_W], preferred_element_type=F32).astype(BF16)
    v_ref[0] = jnp.dot(h, w_ref[:, 2 * QK_W:3 * QK_W], preferred_element_type=F32).astype(BF16)


def _ctxproj(ctx, mods, g_mix, w_in):
    b, n, d = ctx.shape
    return pl.pallas_call(
        _ctxproj_kernel,
        grid=(b,),
        in_specs=[pl.BlockSpec((1, n, d), lambda bi: (bi, 0, 0)),
                  pl.BlockSpec((1, N_MOD, d), lambda bi: (2, 0, 0)),
                  pl.BlockSpec((1, d), lambda bi: (0, 0)),
                  pl.BlockSpec(w_in.shape, lambda bi: (0, 0))],
        out_specs=[pl.BlockSpec((1, n, QK_W), lambda bi: (bi, 0, 0)),
                   pl.BlockSpec((1, n, QK_W), lambda bi: (bi, 0, 0))],
        out_shape=[jax.ShapeDtypeStruct((b, n, QK_W), BF16),
                   jax.ShapeDtypeStruct((b, n, QK_W), BF16)],
        compiler_params=_cparams("arbitrary"),
        name="ctxproj",
    )(ctx, mods, g_mix, w_in)


def _attn_kernel(q_ref, kc_ref, vc_ref, kl_ref, vl_ref, lam_ref, g_ref, o_ref,
                 m_ref, l_ref, acc_ref):
    q = q_ref[0]
    lane = lax.broadcasted_iota(jnp.int32, q.shape, 1)
    zero = jnp.zeros_like(q)
    q0 = jnp.where(lane < HEAD_DIM, q, zero)
    q1 = jnp.where(lane >= HEAD_DIM, q, zero)
    qs = (q0, q1)
    contract_last = (((1,), (1,)), ((), ()))
    m_ref[...] = jnp.full(m_ref.shape, -1e30, F32)
    l_ref[...] = jnp.zeros(l_ref.shape, F32)
    acc_ref[...] = jnp.zeros(acc_ref.shape, F32)

    def step(kb, vb):
        nk = kb.shape[0] // HEAD_W
        for mi in range(2):
            s = lax.dot_general(qs[mi], kb, contract_last, preferred_element_type=F32)
            m_old = m_ref[mi]
            m_new = jnp.maximum(m_old, jnp.max(s, axis=-1, keepdims=True))
            alpha = jnp.exp2(m_old - m_new)
            p = jnp.exp2(s - jnp.concatenate([m_new] * nk, axis=1))
            psum = p[:, 0:HEAD_W]
            for cblk in range(1, nk):
                psum = psum + p[:, cblk * HEAD_W:(cblk + 1) * HEAD_W]
            l_ref[mi] = alpha * l_ref[mi] + psum
            acc_ref[mi] = alpha * acc_ref[mi] + jnp.dot(p.astype(BF16), vb,
                                                       preferred_element_type=F32)
            m_ref[mi] = m_new

    step(kc_ref[0], vc_ref[0])

    def body(i, c):
        off = pl.multiple_of(i * TK, TK)
        step(kl_ref[0, pl.ds(off, TK), :], vl_ref[0, pl.ds(off, TK), :])
        return c

    lax.fori_loop(0, kl_ref.shape[1] // TK, body, 0)

    lp = lam_ref[...]
    t1 = jnp.sum(lp[0:1] * lp[1:2], axis=-1, keepdims=True)
    t2 = jnp.sum(lp[2:3] * lp[3:4], axis=-1, keepdims=True)
    lam = jnp.exp(t1) - jnp.exp(t2) + LAMBDA_INIT
    l0 = jnp.sum(l_ref[0], axis=-1, keepdims=True)
    l1 = jnp.sum(l_ref[1], axis=-1, keepdims=True)
    o = acc_ref[0] / l0 - lam * (acc_ref[1] / l1)
    ms = jnp.mean(o * o, axis=-1, keepdims=True)
    o = o * lax.rsqrt(ms + EPS) * g_ref[...] * (1.0 - LAMBDA_INIT)
    o_ref[0] = o.astype(BF16)


def _attention(q, kc, vc, kl, vl, lam_p, g_subln):
    b, s, _ = q.shape
    n_ctx = kc.shape[1]
    return pl.pallas_call(
        _attn_kernel,
        grid=(b, N_HEADS, s // TQ),
        in_specs=[pl.BlockSpec((1, TQ, HEAD_W), lambda bi, h, i: (bi, i, h)),
                  pl.BlockSpec((1, n_ctx, HEAD_W), lambda bi, h, i: (bi, 0, h)),
                  pl.BlockSpec((1, n_ctx, HEAD_W), lambda bi, h, i: (bi, 0, h)),
                  pl.BlockSpec((1, s, HEAD_W), lambda bi, h, i: (bi, 0, h)),
                  pl.BlockSpec((1, s, HEAD_W), lambda bi, h, i: (bi, 0, h)),
                  pl.BlockSpec((4, HEAD_DIM), lambda bi, h, i: (0, 0)),
                  pl.BlockSpec((1, HEAD_W), lambda bi, h, i: (0, 0))],
        out_specs=pl.BlockSpec((1, TQ, HEAD_W), lambda bi, h, i: (bi, i, h)),
        out_shape=jax.ShapeDtypeStruct((b, s, QK_W), BF16),
        scratch_shapes=[pltpu.VMEM((2, TQ, HEAD_W), F32)] * 3,
        compiler_params=_cparams("parallel", "parallel", "arbitrary"),
        name="diffattn",
    )(q, kc, vc, kl, vl, lam_p, g_subln)


def _dft1_kernel(y_ref, g_ref, br_ref, bi_ref):
    for j in range(F1_NB):
        yb = y_ref[0, :, j * 2 * F_W:(j + 1) * 2 * F_W]
        p = jnp.dot(g_ref[j], yb, preferred_element_type=F32)
        top, bot = p[:DFT_N1], p[DFT_N1:]
        br_ref[0, :, j * F_W:(j + 1) * F_W] = (top[:, :F_W] - bot[:, F_W:]).astype(BF16)
        bi_ref[0, :, j * F_W:(j + 1) * F_W] = (-top[:, F_W:] - bot[:, :F_W]).astype(BF16)


def _dft1(y2d, gmat):
    b = y2d.shape[0]
    return pl.pallas_call(
        _dft1_kernel,
        grid=(b, DFT_N2 // F1_NB),
        in_specs=[pl.BlockSpec((1, DFT_N1, F1_NB * 2 * F_W), lambda bi, j: (bi, 0, j)),
                  pl.BlockSpec((F1_NB, 2 * DFT_N1, DFT_N1), lambda bi, j: (j, 0, 0))],
        out_specs=[pl.BlockSpec((1, DFT_N1, F1_NB * F_W), lambda bi, j: (bi, 0, j)),
                   pl.BlockSpec((1, DFT_N1, F1_NB * F_W), lambda bi, j: (bi, 0, j))],
        out_shape=[jax.ShapeDtypeStruct((b, DFT_N1, DFT_N2 * F_W), BF16),
                   jax.ShapeDtypeStruct((b, DFT_N1, DFT_N2 * F_W), BF16)],
        compiler_params=_cparams("parallel", "arbitrary"),
        name="dft1",
    )(y2d, gmat)


def _dft2_kernel(br_ref, bi_ref, c_ref, s_ref, o_ref):
    for j in range(F2_KB):
        r = (jnp.dot(c_ref[...], br_ref[0, j], preferred_element_type=F32)
             + jnp.dot(s_ref[...], bi_ref[0, j], preferred_element_type=F32))
        o_ref[0, :, j * F_W:(j + 1) * F_W] = r.astype(BF16)


def _dft2(br4, bi4, c2, s2):
    b = br4.shape[0]
    blk = (1, F2_KB, DFT_N2, F_W)
    return pl.pallas_call(
        _dft2_kernel,
        grid=(b, DFT_N1 // F2_KB),
        in_specs=[pl.BlockSpec(blk, lambda bi, j: (bi, j, 0, 0)),
                  pl.BlockSpec(blk, lambda bi, j: (bi, j, 0, 0)),
                  pl.BlockSpec((DFT_N2, DFT_N2), lambda bi, j: (0, 0)),
                  pl.BlockSpec((DFT_N2, DFT_N2), lambda bi, j: (0, 0))],
        out_specs=pl.BlockSpec((1, DFT_N2, F2_KB * F_W), lambda bi, j: (bi, 0, j)),
        out_shape=jax.ShapeDtypeStruct((b, DFT_N2, DFT_N1 * F_W), BF16),
        compiler_params=_cparams("parallel", "arbitrary"),
        name="dft2",
    )(br4, bi4, c2, s2)


def _outproj_kernel(o_ref, f_ref, x_ref, mod_ref, g_ref, wo_ref, wr_ref, br_ref, tri_ref,
                    x1_ref, h3_ref, meta_ref, gate_ref, cnt_out_ref, cnt_ref):
    first = (pl.program_id(0) == 0) & (pl.program_id(1) == 0)

    @pl.when(first)
    def _():
        cnt_ref[...] = jnp.zeros(cnt_ref.shape, F32)

    m = mod_ref[0]
    mix = (jnp.dot(o_ref[0], wo_ref[0:QK_W, :], preferred_element_type=F32)
           + jnp.dot(f_ref[0], wo_ref[QK_W:, :], preferred_element_type=F32))
    x1 = x_ref[0] + m[2:3] * mix
    x1_ref[0] = x1
    h2 = _norm_mod(x1, g_ref[...], m[3:4], m[4:5])
    for cblk in range(ROW_TILE):
        h3_ref[:, cblk, :] = h2[:, cblk * 128:(cblk + 1) * 128]
    h_hi = h2.astype(BF16)
    h_lo = (h2 - h_hi.astype(F32)).astype(BF16)
    lg = (jnp.dot(h_hi, wr_ref[0], preferred_element_type=F32)
          + jnp.dot(h_lo, wr_ref[0], preferred_element_type=F32)
          + jnp.dot(h_hi, wr_ref[1], preferred_element_type=F32)
          + br_ref[...])
    lane = lax.broadcasted_iota(jnp.int32, lg.shape, 1)
    ninf = jnp.float32(-jnp.inf)
    big = jnp.int32(lg.shape[1])
    gl = jnp.where(lane < N_GROUPS, lg, ninf)
    gmax = jnp.max(gl, axis=-1, keepdims=True)
    grp = jnp.min(jnp.where(gl == gmax, lane, big), axis=-1, keepdims=True)
    pg = 1.0 / jnp.sum(jnp.exp(gl - gmax), axis=-1, keepdims=True)
    e_lane = lane - N_GROUPS
    emask = (e_lane >= 0) & (e_lane < N_EXPERTS) & ((e_lane >> 3) == grp)
    el = jnp.where(emask, lg, ninf)
    t1 = jnp.max(el, axis=-1, keepdims=True)
    i1 = jnp.min(jnp.where(el == t1, lane, big), axis=-1, keepdims=True)
    el2 = jnp.where(lane == i1, ninf, el)
    t2 = jnp.max(el2, axis=-1, keepdims=True)
    i2 = jnp.min(jnp.where(el2 == t2, lane, big), axis=-1, keepdims=True)
    dd = jnp.exp(t2 - t1)
    w1 = pg / (1.0 + dd)
    w2 = pg * dd / (1.0 + dd)
    gate_ref[...] = jnp.where(lane == 0, w1, jnp.where(lane == 1, w2, 0.0))
    hit1 = lane == i1
    hit2 = lane == i2
    oh = jnp.where(hit1 | hit2, 1.0, 0.0)
    before = jnp.dot(tri_ref[...], oh.astype(BF16), preferred_element_type=F32) + cnt_ref[0:1, :]
    r1 = jnp.sum(jnp.where(hit1, before, 0.0), axis=-1, keepdims=True).astype(jnp.int32)
    r2 = jnp.sum(jnp.where(hit2, before, 0.0), axis=-1, keepdims=True).astype(jnp.int32)
    cnt_ref[0:1, :] = cnt_ref[0:1, :] + jnp.sum(oh, axis=0, keepdims=True)
    cnt_out_ref[...] = cnt_ref[...]
    meta_ref[...] = jnp.where(lane == 0, i1 - N_GROUPS,
                              jnp.where(lane == 1, i2 - N_GROUPS,
                                        jnp.where(lane == 2, r1, jnp.where(lane == 3, r2, 0))))


def _outproj(attn_o, four, x, mods, g_ffn, w_out, w_r, b_r):
    b, s, d = x.shape
    tm = TM_PROJ
    nt = s // tm
    tok = lambda bi, i: (bi, i, 0)
    flat = lambda bi, i: (bi * nt + i, 0)
    tri = jnp.asarray(np.tril(np.ones((tm, tm), np.float32), -1)).astype(BF16)
    return pl.pallas_call(
        _outproj_kernel,
        grid=(b, nt),
        in_specs=[pl.BlockSpec((1, tm, QK_W), tok),
                  pl.BlockSpec((1, tm, F_W), tok),
                  pl.BlockSpec((1, tm, d), tok),
                  pl.BlockSpec((1, N_MOD, d), lambda bi, i: (bi, 0, 0)),
                  pl.BlockSpec((1, d), lambda bi, i: (0, 0)),
                  pl.BlockSpec(w_out.shape, lambda bi, i: (0, 0)),
                  pl.BlockSpec(w_r.shape, lambda bi, i: (0, 0, 0)),
                  pl.BlockSpec(b_r.shape, lambda bi, i: (0, 0)),
                  pl.BlockSpec((tm, tm), lambda bi, i: (0, 0))],
        out_specs=[pl.BlockSpec((1, tm, d), tok),
                   pl.BlockSpec((tm, ROW_TILE, 128), lambda bi, i: (bi * nt + i, 0, 0)),
                   pl.BlockSpec((tm, 128), flat),
                   pl.BlockSpec((tm, 128), flat),
                   pl.BlockSpec((8, 128), lambda bi, i: (0, 0))],
        out_shape=[jax.ShapeDtypeStruct((b, s, d), F32),
                   jax.ShapeDtypeStruct((b * s, ROW_TILE, 128), F32),
                   jax.ShapeDtypeStruct((b * s, 128), jnp.int32),
                   jax.ShapeDtypeStruct((b * s, 128), F32),
                   jax.ShapeDtypeStruct((8, 128), F32)],
        scratch_shapes=[pltpu.VMEM((8, 128), F32)],
        compiler_params=_cparams("arbitrary", "arbitrary"),
        name="outproj",
    )(attn_o, four, x, mods, g_ffn, w_out, w_r, b_r, tri)


def _row_copy(src_ref, src_row, dst_ref, dst_row, sem):
    return pltpu.make_async_copy(src_ref.at[src_row], dst_ref.at[dst_row], sem)


def _rows_2d(ref3):
    return jnp.concatenate([ref3[:, cblk, :] for cblk in range(ROW_TILE)], axis=1)


def _dispatch_kernel(dest_ref, h_ref, zeros_ref, xs_ref, sem):
    del zeros_ref
    n = 2 * TD

    def issue(a, c):
        _row_copy(h_ref, a >> 1, xs_ref, dest_ref[0, 0, a], sem).start()
        return c

    lax.fori_loop(0, n, issue, 0)

    def drain(a, c):
        _row_copy(h_ref, 0, xs_ref, 0, sem).wait()
        return c

    lax.fori_loop(0, n, drain, 0)


def _dispatch(dest3, h3, xs_zeros):
    t = h3.shape[0]
    return pl.pallas_call(
        _dispatch_kernel,
        grid=(t // TD,),
        in_specs=[pl.BlockSpec((1, 1, 2 * TD), lambda i: (i, 0, 0), memory_space=pltpu.SMEM),
                  pl.BlockSpec((TD, ROW_TILE, 128), lambda i: (i, 0, 0)),
                  pl.BlockSpec(memory_space=pl.ANY)],
        out_specs=pl.BlockSpec(memory_space=pl.ANY),
        out_shape=jax.ShapeDtypeStruct(xs_zeros.shape, xs_zeros.dtype),
        scratch_shapes=[pltpu.SemaphoreType.DMA(())],
        input_output_aliases={2: 0},
        compiler_params=_cparams("arbitrary"),
        name="dispatch",
    )(dest3, h3, xs_zeros)


def _experts_kernel(be_ref, nu_ref, xs_ref, wg_ref, wu_ref, wd_ref, ys_ref):
    @pl.when(pl.program_id(0) < nu_ref[0])
    def _():
        xb = _rows_2d(xs_ref).astype(BF16)
        gate = jnp.dot(xb, wg_ref[0], preferred_element_type=F32)
        up = jnp.dot(xb, wu_ref[0], preferred_element_type=F32)
        hid = (gate * jax.nn.sigmoid(gate) * up).astype(BF16)
        y = jnp.dot(hid, wd_ref[0], preferred_element_type=F32)
        for cblk in range(ROW_TILE):
            ys_ref[:, cblk, :] = y[:, cblk * 128:(cblk + 1) * 128]

    @pl.when(pl.program_id(0) >= nu_ref[0])
    def _():
        ys_ref[...] = jnp.zeros_like(ys_ref)


def _experts(blk_expert, n_used, xs, wg, wu, wd):
    rows = xs.shape[0]
    d = wg.shape[1]
    nb = rows // MOE_BLK
    row_blk = lambda j, be, nu: (jnp.minimum(j, nu[0] - 1), 0, 0)
    out_blk = lambda j, be, nu: (j, 0, 0)
    w_blk = lambda j, be, nu: (be[jnp.minimum(j, nu[0] - 1)], 0, 0)
    return pl.pallas_call(
        _experts_kernel,
        grid_spec=pltpu.PrefetchScalarGridSpec(
            num_scalar_prefetch=2,
            grid=(nb,),
            in_specs=[pl.BlockSpec((MOE_BLK, ROW_TILE, 128), row_blk),
                      pl.BlockSpec((1, d, D_EXPERT), w_blk),
                      pl.BlockSpec((1, d, D_EXPERT), w_blk),
                      pl.BlockSpec((1, D_EXPERT, d), w_blk)],
            out_specs=pl.BlockSpec((MOE_BLK, ROW_TILE, 128), out_blk)),
        out_shape=jax.ShapeDtypeStruct(xs.shape, F32),
        compiler_params=_cparams("arbitrary"),
        name="experts",
    )(blk_expert, n_used, xs, wg, wu, wd)


def _combine_kernel(dest_ref, ys_ref, x1_ref, gate_ref, mod_ref, g_ref, o_ref, ya, yb, sem):
    def issue(r, c):
        _row_copy(ys_ref, dest_ref[0, 0, 2 * r], ya, r, sem).start()
        _row_copy(ys_ref, dest_ref[0, 0, 2 * r + 1], yb, r, sem).start()
        return c

    lax.fori_loop(0, TD, issue, 0)

    def drain(r, c):
        _row_copy(ys_ref, 0, ya, 0, sem).wait()
        _row_copy(ys_ref, 0, yb, 0, sem).wait()
        return c

    lax.fori_loop(0, TD, drain, 0)
    gt = gate_ref[...]
    moe = gt[:, 0:1] * _rows_2d(ya) + gt[:, 1:2] * _rows_2d(yb)
    x2 = x1_ref[0] + mod_ref[0][5:6] * moe
    ms = jnp.mean(x2 * x2, axis=-1, keepdims=True)
    o_ref[0] = x2 * lax.rsqrt(ms + EPS) * g_ref[...]


def _combine(dest3, ys, x1, gates, mods, g_final):
    b, s, d = x1.shape
    nt = s // TD
    return pl.pallas_call(
        _combine_kernel,
        grid=(b, nt),
        in_specs=[pl.BlockSpec((1, 1, 2 * TD), lambda bi, i: (bi * nt + i, 0, 0),
                               memory_space=pltpu.SMEM),
                  pl.BlockSpec(memory_space=pl.ANY),
                  pl.BlockSpec((1, TD, d), lambda bi, i: (bi, i, 0)),
                  pl.BlockSpec((TD, 128), lambda bi, i: (bi * nt + i, 0)),
                  pl.BlockSpec((1, N_MOD, d), lambda bi, i: (bi, 0, 0)),
                  pl.BlockSpec((1, d), lambda bi, i: (0, 0))],
        out_specs=pl.BlockSpec((1, TD, d), lambda bi, i: (bi, i, 0)),
        out_shape=jax.ShapeDtypeStruct((b, s, d), F32),
        scratch_shapes=[pltpu.VMEM((TD, ROW_TILE, 128), F32), pltpu.VMEM((TD, ROW_TILE, 128), F32),
                        pltpu.SemaphoreType.DMA(())],
        compiler_params=_cparams("arbitrary", "arbitrary"),
        name="combine",
    )(dest3, ys, x1, gates, mods, g_final)


def _rope_tables(rows):
    r, col = jnp.meshgrid(jnp.arange(rows), jnp.arange(GRID_W), indexing='ij')
    pos = jnp.stack([r.reshape(-1), col.reshape(-1)], axis=-1).astype(F32)
    inv_freq = ROPE_THETA ** (-jnp.arange(0, ROPE_AXIS, 2, dtype=F32) / ROPE_AXIS)
    ang = pos[:, :, None] * inv_freq
    ang = jnp.concatenate([ang, ang], axis=-1)
    n = ang.shape[0]
    cos = jnp.tile(jnp.cos(ang).reshape(n, HEAD_DIM), (1, 2))
    sin = jnp.tile(jnp.sin(ang).reshape(n, HEAD_DIM), (1, 2))
    upper = (jnp.arange(HEAD_W) % ROPE_AXIS) >= ROPE_HALF
    sa = jnp.where(upper, sin, 0.0)
    sb = jnp.where(upper, 0.0, -sin)
    return cos, sa, sb


def _dft_constants(n_pos):
    c = np.arange(FGROUP_DIM)
    ang_c = 2.0 * np.pi * ((c[:, None] * c[None, :]) % FGROUP_DIM) / FGROUP_DIM
    norm = 1.0 / math.sqrt(n_pos * FGROUP_DIM)
    cmat = (np.cos(ang_c) * norm).astype(np.float32)
    smat = (np.sin(ang_c) * norm).astype(np.float32)
    k1 = np.arange(DFT_N1)[None, :, None]
    n1 = np.arange(DFT_N1)[None, None, :]
    n2 = np.arange(DFT_N2)[:, None, None]
    ang_g = 2.0 * np.pi * ((k1 * (DFT_N2 * n1 + n2)) % n_pos) / n_pos
    gmat = np.concatenate([np.cos(ang_g), np.sin(ang_g)], axis=1).astype(np.float32)
    k2 = np.arange(DFT_N2)
    ang_2 = 2.0 * np.pi * ((k2[:, None] * k2[None, :]) % DFT_N2) / DFT_N2
    c2 = np.cos(ang_2).astype(np.float32)
    s2 = np.sin(ang_2).astype(np.float32)
    return cmat, smat, gmat, c2, s2


def kernel(x, c, ctx, c_ctx, w_ada, b_ada, g_mix_norm, g_ffn_norm, w_in, lambda_q1, lambda_k1, lambda_q2, lambda_k2, g_subln, w_fourier, w_out, w_router_group, b_router_group, w_router_expert, b_router_expert, w_gate, w_up, w_down, g_final):
    b, s, d = x.shape
    t = b * s
    assert d == D_MODEL and s == DFT_N1 * DFT_N2 and s % GRID_W == 0 and b == 2

    cc = jnp.concatenate([c, c_ctx[None, :], jnp.zeros((8 - b - 1, d), F32)], axis=0)
    mods = _adaln(cc, w_ada[0], b_ada[0]).reshape(8, N_MOD, d)

    cmat, smat, gmat, c2, s2 = _dft_constants(s)
    wf = _wfold(jnp.asarray(cmat), jnp.asarray(smat), w_fourier[0])
    cos_t, sa_t, sb_t = _rope_tables(s // GRID_W)

    w_in_b = w_in[0].astype(BF16)
    g_mix = g_mix_norm[0].reshape(1, d)
    q, kl, vl, y = _inproj(x, mods, g_mix, w_in_b, wf, cos_t, sa_t, sb_t)
    kc, vc = _ctxproj(ctx, mods, g_mix, w_in_b)

    lam_p = jnp.stack([lambda_q1[0], lambda_k1[0], lambda_q2[0], lambda_k2[0]], axis=0)
    attn_o = _attention(q, kc, vc, kl, vl, lam_p, g_subln[0].reshape(1, HEAD_W))

    br, bi = _dft1(y.reshape(b, DFT_N1, DFT_N2 * 2 * F_W), jnp.asarray(gmat).astype(BF16))
    four = _dft2(br.reshape(b, DFT_N1, DFT_N2, F_W), bi.reshape(b, DFT_N1, DFT_N2, F_W),
                 jnp.asarray(c2).astype(BF16), jnp.asarray(s2).astype(BF16)).reshape(b, s, F_W)

    n_r = N_GROUPS + N_EXPERTS
    w_r = jnp.concatenate([w_router_group[0], w_router_expert[0],
                           jnp.zeros((d, 128 - n_r), F32)], axis=1)
    b_r = jnp.concatenate([b_router_group[0], b_router_expert[0],
                           jnp.zeros((128 - n_r,), F32)]).reshape(1, 128)
    w_r_hi = w_r.astype(BF16)
    w_r_lo = (w_r - w_r_hi.astype(F32)).astype(BF16)
    x1, h3, meta, gates, cnt = _outproj(attn_o, four, x, mods, g_ffn_norm[0].reshape(1, d),
                                        w_out[0].astype(BF16), jnp.stack([w_r_hi, w_r_lo]), b_r)

    e_flat = meta[:, 0:TOP_K].reshape(t * TOP_K)
    rank = meta[:, TOP_K:2 * TOP_K].reshape(t * TOP_K)
    counts = cnt[0, N_GROUPS:N_GROUPS + N_EXPERTS].astype(jnp.int32)
    nblk = (counts + MOE_BLK - 1) // MOE_BLK
    blk_end = jnp.cumsum(nblk)
    blk_start = blk_end - nblk
    dest = (jnp.take(blk_start * MOE_BLK, e_flat) + rank).astype(jnp.int32)
    n_blocks = t * TOP_K // MOE_BLK + N_EXPERTS
    blk_expert = jnp.minimum(
        jnp.searchsorted(blk_end, jnp.arange(n_blocks, dtype=jnp.int32), side='right'),
        N_EXPERTS - 1).astype(jnp.int32)
    n_used = blk_end[-1:].astype(jnp.int32)
    dest3 = dest.reshape(t // TD, 1, 2 * TD)

    xs = _dispatch(dest3, h3, jnp.zeros((n_blocks * MOE_BLK, ROW_TILE, 128), F32))
    ys = _experts(blk_expert, n_used, xs, w_gate[0].astype(BF16), w_up[0].astype(BF16),
                  w_down[0].astype(BF16))
    return _combine(dest3, ys, x1, gates, mods, g_final.reshape(1, d))
```

```python
import functools
import math

import numpy as np
import jax
import jax.numpy as jnp
from jax import lax
from jax.experimental import pallas as pl
from jax.experimental.pallas import tpu as pltpu

F32 = jnp.float32
BF16 = jnp.bfloat16

D_MODEL = 1024
GRID_W = 64
N_HEADS = 4
HEAD_DIM = 64
HEAD_W = 2 * HEAD_DIM
QK_W = N_HEADS * HEAD_W
N_FGROUPS = 4
FGROUP_DIM = 128
F_W = N_FGROUPS * FGROUP_DIM
ROPE_THETA = 10000.0
ROPE_AXIS = HEAD_DIM // 2
ROPE_HALF = ROPE_AXIS // 2
N_GROUPS = 4
EXPERTS_PER_GROUP = 8
N_EXPERTS = N_GROUPS * EXPERTS_PER_GROUP
TOP_K = 2
D_EXPERT = 512
N_MOD = 6
EPS = 1e-6
LAMBDA_INIT = 0.8 - 0.6 * math.exp(-0.3 * 0)
LOG2_E = 1.4426950408889634

DFT_N1 = 64
DFT_N2 = 128

TM_PROJ = 512
TQ = 1024
TK = 1024
F1_NB = 8
F2_KB = 8
MOE_BLK = 256
ROW_TILE = D_MODEL // 128
TD = 256
VMEM_LIMIT = 48 * 1024 * 1024


def _cparams(*sem):
    return pltpu.CompilerParams(dimension_semantics=sem, vmem_limit_bytes=VMEM_LIMIT)


def _adaln_kernel(c_ref, w_ref, b_ref, o_ref):
    cc = c_ref[...]
    s = cc * jax.nn.sigmoid(cc)
    o_ref[...] = jnp.dot(s, w_ref[...], preferred_element_type=F32,
                         precision=lax.Precision.HIGHEST) + b_ref[...]


def _adaln(cc, w_ada, b_ada):
    n = w_ada.shape[1]
    tn = 1536
    return pl.pallas_call(
        _adaln_kernel,
        grid=(n // tn,),
        in_specs=[pl.BlockSpec((8, D_MODEL), lambda j: (0, 0)),
                  pl.BlockSpec((D_MODEL, tn), lambda j: (0, j)),
                  pl.BlockSpec((1, tn), lambda j: (0, j))],
        out_specs=pl.BlockSpec((8, tn), lambda j: (0, j)),
        out_shape=jax.ShapeDtypeStruct((8, n), F32),
        compiler_params=_cparams("arbitrary"),
        name="adaln",
    )(cc, w_ada, b_ada.reshape(1, n))


def _wfold_kernel(c_ref, s_ref, w_ref, o_ref):
    w = w_ref[0]
    o_ref[0, :, :FGROUP_DIM] = jnp.dot(c_ref[...], w, preferred_element_type=F32,
                                       precision=lax.Precision.HIGHEST).astype(BF16)
    o_ref[0, :, FGROUP_DIM:] = jnp.dot(s_ref[...], w, preferred_element_type=F32,
                                       precision=lax.Precision.HIGHEST).astype(BF16)


def _wfold(cmat, smat, w_fourier):
    return pl.pallas_call(
        _wfold_kernel,
        grid=(N_FGROUPS,),
        in_specs=[pl.BlockSpec((FGROUP_DIM, FGROUP_DIM), lambda g: (0, 0)),
                  pl.BlockSpec((FGROUP_DIM, FGROUP_DIM), lambda g: (0, 0)),
                  pl.BlockSpec((1, FGROUP_DIM, FGROUP_DIM), lambda g: (g, 0, 0))],
        out_specs=pl.BlockSpec((1, FGROUP_DIM, 2 * FGROUP_DIM), lambda g: (g, 0, 0)),
        out_shape=jax.ShapeDtypeStruct((N_FGROUPS, FGROUP_DIM, 2 * FGROUP_DIM), BF16),
        compiler_params=_cparams("arbitrary"),
        name="wfold",
    )(cmat, smat, w_fourier)


def _norm_mod(x, g, shift, scale):
    ms = jnp.mean(x * x, axis=-1, keepdims=True)
    y = x * lax.rsqrt(ms + EPS) * g
    return y * (1.0 + scale) + shift


def _rope_slab(p, cos, sa, sb):
    return (p * cos + pltpu.roll(p, ROPE_HALF, 1) * sa
            + pltpu.roll(p, HEAD_W - ROPE_HALF, 1) * sb)


def _inproj_kernel(x_ref, mod_ref, g_ref, w_ref, wf_ref, cos_ref, sa_ref, sb_ref,
                   q_ref, k_ref, v_ref, y_ref):
    m = mod_ref[0]
    h = _norm_mod(x_ref[0], g_ref[...], m[0:1], m[1:2]).astype(BF16)
    cos, sa, sb = cos_ref[...], sa_ref[...], sb_ref[...]
    scale = HEAD_DIM ** -0.5 * LOG2_E
    pq = jnp.dot(h, w_ref[:, 0:QK_W], preferred_element_type=F32)
    for hh in range(N_HEADS):
        sl = slice(hh * HEAD_W, (hh + 1) * HEAD_W)
        q_ref[0, :, sl] = (_rope_slab(pq[:, sl], cos, sa, sb) * scale).astype(BF16)
    pk = jnp.dot(h, w_ref[:, QK_W:2 * QK_W], preferred_element_type=F32)
    for hh in range(N_HEADS):
        sl = slice(hh * HEAD_W, (hh + 1) * HEAD_W)
        k_ref[0, :, sl] = _rope_slab(pk[:, sl], cos, sa, sb).astype(BF16)
    v_ref[0] = jnp.dot(h, w_ref[:, 2 * QK_W:3 * QK_W], preferred_element_type=F32).astype(BF16)
    pf = jnp.dot(h, w_ref[:, 3 * QK_W:], preferred_element_type=F32).astype(BF16)
    for g in range(N_FGROUPS):
        yy = jnp.dot(pf[:, g * FGROUP_DIM:(g + 1) * FGROUP_DIM], wf_ref[g],
                     preferred_element_type=F32)
        y_ref[0, :, g * FGROUP_DIM:(g + 1) * FGROUP_DIM] = yy[:, :FGROUP_DIM].astype(BF16)
        y_ref[0, :, F_W + g * FGROUP_DIM:F_W + (g + 1) * FGROUP_DIM] = yy[:, FGROUP_DIM:].astype(BF16)


def _inproj(x, mods, g_mix, w_in, wf, cos_t, sa_t, sb_t):
    b, s, d = x.shape
    tm = TM_PROJ
    tok = lambda bi, i: (bi, i, 0)
    return pl.pallas_call(
        _inproj_kernel,
        grid=(b, s // tm),
        in_specs=[pl.BlockSpec((1, tm, d), tok),
                  pl.BlockSpec((1, N_MOD, d), lambda bi, i: (bi, 0, 0)),
                  pl.BlockSpec((1, d), lambda bi, i: (0, 0)),
                  pl.BlockSpec(w_in.shape, lambda bi, i: (0, 0)),
                  pl.BlockSpec(wf.shape, lambda bi, i: (0, 0, 0)),
                  pl.BlockSpec((tm, HEAD_W), lambda bi, i: (i, 0)),
                  pl.BlockSpec((tm, HEAD_W), lambda bi, i: (i, 0)),
                  pl.BlockSpec((tm, HEAD_W), lambda bi, i: (i, 0))],
        out_specs=[pl.BlockSpec((1, tm, QK_W), tok),
                   pl.BlockSpec((1, tm, QK_W), tok),
                   pl.BlockSpec((1, tm, QK_W), tok),
                   pl.BlockSpec((1, tm, 2 * F_W), tok)],
        out_shape=[jax.ShapeDtypeStruct((b, s, QK_W), BF16),
                   jax.ShapeDtypeStruct((b, s, QK_W), BF16),
                   jax.ShapeDtypeStruct((b, s, QK_W), BF16),
                   jax.ShapeDtypeStruct((b, s, 2 * F_W), BF16)],
        compiler_params=_cparams("parallel", "arbitrary"),
        name="inproj",
    )(x, mods, g_mix, w_in, wf, cos_t, sa_t, sb_t)


def _ctxproj_kernel(x_ref, mod_ref, g_ref, w_ref, k_ref, v_ref):
    m = mod_ref[0]
    h = _norm_mod(x_ref[0], g_ref[...], m[0:1], m[1:2]).astype(BF16)
    k_ref[0] = jnp.dot(h, w_ref[:, QK_W:2 * QK_W], preferred_element_type=F32).astype(BF16)
    v_ref[0] = jnp.dot(h, w_ref[:, 2 * QK_W:3 * QK_W], preferred_element_type=F32).astype(BF16)


def _ctxproj(ctx, mods, g_mix, w_in):
    b, n, d = ctx.shape
    return pl.pallas_call(
        _ctxproj_kernel,
        grid=(b,),
        in_specs=[pl.BlockSpec((1, n, d), lambda bi: (bi, 0, 0)),
                  pl.BlockSpec((1, N_MOD, d), lambda bi: (2, 0, 0)),
                  pl.BlockSpec((1, d), lambda bi: (0, 0)),
                  pl.BlockSpec(w_in.shape, lambda bi: (0, 0))],
        out_specs=[pl.BlockSpec((1, n, QK_W), lambda bi: (bi, 0, 0)),
                   pl.BlockSpec((1, n, QK_W), lambda bi: (bi, 0, 0))],
        out_shape=[jax.ShapeDtypeStruct((b, n, QK_W), BF16),
                   jax.ShapeDtypeStruct((b, n, QK_W), BF16)],
        compiler_params=_cparams("arbitrary"),
        name="ctxproj",
    )(ctx, mods, g_mix, w_in)


def _attn_kernel(q_ref, kc_ref, vc_ref, kl_ref, vl_ref, lam_ref, g_ref, o_ref,
                 m_ref, l_ref, acc_ref):
    q = q_ref[0]
    lane = lax.broadcasted_iota(jnp.int32, q.shape, 1)
    zero = jnp.zeros_like(q)
    q0 = jnp.where(lane < HEAD_DIM, q, zero)
    q1 = jnp.where(lane >= HEAD_DIM, q, zero)
    qs = (q0, q1)
    contract_last = (((1,), (1,)), ((), ()))
    m_ref[...] = jnp.full(m_ref.shape, -1e30, F32)
    l_ref[...] = jnp.zeros(l_ref.shape, F32)
    acc_ref[...] = jnp.zeros(acc_ref.shape, F32)

    def step(kb, vb):
        nk = kb.shape[0] // HEAD_W
        for mi in range(2):
            s = lax.dot_general(qs[mi], kb, contract_last, preferred_element_type=F32)
            m_old = m_ref[mi]
            m_new = jnp.maximum(m_old, jnp.max(s, axis=-1, keepdims=True))
            alpha = jnp.exp2(m_old - m_new)
            p = jnp.exp2(s - jnp.concatenate([m_new] * nk, axis=1))
            psum = p[:, 0:HEAD_W]
            for cblk in range(1, nk):
                psum = psum + p[:, cblk * HEAD_W:(cblk + 1) * HEAD_W]
            l_ref[mi] = alpha * l_ref[mi] + psum
            acc_ref[mi] = alpha * acc_ref[mi] + jnp.dot(p.astype(BF16), vb,
                                                       preferred_element_type=F32)
            m_ref[mi] = m_new

    step(kc_ref[0], vc_ref[0])

    def body(i, c):
        off = pl.multiple_of(i * TK, TK)
        step(kl_ref[0, pl.ds(off, TK), :], vl_ref[0, pl.ds(off, TK), :])
        return c

    lax.fori_loop(0, kl_ref.shape[1] // TK, body, 0)

    lp = lam_ref[...]
    t1 = jnp.sum(lp[0:1] * lp[1:2], axis=-1, keepdims=True)
    t2 = jnp.sum(lp[2:3] * lp[3:4], axis=-1, keepdims=True)
    lam = jnp.exp(t1) - jnp.exp(t2) + LAMBDA_INIT
    l0 = jnp.sum(l_ref[0], axis=-1, keepdims=True)
    l1 = jnp.sum(l_ref[1], axis=-1, keepdims=True)
    o = acc_ref[0] / l0 - lam * (acc_ref[1] / l1)
    ms = jnp.mean(o * o, axis=-1, keepdims=True)
    o = o * lax.rsqrt(ms + EPS) * g_ref[...] * (1.0 - LAMBDA_INIT)
    o_ref[0] = o.astype(BF16)


def _attention(q, kc, vc, kl, vl, lam_p, g_subln):
    b, s, _ = q.shape
    n_ctx = kc.shape[1]
    return pl.pallas_call(
        _attn_kernel,
        grid=(b, N_HEADS, s // TQ),
        in_specs=[pl.BlockSpec((1, TQ, HEAD_W), lambda bi, h, i: (bi, i, h)),
                  pl.BlockSpec((1, n_ctx, HEAD_W), lambda bi, h, i: (bi, 0, h)),
                  pl.BlockSpec((1, n_ctx, HEAD_W), lambda bi, h, i: (bi, 0, h)),
                  pl.BlockSpec((1, s, HEAD_W), lambda bi, h, i: (bi, 0, h)),
                  pl.BlockSpec((1, s, HEAD_W), lambda bi, h, i: (bi, 0, h)),
                  pl.BlockSpec((4, HEAD_DIM), lambda bi, h, i: (0, 0)),
                  pl.BlockSpec((1, HEAD_W), lambda bi, h, i: (0, 0))],
        out_specs=pl.BlockSpec((1, TQ, HEAD_W), lambda bi, h, i: (bi, i, h)),
        out_shape=jax.ShapeDtypeStruct((b, s, QK_W), BF16),
        scratch_shapes=[pltpu.VMEM((2, TQ, HEAD_W), F32)] * 3,
        compiler_params=_cparams("parallel", "parallel", "arbitrary"),
        name="diffattn",
    )(q, kc, vc, kl, vl, lam_p, g_subln)


def _dft1_kernel(y_ref, g_ref, br_ref, bi_ref):
    for j in range(F1_NB):
        yb = y_ref[0, :, j * 2 * F_W:(j + 1) * 2 * F_W]
        p = jnp.dot(g_ref[j], yb, preferred_element_type=F32)
        top, bot = p[:DFT_N1], p[DFT_N1:]
        br_ref[0, :, j * F_W:(j + 1) * F_W] = (top[:, :F_W] - bot[:, F_W:]).astype(BF16)
        bi_ref[0, :, j * F_W:(j + 1) * F_W] = (-top[:, F_W:] - bot[:, :F_W]).astype(BF16)


def _dft1(y2d, gmat):
    b = y2d.shape[0]
    return pl.pallas_call(
        _dft1_kernel,
        grid=(b, DFT_N2 // F1_NB),
        in_specs=[pl.BlockSpec((1, DFT_N1, F1_NB * 2 * F_W), lambda bi, j: (bi, 0, j)),
                  pl.BlockSpec((F1_NB, 2 * DFT_N1, DFT_N1), lambda bi, j: (j, 0, 0))],
        out_specs=[pl.BlockSpec((1, DFT_N1, F1_NB * F_W), lambda bi, j: (bi, 0, j)),
                   pl.BlockSpec((1, DFT_N1, F1_NB * F_W), lambda bi, j: (bi, 0, j))],
        out_shape=[jax.ShapeDtypeStruct((b, DFT_N1, DFT_N2 * F_W), BF16),
                   jax.ShapeDtypeStruct((b, DFT_N1, DFT_N2 * F_W), BF16)],
        compiler_params=_cparams("parallel", "arbitrary"),
        name="dft1",
    )(y2d, gmat)


def _dft2_kernel(br_ref, bi_ref, c_ref, s_ref, o_ref):
    for j in range(F2_KB):
        r = (jnp.dot(c_ref[...], br_ref[0, j], preferred_element_type=F32)
             + jnp.dot(s_ref[...], bi_ref[0, j], preferred_element_type=F32))
        o_ref[0, :, j * F_W:(j + 1) * F_W] = r.astype(BF16)


def _dft2(br4, bi4, c2, s2):
    b = br4.shape[0]
    blk = (1, F2_KB, DFT_N2, F_W)
    return pl.pallas_call(
        _dft2_kernel,
        grid=(b, DFT_N1 // F2_KB),
        in_specs=[pl.BlockSpec(blk, lambda bi, j: (bi, j, 0, 0)),
                  pl.BlockSpec(blk, lambda bi, j: (bi, j, 0, 0)),
                  pl.BlockSpec((DFT_N2, DFT_N2), lambda bi, j: (0, 0)),
                  pl.BlockSpec((DFT_N2, DFT_N2), lambda bi, j: (0, 0))],
        out_specs=pl.BlockSpec((1, DFT_N2, F2_KB * F_W), lambda bi, j: (bi, 0, j)),
        out_shape=jax.ShapeDtypeStruct((b, DFT_N2, DFT_N1 * F_W), BF16),
        compiler_params=_cparams("parallel", "arbitrary"),
        name="dft2",
    )(br4, bi4, c2, s2)


def _outproj_kernel(o_ref, f_ref, x_ref, mod_ref, g_ref, wo_ref, wr_ref, br_ref, tri_ref,
                    x1_ref, h3_ref, meta_ref, gate_ref, cnt_out_ref, cnt_ref):
    first = (pl.program_id(0) == 0) & (pl.program_id(1) == 0)

    @pl.when(first)
    def _():
        cnt_ref[...] = jnp.zeros(cnt_ref.shape, F32)

    m = mod_ref[0]
    mix = (jnp.dot(o_ref[0], wo_ref[0:QK_W, :], preferred_element_type=F32)
           + jnp.dot(f_ref[0], wo_ref[QK_W:, :], preferred_element_type=F32))
    x1 = x_ref[0] + m[2:3] * mix
    x1_ref[0] = x1
    h2 = _norm_mod(x1, g_ref[...], m[3:4], m[4:5])
    for cblk in range(ROW_TILE):
        h3_ref[:, cblk, :] = h2[:, cblk * 128:(cblk + 1) * 128]
    h_hi = h2.astype(BF16)
    h_lo = (h2 - h_hi.astype(F32)).astype(BF16)
    lg = (jnp.dot(h_hi, wr_ref[0], preferred_element_type=F32)
          + jnp.dot(h_lo, wr_ref[0], preferred_element_type=F32)
          + jnp.dot(h_hi, wr_ref[1], preferred_element_type=F32)
          + br_ref[...])
    lane = lax.broadcasted_iota(jnp.int32, lg.shape, 1)
    ninf = jnp.float32(-jnp.inf)
    big = jnp.int32(lg.shape[1])
    gl = jnp.where(lane < N_GROUPS, lg, ninf)
    gmax = jnp.max(gl, axis=-1, keepdims=True)
    grp = jnp.min(jnp.where(gl == gmax, lane, big), axis=-1, keepdims=True)
    pg = 1.0 / jnp.sum(jnp.exp(gl - gmax), axis=-1, keepdims=True)
    e_lane = lane - N_GROUPS
    emask = (e_lane >= 0) & (e_lane < N_EXPERTS) & ((e_lane >> 3) == grp)
    el = jnp.where(emask, lg, ninf)
    t1 = jnp.max(el, axis=-1, keepdims=True)
    i1 = jnp.min(jnp.where(el == t1, lane, big), axis=-1, keepdims=True)
    el2 = jnp.where(lane == i1, ninf, el)
    t2 = jnp.max(el2, axis=-1, keepdims=True)
    i2 = jnp.min(jnp.where(el2 == t2, lane, big), axis=-1, keepdims=True)
    dd = jnp.exp(t2 - t1)
    w1 = pg / (1.0 + dd)
    w2 = pg * dd / (1.0 + dd)
    gate_ref[...] = jnp.where(lane == 0, w1, jnp.where(lane == 1, w2, 0.0))
    hit1 = lane == i1
    hit2 = lane == i2
    oh = jnp.where(hit1 | hit2, 1.0, 0.0)
    before = jnp.dot(tri_ref[...], oh.astype(BF16), preferred_element_type=F32) + cnt_ref[0:1, :]
    r1 = jnp.sum(jnp.where(hit1, before, 0.0), axis=-1, keepdims=True).astype(jnp.int32)
    r2 = jnp.sum(jnp.where(hit2, before, 0.0), axis=-1, keepdims=True).astype(jnp.int32)
    cnt_ref[0:1, :] = cnt_ref[0:1, :] + jnp.sum(oh, axis=0, keepdims=True)
    cnt_out_ref[...] = cnt_ref[...]
    meta_ref[...] = jnp.where(lane == 0, i1 - N_GROUPS,
                              jnp.where(lane == 1, i2 - N_GROUPS,
                                        jnp.where(lane == 2, r1, jnp.where(lane == 3, r2, 0))))


def _outproj(attn_o, four, x, mods, g_ffn, w_out, w_r, b_r):
    b, s, d = x.shape
    tm = TM_PROJ
    nt = s // tm
    tok = lambda bi, i: (bi, i, 0)
    flat = lambda bi, i: (bi * nt + i, 0)
    tri = jnp.asarray(np.tril(np.ones((tm, tm), np.float32), -1)).astype(BF16)
    return pl.pallas_call(
        _outproj_kernel,
        grid=(b, nt),
        in_specs=[pl.BlockSpec((1, tm, QK_W), tok),
                  pl.BlockSpec((1, tm, F_W), tok),
                  pl.BlockSpec((1, tm, d), tok),
                  pl.BlockSpec((1, N_MOD, d), lambda bi, i: (bi, 0, 0)),
                  pl.BlockSpec((1, d), lambda bi, i: (0, 0)),
                  pl.BlockSpec(w_out.shape, lambda bi, i: (0, 0)),
                  pl.BlockSpec(w_r.shape, lambda bi, i: (0, 0, 0)),
                  pl.BlockSpec(b_r.shape, lambda bi, i: (0, 0)),
                  pl.BlockSpec((tm, tm), lambda bi, i: (0, 0))],
        out_specs=[pl.BlockSpec((1, tm, d), tok),
                   pl.BlockSpec((tm, ROW_TILE, 128), lambda bi, i: (bi * nt + i, 0, 0)),
                   pl.BlockSpec((tm, 128), flat),
                   pl.BlockSpec((tm, 128), flat),
                   pl.BlockSpec((8, 128), lambda bi, i: (0, 0))],
        out_shape=[jax.ShapeDtypeStruct((b, s, d), F32),
                   jax.ShapeDtypeStruct((b * s, ROW_TILE, 128), F32),
                   jax.ShapeDtypeStruct((b * s, 128), jnp.int32),
                   jax.ShapeDtypeStruct((b * s, 128), F32),
                   jax.ShapeDtypeStruct((8, 128), F32)],
        scratch_shapes=[pltpu.VMEM((8, 128), F32)],
        compiler_params=_cparams("arbitrary", "arbitrary"),
        name="outproj",
    )(attn_o, four, x, mods, g_ffn, w_out, w_r, b_r, tri)


def _row_copy(src_ref, src_row, dst_ref, dst_row, sem):
    return pltpu.make_async_copy(src_ref.at[src_row], dst_ref.at[dst_row], sem)


def _rows_2d(ref3):
    return jnp.concatenate([ref3[:, cblk, :] for cblk in range(ROW_TILE)], axis=1)


def _dispatch_kernel(dest_ref, h_ref, zeros_ref, xs_ref, sem):
    del zeros_ref
    n = 2 * TD

    def issue(a, c):
        _row_copy(h_ref, a >> 1, xs_ref, dest_ref[0, 0, a], sem).start()
        return c

    lax.fori_loop(0, n, issue, 0)

    def drain(a, c):
        _row_copy(h_ref, 0, xs_ref, 0, sem).wait()
        return c

    lax.fori_loop(0, n, drain, 0)


def _dispatch(dest3, h3, xs_zeros):
    t = h3.shape[0]
    return pl.pallas_call(
        _dispatch_kernel,
        grid=(t // TD,),
        in_specs=[pl.BlockSpec((1, 1, 2 * TD), lambda i: (i, 0, 0), memory_space=pltpu.SMEM),
                  pl.BlockSpec((TD, ROW_TILE, 128), lambda i: (i, 0, 0)),
                  pl.BlockSpec(memory_space=pl.ANY)],
        out_specs=pl.BlockSpec(memory_space=pl.ANY),
        out_shape=jax.ShapeDtypeStruct(xs_zeros.shape, xs_zeros.dtype),
        scratch_shapes=[pltpu.SemaphoreType.DMA(())],
        input_output_aliases={2: 0},
        compiler_params=_cparams("arbitrary"),
        name="dispatch",
    )(dest3, h3, xs_zeros)


def _experts_kernel(be_ref, nu_ref, xs_ref, wg_ref, wu_ref, wd_ref, ys_ref, wgb, wub, wdb):
    j = pl.program_id(0)
    used = j < nu_ref[0]
    new_expert = (j == 0) | (be_ref[j] != be_ref[jnp.maximum(j - 1, 0)])

    @pl.when(used & new_expert)
    def _():
        wgb[...] = wg_ref[0].astype(BF16)
        wub[...] = wu_ref[0].astype(BF16)
        wdb[...] = wd_ref[0].astype(BF16)

    @pl.when(used)
    def _():
        xb = _rows_2d(xs_ref).astype(BF16)
        gate = jnp.dot(xb, wgb[...], preferred_element_type=F32)
        up = jnp.dot(xb, wub[...], preferred_element_type=F32)
        hid = (gate * jax.nn.sigmoid(gate) * up).astype(BF16)
        y = jnp.dot(hid, wdb[...], preferred_element_type=F32)
        for cblk in range(ROW_TILE):
            ys_ref[:, cblk, :] = y[:, cblk * 128:(cblk + 1) * 128]

    @pl.when(pl.program_id(0) >= nu_ref[0])
    def _():
        ys_ref[...] = jnp.zeros_like(ys_ref)


def _experts(blk_expert, n_used, xs, wg, wu, wd):
    rows = xs.shape[0]
    d = wg.shape[1]
    nb = rows // MOE_BLK
    row_blk = lambda j, be, nu: (jnp.minimum(j, nu[0] - 1), 0, 0)
    out_blk = lambda j, be, nu: (j, 0, 0)
    w_blk = lambda j, be, nu: (be[jnp.minimum(j, nu[0] - 1)], 0, 0)
    return pl.pallas_call(
        _experts_kernel,
        grid_spec=pltpu.PrefetchScalarGridSpec(
            num_scalar_prefetch=2,
            grid=(nb,),
            in_specs=[pl.BlockSpec((MOE_BLK, ROW_TILE, 128), row_blk),
                      pl.BlockSpec((1, d, D_EXPERT), w_blk),
                      pl.BlockSpec((1, d, D_EXPERT), w_blk),
                      pl.BlockSpec((1, D_EXPERT, d), w_blk)],
            out_specs=pl.BlockSpec((MOE_BLK, ROW_TILE, 128), out_blk),
            scratch_shapes=[pltpu.VMEM((d, D_EXPERT), BF16), pltpu.VMEM((d, D_EXPERT), BF16),
                            pltpu.VMEM((D_EXPERT, d), BF16)]),
        out_shape=jax.ShapeDtypeStruct(xs.shape, F32),
        compiler_params=_cparams("arbitrary"),
        name="experts",
    )(blk_expert, n_used, xs, wg, wu, wd)


def _combine_kernel(dest_ref, ys_ref, x1_ref, gate_ref, mod_ref, g_ref, o_ref, ya, yb, sem):
    def issue(r, c):
        _row_copy(ys_ref, dest_ref[0, 0, 2 * r], ya, r, sem).start()
        _row_copy(ys_ref, dest_ref[0, 0, 2 * r + 1], yb, r, sem).start()
        return c

    lax.fori_loop(0, TD, issue, 0)

    def drain(r, c):
        _row_copy(ys_ref, 0, ya, 0, sem).wait()
        _row_copy(ys_ref, 0, yb, 0, sem).wait()
        return c

    lax.fori_loop(0, TD, drain, 0)
    gt = gate_ref[...]
    moe = gt[:, 0:1] * _rows_2d(ya) + gt[:, 1:2] * _rows_2d(yb)
    x2 = x1_ref[0] + mod_ref[0][5:6] * moe
    ms = jnp.mean(x2 * x2, axis=-1, keepdims=True)
    o_ref[0] = x2 * lax.rsqrt(ms + EPS) * g_ref[...]


def _combine(dest3, ys, x1, gates, mods, g_final):
    b, s, d = x1.shape
    nt = s // TD
    return pl.pallas_call(
        _combine_kernel,
        grid=(b, nt),
        in_specs=[pl.BlockSpec((1, 1, 2 * TD), lambda bi, i: (bi * nt + i, 0, 0),
                               memory_space=pltpu.SMEM),
                  pl.BlockSpec(memory_space=pl.ANY),
                  pl.BlockSpec((1, TD, d), lambda bi, i: (bi, i, 0)),
                  pl.BlockSpec((TD, 128), lambda bi, i: (bi * nt + i, 0)),
                  pl.BlockSpec((1, N_MOD, d), lambda bi, i: (bi, 0, 0)),
                  pl.BlockSpec((1, d), lambda bi, i: (0, 0))],
        out_specs=pl.BlockSpec((1, TD, d), lambda bi, i: (bi, i, 0)),
        out_shape=jax.ShapeDtypeStruct((b, s, d), F32),
        scratch_shapes=[pltpu.VMEM((TD, ROW_TILE, 128), F32), pltpu.VMEM((TD, ROW_TILE, 128), F32),
                        pltpu.SemaphoreType.DMA(())],
        compiler_params=_cparams("arbitrary", "arbitrary"),
        name="combine",
    )(dest3, ys, x1, gates, mods, g_final)


def _rope_tables(rows):
    r, col = jnp.meshgrid(jnp.arange(rows), jnp.arange(GRID_W), indexing='ij')
    pos = jnp.stack([r.reshape(-1), col.reshape(-1)], axis=-1).astype(F32)
    inv_freq = ROPE_THETA ** (-jnp.arange(0, ROPE_AXIS, 2, dtype=F32) / ROPE_AXIS)
    ang = pos[:, :, None] * inv_freq
    ang = jnp.concatenate([ang, ang], axis=-1)
    n = ang.shape[0]
    cos = jnp.tile(jnp.cos(ang).reshape(n, HEAD_DIM), (1, 2))
    sin = jnp.tile(jnp.sin(ang).reshape(n, HEAD_DIM), (1, 2))
    upper = (jnp.arange(HEAD_W) % ROPE_AXIS) >= ROPE_HALF
    sa = jnp.where(upper, sin, 0.0)
    sb = jnp.where(upper, 0.0, -sin)
    return cos, sa, sb


def _dft_constants(n_pos):
    c = np.arange(FGROUP_DIM)
    ang_c = 2.0 * np.pi * ((c[:, None] * c[None, :]) % FGROUP_DIM) / FGROUP_DIM
    norm = 1.0 / math.sqrt(n_pos * FGROUP_DIM)
    cmat = (np.cos(ang_c) * norm).astype(np.float32)
    smat = (np.sin(ang_c) * norm).astype(np.float32)
    k1 = np.arange(DFT_N1)[None, :, None]
    n1 = np.arange(DFT_N1)[None, None, :]
    n2 = np.arange(DFT_N2)[:, None, None]
    ang_g = 2.0 * np.pi * ((k1 * (DFT_N2 * n1 + n2)) % n_pos) / n_pos
    gmat = np.concatenate([np.cos(ang_g), np.sin(ang_g)], axis=1).astype(np.float32)
    k2 = np.arange(DFT_N2)
    ang_2 = 2.0 * np.pi * ((k2[:, None] * k2[None, :]) % DFT_N2) / DFT_N2
    c2 = np.cos(ang_2).astype(np.float32)
    s2 = np.sin(ang_2).astype(np.float32)
    return cmat, smat, gmat, c2, s2


def kernel(x, c, ctx, c_ctx, w_ada, b_ada, g_mix_norm, g_ffn_norm, w_in, lambda_q1, lambda_k1, lambda_q2, lambda_k2, g_subln, w_fourier, w_out, w_router_group, b_router_group, w_router_expert, b_router_expert, w_gate, w_up, w_down, g_final):
    b, s, d = x.shape
    t = b * s
    assert d == D_MODEL and s == DFT_N1 * DFT_N2 and s % GRID_W == 0 and b == 2

    cc = jnp.concatenate([c, c_ctx[None, :], jnp.zeros((8 - b - 1, d), F32)], axis=0)
    mods = _adaln(cc, w_ada[0], b_ada[0]).reshape(8, N_MOD, d)

    cmat, smat, gmat, c2, s2 = _dft_constants(s)
    wf = _wfold(jnp.asarray(cmat), jnp.asarray(smat), w_fourier[0])
    cos_t, sa_t, sb_t = _rope_tables(s // GRID_W)

    w_in_b = w_in[0].astype(BF16)
    g_mix = g_mix_norm[0].reshape(1, d)
    q, kl, vl, y = _inproj(x, mods, g_mix, w_in_b, wf, cos_t, sa_t, sb_t)
    kc, vc = _ctxproj(ctx, mods, g_mix, w_in_b)

    lam_p = jnp.stack([lambda_q1[0], lambda_k1[0], lambda_q2[0], lambda_k2[0]], axis=0)
    attn_o = _attention(q, kc, vc, kl, vl, lam_p, g_subln[0].reshape(1, HEAD_W))

    br, bi = _dft1(y.reshape(b, DFT_N1, DFT_N2 * 2 * F_W), jnp.asarray(gmat).astype(BF16))
    four = _dft2(br.reshape(b, DFT_N1, DFT_N2, F_W), bi.reshape(b, DFT_N1, DFT_N2, F_W),
                 jnp.asarray(c2).astype(BF16), jnp.asarray(s2).astype(BF16)).reshape(b, s, F_W)

    n_r = N_GROUPS + N_EXPERTS
    w_r = jnp.concatenate([w_router_group[0], w_router_expert[0],
                           jnp.zeros((d, 128 - n_r), F32)], axis=1)
    b_r = jnp.concatenate([b_router_group[0], b_router_expert[0],
                           jnp.zeros((128 - n_r,), F32)]).reshape(1, 128)
    w_r_hi = w_r.astype(BF16)
    w_r_lo = (w_r - w_r_hi.astype(F32)).astype(BF16)
    x1, h3, meta, gates, cnt = _outproj(attn_o, four, x, mods, g_ffn_norm[0].reshape(1, d),
                                        w_out[0].astype(BF16), jnp.stack([w_r_hi, w_r_lo]), b_r)

    e_flat = meta[:, 0:TOP_K].reshape(t * TOP_K)
    rank = meta[:, TOP_K:2 * TOP_K].reshape(t * TOP_K)
    counts = cnt[0, N_GROUPS:N_GROUPS + N_EXPERTS].astype(jnp.int32)
    nblk = (counts + MOE_BLK - 1) // MOE_BLK
    blk_end = jnp.cumsum(nblk)
    blk_start = blk_end - nblk
    dest = (jnp.take(blk_start * MOE_BLK, e_flat) + rank).astype(jnp.int32)
    n_blocks = t * TOP_K // MOE_BLK + N_EXPERTS
    blk_ids = jnp.arange(n_blocks, dtype=jnp.int32)
    blk_expert = jnp.minimum(
        jnp.sum((blk_end[None, :] <= blk_ids[:, None]).astype(jnp.int32), axis=1),
        N_EXPERTS - 1).astype(jnp.int32)
    n_used = blk_end[-1:].astype(jnp.int32)
    dest3 = dest.reshape(t // TD, 1, 2 * TD)

    xs = _dispatch(dest3, h3, jnp.zeros((n_blocks * MOE_BLK, ROW_TILE, 128), F32))
    ys = _experts(blk_expert, n_used, xs, w_gate[0], w_up[0], w_down[0])
    return _combine(dest3, ys, x1, gates, mods, g_final.reshape(1, d))
```

```python
import functools
import math

import numpy as np
import jax
import jax.numpy as jnp
from jax import lax
from jax.experimental import pallas as pl
from jax.experimental.pallas import tpu as pltpu

F32 = jnp.float32
BF16 = jnp.bfloat16

D_MODEL = 1024
GRID_W = 64
N_HEADS = 4
HEAD_DIM = 64
HEAD_W = 2 * HEAD_DIM
QK_W = N_HEADS * HEAD_W
N_FGROUPS = 4
FGROUP_DIM = 128
F_W = N_FGROUPS * FGROUP_DIM
ROPE_THETA = 10000.0
ROPE_AXIS = HEAD_DIM // 2
ROPE_HALF = ROPE_AXIS // 2
N_GROUPS = 4
EXPERTS_PER_GROUP = 8
N_EXPERTS = N_GROUPS * EXPERTS_PER_GROUP
TOP_K = 2
D_EXPERT = 512
N_MOD = 6
EPS = 1e-6
LAMBDA_INIT = 0.8 - 0.6 * math.exp(-0.3 * 0)
LOG2_E = 1.4426950408889634

DFT_N1 = 64
DFT_N2 = 128

TM_PROJ = 512
TQ = 1024
TK = 1024
F1_NB = 8
F2_KB = 8
MOE_BLK = 256
ROW_TILE = D_MODEL // 128
TD = 256
DMA_UNROLL = 16
VMEM_LIMIT = 48 * 1024 * 1024


def _cparams(*sem):
    return pltpu.CompilerParams(dimension_semantics=sem, vmem_limit_bytes=VMEM_LIMIT)


def _adaln_kernel(c_ref, w_ref, b_ref, o_ref):
    cc = c_ref[...]
    s = cc * jax.nn.sigmoid(cc)
    o_ref[...] = jnp.dot(s, w_ref[...], preferred_element_type=F32,
                         precision=lax.Precision.HIGHEST) + b_ref[...]


def _adaln(cc, w_ada, b_ada):
    n = w_ada.shape[1]
    tn = 1536
    return pl.pallas_call(
        _adaln_kernel,
        grid=(n // tn,),
        in_specs=[pl.BlockSpec((8, D_MODEL), lambda j: (0, 0)),
                  pl.BlockSpec((D_MODEL, tn), lambda j: (0, j)),
                  pl.BlockSpec((1, tn), lambda j: (0, j))],
        out_specs=pl.BlockSpec((8, tn), lambda j: (0, j)),
        out_shape=jax.ShapeDtypeStruct((8, n), F32),
        compiler_params=_cparams("arbitrary"),
        name="adaln",
    )(cc, w_ada, b_ada.reshape(1, n))


def _wfold_kernel(c_ref, s_ref, w_ref, o_ref):
    w = w_ref[0]
    o_ref[0, :, :FGROUP_DIM] = jnp.dot(c_ref[...], w, preferred_element_type=F32,
                                       precision=lax.Precision.HIGHEST).astype(BF16)
    o_ref[0, :, FGROUP_DIM:] = jnp.dot(s_ref[...], w, preferred_element_type=F32,
                                       precision=lax.Precision.HIGHEST).astype(BF16)


def _wfold(cmat, smat, w_fourier):
    return pl.pallas_call(
        _wfold_kernel,
        grid=(N_FGROUPS,),
        in_specs=[pl.BlockSpec((FGROUP_DIM, FGROUP_DIM), lambda g: (0, 0)),
                  pl.BlockSpec((FGROUP_DIM, FGROUP_DIM), lambda g: (0, 0)),
                  pl.BlockSpec((1, FGROUP_DIM, FGROUP_DIM), lambda g: (g, 0, 0))],
        out_specs=pl.BlockSpec((1, FGROUP_DIM, 2 * FGROUP_DIM), lambda g: (g, 0, 0)),
        out_shape=jax.ShapeDtypeStruct((N_FGROUPS, FGROUP_DIM, 2 * FGROUP_DIM), BF16),
        compiler_params=_cparams("arbitrary"),
        name="wfold",
    )(cmat, smat, w_fourier)


def _norm_mod(x, g, shift, scale):
    ms = jnp.mean(x * x, axis=-1, keepdims=True)
    y = x * lax.rsqrt(ms + EPS) * g
    return y * (1.0 + scale) + shift


def _rope_slab(p, cos, sa, sb):
    return (p * cos + pltpu.roll(p, ROPE_HALF, 1) * sa
            + pltpu.roll(p, HEAD_W - ROPE_HALF, 1) * sb)


def _inproj_kernel(x_ref, mod_ref, g_ref, w_ref, wf_ref, cos_ref, sa_ref, sb_ref,
                   q_ref, k_ref, v_ref, y_ref):
    m = mod_ref[0]
    h = _norm_mod(x_ref[0], g_ref[...], m[0:1], m[1:2]).astype(BF16)
    cos, sa, sb = cos_ref[...], sa_ref[...], sb_ref[...]
    scale = HEAD_DIM ** -0.5 * LOG2_E
    pq = jnp.dot(h, w_ref[:, 0:QK_W], preferred_element_type=F32)
    for hh in range(N_HEADS):
        sl = slice(hh * HEAD_W, (hh + 1) * HEAD_W)
        q_ref[0, :, sl] = (_rope_slab(pq[:, sl], cos, sa, sb) * scale).astype(BF16)
    pk = jnp.dot(h, w_ref[:, QK_W:2 * QK_W], preferred_element_type=F32)
    for hh in range(N_HEADS):
        sl = slice(hh * HEAD_W, (hh + 1) * HEAD_W)
        k_ref[0, :, sl] = _rope_slab(pk[:, sl], cos, sa, sb).astype(BF16)
    v_ref[0] = jnp.dot(h, w_ref[:, 2 * QK_W:3 * QK_W], preferred_element_type=F32).astype(BF16)
    pf = jnp.dot(h, w_ref[:, 3 * QK_W:], preferred_element_type=F32).astype(BF16)
    for g in range(N_FGROUPS):
        yy = jnp.dot(pf[:, g * FGROUP_DIM:(g + 1) * FGROUP_DIM], wf_ref[g],
                     preferred_element_type=F32)
        y_ref[0, :, g * FGROUP_DIM:(g + 1) * FGROUP_DIM] = yy[:, :FGROUP_DIM].astype(BF16)
        y_ref[0, :, F_W + g * FGROUP_DIM:F_W + (g + 1) * FGROUP_DIM] = yy[:, FGROUP_DIM:].astype(BF16)


def _inproj(x, mods, g_mix, w_in, wf, cos_t, sa_t, sb_t):
    b, s, d = x.shape
    tm = TM_PROJ
    tok = lambda bi, i: (bi, i, 0)
    return pl.pallas_call(
        _inproj_kernel,
        grid=(b, s // tm),
        in_specs=[pl.BlockSpec((1, tm, d), tok),
                  pl.BlockSpec((1, N_MOD, d), lambda bi, i: (bi, 0, 0)),
                  pl.BlockSpec((1, d), lambda bi, i: (0, 0)),
                  pl.BlockSpec(w_in.shape, lambda bi, i: (0, 0)),
                  pl.BlockSpec(wf.shape, lambda bi, i: (0, 0, 0)),
                  pl.BlockSpec((tm, HEAD_W), lambda bi, i: (i, 0)),
                  pl.BlockSpec((tm, HEAD_W), lambda bi, i: (i, 0)),
                  pl.BlockSpec((tm, HEAD_W), lambda bi, i: (i, 0))],
        out_specs=[pl.BlockSpec((1, tm, QK_W), tok),
                   pl.BlockSpec((1, tm, QK_W), tok),
                   pl.BlockSpec((1, tm, QK_W), tok),
                   pl.BlockSpec((1, tm, 2 * F_W), tok)],
        out_shape=[jax.ShapeDtypeStruct((b, s, QK_W), BF16),
                   jax.ShapeDtypeStruct((b, s, QK_W), BF16),
                   jax.ShapeDtypeStruct((b, s, QK_W), BF16),
                   jax.ShapeDtypeStruct((b, s, 2 * F_W), BF16)],
        compiler_params=_cparams("parallel", "arbitrary"),
        name="inproj",
    )(x, mods, g_mix, w_in, wf, cos_t, sa_t, sb_t)


def _ctxproj_kernel(x_ref, mod_ref, g_ref, w_ref, k_ref, v_ref):
    m = mod_ref[0]
    h = _norm_mod(x_ref[0], g_ref[...], m[0:1], m[1:2]).astype(BF16)
    k_ref[0] = jnp.dot(h, w_ref[:, QK_W:2 * QK_W], preferred_element_type=F32).astype(BF16)
    v_ref[0] = jnp.dot(h, w_ref[:, 2 * QK_W:3 * QK_W], preferred_element_type=F32).astype(BF16)


def _ctxproj(ctx, mods, g_mix, w_in):
    b, n, d = ctx.shape
    return pl.pallas_call(
        _ctxproj_kernel,
        grid=(b,),
        in_specs=[pl.BlockSpec((1, n, d), lambda bi: (bi, 0, 0)),
                  pl.BlockSpec((1, N_MOD, d), lambda bi: (2, 0, 0)),
                  pl.BlockSpec((1, d), lambda bi: (0, 0)),
                  pl.BlockSpec(w_in.shape, lambda bi: (0, 0))],
        out_specs=[pl.BlockSpec((1, n, QK_W), lambda bi: (bi, 0, 0)),
                   pl.BlockSpec((1, n, QK_W), lambda bi: (bi, 0, 0))],
        out_shape=[jax.ShapeDtypeStruct((b, n, QK_W), BF16),
                   jax.ShapeDtypeStruct((b, n, QK_W), BF16)],
        compiler_params=_cparams("arbitrary"),
        name="ctxproj",
    )(ctx, mods, g_mix, w_in)


def _attn_kernel(q_ref, kc_ref, vc_ref, kl_ref, vl_ref, lam_ref, g_ref, o_ref,
                 m_ref, l_ref, acc_ref):
    q = q_ref[0]
    lane = lax.broadcasted_iota(jnp.int32, q.shape, 1)
    zero = jnp.zeros_like(q)
    q0 = jnp.where(lane < HEAD_DIM, q, zero)
    q1 = jnp.where(lane >= HEAD_DIM, q, zero)
    qs = (q0, q1)
    contract_last = (((1,), (1,)), ((), ()))
    m_ref[...] = jnp.full(m_ref.shape, -1e30, F32)
    l_ref[...] = jnp.zeros(l_ref.shape, F32)
    acc_ref[...] = jnp.zeros(acc_ref.shape, F32)

    def step(kb, vb):
        nk = kb.shape[0] // HEAD_W
        for mi in range(2):
            s = lax.dot_general(qs[mi], kb, contract_last, preferred_element_type=F32)
            m_old = m_ref[mi]
            m_new = jnp.maximum(m_old, jnp.max(s, axis=-1, keepdims=True))
            alpha = jnp.exp2(m_old - m_new)
            p = jnp.exp2(s - jnp.concatenate([m_new] * nk, axis=1))
            psum = p[:, 0:HEAD_W]
            for cblk in range(1, nk):
                psum = psum + p[:, cblk * HEAD_W:(cblk + 1) * HEAD_W]
            l_ref[mi] = alpha * l_ref[mi] + psum
            acc_ref[mi] = alpha * acc_ref[mi] + jnp.dot(p.astype(BF16), vb,
                                                       preferred_element_type=F32)
            m_ref[mi] = m_new

    step(kc_ref[0], vc_ref[0])

    def body(i, c):
        off = pl.multiple_of(i * TK, TK)
        step(kl_ref[0, pl.ds(off, TK), :], vl_ref[0, pl.ds(off, TK), :])
        return c

    lax.fori_loop(0, kl_ref.shape[1] // TK, body, 0)

    lp = lam_ref[...]
    t1 = jnp.sum(lp[0:1] * lp[1:2], axis=-1, keepdims=True)
    t2 = jnp.sum(lp[2:3] * lp[3:4], axis=-1, keepdims=True)
    lam = jnp.exp(t1) - jnp.exp(t2) + LAMBDA_INIT
    l0 = jnp.sum(l_ref[0], axis=-1, keepdims=True)
    l1 = jnp.sum(l_ref[1], axis=-1, keepdims=True)
    o = acc_ref[0] / l0 - lam * (acc_ref[1] / l1)
    ms = jnp.mean(o * o, axis=-1, keepdims=True)
    o = o * lax.rsqrt(ms + EPS) * g_ref[...] * (1.0 - LAMBDA_INIT)
    o_ref[0] = o.astype(BF16)


def _attention(q, kc, vc, kl, vl, lam_p, g_subln):
    b, s, _ = q.shape
    n_ctx = kc.shape[1]
    return pl.pallas_call(
        _attn_kernel,
        grid=(b, N_HEADS, s // TQ),
        in_specs=[pl.BlockSpec((1, TQ, HEAD_W), lambda bi, h, i: (bi, i, h)),
                  pl.BlockSpec((1, n_ctx, HEAD_W), lambda bi, h, i: (bi, 0, h)),
                  pl.BlockSpec((1, n_ctx, HEAD_W), lambda bi, h, i: (bi, 0, h)),
                  pl.BlockSpec((1, s, HEAD_W), lambda bi, h, i: (bi, 0, h)),
                  pl.BlockSpec((1, s, HEAD_W), lambda bi, h, i: (bi, 0, h)),
                  pl.BlockSpec((4, HEAD_DIM), lambda bi, h, i: (0, 0)),
                  pl.BlockSpec((1, HEAD_W), lambda bi, h, i: (0, 0))],
        out_specs=pl.BlockSpec((1, TQ, HEAD_W), lambda bi, h, i: (bi, i, h)),
        out_shape=jax.ShapeDtypeStruct((b, s, QK_W), BF16),
        scratch_shapes=[pltpu.VMEM((2, TQ, HEAD_W), F32)] * 3,
        compiler_params=_cparams("parallel", "parallel", "arbitrary"),
        name="diffattn",
    )(q, kc, vc, kl, vl, lam_p, g_subln)


def _dft1_kernel(y_ref, g_ref, br_ref, bi_ref):
    for j in range(F1_NB):
        yb = y_ref[0, :, j * 2 * F_W:(j + 1) * 2 * F_W]
        p = jnp.dot(g_ref[j], yb, preferred_element_type=F32)
        top, bot = p[:DFT_N1], p[DFT_N1:]
        br_ref[0, :, j * F_W:(j + 1) * F_W] = (top[:, :F_W] - bot[:, F_W:]).astype(BF16)
        bi_ref[0, :, j * F_W:(j + 1) * F_W] = (-top[:, F_W:] - bot[:, :F_W]).astype(BF16)


def _dft1(y2d, gmat):
    b = y2d.shape[0]
    return pl.pallas_call(
        _dft1_kernel,
        grid=(b, DFT_N2 // F1_NB),
        in_specs=[pl.BlockSpec((1, DFT_N1, F1_NB * 2 * F_W), lambda bi, j: (bi, 0, j)),
                  pl.BlockSpec((F1_NB, 2 * DFT_N1, DFT_N1), lambda bi, j: (j, 0, 0))],
        out_specs=[pl.BlockSpec((1, DFT_N1, F1_NB * F_W), lambda bi, j: (bi, 0, j)),
                   pl.BlockSpec((1, DFT_N1, F1_NB * F_W), lambda bi, j: (bi, 0, j))],
        out_shape=[jax.ShapeDtypeStruct((b, DFT_N1, DFT_N2 * F_W), BF16),
                   jax.ShapeDtypeStruct((b, DFT_N1, DFT_N2 * F_W), BF16)],
        compiler_params=_cparams("parallel", "arbitrary"),
        name="dft1",
    )(y2d, gmat)


def _dft2_kernel(br_ref, bi_ref, c_ref, s_ref, o_ref):
    for j in range(F2_KB):
        r = (jnp.dot(c_ref[...], br_ref[0, j], preferred_element_type=F32)
             + jnp.dot(s_ref[...], bi_ref[0, j], preferred_element_type=F32))
        o_ref[0, :, j * F_W:(j + 1) * F_W] = r.astype(BF16)


def _dft2(br4, bi4, c2, s2):
    b = br4.shape[0]
    blk = (1, F2_KB, DFT_N2, F_W)
    return pl.pallas_call(
        _dft2_kernel,
        grid=(b, DFT_N1 // F2_KB),
        in_specs=[pl.BlockSpec(blk, lambda bi, j: (bi, j, 0, 0)),
                  pl.BlockSpec(blk, lambda bi, j: (bi, j, 0, 0)),
                  pl.BlockSpec((DFT_N2, DFT_N2), lambda bi, j: (0, 0)),
                  pl.BlockSpec((DFT_N2, DFT_N2), lambda bi, j: (0, 0))],
        out_specs=pl.BlockSpec((1, DFT_N2, F2_KB * F_W), lambda bi, j: (bi, 0, j)),
        out_shape=jax.ShapeDtypeStruct((b, DFT_N2, DFT_N1 * F_W), BF16),
        compiler_params=_cparams("parallel", "arbitrary"),
        name="dft2",
    )(br4, bi4, c2, s2)


def _outproj_kernel(o_ref, f_ref, x_ref, mod_ref, g_ref, wo_ref, wr_ref, br_ref, tri_ref,
                    x1_ref, h3_ref, meta_ref, gate_ref, cnt_out_ref, cnt_ref):
    first = (pl.program_id(0) == 0) & (pl.program_id(1) == 0)

    @pl.when(first)
    def _():
        cnt_ref[...] = jnp.zeros(cnt_ref.shape, F32)

    m = mod_ref[0]
    mix = (jnp.dot(o_ref[0], wo_ref[0:QK_W, :], preferred_element_type=F32)
           + jnp.dot(f_ref[0], wo_ref[QK_W:, :], preferred_element_type=F32))
    x1 = x_ref[0] + m[2:3] * mix
    x1_ref[0] = x1
    h2 = _norm_mod(x1, g_ref[...], m[3:4], m[4:5])
    for cblk in range(ROW_TILE):
        h3_ref[:, cblk, :] = h2[:, cblk * 128:(cblk + 1) * 128]
    h_hi = h2.astype(BF16)
    h_lo = (h2 - h_hi.astype(F32)).astype(BF16)
    lg = (jnp.dot(h_hi, wr_ref[0], preferred_element_type=F32)
          + jnp.dot(h_lo, wr_ref[0], preferred_element_type=F32)
          + jnp.dot(h_hi, wr_ref[1], preferred_element_type=F32)
          + br_ref[...])
    lane = lax.broadcasted_iota(jnp.int32, lg.shape, 1)
    ninf = jnp.float32(-jnp.inf)
    big = jnp.int32(lg.shape[1])
    gl = jnp.where(lane < N_GROUPS, lg, ninf)
    gmax = jnp.max(gl, axis=-1, keepdims=True)
    grp = jnp.min(jnp.where(gl == gmax, lane, big), axis=-1, keepdims=True)
    pg = 1.0 / jnp.sum(jnp.exp(gl - gmax), axis=-1, keepdims=True)
    e_lane = lane - N_GROUPS
    emask = (e_lane >= 0) & (e_lane < N_EXPERTS) & ((e_lane >> 3) == grp)
    el = jnp.where(emask, lg, ninf)
    t1 = jnp.max(el, axis=-1, keepdims=True)
    i1 = jnp.min(jnp.where(el == t1, lane, big), axis=-1, keepdims=True)
    el2 = jnp.where(lane == i1, ninf, el)
    t2 = jnp.max(el2, axis=-1, keepdims=True)
    i2 = jnp.min(jnp.where(el2 == t2, lane, big), axis=-1, keepdims=True)
    dd = jnp.exp(t2 - t1)
    w1 = pg / (1.0 + dd)
    w2 = pg * dd / (1.0 + dd)
    gate_ref[...] = jnp.where(lane == 0, w1, jnp.where(lane == 1, w2, 0.0))
    hit1 = lane == i1
    hit2 = lane == i2
    oh = jnp.where(hit1 | hit2, 1.0, 0.0)
    before = jnp.dot(tri_ref[...], oh.astype(BF16), preferred_element_type=F32) + cnt_ref[0:1, :]
    r1 = jnp.sum(jnp.where(hit1, before, 0.0), axis=-1, keepdims=True).astype(jnp.int32)
    r2 = jnp.sum(jnp.where(hit2, before, 0.0), axis=-1, keepdims=True).astype(jnp.int32)
    cnt_ref[0:1, :] = cnt_ref[0:1, :] + jnp.sum(oh, axis=0, keepdims=True)
    cnt_out_ref[...] = cnt_ref[...]
    meta_ref[...] = jnp.where(lane == 0, i1 - N_GROUPS,
                              jnp.where(lane == 1, i2 - N_GROUPS,
                                        jnp.where(lane == 2, r1, jnp.where(lane == 3, r2, 0))))


def _outproj(attn_o, four, x, mods, g_ffn, w_out, w_r, b_r):
    b, s, d = x.shape
    tm = TM_PROJ
    nt = s // tm
    tok = lambda bi, i: (bi, i, 0)
    flat = lambda bi, i: (bi * nt + i, 0)
    tri = jnp.asarray(np.tril(np.ones((tm, tm), np.float32), -1)).astype(BF16)
    return pl.pallas_call(
        _outproj_kernel,
        grid=(b, nt),
        in_specs=[pl.BlockSpec((1, tm, QK_W), tok),
                  pl.BlockSpec((1, tm, F_W), tok),
                  pl.BlockSpec((1, tm, d), tok),
                  pl.BlockSpec((1, N_MOD, d), lambda bi, i: (bi, 0, 0)),
                  pl.BlockSpec((1, d), lambda bi, i: (0, 0)),
                  pl.BlockSpec(w_out.shape, lambda bi, i: (0, 0)),
                  pl.BlockSpec(w_r.shape, lambda bi, i: (0, 0, 0)),
                  pl.BlockSpec(b_r.shape, lambda bi, i: (0, 0)),
                  pl.BlockSpec((tm, tm), lambda bi, i: (0, 0))],
        out_specs=[pl.BlockSpec((1, tm, d), tok),
                   pl.BlockSpec((tm, ROW_TILE, 128), lambda bi, i: (bi * nt + i, 0, 0)),
                   pl.BlockSpec((tm, 128), flat),
                   pl.BlockSpec((tm, 128), flat),
                   pl.BlockSpec((8, 128), lambda bi, i: (0, 0))],
        out_shape=[jax.ShapeDtypeStruct((b, s, d), F32),
                   jax.ShapeDtypeStruct((b * s, ROW_TILE, 128), F32),
                   jax.ShapeDtypeStruct((b * s, 128), jnp.int32),
                   jax.ShapeDtypeStruct((b * s, 128), F32),
                   jax.ShapeDtypeStruct((8, 128), F32)],
        scratch_shapes=[pltpu.VMEM((8, 128), F32)],
        compiler_params=_cparams("arbitrary", "arbitrary"),
        name="outproj",
    )(attn_o, four, x, mods, g_ffn, w_out, w_r, b_r, tri)


def _row_copy(src_ref, src_row, dst_ref, dst_row, sem):
    return pltpu.make_async_copy(src_ref.at[src_row], dst_ref.at[dst_row], sem)


def _rows_2d(ref3):
    return jnp.concatenate([ref3[:, cblk, :] for cblk in range(ROW_TILE)], axis=1)


def _dispatch_kernel(dest_ref, h_ref, zeros_ref, xs_ref, sem):
    del zeros_ref
    n = 2 * TD

    def issue(a, c):
        _row_copy(h_ref, a >> 1, xs_ref, dest_ref[0, 0, a], sem).start()
        return c

    lax.fori_loop(0, n, issue, 0, unroll=DMA_UNROLL)

    def drain(a, c):
        _row_copy(h_ref, 0, xs_ref, 0, sem).wait()
        return c

    lax.fori_loop(0, n, drain, 0, unroll=DMA_UNROLL)


def _dispatch(dest3, h3, xs_zeros):
    t = h3.shape[0]
    return pl.pallas_call(
        _dispatch_kernel,
        grid=(t // TD,),
        in_specs=[pl.BlockSpec((1, 1, 2 * TD), lambda i: (i, 0, 0), memory_space=pltpu.SMEM),
                  pl.BlockSpec((TD, ROW_TILE, 128), lambda i: (i, 0, 0)),
                  pl.BlockSpec(memory_space=pl.ANY)],
        out_specs=pl.BlockSpec(memory_space=pl.ANY),
        out_shape=jax.ShapeDtypeStruct(xs_zeros.shape, xs_zeros.dtype),
        scratch_shapes=[pltpu.SemaphoreType.DMA(())],
        input_output_aliases={2: 0},
        compiler_params=_cparams("arbitrary"),
        name="dispatch",
    )(dest3, h3, xs_zeros)


def _experts_kernel(be_ref, nu_ref, xs_ref, wg_ref, wu_ref, wd_ref, ys_ref, wgb, wub, wdb):
    j = pl.program_id(0)
    used = j < nu_ref[0]
    new_expert = (j == 0) | (be_ref[j] != be_ref[jnp.maximum(j - 1, 0)])

    @pl.when(used & new_expert)
    def _():
        wgb[...] = wg_ref[0].astype(BF16)
        wub[...] = wu_ref[0].astype(BF16)
        wdb[...] = wd_ref[0].astype(BF16)

    @pl.when(used)
    def _():
        xb = _rows_2d(xs_ref).astype(BF16)
        gate = jnp.dot(xb, wgb[...], preferred_element_type=F32)
        up = jnp.dot(xb, wub[...], preferred_element_type=F32)
        hid = (gate * jax.nn.sigmoid(gate) * up).astype(BF16)
        y = jnp.dot(hid, wdb[...], preferred_element_type=F32)
        for cblk in range(ROW_TILE):
            ys_ref[:, cblk, :] = y[:, cblk * 128:(cblk + 1) * 128]

    @pl.when(pl.program_id(0) >= nu_ref[0])
    def _():
        ys_ref[...] = jnp.zeros_like(ys_ref)


def _experts(blk_expert, n_used, xs, wg, wu, wd):
    rows = xs.shape[0]
    d = wg.shape[1]
    nb = rows // MOE_BLK
    row_blk = lambda j, be, nu: (jnp.minimum(j, nu[0] - 1), 0, 0)
    out_blk = lambda j, be, nu: (j, 0, 0)
    w_blk = lambda j, be, nu: (be[jnp.minimum(j, nu[0] - 1)], 0, 0)
    return pl.pallas_call(
        _experts_kernel,
        grid_spec=pltpu.PrefetchScalarGridSpec(
            num_scalar_prefetch=2,
            grid=(nb,),
            in_specs=[pl.BlockSpec((MOE_BLK, ROW_TILE, 128), row_blk),
                      pl.BlockSpec((1, d, D_EXPERT), w_blk),
                      pl.BlockSpec((1, d, D_EXPERT), w_blk),
                      pl.BlockSpec((1, D_EXPERT, d), w_blk)],
            out_specs=pl.BlockSpec((MOE_BLK, ROW_TILE, 128), out_blk),
            scratch_shapes=[pltpu.VMEM((d, D_EXPERT), BF16), pltpu.VMEM((d, D_EXPERT), BF16),
                            pltpu.VMEM((D_EXPERT, d), BF16)]),
        out_shape=jax.ShapeDtypeStruct(xs.shape, F32),
        compiler_params=_cparams("arbitrary"),
        name="experts",
    )(blk_expert, n_used, xs, wg, wu, wd)


def _combine_kernel(dest_ref, ys_ref, x1_ref, gate_ref, mod_ref, g_ref, o_ref, ya, yb, sem):
    def issue(r, c):
        _row_copy(ys_ref, dest_ref[0, 0, 2 * r], ya, r, sem).start()
        _row_copy(ys_ref, dest_ref[0, 0, 2 * r + 1], yb, r, sem).start()
        return c

    lax.fori_loop(0, TD, issue, 0, unroll=DMA_UNROLL)

    def drain(r, c):
        _row_copy(ys_ref, 0, ya, 0, sem).wait()
        _row_copy(ys_ref, 0, yb, 0, sem).wait()
        return c

    lax.fori_loop(0, TD, drain, 0, unroll=DMA_UNROLL)
    gt = gate_ref[...]
    moe = gt[:, 0:1] * _rows_2d(ya) + gt[:, 1:2] * _rows_2d(yb)
    x2 = x1_ref[0] + mod_ref[0][5:6] * moe
    ms = jnp.mean(x2 * x2, axis=-1, keepdims=True)
    o_ref[0] = x2 * lax.rsqrt(ms + EPS) * g_ref[...]


def _combine(dest3, ys, x1, gates, mods, g_final):
    b, s, d = x1.shape
    nt = s // TD
    return pl.pallas_call(
        _combine_kernel,
        grid=(b, nt),
        in_specs=[pl.BlockSpec((1, 1, 2 * TD), lambda bi, i: (bi * nt + i, 0, 0),
                               memory_space=pltpu.SMEM),
                  pl.BlockSpec(memory_space=pl.ANY),
                  pl.BlockSpec((1, TD, d), lambda bi, i: (bi, i, 0)),
                  pl.BlockSpec((TD, 128), lambda bi, i: (bi * nt + i, 0)),
                  pl.BlockSpec((1, N_MOD, d), lambda bi, i: (bi, 0, 0)),
                  pl.BlockSpec((1, d), lambda bi, i: (0, 0))],
        out_specs=pl.BlockSpec((1, TD, d), lambda bi, i: (bi, i, 0)),
        out_shape=jax.ShapeDtypeStruct((b, s, d), F32),
        scratch_shapes=[pltpu.VMEM((TD, ROW_TILE, 128), F32), pltpu.VMEM((TD, ROW_TILE, 128), F32),
                        pltpu.SemaphoreType.DMA(())],
        compiler_params=_cparams("arbitrary", "arbitrary"),
        name="combine",
    )(dest3, ys, x1, gates, mods, g_final)


def _rope_tables(rows):
    r, col = jnp.meshgrid(jnp.arange(rows), jnp.arange(GRID_W), indexing='ij')
    pos = jnp.stack([r.reshape(-1), col.reshape(-1)], axis=-1).astype(F32)
    inv_freq = ROPE_THETA ** (-jnp.arange(0, ROPE_AXIS, 2, dtype=F32) / ROPE_AXIS)
    ang = pos[:, :, None] * inv_freq
    ang = jnp.concatenate([ang, ang], axis=-1)
    n = ang.shape[0]
    cos = jnp.tile(jnp.cos(ang).reshape(n, HEAD_DIM), (1, 2))
    sin = jnp.tile(jnp.sin(ang).reshape(n, HEAD_DIM), (1, 2))
    upper = (jnp.arange(HEAD_W) % ROPE_AXIS) >= ROPE_HALF
    sa = jnp.where(upper, sin, 0.0)
    sb = jnp.where(upper, 0.0, -sin)
    return cos, sa, sb


def _dft_constants(n_pos):
    c = np.arange(FGROUP_DIM)
    ang_c = 2.0 * np.pi * ((c[:, None] * c[None, :]) % FGROUP_DIM) / FGROUP_DIM
    norm = 1.0 / math.sqrt(n_pos * FGROUP_DIM)
    cmat = (np.cos(ang_c) * norm).astype(np.float32)
    smat = (np.sin(ang_c) * norm).astype(np.float32)
    k1 = np.arange(DFT_N1)[None, :, None]
    n1 = np.arange(DFT_N1)[None, None, :]
    n2 = np.arange(DFT_N2)[:, None, None]
    ang_g = 2.0 * np.pi * ((k1 * (DFT_N2 * n1 + n2)) % n_pos) / n_pos
    gmat = np.concatenate([np.cos(ang_g), np.sin(ang_g)], axis=1).astype(np.float32)
    k2 = np.arange(DFT_N2)
    ang_2 = 2.0 * np.pi * ((k2[:, None] * k2[None, :]) % DFT_N2) / DFT_N2
    c2 = np.cos(ang_2).astype(np.float32)
    s2 = np.sin(ang_2).astype(np.float32)
    return cmat, smat, gmat, c2, s2


def kernel(x, c, ctx, c_ctx, w_ada, b_ada, g_mix_norm, g_ffn_norm, w_in, lambda_q1, lambda_k1, lambda_q2, lambda_k2, g_subln, w_fourier, w_out, w_router_group, b_router_group, w_router_expert, b_router_expert, w_gate, w_up, w_down, g_final):
    b, s, d = x.shape
    t = b * s
    assert d == D_MODEL and s == DFT_N1 * DFT_N2 and s % GRID_W == 0 and b == 2

    cc = jnp.concatenate([c, c_ctx[None, :], jnp.zeros((8 - b - 1, d), F32)], axis=0)
    mods = _adaln(cc, w_ada[0], b_ada[0]).reshape(8, N_MOD, d)

    cmat, smat, gmat, c2, s2 = _dft_constants(s)
    wf = _wfold(jnp.asarray(cmat), jnp.asarray(smat), w_fourier[0])
    cos_t, sa_t, sb_t = _rope_tables(s // GRID_W)

    w_in_b = w_in[0].astype(BF16)
    g_mix = g_mix_norm[0].reshape(1, d)
    q, kl, vl, y = _inproj(x, mods, g_mix, w_in_b, wf, cos_t, sa_t, sb_t)
    kc, vc = _ctxproj(ctx, mods, g_mix, w_in_b)

    lam_p = jnp.stack([lambda_q1[0], lambda_k1[0], lambda_q2[0], lambda_k2[0]], axis=0)
    attn_o = _attention(q, kc, vc, kl, vl, lam_p, g_subln[0].reshape(1, HEAD_W))

    br, bi = _dft1(y.reshape(b, DFT_N1, DFT_N2 * 2 * F_W), jnp.asarray(gmat).astype(BF16))
    four = _dft2(br.reshape(b, DFT_N1, DFT_N2, F_W), bi.reshape(b, DFT_N1, DFT_N2, F_W),
                 jnp.asarray(c2).astype(BF16), jnp.asarray(s2).astype(BF16)).reshape(b, s, F_W)

    n_r = N_GROUPS + N_EXPERTS
    w_r = jnp.concatenate([w_router_group[0], w_router_expert[0],
                           jnp.zeros((d, 128 - n_r), F32)], axis=1)
    b_r = jnp.concatenate([b_router_group[0], b_router_expert[0],
                           jnp.zeros((128 - n_r,), F32)]).reshape(1, 128)
    w_r_hi = w_r.astype(BF16)
    w_r_lo = (w_r - w_r_hi.astype(F32)).astype(BF16)
    x1, h3, meta, gates, cnt = _outproj(attn_o, four, x, mods, g_ffn_norm[0].reshape(1, d),
                                        w_out[0].astype(BF16), jnp.stack([w_r_hi, w_r_lo]), b_r)

    e_flat = meta[:, 0:TOP_K].reshape(t * TOP_K)
    rank = meta[:, TOP_K:2 * TOP_K].reshape(t * TOP_K)
    counts = cnt[0, N_GROUPS:N_GROUPS + N_EXPERTS].astype(jnp.int32)
    nblk = (counts + MOE_BLK - 1) // MOE_BLK
    blk_end = jnp.cumsum(nblk)
    blk_start = blk_end - nblk
    dest = (jnp.take(blk_start * MOE_BLK, e_flat) + rank).astype(jnp.int32)
    n_blocks = t * TOP_K // MOE_BLK + N_EXPERTS
    blk_ids = jnp.arange(n_blocks, dtype=jnp.int32)
    blk_expert = jnp.minimum(
        jnp.sum((blk_end[None, :] <= blk_ids[:, None]).astype(jnp.int32), axis=1),
        N_EXPERTS - 1).astype(jnp.int32)
    n_used = blk_end[-1:].astype(jnp.int32)
    dest3 = dest.reshape(t // TD, 1, 2 * TD)

    xs = _dispatch(dest3, h3, jnp.zeros((n_blocks * MOE_BLK, ROW_TILE, 128), F32))
    ys = _experts(blk_expert, n_used, xs, w_gate[0], w_up[0], w_down[0])
    return _combine(dest3, ys, x1, gates, mods, g_final.reshape(1, d))
```

```python
import functools
import math

import numpy as np
import jax
import jax.numpy as jnp
from jax import lax
from jax.experimental import pallas as pl
from jax.experimental.pallas import tpu as pltpu

F32 = jnp.float32
BF16 = jnp.bfloat16

D_MODEL = 1024
GRID_W = 64
N_HEADS = 4
HEAD_DIM = 64
HEAD_W = 2 * HEAD_DIM
QK_W = N_HEADS * HEAD_W
N_FGROUPS = 4
FGROUP_DIM = 128
F_W = N_FGROUPS * FGROUP_DIM
ROPE_THETA = 10000.0
ROPE_AXIS = HEAD_DIM // 2
ROPE_HALF = ROPE_AXIS // 2
N_GROUPS = 4
EXPERTS_PER_GROUP = 8
N_EXPERTS = N_GROUPS * EXPERTS_PER_GROUP
TOP_K = 2
D_EXPERT = 512
N_MOD = 6
EPS = 1e-6
LAMBDA_INIT = 0.8 - 0.6 * math.exp(-0.3 * 0)
LOG2_E = 1.4426950408889634

DFT_N1 = 64
DFT_N2 = 128

TM_PROJ = 512
TQ = 1024
TK = 1024
F1_NB = 8
F2_KB = 8
MOE_BLK = 256
ROW_TILE = D_MODEL // 128
TD = 256
DMA_UNROLL = 16
VMEM_LIMIT = 48 * 1024 * 1024


def _cparams(*sem):
    return pltpu.CompilerParams(dimension_semantics=sem, vmem_limit_bytes=VMEM_LIMIT)


def _adaln_kernel(c_ref, w_ref, b_ref, o_ref):
    cc = c_ref[...]
    s = cc * jax.nn.sigmoid(cc)
    o_ref[...] = jnp.dot(s, w_ref[...], preferred_element_type=F32,
                         precision=lax.Precision.HIGHEST) + b_ref[...]


def _adaln(cc, w_ada, b_ada):
    n = w_ada.shape[1]
    tn = 1536
    return pl.pallas_call(
        _adaln_kernel,
        grid=(n // tn,),
        in_specs=[pl.BlockSpec((8, D_MODEL), lambda j: (0, 0)),
                  pl.BlockSpec((D_MODEL, tn), lambda j: (0, j)),
                  pl.BlockSpec((1, tn), lambda j: (0, j))],
        out_specs=pl.BlockSpec((8, tn), lambda j: (0, j)),
        out_shape=jax.ShapeDtypeStruct((8, n), F32),
        compiler_params=_cparams("arbitrary"),
        name="adaln",
    )(cc, w_ada, b_ada.reshape(1, n))


def _wfold_kernel(c_ref, s_ref, w_ref, o_ref):
    w = w_ref[0]
    o_ref[0, :, :FGROUP_DIM] = jnp.dot(c_ref[...], w, preferred_element_type=F32,
                                       precision=lax.Precision.HIGHEST).astype(BF16)
    o_ref[0, :, FGROUP_DIM:] = jnp.dot(s_ref[...], w, preferred_element_type=F32,
                                       precision=lax.Precision.HIGHEST).astype(BF16)


def _wfold(cmat, smat, w_fourier):
    return pl.pallas_call(
        _wfold_kernel,
        grid=(N_FGROUPS,),
        in_specs=[pl.BlockSpec((FGROUP_DIM, FGROUP_DIM), lambda g: (0, 0)),
                  pl.BlockSpec((FGROUP_DIM, FGROUP_DIM), lambda g: (0, 0)),
                  pl.BlockSpec((1, FGROUP_DIM, FGROUP_DIM), lambda g: (g, 0, 0))],
        out_specs=pl.BlockSpec((1, FGROUP_DIM, 2 * FGROUP_DIM), lambda g: (g, 0, 0)),
        out_shape=jax.ShapeDtypeStruct((N_FGROUPS, FGROUP_DIM, 2 * FGROUP_DIM), BF16),
        compiler_params=_cparams("arbitrary"),
        name="wfold",
    )(cmat, smat, w_fourier)


def _norm_mod(x, g, shift, scale):
    ms = jnp.mean(x * x, axis=-1, keepdims=True)
    y = x * lax.rsqrt(ms + EPS) * g
    return y * (1.0 + scale) + shift


def _rope_slab(p, cos, sa, sb):
    return (p * cos + pltpu.roll(p, ROPE_HALF, 1) * sa
            + pltpu.roll(p, HEAD_W - ROPE_HALF, 1) * sb)


def _inproj_kernel(x_ref, mod_ref, g_ref, w_ref, wf_ref, cos_ref, sa_ref, sb_ref,
                   q_ref, k_ref, v_ref, y_ref):
    m = mod_ref[0]
    h = _norm_mod(x_ref[0], g_ref[...], m[0:1], m[1:2]).astype(BF16)
    cos, sa, sb = cos_ref[...], sa_ref[...], sb_ref[...]
    scale = HEAD_DIM ** -0.5 * LOG2_E
    pq = jnp.dot(h, w_ref[:, 0:QK_W], preferred_element_type=F32)
    for hh in range(N_HEADS):
        sl = slice(hh * HEAD_W, (hh + 1) * HEAD_W)
        q_ref[0, :, sl] = (_rope_slab(pq[:, sl], cos, sa, sb) * scale).astype(BF16)
    pk = jnp.dot(h, w_ref[:, QK_W:2 * QK_W], preferred_element_type=F32)
    for hh in range(N_HEADS):
        sl = slice(hh * HEAD_W, (hh + 1) * HEAD_W)
        k_ref[0, :, sl] = _rope_slab(pk[:, sl], cos, sa, sb).astype(BF16)
    v_ref[0] = jnp.dot(h, w_ref[:, 2 * QK_W:3 * QK_W], preferred_element_type=F32).astype(BF16)
    pf = jnp.dot(h, w_ref[:, 3 * QK_W:], preferred_element_type=F32).astype(BF16)
    for g in range(N_FGROUPS):
        yy = jnp.dot(pf[:, g * FGROUP_DIM:(g + 1) * FGROUP_DIM], wf_ref[g],
                     preferred_element_type=F32)
        y_ref[0, :, g * FGROUP_DIM:(g + 1) * FGROUP_DIM] = yy[:, :FGROUP_DIM]
        y_ref[0, :, F_W + g * FGROUP_DIM:F_W + (g + 1) * FGROUP_DIM] = yy[:, FGROUP_DIM:]


def _inproj(x, mods, g_mix, w_in, wf, cos_t, sa_t, sb_t):
    b, s, d = x.shape
    tm = TM_PROJ
    tok = lambda bi, i: (bi, i, 0)
    return pl.pallas_call(
        _inproj_kernel,
        grid=(b, s // tm),
        in_specs=[pl.BlockSpec((1, tm, d), tok),
                  pl.BlockSpec((1, N_MOD, d), lambda bi, i: (bi, 0, 0)),
                  pl.BlockSpec((1, d), lambda bi, i: (0, 0)),
                  pl.BlockSpec(w_in.shape, lambda bi, i: (0, 0)),
                  pl.BlockSpec(wf.shape, lambda bi, i: (0, 0, 0)),
                  pl.BlockSpec((tm, HEAD_W), lambda bi, i: (i, 0)),
                  pl.BlockSpec((tm, HEAD_W), lambda bi, i: (i, 0)),
                  pl.BlockSpec((tm, HEAD_W), lambda bi, i: (i, 0))],
        out_specs=[pl.BlockSpec((1, tm, QK_W), tok),
                   pl.BlockSpec((1, tm, QK_W), tok),
                   pl.BlockSpec((1, tm, QK_W), tok),
                   pl.BlockSpec((1, tm, 2 * F_W), tok)],
        out_shape=[jax.ShapeDtypeStruct((b, s, QK_W), BF16),
                   jax.ShapeDtypeStruct((b, s, QK_W), BF16),
                   jax.ShapeDtypeStruct((b, s, QK_W), BF16),
                   jax.ShapeDtypeStruct((b, s, 2 * F_W), F32)],
        compiler_params=_cparams("parallel", "arbitrary"),
        name="inproj",
    )(x, mods, g_mix, w_in, wf, cos_t, sa_t, sb_t)


def _ctxproj_kernel(x_ref, mod_ref, g_ref, w_ref, k_ref, v_ref):
    m = mod_ref[0]
    h = _norm_mod(x_ref[0], g_ref[...], m[0:1], m[1:2]).astype(BF16)
    k_ref[0] = jnp.dot(h, w_ref[:, QK_W:2 * QK_W], preferred_element_type=F32).astype(BF16)
    v_ref[0] = jnp.dot(h, w_ref[:, 2 * QK_W:3 * QK_W], preferred_element_type=F32).astype(BF16)


def _ctxproj(ctx, mods, g_mix, w_in):
    b, n, d = ctx.shape
    return pl.pallas_call(
        _ctxproj_kernel,
        grid=(b,),
        in_specs=[pl.BlockSpec((1, n, d), lambda bi: (bi, 0, 0)),
                  pl.BlockSpec((1, N_MOD, d), lambda bi: (2, 0, 0)),
                  pl.BlockSpec((1, d), lambda bi: (0, 0)),
                  pl.BlockSpec(w_in.shape, lambda bi: (0, 0))],
        out_specs=[pl.BlockSpec((1, n, QK_W), lambda bi: (bi, 0, 0)),
                   pl.BlockSpec((1, n, QK_W), lambda bi: (bi, 0, 0))],
        out_shape=[jax.ShapeDtypeStruct((b, n, QK_W), BF16),
                   jax.ShapeDtypeStruct((b, n, QK_W), BF16)],
        compiler_params=_cparams("arbitrary"),
        name="ctxproj",
    )(ctx, mods, g_mix, w_in)


def _attn_kernel(q_ref, kc_ref, vc_ref, kl_ref, vl_ref, lam_ref, g_ref, o_ref,
                 m_ref, l_ref, acc_ref):
    q = q_ref[0]
    lane = lax.broadcasted_iota(jnp.int32, q.shape, 1)
    zero = jnp.zeros_like(q)
    q0 = jnp.where(lane < HEAD_DIM, q, zero)
    q1 = jnp.where(lane >= HEAD_DIM, q, zero)
    qs = (q0, q1)
    contract_last = (((1,), (1,)), ((), ()))
    m_ref[...] = jnp.full(m_ref.shape, -1e30, F32)
    l_ref[...] = jnp.zeros(l_ref.shape, F32)
    acc_ref[...] = jnp.zeros(acc_ref.shape, F32)

    def step(kb, vb):
        nk = kb.shape[0] // HEAD_W
        for mi in range(2):
            s = lax.dot_general(qs[mi], kb, contract_last, preferred_element_type=F32)
            m_old = m_ref[mi]
            m_new = jnp.maximum(m_old, jnp.max(s, axis=-1, keepdims=True))
            alpha = jnp.exp2(m_old - m_new)
            p = jnp.exp2(s - jnp.concatenate([m_new] * nk, axis=1))
            psum = p[:, 0:HEAD_W]
            for cblk in range(1, nk):
                psum = psum + p[:, cblk * HEAD_W:(cblk + 1) * HEAD_W]
            l_ref[mi] = alpha * l_ref[mi] + psum
            acc_ref[mi] = alpha * acc_ref[mi] + jnp.dot(p.astype(BF16), vb,
                                                       preferred_element_type=F32)
            m_ref[mi] = m_new

    step(kc_ref[0], vc_ref[0])

    def body(i, c):
        off = pl.multiple_of(i * TK, TK)
        step(kl_ref[0, pl.ds(off, TK), :], vl_ref[0, pl.ds(off, TK), :])
        return c

    lax.fori_loop(0, kl_ref.shape[1] // TK, body, 0)

    lp = lam_ref[...]
    t1 = jnp.sum(lp[0:1] * lp[1:2], axis=-1, keepdims=True)
    t2 = jnp.sum(lp[2:3] * lp[3:4], axis=-1, keepdims=True)
    lam = jnp.exp(t1) - jnp.exp(t2) + LAMBDA_INIT
    l0 = jnp.sum(l_ref[0], axis=-1, keepdims=True)
    l1 = jnp.sum(l_ref[1], axis=-1, keepdims=True)
    o = acc_ref[0] / l0 - lam * (acc_ref[1] / l1)
    ms = jnp.mean(o * o, axis=-1, keepdims=True)
    o = o * lax.rsqrt(ms + EPS) * g_ref[...] * (1.0 - LAMBDA_INIT)
    o_ref[0] = o.astype(BF16)


def _attention(q, kc, vc, kl, vl, lam_p, g_subln):
    b, s, _ = q.shape
    n_ctx = kc.shape[1]
    return pl.pallas_call(
        _attn_kernel,
        grid=(b, N_HEADS, s // TQ),
        in_specs=[pl.BlockSpec((1, TQ, HEAD_W), lambda bi, h, i: (bi, i, h)),
                  pl.BlockSpec((1, n_ctx, HEAD_W), lambda bi, h, i: (bi, 0, h)),
                  pl.BlockSpec((1, n_ctx, HEAD_W), lambda bi, h, i: (bi, 0, h)),
                  pl.BlockSpec((1, s, HEAD_W), lambda bi, h, i: (bi, 0, h)),
                  pl.BlockSpec((1, s, HEAD_W), lambda bi, h, i: (bi, 0, h)),
                  pl.BlockSpec((4, HEAD_DIM), lambda bi, h, i: (0, 0)),
                  pl.BlockSpec((1, HEAD_W), lambda bi, h, i: (0, 0))],
        out_specs=pl.BlockSpec((1, TQ, HEAD_W), lambda bi, h, i: (bi, i, h)),
        out_shape=jax.ShapeDtypeStruct((b, s, QK_W), BF16),
        scratch_shapes=[pltpu.VMEM((2, TQ, HEAD_W), F32)] * 3,
        compiler_params=_cparams("parallel", "parallel", "arbitrary"),
        name="diffattn",
    )(q, kc, vc, kl, vl, lam_p, g_subln)


def _dft1_kernel(y_ref, g_ref, br_ref, bi_ref):
    rows = DFT_N1 * F1_NB
    yb = y_ref[0].reshape(rows, 2 * F_W).astype(BF16)
    p = jnp.dot(g_ref[0], yb, preferred_element_type=F32)
    top, bot = p[:rows], p[rows:]
    br_ref[0] = (top[:, :F_W] - bot[:, F_W:]).reshape(DFT_N1, F1_NB, F_W)
    bi_ref[0] = (-top[:, F_W:] - bot[:, :F_W]).reshape(DFT_N1, F1_NB, F_W)


def _dft1(y4, gbig):
    b = y4.shape[0]
    rows = DFT_N1 * F1_NB
    gmat = gbig
    return pl.pallas_call(
        _dft1_kernel,
        grid=(b, DFT_N2 // F1_NB),
        in_specs=[pl.BlockSpec((1, DFT_N1, F1_NB, 2 * F_W), lambda bi, j: (bi, 0, j, 0)),
                  pl.BlockSpec((1, 2 * rows, rows), lambda bi, j: (j, 0, 0))],
        out_specs=[pl.BlockSpec((1, DFT_N1, F1_NB, F_W), lambda bi, j: (bi, 0, j, 0)),
                   pl.BlockSpec((1, DFT_N1, F1_NB, F_W), lambda bi, j: (bi, 0, j, 0))],
        out_shape=[jax.ShapeDtypeStruct((b, DFT_N1, DFT_N2, F_W), F32),
                   jax.ShapeDtypeStruct((b, DFT_N1, DFT_N2, F_W), F32)],
        compiler_params=_cparams("parallel", "arbitrary"),
        name="dft1",
    )(y4, gmat)


def _dft2_kernel(br_ref, bi_ref, c_ref, s_ref, o_ref):
    for j in range(F2_KB):
        r = (jnp.dot(c_ref[...], br_ref[0, j].astype(BF16), preferred_element_type=F32)
             + jnp.dot(s_ref[...], bi_ref[0, j].astype(BF16), preferred_element_type=F32))
        o_ref[0, :, j, :] = r


def _dft2(br4, bi4, c2, s2):
    b = br4.shape[0]
    blk = (1, F2_KB, DFT_N2, F_W)
    return pl.pallas_call(
        _dft2_kernel,
        grid=(b, DFT_N1 // F2_KB),
        in_specs=[pl.BlockSpec(blk, lambda bi, j: (bi, j, 0, 0)),
                  pl.BlockSpec(blk, lambda bi, j: (bi, j, 0, 0)),
                  pl.BlockSpec((DFT_N2, DFT_N2), lambda bi, j: (0, 0)),
                  pl.BlockSpec((DFT_N2, DFT_N2), lambda bi, j: (0, 0))],
        out_specs=pl.BlockSpec((1, DFT_N2, F2_KB, F_W), lambda bi, j: (bi, 0, j, 0)),
        out_shape=jax.ShapeDtypeStruct((b, DFT_N2, DFT_N1, F_W), F32),
        compiler_params=_cparams("parallel", "arbitrary"),
        name="dft2",
    )(br4, bi4, c2, s2)


def _outproj_kernel(o_ref, f_ref, x_ref, mod_ref, g_ref, wo_ref, wr_ref, br_ref, tri_ref,
                    x1_ref, h3_ref, meta_ref, gate_ref, cnt_out_ref, cnt_ref):
    first = (pl.program_id(0) == 0) & (pl.program_id(1) == 0)

    @pl.when(first)
    def _():
        cnt_ref[...] = jnp.zeros(cnt_ref.shape, F32)

    m = mod_ref[0]
    mix = (jnp.dot(o_ref[0], wo_ref[0:QK_W, :], preferred_element_type=F32)
           + jnp.dot(f_ref[0].astype(BF16), wo_ref[QK_W:, :], preferred_element_type=F32))
    x1 = x_ref[0] + m[2:3] * mix
    x1_ref[0] = x1
    h2 = _norm_mod(x1, g_ref[...], m[3:4], m[4:5])
    for cblk in range(ROW_TILE):
        h3_ref[:, cblk, :] = h2[:, cblk * 128:(cblk + 1) * 128]
    h_hi = h2.astype(BF16)
    h_lo = (h2 - h_hi.astype(F32)).astype(BF16)
    lg = (jnp.dot(h_hi, wr_ref[0], preferred_element_type=F32)
          + jnp.dot(h_lo, wr_ref[0], preferred_element_type=F32)
          + jnp.dot(h_hi, wr_ref[1], preferred_element_type=F32)
          + br_ref[...])
    lane = lax.broadcasted_iota(jnp.int32, lg.shape, 1)
    ninf = jnp.float32(-jnp.inf)
    big = jnp.int32(lg.shape[1])
    gl = jnp.where(lane < N_GROUPS, lg, ninf)
    gmax = jnp.max(gl, axis=-1, keepdims=True)
    grp = jnp.min(jnp.where(gl == gmax, lane, big), axis=-1, keepdims=True)
    pg = 1.0 / jnp.sum(jnp.exp(gl - gmax), axis=-1, keepdims=True)
    e_lane = lane - N_GROUPS
    emask = (e_lane >= 0) & (e_lane < N_EXPERTS) & ((e_lane >> 3) == grp)
    el = jnp.where(emask, lg, ninf)
    t1 = jnp.max(el, axis=-1, keepdims=True)
    i1 = jnp.min(jnp.where(el == t1, lane, big), axis=-1, keepdims=True)
    el2 = jnp.where(lane == i1, ninf, el)
    t2 = jnp.max(el2, axis=-1, keepdims=True)
    i2 = jnp.min(jnp.where(el2 == t2, lane, big), axis=-1, keepdims=True)
    dd = jnp.exp(t2 - t1)
    w1 = pg / (1.0 + dd)
    w2 = pg * dd / (1.0 + dd)
    gate_ref[...] = jnp.where(lane == 0, w1, jnp.where(lane == 1, w2, 0.0))
    hit1 = lane == i1
    hit2 = lane == i2
    oh = jnp.where(hit1 | hit2, 1.0, 0.0)
    before = jnp.dot(tri_ref[...], oh.astype(BF16), preferred_element_type=F32) + cnt_ref[0:1, :]
    r1 = jnp.sum(jnp.where(hit1, before, 0.0), axis=-1, keepdims=True).astype(jnp.int32)
    r2 = jnp.sum(jnp.where(hit2, before, 0.0), axis=-1, keepdims=True).astype(jnp.int32)
    cnt_ref[0:1, :] = cnt_ref[0:1, :] + jnp.sum(oh, axis=0, keepdims=True)
    cnt_out_ref[...] = cnt_ref[...]
    meta_ref[...] = jnp.where(lane == 0, i1 - N_GROUPS,
                              jnp.where(lane == 1, i2 - N_GROUPS,
                                        jnp.where(lane == 2, r1, jnp.where(lane == 3, r2, 0))))


def _outproj(attn_o, four, x, mods, g_ffn, w_out, w_r, b_r):
    b, s, d = x.shape
    tm = TM_PROJ
    nt = s // tm
    tok = lambda bi, i: (bi, i, 0)
    flat = lambda bi, i: (bi * nt + i, 0)
    tri = jnp.asarray(np.tril(np.ones((tm, tm), np.float32), -1)).astype(BF16)
    return pl.pallas_call(
        _outproj_kernel,
        grid=(b, nt),
        in_specs=[pl.BlockSpec((1, tm, QK_W), tok),
                  pl.BlockSpec((1, tm, F_W), tok),
                  pl.BlockSpec((1, tm, d), tok),
                  pl.BlockSpec((1, N_MOD, d), lambda bi, i: (bi, 0, 0)),
                  pl.BlockSpec((1, d), lambda bi, i: (0, 0)),
                  pl.BlockSpec(w_out.shape, lambda bi, i: (0, 0)),
                  pl.BlockSpec(w_r.shape, lambda bi, i: (0, 0, 0)),
                  pl.BlockSpec(b_r.shape, lambda bi, i: (0, 0)),
                  pl.BlockSpec((tm, tm), lambda bi, i: (0, 0))],
        out_specs=[pl.BlockSpec((1, tm, d), tok),
                   pl.BlockSpec((tm, ROW_TILE, 128), lambda bi, i: (bi * nt + i, 0, 0)),
                   pl.BlockSpec((tm, 128), flat),
                   pl.BlockSpec((tm, 128), flat),
                   pl.BlockSpec((8, 128), lambda bi, i: (0, 0))],
        out_shape=[jax.ShapeDtypeStruct((b, s, d), F32),
                   jax.ShapeDtypeStruct((b * s, ROW_TILE, 128), F32),
                   jax.ShapeDtypeStruct((b * s, 128), jnp.int32),
                   jax.ShapeDtypeStruct((b * s, 128), F32),
                   jax.ShapeDtypeStruct((8, 128), F32)],
        scratch_shapes=[pltpu.VMEM((8, 128), F32)],
        compiler_params=_cparams("arbitrary", "arbitrary"),
        name="outproj",
    )(attn_o, four, x, mods, g_ffn, w_out, w_r, b_r, tri)


def _row_copy(src_ref, src_row, dst_ref, dst_row, sem):
    return pltpu.make_async_copy(src_ref.at[src_row], dst_ref.at[dst_row], sem)


def _rows_2d(ref3):
    return jnp.concatenate([ref3[:, cblk, :] for cblk in range(ROW_TILE)], axis=1)


def _dispatch_kernel(dest_ref, h_ref, zeros_ref, xs_ref, sem):
    del zeros_ref
    n = 2 * TD

    def issue(a, c):
        _row_copy(h_ref, a >> 1, xs_ref, dest_ref[0, 0, a], sem).start()
        return c

    lax.fori_loop(0, n, issue, 0, unroll=DMA_UNROLL)

    def drain(a, c):
        _row_copy(h_ref, 0, xs_ref, 0, sem).wait()
        return c

    lax.fori_loop(0, n, drain, 0, unroll=DMA_UNROLL)


def _dispatch(dest3, h3, xs_zeros):
    t = h3.shape[0]
    return pl.pallas_call(
        _dispatch_kernel,
        grid=(t // TD,),
        in_specs=[pl.BlockSpec((1, 1, 2 * TD), lambda i: (i, 0, 0), memory_space=pltpu.SMEM),
                  pl.BlockSpec((TD, ROW_TILE, 128), lambda i: (i, 0, 0)),
                  pl.BlockSpec(memory_space=pl.ANY)],
        out_specs=pl.BlockSpec(memory_space=pl.ANY),
        out_shape=jax.ShapeDtypeStruct(xs_zeros.shape, xs_zeros.dtype),
        scratch_shapes=[pltpu.SemaphoreType.DMA(())],
        input_output_aliases={2: 0},
        compiler_params=_cparams("arbitrary"),
        name="dispatch",
    )(dest3, h3, xs_zeros)


def _experts_kernel(be_ref, nu_ref, xs_ref, wg_ref, wu_ref, wd_ref, ys_ref, wgb, wub, wdb):
    j = pl.program_id(0)
    used = j < nu_ref[0]
    new_expert = (j == 0) | (be_ref[j] != be_ref[jnp.maximum(j - 1, 0)])

    @pl.when(used & new_expert)
    def _():
        wgb[...] = wg_ref[0].astype(BF16)
        wub[...] = wu_ref[0].astype(BF16)
        wdb[...] = wd_ref[0].astype(BF16)

    @pl.when(used)
    def _():
        xb = _rows_2d(xs_ref).astype(BF16)
        gate = jnp.dot(xb, wgb[...], preferred_element_type=F32)
        up = jnp.dot(xb, wub[...], preferred_element_type=F32)
        hid = (gate * jax.nn.sigmoid(gate) * up).astype(BF16)
        y = jnp.dot(hid, wdb[...], preferred_element_type=F32)
        for cblk in range(ROW_TILE):
            ys_ref[:, cblk, :] = y[:, cblk * 128:(cblk + 1) * 128]

    @pl.when(pl.program_id(0) >= nu_ref[0])
    def _():
        ys_ref[...] = jnp.zeros_like(ys_ref)


def _experts(blk_expert, n_used, xs, wg, wu, wd):
    rows = xs.shape[0]
    d = wg.shape[1]
    nb = rows // MOE_BLK
    row_blk = lambda j, be, nu: (jnp.minimum(j, nu[0] - 1), 0, 0)
    out_blk = lambda j, be, nu: (j, 0, 0)
    w_blk = lambda j, be, nu: (be[jnp.minimum(j, nu[0] - 1)], 0, 0)
    return pl.pallas_call(
        _experts_kernel,
        grid_spec=pltpu.PrefetchScalarGridSpec(
            num_scalar_prefetch=2,
            grid=(nb,),
            in_specs=[pl.BlockSpec((MOE_BLK, ROW_TILE, 128), row_blk),
                      pl.BlockSpec((1, d, D_EXPERT), w_blk),
                      pl.BlockSpec((1, d, D_EXPERT), w_blk),
                      pl.BlockSpec((1, D_EXPERT, d), w_blk)],
            out_specs=pl.BlockSpec((MOE_BLK, ROW_TILE, 128), out_blk),
            scratch_shapes=[pltpu.VMEM((d, D_EXPERT), BF16), pltpu.VMEM((d, D_EXPERT), BF16),
                            pltpu.VMEM((D_EXPERT, d), BF16)]),
        out_shape=jax.ShapeDtypeStruct(xs.shape, F32),
        compiler_params=_cparams("arbitrary"),
        name="experts",
    )(blk_expert, n_used, xs, wg, wu, wd)


def _combine_kernel(dest_ref, dest_next_ref, ys_ref, x1_ref, gate_ref, mod_ref, g_ref, o_ref,
                    ya, yb, sem):
    step = pl.program_id(0) * pl.num_programs(1) + pl.program_id(1)
    n_steps = pl.num_programs(0) * pl.num_programs(1)
    slot = step % 2

    def start_gathers(idx_ref, to_slot):
        def issue(r, c):
            _row_copy(ys_ref, idx_ref[0, 0, 2 * r], ya.at[to_slot], r, sem.at[to_slot]).start()
            _row_copy(ys_ref, idx_ref[0, 0, 2 * r + 1], yb.at[to_slot], r, sem.at[to_slot]).start()
            return c

        lax.fori_loop(0, TD, issue, 0, unroll=DMA_UNROLL)

    @pl.when(step == 0)
    def _():
        start_gathers(dest_ref, 0)

    @pl.when(step + 1 < n_steps)
    def _():
        start_gathers(dest_next_ref, 1 - slot)

    def drain(r, c):
        _row_copy(ys_ref, 0, ya.at[slot], 0, sem.at[slot]).wait()
        _row_copy(ys_ref, 0, yb.at[slot], 0, sem.at[slot]).wait()
        return c

    lax.fori_loop(0, TD, drain, 0, unroll=DMA_UNROLL)
    gt = gate_ref[...]
    moe = gt[:, 0:1] * _rows_2d(ya.at[slot]) + gt[:, 1:2] * _rows_2d(yb.at[slot])
    x2 = x1_ref[0] + mod_ref[0][5:6] * moe
    ms = jnp.mean(x2 * x2, axis=-1, keepdims=True)
    o_ref[0] = x2 * lax.rsqrt(ms + EPS) * g_ref[...]


def _combine(dest3, ys, x1, gates, mods, g_final):
    b, s, d = x1.shape
    nt = s // TD
    return pl.pallas_call(
        _combine_kernel,
        grid=(b, nt),
        in_specs=[pl.BlockSpec((1, 1, 2 * TD), lambda bi, i: (bi * nt + i, 0, 0),
                               memory_space=pltpu.SMEM),
                  pl.BlockSpec((1, 1, 2 * TD),
                               lambda bi, i: (jnp.minimum(bi * nt + i + 1, b * nt - 1), 0, 0),
                               memory_space=pltpu.SMEM),
                  pl.BlockSpec(memory_space=pl.ANY),
                  pl.BlockSpec((1, TD, d), lambda bi, i: (bi, i, 0)),
                  pl.BlockSpec((TD, 128), lambda bi, i: (bi * nt + i, 0)),
                  pl.BlockSpec((1, N_MOD, d), lambda bi, i: (bi, 0, 0)),
                  pl.BlockSpec((1, d), lambda bi, i: (0, 0))],
        out_specs=pl.BlockSpec((1, TD, d), lambda bi, i: (bi, i, 0)),
        out_shape=jax.ShapeDtypeStruct((b, s, d), F32),
        scratch_shapes=[pltpu.VMEM((2, TD, ROW_TILE, 128), F32),
                        pltpu.VMEM((2, TD, ROW_TILE, 128), F32),
                        pltpu.SemaphoreType.DMA((2,))],
        compiler_params=_cparams("arbitrary", "arbitrary"),
        name="combine",
    )(dest3, dest3, ys, x1, gates, mods, g_final)


def _rope_tables(rows):
    r, col = jnp.meshgrid(jnp.arange(rows), jnp.arange(GRID_W), indexing='ij')
    pos = jnp.stack([r.reshape(-1), col.reshape(-1)], axis=-1).astype(F32)
    inv_freq = ROPE_THETA ** (-jnp.arange(0, ROPE_AXIS, 2, dtype=F32) / ROPE_AXIS)
    ang = pos[:, :, None] * inv_freq
    ang = jnp.concatenate([ang, ang], axis=-1)
    n = ang.shape[0]
    cos = jnp.tile(jnp.cos(ang).reshape(n, HEAD_DIM), (1, 2))
    sin = jnp.tile(jnp.sin(ang).reshape(n, HEAD_DIM), (1, 2))
    upper = (jnp.arange(HEAD_W) % ROPE_AXIS) >= ROPE_HALF
    sa = jnp.where(upper, sin, 0.0)
    sb = jnp.where(upper, 0.0, -sin)
    return cos, sa, sb


def _dft_constants(n_pos):
    c = np.arange(FGROUP_DIM)
    ang_c = 2.0 * np.pi * ((c[:, None] * c[None, :]) % FGROUP_DIM) / FGROUP_DIM
    norm = 1.0 / math.sqrt(n_pos * FGROUP_DIM)
    cmat = (np.cos(ang_c) * norm).astype(np.float32)
    smat = (np.sin(ang_c) * norm).astype(np.float32)
    k1 = np.arange(DFT_N1)[None, :, None]
    n1 = np.arange(DFT_N1)[None, None, :]
    n2 = np.arange(DFT_N2)[:, None, None]
    ang_g = 2.0 * np.pi * ((k1 * (DFT_N2 * n1 + n2)) % n_pos) / n_pos
    gmat = np.concatenate([np.cos(ang_g), np.sin(ang_g)], axis=1).astype(np.float32)
    k2 = np.arange(DFT_N2)
    ang_2 = 2.0 * np.pi * ((k2[:, None] * k2[None, :]) % DFT_N2) / DFT_N2
    c2 = np.cos(ang_2).astype(np.float32)
    s2 = np.sin(ang_2).astype(np.float32)
    return cmat, smat, gmat, c2, s2


def kernel(x, c, ctx, c_ctx, w_ada, b_ada, g_mix_norm, g_ffn_norm, w_in, lambda_q1, lambda_k1, lambda_q2, lambda_k2, g_subln, w_fourier, w_out, w_router_group, b_router_group, w_router_expert, b_router_expert, w_gate, w_up, w_down, g_final):
    b, s, d = x.shape
    t = b * s
    assert d == D_MODEL and s == DFT_N1 * DFT_N2 and s % GRID_W == 0 and b == 2

    cc = jnp.concatenate([c, c_ctx[None, :], jnp.zeros((8 - b - 1, d), F32)], axis=0)
    mods = _adaln(cc, w_ada[0], b_ada[0]).reshape(8, N_MOD, d)

    cmat, smat, gmat, c2, s2 = _dft_constants(s)
    wf = _wfold(jnp.asarray(cmat), jnp.asarray(smat), w_fourier[0])
    cos_t, sa_t, sb_t = _rope_tables(s // GRID_W)

    w_in_b = w_in[0].astype(BF16)
    g_mix = g_mix_norm[0].reshape(1, d)
    q, kl, vl, y = _inproj(x, mods, g_mix, w_in_b, wf, cos_t, sa_t, sb_t)
    kc, vc = _ctxproj(ctx, mods, g_mix, w_in_b)

    lam_p = jnp.stack([lambda_q1[0], lambda_k1[0], lambda_q2[0], lambda_k2[0]], axis=0)
    attn_o = _attention(q, kc, vc, kl, vl, lam_p, g_subln[0].reshape(1, HEAD_W))

    gm = jnp.asarray(gmat).reshape(DFT_N2 // F1_NB, F1_NB, 2, DFT_N1, DFT_N1)
    gbig = jnp.einsum('japkn,ab->jpkanb', gm, jnp.eye(F1_NB, dtype=F32)).reshape(
        DFT_N2 // F1_NB, 2 * DFT_N1 * F1_NB, DFT_N1 * F1_NB).astype(BF16)
    br, bi = _dft1(y.reshape(b, DFT_N1, DFT_N2, 2 * F_W), gbig)
    four = _dft2(br, bi, jnp.asarray(c2).astype(BF16),
                 jnp.asarray(s2).astype(BF16)).reshape(b, s, F_W)

    n_r = N_GROUPS + N_EXPERTS
    w_r = jnp.concatenate([w_router_group[0], w_router_expert[0],
                           jnp.zeros((d, 128 - n_r), F32)], axis=1)
    b_r = jnp.concatenate([b_router_group[0], b_router_expert[0],
                           jnp.zeros((128 - n_r,), F32)]).reshape(1, 128)
    w_r_hi = w_r.astype(BF16)
    w_r_lo = (w_r - w_r_hi.astype(F32)).astype(BF16)
    x1, h3, meta, gates, cnt = _outproj(attn_o, four, x, mods, g_ffn_norm[0].reshape(1, d),
                                        w_out[0].astype(BF16), jnp.stack([w_r_hi, w_r_lo]), b_r)

    e_flat = meta[:, 0:TOP_K].reshape(t * TOP_K)
    rank = meta[:, TOP_K:2 * TOP_K].reshape(t * TOP_K)
    counts = cnt[0, N_GROUPS:N_GROUPS + N_EXPERTS].astype(jnp.int32)
    nblk = (counts + MOE_BLK - 1) // MOE_BLK
    blk_end = jnp.cumsum(nblk)
    blk_start = blk_end - nblk
    dest = (jnp.take(blk_start * MOE_BLK, e_flat) + rank).astype(jnp.int32)
    n_blocks = t * TOP_K // MOE_BLK + N_EXPERTS
    blk_ids = jnp.arange(n_blocks, dtype=jnp.int32)
    blk_expert = jnp.minimum(
        jnp.sum((blk_end[None, :] <= blk_ids[:, None]).astype(jnp.int32), axis=1),
        N_EXPERTS - 1).astype(jnp.int32)
    n_used = blk_end[-1:].astype(jnp.int32)
    dest3 = dest.reshape(t // TD, 1, 2 * TD)

    xs = _dispatch(dest3, h3, jnp.zeros((n_blocks * MOE_BLK, ROW_TILE, 128), F32))
    ys = _experts(blk_expert, n_used, xs, w_gate[0], w_up[0], w_down[0])
    return _combine(dest3, ys, x1, gates, mods, g_final.reshape(1, d))
```

```python
import functools
import math

import numpy as np
import jax
import jax.numpy as jnp
from jax import lax
from jax.experimental import pallas as pl
from jax.experimental.pallas import tpu as pltpu

F32 = jnp.float32
BF16 = jnp.bfloat16

D_MODEL = 1024
GRID_W = 64
N_HEADS = 4
HEAD_DIM = 64
HEAD_W = 2 * HEAD_DIM
QK_W = N_HEADS * HEAD_W
N_FGROUPS = 4
FGROUP_DIM = 128
F_W = N_FGROUPS * FGROUP_DIM
ROPE_THETA = 10000.0
ROPE_AXIS = HEAD_DIM // 2
ROPE_HALF = ROPE_AXIS // 2
N_GROUPS = 4
EXPERTS_PER_GROUP = 8
N_EXPERTS = N_GROUPS * EXPERTS_PER_GROUP
TOP_K = 2
D_EXPERT = 512
N_MOD = 6
EPS = 1e-6
LAMBDA_INIT = 0.8 - 0.6 * math.exp(-0.3 * 0)
LOG2_E = 1.4426950408889634

DFT_N1 = 64
DFT_N2 = 128

TM_PROJ = 512
TQ = 1024
TK = 1024
F1_NB = 8
F2_KB = 8
MOE_BLK = 256
ROW_TILE = D_MODEL // 128
TD = 256
DMA_UNROLL = 16
VMEM_LIMIT = 48 * 1024 * 1024


def _cparams(*sem):
    return pltpu.CompilerParams(dimension_semantics=sem, vmem_limit_bytes=VMEM_LIMIT)


def _adaln_kernel(c_ref, w_ref, b_ref, o_ref):
    cc = c_ref[...]
    s = cc * jax.nn.sigmoid(cc)
    o_ref[...] = jnp.dot(s, w_ref[...], preferred_element_type=F32,
                         precision=lax.Precision.HIGHEST) + b_ref[...]


def _adaln(cc, w_ada, b_ada):
    n = w_ada.shape[1]
    tn = 1536
    return pl.pallas_call(
        _adaln_kernel,
        grid=(n // tn,),
        in_specs=[pl.BlockSpec((8, D_MODEL), lambda j: (0, 0)),
                  pl.BlockSpec((D_MODEL, tn), lambda j: (0, j)),
                  pl.BlockSpec((1, tn), lambda j: (0, j))],
        out_specs=pl.BlockSpec((8, tn), lambda j: (0, j)),
        out_shape=jax.ShapeDtypeStruct((8, n), F32),
        compiler_params=_cparams("arbitrary"),
        name="adaln",
    )(cc, w_ada, b_ada.reshape(1, n))


def _wfold_kernel(c_ref, s_ref, w_ref, o_ref):
    w = w_ref[0]
    o_ref[0, :, :FGROUP_DIM] = jnp.dot(c_ref[...], w, preferred_element_type=F32,
                                       precision=lax.Precision.HIGHEST).astype(BF16)
    o_ref[0, :, FGROUP_DIM:] = jnp.dot(s_ref[...], w, preferred_element_type=F32,
                                       precision=lax.Precision.HIGHEST).astype(BF16)


def _wfold(cmat, smat, w_fourier):
    return pl.pallas_call(
        _wfold_kernel,
        grid=(N_FGROUPS,),
        in_specs=[pl.BlockSpec((FGROUP_DIM, FGROUP_DIM), lambda g: (0, 0)),
                  pl.BlockSpec((FGROUP_DIM, FGROUP_DIM), lambda g: (0, 0)),
                  pl.BlockSpec((1, FGROUP_DIM, FGROUP_DIM), lambda g: (g, 0, 0))],
        out_specs=pl.BlockSpec((1, FGROUP_DIM, 2 * FGROUP_DIM), lambda g: (g, 0, 0)),
        out_shape=jax.ShapeDtypeStruct((N_FGROUPS, FGROUP_DIM, 2 * FGROUP_DIM), BF16),
        compiler_params=_cparams("arbitrary"),
        name="wfold",
    )(cmat, smat, w_fourier)


def _norm_mod(x, g, shift, scale):
    ms = jnp.mean(x * x, axis=-1, keepdims=True)
    y = x * lax.rsqrt(ms + EPS) * g
    return y * (1.0 + scale) + shift


def _rope_slab(p, cos, sa, sb):
    return (p * cos + pltpu.roll(p, ROPE_HALF, 1) * sa
            + pltpu.roll(p, HEAD_W - ROPE_HALF, 1) * sb)


def _inproj_kernel(x_ref, mod_ref, g_ref, w_ref, wf_ref, cos_ref, sa_ref, sb_ref,
                   q_ref, k_ref, v_ref, y_ref):
    m = mod_ref[0]
    h = _norm_mod(x_ref[0], g_ref[...], m[0:1], m[1:2]).astype(BF16)
    cos, sa, sb = cos_ref[...], sa_ref[...], sb_ref[...]
    scale = HEAD_DIM ** -0.5 * LOG2_E
    pq = jnp.dot(h, w_ref[:, 0:QK_W], preferred_element_type=F32)
    for hh in range(N_HEADS):
        sl = slice(hh * HEAD_W, (hh + 1) * HEAD_W)
        q_ref[0, :, sl] = (_rope_slab(pq[:, sl], cos, sa, sb) * scale).astype(BF16)
    pk = jnp.dot(h, w_ref[:, QK_W:2 * QK_W], preferred_element_type=F32)
    for hh in range(N_HEADS):
        sl = slice(hh * HEAD_W, (hh + 1) * HEAD_W)
        k_ref[0, :, sl] = _rope_slab(pk[:, sl], cos, sa, sb).astype(BF16)
    v_ref[0] = jnp.dot(h, w_ref[:, 2 * QK_W:3 * QK_W], preferred_element_type=F32).astype(BF16)
    pf = jnp.dot(h, w_ref[:, 3 * QK_W:], preferred_element_type=F32).astype(BF16)
    for g in range(N_FGROUPS):
        yy = jnp.dot(pf[:, g * FGROUP_DIM:(g + 1) * FGROUP_DIM], wf_ref[g],
                     preferred_element_type=F32)
        y_ref[0, :, g * FGROUP_DIM:(g + 1) * FGROUP_DIM] = yy[:, :FGROUP_DIM]
        y_ref[0, :, F_W + g * FGROUP_DIM:F_W + (g + 1) * FGROUP_DIM] = yy[:, FGROUP_DIM:]


def _inproj(x, mods, g_mix, w_in, wf, cos_t, sa_t, sb_t):
    b, s, d = x.shape
    tm = TM_PROJ
    tok = lambda bi, i: (bi, i, 0)
    return pl.pallas_call(
        _inproj_kernel,
        grid=(b, s // tm),
        in_specs=[pl.BlockSpec((1, tm, d), tok),
                  pl.BlockSpec((1, N_MOD, d), lambda bi, i: (bi, 0, 0)),
                  pl.BlockSpec((1, d), lambda bi, i: (0, 0)),
                  pl.BlockSpec(w_in.shape, lambda bi, i: (0, 0)),
                  pl.BlockSpec(wf.shape, lambda bi, i: (0, 0, 0)),
                  pl.BlockSpec((tm, HEAD_W), lambda bi, i: (i, 0)),
                  pl.BlockSpec((tm, HEAD_W), lambda bi, i: (i, 0)),
                  pl.BlockSpec((tm, HEAD_W), lambda bi, i: (i, 0))],
        out_specs=[pl.BlockSpec((1, tm, QK_W), tok),
                   pl.BlockSpec((1, tm, QK_W), tok),
                   pl.BlockSpec((1, tm, QK_W), tok),
                   pl.BlockSpec((1, tm, 2 * F_W), tok)],
        out_shape=[jax.ShapeDtypeStruct((b, s, QK_W), BF16),
                   jax.ShapeDtypeStruct((b, s, QK_W), BF16),
                   jax.ShapeDtypeStruct((b, s, QK_W), BF16),
                   jax.ShapeDtypeStruct((b, s, 2 * F_W), F32)],
        compiler_params=_cparams("parallel", "arbitrary"),
        name="inproj",
    )(x, mods, g_mix, w_in, wf, cos_t, sa_t, sb_t)


def _ctxproj_kernel(x_ref, mod_ref, g_ref, w_ref, k_ref, v_ref):
    m = mod_ref[0]
    h = _norm_mod(x_ref[0], g_ref[...], m[0:1], m[1:2]).astype(BF16)
    k_ref[0] = jnp.dot(h, w_ref[:, QK_W:2 * QK_W], preferred_element_type=F32).astype(BF16)
    v_ref[0] = jnp.dot(h, w_ref[:, 2 * QK_W:3 * QK_W], preferred_element_type=F32).astype(BF16)


def _ctxproj(ctx, mods, g_mix, w_in):
    b, n, d = ctx.shape
    return pl.pallas_call(
        _ctxproj_kernel,
        grid=(b,),
        in_specs=[pl.BlockSpec((1, n, d), lambda bi: (bi, 0, 0)),
                  pl.BlockSpec((1, N_MOD, d), lambda bi: (2, 0, 0)),
                  pl.BlockSpec((1, d), lambda bi: (0, 0)),
                  pl.BlockSpec(w_in.shape, lambda bi: (0, 0))],
        out_specs=[pl.BlockSpec((1, n, QK_W), lambda bi: (bi, 0, 0)),
                   pl.BlockSpec((1, n, QK_W), lambda bi: (bi, 0, 0))],
        out_shape=[jax.ShapeDtypeStruct((b, n, QK_W), BF16),
                   jax.ShapeDtypeStruct((b, n, QK_W), BF16)],
        compiler_params=_cparams("arbitrary"),
        name="ctxproj",
    )(ctx, mods, g_mix, w_in)


def _attn_kernel(q_ref, kc_ref, vc_ref, kl_ref, vl_ref, lam_ref, g_ref, o_ref,
                 m_ref, l_ref, acc_ref):
    q = q_ref[0]
    lane = lax.broadcasted_iota(jnp.int32, q.shape, 1)
    zero = jnp.zeros_like(q)
    q0 = jnp.where(lane < HEAD_DIM, q, zero)
    q1 = jnp.where(lane >= HEAD_DIM, q, zero)
    qs = (q0, q1)
    contract_last = (((1,), (1,)), ((), ()))
    m_ref[...] = jnp.full(m_ref.shape, -1e30, F32)
    l_ref[...] = jnp.zeros(l_ref.shape, F32)
    acc_ref[...] = jnp.zeros(acc_ref.shape, F32)

    def step(kb, vb):
        nk = kb.shape[0] // HEAD_W
        for mi in range(2):
            s = lax.dot_general(qs[mi], kb, contract_last, preferred_element_type=F32)
            m_old = m_ref[mi]
            m_new = jnp.maximum(m_old, jnp.max(s, axis=-1, keepdims=True))
            alpha = jnp.exp2(m_old - m_new)
            p = jnp.exp2(s - jnp.concatenate([m_new] * nk, axis=1))
            psum = p[:, 0:HEAD_W]
            for cblk in range(1, nk):
                psum = psum + p[:, cblk * HEAD_W:(cblk + 1) * HEAD_W]
            l_ref[mi] = alpha * l_ref[mi] + psum
            acc_ref[mi] = alpha * acc_ref[mi] + jnp.dot(p.astype(BF16), vb,
                                                       preferred_element_type=F32)
            m_ref[mi] = m_new

    step(kc_ref[0], vc_ref[0])

    def body(i, c):
        off = pl.multiple_of(i * TK, TK)
        step(kl_ref[0, pl.ds(off, TK), :], vl_ref[0, pl.ds(off, TK), :])
        return c

    lax.fori_loop(0, kl_ref.shape[1] // TK, body, 0)

    lp = lam_ref[...]
    t1 = jnp.sum(lp[0:1] * lp[1:2], axis=-1, keepdims=True)
    t2 = jnp.sum(lp[2:3] * lp[3:4], axis=-1, keepdims=True)
    lam = jnp.exp(t1) - jnp.exp(t2) + LAMBDA_INIT
    l0 = jnp.sum(l_ref[0], axis=-1, keepdims=True)
    l1 = jnp.sum(l_ref[1], axis=-1, keepdims=True)
    o = acc_ref[0] / l0 - lam * (acc_ref[1] / l1)
    ms = jnp.mean(o * o, axis=-1, keepdims=True)
    o = o * lax.rsqrt(ms + EPS) * g_ref[...] * (1.0 - LAMBDA_INIT)
    o_ref[0] = o.astype(BF16)


def _attention(q, kc, vc, kl, vl, lam_p, g_subln):
    b, s, _ = q.shape
    n_ctx = kc.shape[1]
    return pl.pallas_call(
        _attn_kernel,
        grid=(b, N_HEADS, s // TQ),
        in_specs=[pl.BlockSpec((1, TQ, HEAD_W), lambda bi, h, i: (bi, i, h)),
                  pl.BlockSpec((1, n_ctx, HEAD_W), lambda bi, h, i: (bi, 0, h)),
                  pl.BlockSpec((1, n_ctx, HEAD_W), lambda bi, h, i: (bi, 0, h)),
                  pl.BlockSpec((1, s, HEAD_W), lambda bi, h, i: (bi, 0, h)),
                  pl.BlockSpec((1, s, HEAD_W), lambda bi, h, i: (bi, 0, h)),
                  pl.BlockSpec((4, HEAD_DIM), lambda bi, h, i: (0, 0)),
                  pl.BlockSpec((1, HEAD_W), lambda bi, h, i: (0, 0))],
        out_specs=pl.BlockSpec((1, TQ, HEAD_W), lambda bi, h, i: (bi, i, h)),
        out_shape=jax.ShapeDtypeStruct((b, s, QK_W), BF16),
        scratch_shapes=[pltpu.VMEM((2, TQ, HEAD_W), F32)] * 3,
        compiler_params=_cparams("parallel", "parallel", "arbitrary"),
        name="diffattn",
    )(q, kc, vc, kl, vl, lam_p, g_subln)


def _dft1_kernel(y_ref, g_ref, br_ref, bi_ref):
    rows = DFT_N1 * F1_NB
    yb = y_ref[0].reshape(rows, 2 * F_W).astype(BF16)
    p = jnp.dot(g_ref[0], yb, preferred_element_type=F32)
    top, bot = p[:rows], p[rows:]
    br_ref[0] = (top[:, :F_W] - bot[:, F_W:]).reshape(DFT_N1, F1_NB, F_W)
    bi_ref[0] = (-top[:, F_W:] - bot[:, :F_W]).reshape(DFT_N1, F1_NB, F_W)


def _dft1(y4, gmat):
    b = y4.shape[0]
    rows = DFT_N1 * F1_NB
    return pl.pallas_call(
        _dft1_kernel,
        grid=(b, DFT_N2 // F1_NB),
        in_specs=[pl.BlockSpec((1, DFT_N1, F1_NB, 2 * F_W), lambda bi, j: (bi, 0, j, 0)),
                  pl.BlockSpec((1, 2 * rows, rows), lambda bi, j: (j, 0, 0))],
        out_specs=[pl.BlockSpec((1, DFT_N1, F1_NB, F_W), lambda bi, j: (bi, 0, j, 0)),
                   pl.BlockSpec((1, DFT_N1, F1_NB, F_W), lambda bi, j: (bi, 0, j, 0))],
        out_shape=[jax.ShapeDtypeStruct((b, DFT_N1, DFT_N2, F_W), F32),
                   jax.ShapeDtypeStruct((b, DFT_N1, DFT_N2, F_W), F32)],
        compiler_params=_cparams("parallel", "arbitrary"),
        name="dft1",
    )(y4, gmat)


def _dft2_kernel(br_ref, bi_ref, c_ref, s_ref, o_ref):
    for j in range(F2_KB):
        r = (jnp.dot(c_ref[...], br_ref[0, j].astype(BF16), preferred_element_type=F32)
             + jnp.dot(s_ref[...], bi_ref[0, j].astype(BF16), preferred_element_type=F32))
        o_ref[0, :, j, :] = r


def _dft2(br4, bi4, c2, s2):
    b = br4.shape[0]
    blk = (1, F2_KB, DFT_N2, F_W)
    return pl.pallas_call(
        _dft2_kernel,
        grid=(b, DFT_N1 // F2_KB),
        in_specs=[pl.BlockSpec(blk, lambda bi, j: (bi, j, 0, 0)),
                  pl.BlockSpec(blk, lambda bi, j: (bi, j, 0, 0)),
                  pl.BlockSpec((DFT_N2, DFT_N2), lambda bi, j: (0, 0)),
                  pl.BlockSpec((DFT_N2, DFT_N2), lambda bi, j: (0, 0))],
        out_specs=pl.BlockSpec((1, DFT_N2, F2_KB, F_W), lambda bi, j: (bi, 0, j, 0)),
        out_shape=jax.ShapeDtypeStruct((b, DFT_N2, DFT_N1, F_W), F32),
        compiler_params=_cparams("parallel", "arbitrary"),
        name="dft2",
    )(br4, bi4, c2, s2)


def _outproj_kernel(o_ref, f_ref, x_ref, mod_ref, g_ref, wo_ref, wr_ref, br_ref, tri_ref,
                    x1_ref, h3_ref, meta_ref, gate_ref, cnt_out_ref, cnt_ref):
    first = (pl.program_id(0) == 0) & (pl.program_id(1) == 0)

    @pl.when(first)
    def _():
        cnt_ref[...] = jnp.zeros(cnt_ref.shape, F32)

    m = mod_ref[0]
    mix = (jnp.dot(o_ref[0], wo_ref[0:QK_W, :], preferred_element_type=F32)
           + jnp.dot(f_ref[0].astype(BF16), wo_ref[QK_W:, :], preferred_element_type=F32))
    x1 = x_ref[0] + m[2:3] * mix
    x1_ref[0] = x1
    h2 = _norm_mod(x1, g_ref[...], m[3:4], m[4:5])
    _store_rows(h3_ref, h2)
    h_hi = h2.astype(BF16)
    h_lo = (h2 - h_hi.astype(F32)).astype(BF16)
    lg = (jnp.dot(h_hi, wr_ref[0], preferred_element_type=F32)
          + jnp.dot(h_lo, wr_ref[0], preferred_element_type=F32)
          + jnp.dot(h_hi, wr_ref[1], preferred_element_type=F32)
          + br_ref[...])
    lane = lax.broadcasted_iota(jnp.int32, lg.shape, 1)
    ninf = jnp.float32(-jnp.inf)
    big = jnp.int32(lg.shape[1])
    gl = jnp.where(lane < N_GROUPS, lg, ninf)
    gmax = jnp.max(gl, axis=-1, keepdims=True)
    grp = jnp.min(jnp.where(gl == gmax, lane, big), axis=-1, keepdims=True)
    pg = 1.0 / jnp.sum(jnp.exp(gl - gmax), axis=-1, keepdims=True)
    e_lane = lane - N_GROUPS
    emask = (e_lane >= 0) & (e_lane < N_EXPERTS) & ((e_lane >> 3) == grp)
    el = jnp.where(emask, lg, ninf)
    t1 = jnp.max(el, axis=-1, keepdims=True)
    i1 = jnp.min(jnp.where(el == t1, lane, big), axis=-1, keepdims=True)
    el2 = jnp.where(lane == i1, ninf, el)
    t2 = jnp.max(el2, axis=-1, keepdims=True)
    i2 = jnp.min(jnp.where(el2 == t2, lane, big), axis=-1, keepdims=True)
    dd = jnp.exp(t2 - t1)
    w1 = pg / (1.0 + dd)
    w2 = pg * dd / (1.0 + dd)
    gate_ref[...] = jnp.where(lane == 0, w1, jnp.where(lane == 1, w2, 0.0))
    hit1 = lane == i1
    hit2 = lane == i2
    oh = jnp.where(hit1 | hit2, 1.0, 0.0)
    before = jnp.dot(tri_ref[...], oh.astype(BF16), preferred_element_type=F32) + cnt_ref[0:1, :]
    r1 = jnp.sum(jnp.where(hit1, before, 0.0), axis=-1, keepdims=True).astype(jnp.int32)
    r2 = jnp.sum(jnp.where(hit2, before, 0.0), axis=-1, keepdims=True).astype(jnp.int32)
    cnt_ref[0:1, :] = cnt_ref[0:1, :] + jnp.sum(oh, axis=0, keepdims=True)
    cnt_out_ref[...] = cnt_ref[...]
    meta_ref[...] = jnp.where(lane == 0, i1 - N_GROUPS,
                              jnp.where(lane == 1, i2 - N_GROUPS,
                                        jnp.where(lane == 2, r1, jnp.where(lane == 3, r2, 0))))


def _outproj(attn_o, four, x, mods, g_ffn, w_out, w_r, b_r):
    b, s, d = x.shape
    tm = TM_PROJ
    nt = s // tm
    tok = lambda bi, i: (bi, i, 0)
    flat = lambda bi, i: (bi * nt + i, 0)
    tri = jnp.asarray(np.tril(np.ones((tm, tm), np.float32), -1)).astype(BF16)
    return pl.pallas_call(
        _outproj_kernel,
        grid=(b, nt),
        in_specs=[pl.BlockSpec((1, tm, QK_W), tok),
                  pl.BlockSpec((1, tm, F_W), tok),
                  pl.BlockSpec((1, tm, d), tok),
                  pl.BlockSpec((1, N_MOD, d), lambda bi, i: (bi, 0, 0)),
                  pl.BlockSpec((1, d), lambda bi, i: (0, 0)),
                  pl.BlockSpec(w_out.shape, lambda bi, i: (0, 0)),
                  pl.BlockSpec(w_r.shape, lambda bi, i: (0, 0, 0)),
                  pl.BlockSpec(b_r.shape, lambda bi, i: (0, 0)),
                  pl.BlockSpec((tm, tm), lambda bi, i: (0, 0))],
        out_specs=[pl.BlockSpec((1, tm, d), tok),
                   pl.BlockSpec((tm * ROW_TILE, 128), flat),
                   pl.BlockSpec((tm, 128), flat),
                   pl.BlockSpec((tm, 128), flat),
                   pl.BlockSpec((8, 128), lambda bi, i: (0, 0))],
        out_shape=[jax.ShapeDtypeStruct((b, s, d), F32),
                   jax.ShapeDtypeStruct((b * s * ROW_TILE, 128), F32),
                   jax.ShapeDtypeStruct((b * s, 128), jnp.int32),
                   jax.ShapeDtypeStruct((b * s, 128), F32),
                   jax.ShapeDtypeStruct((8, 128), F32)],
        scratch_shapes=[pltpu.VMEM((8, 128), F32)],
        compiler_params=_cparams("arbitrary", "arbitrary"),
        name="outproj",
    )(attn_o, four, x, mods, g_ffn, w_out, w_r, b_r, tri)


def _row_slice(row):
    return pl.ds(pl.multiple_of(row * ROW_TILE, ROW_TILE), ROW_TILE)


def _row_copy(src_ref, src_row, dst_ref, dst_row, sem):
    return pltpu.make_async_copy(src_ref.at[_row_slice(src_row)], dst_ref.at[_row_slice(dst_row)], sem)


def _rows_2d(ref, n_rows):
    return jnp.concatenate([ref[pl.ds(cblk, n_rows, stride=ROW_TILE), :]
                            for cblk in range(ROW_TILE)], axis=1)


def _store_rows(ref, val):
    for cblk in range(ROW_TILE):
        ref[pl.ds(cblk, val.shape[0], stride=ROW_TILE), :] = val[:, cblk * 128:(cblk + 1) * 128]


def _dispatch_kernel(dest_ref, h_ref, zeros_ref, xs_ref, sem):
    del zeros_ref
    n = 2 * TD

    def issue(a, c):
        _row_copy(h_ref, a >> 1, xs_ref, dest_ref[0, 0, a], sem).start()
        return c

    lax.fori_loop(0, n, issue, 0, unroll=DMA_UNROLL)

    def drain(a, c):
        _row_copy(h_ref, 0, xs_ref, 0, sem).wait()
        return c

    lax.fori_loop(0, n, drain, 0, unroll=DMA_UNROLL)


def _dispatch(dest3, h3, xs_zeros):
    t = h3.shape[0] // ROW_TILE
    return pl.pallas_call(
        _dispatch_kernel,
        grid=(t // TD,),
        in_specs=[pl.BlockSpec((1, 1, 2 * TD), lambda i: (i, 0, 0), memory_space=pltpu.SMEM),
                  pl.BlockSpec((TD * ROW_TILE, 128), lambda i: (i, 0)),
                  pl.BlockSpec(memory_space=pl.ANY)],
        out_specs=pl.BlockSpec(memory_space=pl.ANY),
        out_shape=jax.ShapeDtypeStruct(xs_zeros.shape, xs_zeros.dtype),
        scratch_shapes=[pltpu.SemaphoreType.DMA(())],
        input_output_aliases={2: 0},
        compiler_params=_cparams("arbitrary"),
        name="dispatch",
    )(dest3, h3, xs_zeros)


def _experts_kernel(be_ref, nu_ref, xs_ref, wg_ref, wu_ref, wd_ref, ys_ref, wgb, wub, wdb):
    j = pl.program_id(0)
    used = j < nu_ref[0]
    new_expert = (j == 0) | (be_ref[j] != be_ref[jnp.maximum(j - 1, 0)])

    @pl.when(used & new_expert)
    def _():
        wgb[...] = wg_ref[0].astype(BF16)
        wub[...] = wu_ref[0].astype(BF16)
        wdb[...] = wd_ref[0].astype(BF16)

    @pl.when(used)
    def _():
        xb = _rows_2d(xs_ref, MOE_BLK).astype(BF16)
        gate = jnp.dot(xb, wgb[...], preferred_element_type=F32)
        up = jnp.dot(xb, wub[...], preferred_element_type=F32)
        hid = (gate * jax.nn.sigmoid(gate) * up).astype(BF16)
        _store_rows(ys_ref, jnp.dot(hid, wdb[...], preferred_element_type=F32))

    @pl.when(pl.program_id(0) >= nu_ref[0])
    def _():
        ys_ref[...] = jnp.zeros_like(ys_ref)


def _experts(blk_expert, n_used, xs, wg, wu, wd):
    rows = xs.shape[0] // ROW_TILE
    d = wg.shape[1]
    nb = rows // MOE_BLK
    row_blk = lambda j, be, nu: (jnp.minimum(j, nu[0] - 1), 0)
    out_blk = lambda j, be, nu: (j, 0)
    w_blk = lambda j, be, nu: (be[jnp.minimum(j, nu[0] - 1)], 0, 0)
    return pl.pallas_call(
        _experts_kernel,
        grid_spec=pltpu.PrefetchScalarGridSpec(
            num_scalar_prefetch=2,
            grid=(nb,),
            in_specs=[pl.BlockSpec((MOE_BLK * ROW_TILE, 128), row_blk),
                      pl.BlockSpec((1, d, D_EXPERT), w_blk),
                      pl.BlockSpec((1, d, D_EXPERT), w_blk),
                      pl.BlockSpec((1, D_EXPERT, d), w_blk)],
            out_specs=pl.BlockSpec((MOE_BLK * ROW_TILE, 128), out_blk),
            scratch_shapes=[pltpu.VMEM((d, D_EXPERT), BF16), pltpu.VMEM((d, D_EXPERT), BF16),
                            pltpu.VMEM((D_EXPERT, d), BF16)]),
        out_shape=jax.ShapeDtypeStruct(xs.shape, F32),
        compiler_params=_cparams("arbitrary"),
        name="experts",
    )(blk_expert, n_used, xs, wg, wu, wd)


def _combine_kernel(dest_ref, dest_next_ref, ys_ref, x1_ref, gate_ref, mod_ref, g_ref, o_ref,
                    ya, yb, sem):
    step = pl.program_id(0) * pl.num_programs(1) + pl.program_id(1)
    n_steps = pl.num_programs(0) * pl.num_programs(1)
    slot = step % 2

    def start_gathers(idx_ref, to_slot):
        def issue(r, c):
            _row_copy(ys_ref, idx_ref[0, 0, 2 * r], ya.at[to_slot], r, sem.at[to_slot]).start()
            _row_copy(ys_ref, idx_ref[0, 0, 2 * r + 1], yb.at[to_slot], r, sem.at[to_slot]).start()
            return c

        lax.fori_loop(0, TD, issue, 0, unroll=DMA_UNROLL)

    @pl.when(step == 0)
    def _():
        start_gathers(dest_ref, 0)

    @pl.when(step + 1 < n_steps)
    def _():
        start_gathers(dest_next_ref, 1 - slot)

    def drain(r, c):
        _row_copy(ys_ref, 0, ya.at[slot], 0, sem.at[slot]).wait()
        _row_copy(ys_ref, 0, yb.at[slot], 0, sem.at[slot]).wait()
        return c

    lax.fori_loop(0, TD, drain, 0, unroll=DMA_UNROLL)
    gt = gate_ref[...]
    moe = gt[:, 0:1] * _rows_2d(ya.at[slot], TD) + gt[:, 1:2] * _rows_2d(yb.at[slot], TD)
    x2 = x1_ref[0] + mod_ref[0][5:6] * moe
    ms = jnp.mean(x2 * x2, axis=-1, keepdims=True)
    o_ref[0] = x2 * lax.rsqrt(ms + EPS) * g_ref[...]


def _combine(dest3, ys, x1, gates, mods, g_final):
    b, s, d = x1.shape
    nt = s // TD
    return pl.pallas_call(
        _combine_kernel,
        grid=(b, nt),
        in_specs=[pl.BlockSpec((1, 1, 2 * TD), lambda bi, i: (bi * nt + i, 0, 0),
                               memory_space=pltpu.SMEM),
                  pl.BlockSpec((1, 1, 2 * TD),
                               lambda bi, i: (jnp.minimum(bi * nt + i + 1, b * nt - 1), 0, 0),
                               memory_space=pltpu.SMEM),
                  pl.BlockSpec(memory_space=pl.ANY),
                  pl.BlockSpec((1, TD, d), lambda bi, i: (bi, i, 0)),
                  pl.BlockSpec((TD, 128), lambda bi, i: (bi * nt + i, 0)),
                  pl.BlockSpec((1, N_MOD, d), lambda bi, i: (bi, 0, 0)),
                  pl.BlockSpec((1, d), lambda bi, i: (0, 0))],
        out_specs=pl.BlockSpec((1, TD, d), lambda bi, i: (bi, i, 0)),
        out_shape=jax.ShapeDtypeStruct((b, s, d), F32),
        scratch_shapes=[pltpu.VMEM((2, TD * ROW_TILE, 128), F32),
                        pltpu.VMEM((2, TD * ROW_TILE, 128), F32),
                        pltpu.SemaphoreType.DMA((2,))],
        compiler_params=_cparams("arbitrary", "arbitrary"),
        name="combine",
    )(dest3, dest3, ys, x1, gates, mods, g_final)


def _rope_tables(rows):
    r, col = jnp.meshgrid(jnp.arange(rows), jnp.arange(GRID_W), indexing='ij')
    pos = jnp.stack([r.reshape(-1), col.reshape(-1)], axis=-1).astype(F32)
    inv_freq = ROPE_THETA ** (-jnp.arange(0, ROPE_AXIS, 2, dtype=F32) / ROPE_AXIS)
    ang = pos[:, :, None] * inv_freq
    ang = jnp.concatenate([ang, ang], axis=-1)
    n = ang.shape[0]
    cos = jnp.tile(jnp.cos(ang).reshape(n, HEAD_DIM), (1, 2))
    sin = jnp.tile(jnp.sin(ang).reshape(n, HEAD_DIM), (1, 2))
    upper = (jnp.arange(HEAD_W) % ROPE_AXIS) >= ROPE_HALF
    sa = jnp.where(upper, sin, 0.0)
    sb = jnp.where(upper, 0.0, -sin)
    return cos, sa, sb


def _dft_constants(n_pos):
    c = np.arange(FGROUP_DIM)
    ang_c = 2.0 * np.pi * ((c[:, None] * c[None, :]) % FGROUP_DIM) / FGROUP_DIM
    norm = 1.0 / math.sqrt(n_pos * FGROUP_DIM)
    cmat = (np.cos(ang_c) * norm).astype(np.float32)
    smat = (np.sin(ang_c) * norm).astype(np.float32)
    k1 = np.arange(DFT_N1)[None, :, None]
    n1 = np.arange(DFT_N1)[None, None, :]
    n2 = np.arange(DFT_N2)[:, None, None]
    ang_g = 2.0 * np.pi * ((k1 * (DFT_N2 * n1 + n2)) % n_pos) / n_pos
    gsmall = np.stack([np.cos(ang_g), np.sin(ang_g)], axis=1)
    gsmall = gsmall.reshape(DFT_N2 // F1_NB, F1_NB, 2, DFT_N1, DFT_N1)
    gmat = np.einsum('japkn,ab->jpkanb', gsmall, np.eye(F1_NB)).reshape(
        DFT_N2 // F1_NB, 2 * DFT_N1 * F1_NB, DFT_N1 * F1_NB).astype(BF16)
    k2 = np.arange(DFT_N2)
    ang_2 = 2.0 * np.pi * ((k2[:, None] * k2[None, :]) % DFT_N2) / DFT_N2
    c2 = np.cos(ang_2).astype(np.float32)
    s2 = np.sin(ang_2).astype(np.float32)
    return cmat, smat, gmat, c2, s2


def kernel(x, c, ctx, c_ctx, w_ada, b_ada, g_mix_norm, g_ffn_norm, w_in, lambda_q1, lambda_k1, lambda_q2, lambda_k2, g_subln, w_fourier, w_out, w_router_group, b_router_group, w_router_expert, b_router_expert, w_gate, w_up, w_down, g_final):
    b, s, d = x.shape
    t = b * s
    assert d == D_MODEL and s == DFT_N1 * DFT_N2 and s % GRID_W == 0 and b == 2

    cc = jnp.concatenate([c, c_ctx[None, :], jnp.zeros((8 - b - 1, d), F32)], axis=0)
    mods = _adaln(cc, w_ada[0], b_ada[0]).reshape(8, N_MOD, d)

    cmat, smat, gmat, c2, s2 = _dft_constants(s)
    wf = _wfold(jnp.asarray(cmat), jnp.asarray(smat), w_fourier[0])
    cos_t, sa_t, sb_t = _rope_tables(s // GRID_W)

    w_in_b = w_in[0].astype(BF16)
    g_mix = g_mix_norm[0].reshape(1, d)
    q, kl, vl, y = _inproj(x, mods, g_mix, w_in_b, wf, cos_t, sa_t, sb_t)
    kc, vc = _ctxproj(ctx, mods, g_mix, w_in_b)

    lam_p = jnp.stack([lambda_q1[0], lambda_k1[0], lambda_q2[0], lambda_k2[0]], axis=0)
    attn_o = _attention(q, kc, vc, kl, vl, lam_p, g_subln[0].reshape(1, HEAD_W))

    br, bi = _dft1(y.reshape(b, DFT_N1, DFT_N2, 2 * F_W), jnp.asarray(gmat))
    four = _dft2(br, bi, jnp.asarray(c2).astype(BF16),
                 jnp.asarray(s2).astype(BF16)).reshape(b, s, F_W)

    n_r = N_GROUPS + N_EXPERTS
    w_r = jnp.concatenate([w_router_group[0], w_router_expert[0],
                           jnp.zeros((d, 128 - n_r), F32)], axis=1)
    b_r = jnp.concatenate([b_router_group[0], b_router_expert[0],
                           jnp.zeros((128 - n_r,), F32)]).reshape(1, 128)
    w_r_hi = w_r.astype(BF16)
    w_r_lo = (w_r - w_r_hi.astype(F32)).astype(BF16)
    x1, h3, meta, gates, cnt = _outproj(attn_o, four, x, mods, g_ffn_norm[0].reshape(1, d),
                                        w_out[0].astype(BF16), jnp.stack([w_r_hi, w_r_lo]), b_r)

    e_flat = meta[:, 0:TOP_K].reshape(t * TOP_K)
    rank = meta[:, TOP_K:2 * TOP_K].reshape(t * TOP_K)
    counts = cnt[0, N_GROUPS:N_GROUPS + N_EXPERTS].astype(jnp.int32)
    nblk = (counts + MOE_BLK - 1) // MOE_BLK
    blk_end = jnp.cumsum(nblk)
    blk_start = blk_end - nblk
    dest = (jnp.take(blk_start * MOE_BLK, e_flat) + rank).astype(jnp.int32)
    n_blocks = t * TOP_K // MOE_BLK + N_EXPERTS
    blk_ids = jnp.arange(n_blocks, dtype=jnp.int32)
    blk_expert = jnp.minimum(
        jnp.sum((blk_end[None, :] <= blk_ids[:, None]).astype(jnp.int32), axis=1),
        N_EXPERTS - 1).astype(jnp.int32)
    n_used = blk_end[-1:].astype(jnp.int32)
    dest3 = dest.reshape(t // TD, 1, 2 * TD)

    xs = _dispatch(dest3, h3, jnp.zeros((n_blocks * MOE_BLK * ROW_TILE, 128), F32))
    ys = _experts(blk_expert, n_used, xs, w_gate[0], w_up[0], w_down[0])
    return _combine(dest3, ys, x1, gates, mods, g_final.reshape(1, d))
```

```python
import functools
import math

import numpy as np
import jax
import jax.numpy as jnp
from jax import lax
from jax.experimental import pallas as pl
from jax.experimental.pallas import tpu as pltpu

F32 = jnp.float32
BF16 = jnp.bfloat16

D_MODEL = 1024
GRID_W = 64
N_HEADS = 4
HEAD_DIM = 64
HEAD_W = 2 * HEAD_DIM
QK_W = N_HEADS * HEAD_W
N_FGROUPS = 4
FGROUP_DIM = 128
F_W = N_FGROUPS * FGROUP_DIM
ROPE_THETA = 10000.0
ROPE_AXIS = HEAD_DIM // 2
ROPE_HALF = ROPE_AXIS // 2
N_GROUPS = 4
EXPERTS_PER_GROUP = 8
N_EXPERTS = N_GROUPS * EXPERTS_PER_GROUP
TOP_K = 2
D_EXPERT = 512
N_MOD = 6
EPS = 1e-6
LAMBDA_INIT = 0.8 - 0.6 * math.exp(-0.3 * 0)
LOG2_E = 1.4426950408889634

DFT_N1 = 64
DFT_N2 = 128

TM_PROJ = 512
TQ = 1024
TK = 1024
F1_NB = 8
F2_KB = 8
MOE_BLK = 256
ROW_TILE = D_MODEL // 128
TD = 256
DMA_UNROLL = 16
VMEM_LIMIT = 48 * 1024 * 1024


def _cparams(*sem):
    return pltpu.CompilerParams(dimension_semantics=sem, vmem_limit_bytes=VMEM_LIMIT)


def _adaln_kernel(c_ref, w_ref, b_ref, o_ref):
    cc = c_ref[...]
    s = cc * jax.nn.sigmoid(cc)
    o_ref[...] = jnp.dot(s, w_ref[...], preferred_element_type=F32,
                         precision=lax.Precision.HIGHEST) + b_ref[...]


def _adaln(cc, w_ada, b_ada):
    n = w_ada.shape[1]
    tn = 1536
    return pl.pallas_call(
        _adaln_kernel,
        grid=(n // tn,),
        in_specs=[pl.BlockSpec((8, D_MODEL), lambda j: (0, 0)),
                  pl.BlockSpec((D_MODEL, tn), lambda j: (0, j)),
                  pl.BlockSpec((1, tn), lambda j: (0, j))],
        out_specs=pl.BlockSpec((8, tn), lambda j: (0, j)),
        out_shape=jax.ShapeDtypeStruct((8, n), F32),
        compiler_params=_cparams("arbitrary"),
        name="adaln",
    )(cc, w_ada, b_ada.reshape(1, n))


def _wfold_kernel(c_ref, s_ref, w_ref, o_ref):
    w = w_ref[0]
    o_ref[0, :, :FGROUP_DIM] = jnp.dot(c_ref[...], w, preferred_element_type=F32,
                                       precision=lax.Precision.HIGHEST).astype(BF16)
    o_ref[0, :, FGROUP_DIM:] = jnp.dot(s_ref[...], w, preferred_element_type=F32,
                                       precision=lax.Precision.HIGHEST).astype(BF16)


def _wfold(cmat, smat, w_fourier):
    return pl.pallas_call(
        _wfold_kernel,
        grid=(N_FGROUPS,),
        in_specs=[pl.BlockSpec((FGROUP_DIM, FGROUP_DIM), lambda g: (0, 0)),
                  pl.BlockSpec((FGROUP_DIM, FGROUP_DIM), lambda g: (0, 0)),
                  pl.BlockSpec((1, FGROUP_DIM, FGROUP_DIM), lambda g: (g, 0, 0))],
        out_specs=pl.BlockSpec((1, FGROUP_DIM, 2 * FGROUP_DIM), lambda g: (g, 0, 0)),
        out_shape=jax.ShapeDtypeStruct((N_FGROUPS, FGROUP_DIM, 2 * FGROUP_DIM), BF16),
        compiler_params=_cparams("arbitrary"),
        name="wfold",
    )(cmat, smat, w_fourier)


def _norm_mod(x, g, shift, scale):
    ms = jnp.mean(x * x, axis=-1, keepdims=True)
    y = x * lax.rsqrt(ms + EPS) * g
    return y * (1.0 + scale) + shift


def _rope_slab(p, cos, sa, sb):
    return (p * cos + pltpu.roll(p, ROPE_HALF, 1) * sa
            + pltpu.roll(p, HEAD_W - ROPE_HALF, 1) * sb)


def _inproj_kernel(x_ref, mod_ref, g_ref, w_ref, wf_ref, cos_ref, sa_ref, sb_ref,
                   q_ref, k_ref, v_ref, y_ref):
    m = mod_ref[0]
    h = _norm_mod(x_ref[0], g_ref[...], m[0:1], m[1:2]).astype(BF16)
    cos, sa, sb = cos_ref[...], sa_ref[...], sb_ref[...]
    scale = HEAD_DIM ** -0.5 * LOG2_E
    pq = jnp.dot(h, w_ref[:, 0:QK_W], preferred_element_type=F32)
    for hh in range(N_HEADS):
        sl = slice(hh * HEAD_W, (hh + 1) * HEAD_W)
        q_ref[0, :, sl] = (_rope_slab(pq[:, sl], cos, sa, sb) * scale).astype(BF16)
    pk = jnp.dot(h, w_ref[:, QK_W:2 * QK_W], preferred_element_type=F32)
    for hh in range(N_HEADS):
        sl = slice(hh * HEAD_W, (hh + 1) * HEAD_W)
        k_ref[0, :, sl] = _rope_slab(pk[:, sl], cos, sa, sb).astype(BF16)
    v_ref[0] = jnp.dot(h, w_ref[:, 2 * QK_W:3 * QK_W], preferred_element_type=F32).astype(BF16)
    pf = jnp.dot(h, w_ref[:, 3 * QK_W:], preferred_element_type=F32).astype(BF16)
    for g in range(N_FGROUPS):
        yy = jnp.dot(pf[:, g * FGROUP_DIM:(g + 1) * FGROUP_DIM], wf_ref[g],
                     preferred_element_type=F32)
        y_ref[0, :, g * FGROUP_DIM:(g + 1) * FGROUP_DIM] = yy[:, :FGROUP_DIM]
        y_ref[0, :, F_W + g * FGROUP_DIM:F_W + (g + 1) * FGROUP_DIM] = yy[:, FGROUP_DIM:]


def _inproj(x, mods, g_mix, w_in, wf, cos_t, sa_t, sb_t):
    b, s, d = x.shape
    tm = TM_PROJ
    tok = lambda bi, i: (bi, i, 0)
    return pl.pallas_call(
        _inproj_kernel,
        grid=(b, s // tm),
        in_specs=[pl.BlockSpec((1, tm, d), tok),
                  pl.BlockSpec((1, N_MOD, d), lambda bi, i: (bi, 0, 0)),
                  pl.BlockSpec((1, d), lambda bi, i: (0, 0)),
                  pl.BlockSpec(w_in.shape, lambda bi, i: (0, 0)),
                  pl.BlockSpec(wf.shape, lambda bi, i: (0, 0, 0)),
                  pl.BlockSpec((tm, HEAD_W), lambda bi, i: (i, 0)),
                  pl.BlockSpec((tm, HEAD_W), lambda bi, i: (i, 0)),
                  pl.BlockSpec((tm, HEAD_W), lambda bi, i: (i, 0))],
        out_specs=[pl.BlockSpec((1, tm, QK_W), tok),
                   pl.BlockSpec((1, tm, QK_W), tok),
                   pl.BlockSpec((1, tm, QK_W), tok),
                   pl.BlockSpec((1, tm, 2 * F_W), tok)],
        out_shape=[jax.ShapeDtypeStruct((b, s, QK_W), BF16),
                   jax.ShapeDtypeStruct((b, s, QK_W), BF16),
                   jax.ShapeDtypeStruct((b, s, QK_W), BF16),
                   jax.ShapeDtypeStruct((b, s, 2 * F_W), F32)],
        compiler_params=_cparams("parallel", "arbitrary"),
        name="inproj",
    )(x, mods, g_mix, w_in, wf, cos_t, sa_t, sb_t)


def _ctxproj_kernel(x_ref, mod_ref, g_ref, w_ref, k_ref, v_ref):
    m = mod_ref[0]
    h = _norm_mod(x_ref[0], g_ref[...], m[0:1], m[1:2]).astype(BF16)
    k_ref[0] = jnp.dot(h, w_ref[:, QK_W:2 * QK_W], preferred_element_type=F32).astype(BF16)
    v_ref[0] = jnp.dot(h, w_ref[:, 2 * QK_W:3 * QK_W], preferred_element_type=F32).astype(BF16)


def _ctxproj(ctx, mods, g_mix, w_in):
    b, n, d = ctx.shape
    return pl.pallas_call(
        _ctxproj_kernel,
        grid=(b,),
        in_specs=[pl.BlockSpec((1, n, d), lambda bi: (bi, 0, 0)),
                  pl.BlockSpec((1, N_MOD, d), lambda bi: (2, 0, 0)),
                  pl.BlockSpec((1, d), lambda bi: (0, 0)),
                  pl.BlockSpec(w_in.shape, lambda bi: (0, 0))],
        out_specs=[pl.BlockSpec((1, n, QK_W), lambda bi: (bi, 0, 0)),
                   pl.BlockSpec((1, n, QK_W), lambda bi: (bi, 0, 0))],
        out_shape=[jax.ShapeDtypeStruct((b, n, QK_W), BF16),
                   jax.ShapeDtypeStruct((b, n, QK_W), BF16)],
        compiler_params=_cparams("arbitrary"),
        name="ctxproj",
    )(ctx, mods, g_mix, w_in)


def _attn_kernel(q_ref, kc_ref, vc_ref, kl_ref, vl_ref, lam_ref, g_ref, o_ref,
                 m_ref, l_ref, acc_ref):
    q = q_ref[0]
    lane = lax.broadcasted_iota(jnp.int32, q.shape, 1)
    zero = jnp.zeros_like(q)
    q0 = jnp.where(lane < HEAD_DIM, q, zero)
    q1 = jnp.where(lane >= HEAD_DIM, q, zero)
    qs = (q0, q1)
    contract_last = (((1,), (1,)), ((), ()))
    m_ref[...] = jnp.full(m_ref.shape, -1e30, F32)
    l_ref[...] = jnp.zeros(l_ref.shape, F32)
    acc_ref[...] = jnp.zeros(acc_ref.shape, F32)

    def step(kb, vb):
        nk = kb.shape[0] // HEAD_W
        for mi in range(2):
            s = lax.dot_general(qs[mi], kb, contract_last, preferred_element_type=F32)
            m_old = m_ref[mi]
            m_new = jnp.maximum(m_old, jnp.max(s, axis=-1, keepdims=True))
            alpha = jnp.exp2(m_old - m_new)
            p = jnp.exp2(s - jnp.concatenate([m_new] * nk, axis=1))
            psum = p[:, 0:HEAD_W]
            for cblk in range(1, nk):
                psum = psum + p[:, cblk * HEAD_W:(cblk + 1) * HEAD_W]
            l_ref[mi] = alpha * l_ref[mi] + psum
            acc_ref[mi] = alpha * acc_ref[mi] + jnp.dot(p.astype(BF16), vb,
                                                       preferred_element_type=F32)
            m_ref[mi] = m_new

    step(kc_ref[0], vc_ref[0])

    def body(i, c):
        off = pl.multiple_of(i * TK, TK)
        step(kl_ref[0, pl.ds(off, TK), :], vl_ref[0, pl.ds(off, TK), :])
        return c

    lax.fori_loop(0, kl_ref.shape[1] // TK, body, 0)

    lp = lam_ref[...]
    t1 = jnp.sum(lp[0:1] * lp[1:2], axis=-1, keepdims=True)
    t2 = jnp.sum(lp[2:3] * lp[3:4], axis=-1, keepdims=True)
    lam = jnp.exp(t1) - jnp.exp(t2) + LAMBDA_INIT
    l0 = jnp.sum(l_ref[0], axis=-1, keepdims=True)
    l1 = jnp.sum(l_ref[1], axis=-1, keepdims=True)
    o = acc_ref[0] / l0 - lam * (acc_ref[1] / l1)
    ms = jnp.mean(o * o, axis=-1, keepdims=True)
    o = o * lax.rsqrt(ms + EPS) * g_ref[...] * (1.0 - LAMBDA_INIT)
    o_ref[0] = o.astype(BF16)


def _attention(q, kc, vc, kl, vl, lam_p, g_subln):
    b, s, _ = q.shape
    n_ctx = kc.shape[1]
    return pl.pallas_call(
        _attn_kernel,
        grid=(b, N_HEADS, s // TQ),
        in_specs=[pl.BlockSpec((1, TQ, HEAD_W), lambda bi, h, i: (bi, i, h)),
                  pl.BlockSpec((1, n_ctx, HEAD_W), lambda bi, h, i: (bi, 0, h)),
                  pl.BlockSpec((1, n_ctx, HEAD_W), lambda bi, h, i: (bi, 0, h)),
                  pl.BlockSpec((1, s, HEAD_W), lambda bi, h, i: (bi, 0, h)),
                  pl.BlockSpec((1, s, HEAD_W), lambda bi, h, i: (bi, 0, h)),
                  pl.BlockSpec((4, HEAD_DIM), lambda bi, h, i: (0, 0)),
                  pl.BlockSpec((1, HEAD_W), lambda bi, h, i: (0, 0))],
        out_specs=pl.BlockSpec((1, TQ, HEAD_W), lambda bi, h, i: (bi, i, h)),
        out_shape=jax.ShapeDtypeStruct((b, s, QK_W), BF16),
        scratch_shapes=[pltpu.VMEM((2, TQ, HEAD_W), F32)] * 3,
        compiler_params=_cparams("parallel", "parallel", "arbitrary"),
        name="diffattn",
    )(q, kc, vc, kl, vl, lam_p, g_subln)


def _dft1_kernel(y_ref, g_ref, br_ref, bi_ref):
    rows = DFT_N1 * F1_NB
    yb = y_ref[0].reshape(rows, 2 * F_W).astype(BF16)
    p = jnp.dot(g_ref[0], yb, preferred_element_type=F32)
    top, bot = p[:rows], p[rows:]
    br_ref[0] = (top[:, :F_W] - bot[:, F_W:]).reshape(DFT_N1, F1_NB, F_W)
    bi_ref[0] = (-top[:, F_W:] - bot[:, :F_W]).reshape(DFT_N1, F1_NB, F_W)


def _dft1(y4, gmat):
    b = y4.shape[0]
    rows = DFT_N1 * F1_NB
    return pl.pallas_call(
        _dft1_kernel,
        grid=(DFT_N2 // F1_NB, b),
        in_specs=[pl.BlockSpec((1, DFT_N1, F1_NB, 2 * F_W), lambda j, bi: (bi, 0, j, 0)),
                  pl.BlockSpec((1, 2 * rows, rows), lambda j, bi: (j, 0, 0))],
        out_specs=[pl.BlockSpec((1, DFT_N1, F1_NB, F_W), lambda j, bi: (bi, 0, j, 0)),
                   pl.BlockSpec((1, DFT_N1, F1_NB, F_W), lambda j, bi: (bi, 0, j, 0))],
        out_shape=[jax.ShapeDtypeStruct((b, DFT_N1, DFT_N2, F_W), F32),
                   jax.ShapeDtypeStruct((b, DFT_N1, DFT_N2, F_W), F32)],
        compiler_params=_cparams("arbitrary", "arbitrary"),
        name="dft1",
    )(y4, gmat)


def _dft2_kernel(br_ref, bi_ref, c_ref, s_ref, o_ref):
    for j in range(F2_KB):
        r = (jnp.dot(c_ref[...], br_ref[0, j].astype(BF16), preferred_element_type=F32)
             + jnp.dot(s_ref[...], bi_ref[0, j].astype(BF16), preferred_element_type=F32))
        o_ref[0, :, j, :] = r


def _dft2(br4, bi4, c2, s2):
    b = br4.shape[0]
    blk = (1, F2_KB, DFT_N2, F_W)
    return pl.pallas_call(
        _dft2_kernel,
        grid=(b, DFT_N1 // F2_KB),
        in_specs=[pl.BlockSpec(blk, lambda bi, j: (bi, j, 0, 0)),
                  pl.BlockSpec(blk, lambda bi, j: (bi, j, 0, 0)),
                  pl.BlockSpec((DFT_N2, DFT_N2), lambda bi, j: (0, 0)),
                  pl.BlockSpec((DFT_N2, DFT_N2), lambda bi, j: (0, 0))],
        out_specs=pl.BlockSpec((1, DFT_N2, F2_KB, F_W), lambda bi, j: (bi, 0, j, 0)),
        out_shape=jax.ShapeDtypeStruct((b, DFT_N2, DFT_N1, F_W), F32),
        compiler_params=_cparams("parallel", "arbitrary"),
        name="dft2",
    )(br4, bi4, c2, s2)


def _outproj_kernel(o_ref, f_ref, x_ref, mod_ref, g_ref, wo_ref, wr_ref, br_ref, tri_ref,
                    x1_ref, h3_ref, meta_ref, gate_ref, cnt_out_ref, cnt_ref):
    first = (pl.program_id(0) == 0) & (pl.program_id(1) == 0)

    @pl.when(first)
    def _():
        cnt_ref[...] = jnp.zeros(cnt_ref.shape, F32)

    m = mod_ref[0]
    mix = (jnp.dot(o_ref[0], wo_ref[0:QK_W, :], preferred_element_type=F32)
           + jnp.dot(f_ref[0].astype(BF16), wo_ref[QK_W:, :], preferred_element_type=F32))
    x1 = x_ref[0] + m[2:3] * mix
    x1_ref[0] = x1
    h2 = _norm_mod(x1, g_ref[...], m[3:4], m[4:5])
    _store_rows(h3_ref, h2)
    h_hi = h2.astype(BF16)
    h_lo = (h2 - h_hi.astype(F32)).astype(BF16)
    lg = (jnp.dot(h_hi, wr_ref[0], preferred_element_type=F32)
          + jnp.dot(h_lo, wr_ref[0], preferred_element_type=F32)
          + jnp.dot(h_hi, wr_ref[1], preferred_element_type=F32)
          + br_ref[...])
    lane = lax.broadcasted_iota(jnp.int32, lg.shape, 1)
    ninf = jnp.float32(-jnp.inf)
    big = jnp.int32(lg.shape[1])
    gl = jnp.where(lane < N_GROUPS, lg, ninf)
    gmax = jnp.max(gl, axis=-1, keepdims=True)
    grp = jnp.min(jnp.where(gl == gmax, lane, big), axis=-1, keepdims=True)
    pg = 1.0 / jnp.sum(jnp.exp(gl - gmax), axis=-1, keepdims=True)
    e_lane = lane - N_GROUPS
    emask = (e_lane >= 0) & (e_lane < N_EXPERTS) & ((e_lane >> 3) == grp)
    el = jnp.where(emask, lg, ninf)
    t1 = jnp.max(el, axis=-1, keepdims=True)
    i1 = jnp.min(jnp.where(el == t1, lane, big), axis=-1, keepdims=True)
    el2 = jnp.where(lane == i1, ninf, el)
    t2 = jnp.max(el2, axis=-1, keepdims=True)
    i2 = jnp.min(jnp.where(el2 == t2, lane, big), axis=-1, keepdims=True)
    dd = jnp.exp(t2 - t1)
    w1 = pg / (1.0 + dd)
    w2 = pg * dd / (1.0 + dd)
    gate_ref[...] = jnp.where(lane == 0, w1, jnp.where(lane == 1, w2, 0.0))
    hit1 = lane == i1
    hit2 = lane == i2
    oh = jnp.where(hit1 | hit2, 1.0, 0.0)
    before = jnp.dot(tri_ref[...], oh.astype(BF16), preferred_element_type=F32) + cnt_ref[0:1, :]
    r1 = jnp.sum(jnp.where(hit1, before, 0.0), axis=-1, keepdims=True).astype(jnp.int32)
    r2 = jnp.sum(jnp.where(hit2, before, 0.0), axis=-1, keepdims=True).astype(jnp.int32)
    cnt_ref[0:1, :] = cnt_ref[0:1, :] + jnp.sum(oh, axis=0, keepdims=True)
    cnt_out_ref[...] = cnt_ref[...]
    meta_ref[...] = jnp.where(lane == 0, i1 - N_GROUPS,
                              jnp.where(lane == 1, i2 - N_GROUPS,
                                        jnp.where(lane == 2, r1, jnp.where(lane == 3, r2, 0))))


def _outproj(attn_o, four, x, mods, g_ffn, w_out, w_r, b_r):
    b, s, d = x.shape
    tm = TM_PROJ
    nt = s // tm
    tok = lambda bi, i: (bi, i, 0)
    flat = lambda bi, i: (bi * nt + i, 0)
    tri = jnp.asarray(np.tril(np.ones((tm, tm), np.float32), -1)).astype(BF16)
    return pl.pallas_call(
        _outproj_kernel,
        grid=(b, nt),
        in_specs=[pl.BlockSpec((1, tm, QK_W), tok),
                  pl.BlockSpec((1, tm, F_W), tok),
                  pl.BlockSpec((1, tm, d), tok),
                  pl.BlockSpec((1, N_MOD, d), lambda bi, i: (bi, 0, 0)),
                  pl.BlockSpec((1, d), lambda bi, i: (0, 0)),
                  pl.BlockSpec(w_out.shape, lambda bi, i: (0, 0)),
                  pl.BlockSpec(w_r.shape, lambda bi, i: (0, 0, 0)),
                  pl.BlockSpec(b_r.shape, lambda bi, i: (0, 0)),
                  pl.BlockSpec((tm, tm), lambda bi, i: (0, 0))],
        out_specs=[pl.BlockSpec((1, tm, d), tok),
                   pl.BlockSpec((tm * ROW_TILE, 128), flat),
                   pl.BlockSpec((tm, 128), flat),
                   pl.BlockSpec((tm, 128), flat),
                   pl.BlockSpec((8, 128), lambda bi, i: (0, 0))],
        out_shape=[jax.ShapeDtypeStruct((b, s, d), F32),
                   jax.ShapeDtypeStruct((b * s * ROW_TILE, 128), F32),
                   jax.ShapeDtypeStruct((b * s, 128), jnp.int32),
                   jax.ShapeDtypeStruct((b * s, 128), F32),
                   jax.ShapeDtypeStruct((8, 128), F32)],
        scratch_shapes=[pltpu.VMEM((8, 128), F32)],
        compiler_params=_cparams("arbitrary", "arbitrary"),
        name="outproj",
    )(attn_o, four, x, mods, g_ffn, w_out, w_r, b_r, tri)


def _row_slice(row):
    return pl.ds(pl.multiple_of(row * ROW_TILE, ROW_TILE), ROW_TILE)


def _row_copy(src_ref, src_row, dst_ref, dst_row, sem):
    return pltpu.make_async_copy(src_ref.at[_row_slice(src_row)], dst_ref.at[_row_slice(dst_row)], sem)


def _rows_2d(ref, n_rows):
    return jnp.concatenate([ref[pl.ds(cblk, n_rows, stride=ROW_TILE), :]
                            for cblk in range(ROW_TILE)], axis=1)


def _store_rows(ref, val):
    for cblk in range(ROW_TILE):
        ref[pl.ds(cblk, val.shape[0], stride=ROW_TILE), :] = val[:, cblk * 128:(cblk + 1) * 128]


def _dispatch_kernel(dest_ref, h_ref, zeros_ref, xs_ref, sem):
    del zeros_ref
    n = 2 * TD
    step = pl.program_id(0)
    slot = step % 2
    tok0 = step * TD

    def issue(a, c):
        _row_copy(h_ref, tok0 + (a >> 1), xs_ref, dest_ref[0, 0, a], sem.at[slot]).start()
        return c

    lax.fori_loop(0, n, issue, 0, unroll=DMA_UNROLL)

    def drain_slot(s):
        def drain(a, c):
            _row_copy(h_ref, 0, xs_ref, 0, sem.at[s]).wait()
            return c

        lax.fori_loop(0, n, drain, 0, unroll=DMA_UNROLL)

    @pl.when(step > 0)
    def _():
        drain_slot(1 - slot)

    @pl.when(step == pl.num_programs(0) - 1)
    def _():
        drain_slot(slot)


def _dispatch(dest3, h3, xs_zeros):
    t = h3.shape[0] // ROW_TILE
    return pl.pallas_call(
        _dispatch_kernel,
        grid=(t // TD,),
        in_specs=[pl.BlockSpec((1, 1, 2 * TD), lambda i: (i, 0, 0), memory_space=pltpu.SMEM),
                  pl.BlockSpec(memory_space=pl.ANY),
                  pl.BlockSpec(memory_space=pl.ANY)],
        out_specs=pl.BlockSpec(memory_space=pl.ANY),
        out_shape=jax.ShapeDtypeStruct(xs_zeros.shape, xs_zeros.dtype),
        scratch_shapes=[pltpu.SemaphoreType.DMA((2,))],
        input_output_aliases={2: 0},
        compiler_params=_cparams("arbitrary"),
        name="dispatch",
    )(dest3, h3, xs_zeros)


def _experts_kernel(be_ref, nu_ref, xs_ref, wg_ref, wu_ref, wd_ref, ys_ref, wgb, wub, wdb):
    j = pl.program_id(0)
    used = j < nu_ref[0]
    new_expert = (j == 0) | (be_ref[j] != be_ref[jnp.maximum(j - 1, 0)])

    @pl.when(used & new_expert)
    def _():
        wgb[...] = wg_ref[0].astype(BF16)
        wub[...] = wu_ref[0].astype(BF16)
        wdb[...] = wd_ref[0].astype(BF16)

    @pl.when(used)
    def _():
        xb = _rows_2d(xs_ref, MOE_BLK).astype(BF16)
        gate = jnp.dot(xb, wgb[...], preferred_element_type=F32)
        up = jnp.dot(xb, wub[...], preferred_element_type=F32)
        hid = (gate * jax.nn.sigmoid(gate) * up).astype(BF16)
        _store_rows(ys_ref, jnp.dot(hid, wdb[...], preferred_element_type=F32))

    @pl.when(pl.program_id(0) >= nu_ref[0])
    def _():
        ys_ref[...] = jnp.zeros_like(ys_ref)


def _experts(blk_expert, n_used, xs, wg, wu, wd):
    rows = xs.shape[0] // ROW_TILE
    d = wg.shape[1]
    nb = rows // MOE_BLK
    row_blk = lambda j, be, nu: (jnp.minimum(j, nu[0] - 1), 0)
    out_blk = lambda j, be, nu: (j, 0)
    w_blk = lambda j, be, nu: (be[jnp.minimum(j, nu[0] - 1)], 0, 0)
    return pl.pallas_call(
        _experts_kernel,
        grid_spec=pltpu.PrefetchScalarGridSpec(
            num_scalar_prefetch=2,
            grid=(nb,),
            in_specs=[pl.BlockSpec((MOE_BLK * ROW_TILE, 128), row_blk),
                      pl.BlockSpec((1, d, D_EXPERT), w_blk),
                      pl.BlockSpec((1, d, D_EXPERT), w_blk),
                      pl.BlockSpec((1, D_EXPERT, d), w_blk)],
            out_specs=pl.BlockSpec((MOE_BLK * ROW_TILE, 128), out_blk),
            scratch_shapes=[pltpu.VMEM((d, D_EXPERT), BF16), pltpu.VMEM((d, D_EXPERT), BF16),
                            pltpu.VMEM((D_EXPERT, d), BF16)]),
        out_shape=jax.ShapeDtypeStruct(xs.shape, F32),
        compiler_params=_cparams("arbitrary"),
        name="experts",
    )(blk_expert, n_used, xs, wg, wu, wd)


def _combine_kernel(dest_ref, dest_next_ref, ys_ref, x1_ref, gate_ref, mod_ref, g_ref, o_ref,
                    ya, yb, sem):
    step = pl.program_id(0) * pl.num_programs(1) + pl.program_id(1)
    n_steps = pl.num_programs(0) * pl.num_programs(1)
    slot = step % 2

    def start_gathers(idx_ref, to_slot):
        def issue(r, c):
            _row_copy(ys_ref, idx_ref[0, 0, 2 * r], ya.at[to_slot], r, sem.at[to_slot]).start()
            _row_copy(ys_ref, idx_ref[0, 0, 2 * r + 1], yb.at[to_slot], r, sem.at[to_slot]).start()
            return c

        lax.fori_loop(0, TD, issue, 0, unroll=DMA_UNROLL)

    @pl.when(step == 0)
    def _():
        start_gathers(dest_ref, 0)

    @pl.when(step + 1 < n_steps)
    def _():
        start_gathers(dest_next_ref, 1 - slot)

    def drain(r, c):
        _row_copy(ys_ref, 0, ya.at[slot], 0, sem.at[slot]).wait()
        _row_copy(ys_ref, 0, yb.at[slot], 0, sem.at[slot]).wait()
        return c

    lax.fori_loop(0, TD, drain, 0, unroll=DMA_UNROLL)
    gt = gate_ref[...]
    moe = gt[:, 0:1] * _rows_2d(ya.at[slot], TD) + gt[:, 1:2] * _rows_2d(yb.at[slot], TD)
    x2 = x1_ref[0] + mod_ref[0][5:6] * moe
    ms = jnp.mean(x2 * x2, axis=-1, keepdims=True)
    o_ref[0] = x2 * lax.rsqrt(ms + EPS) * g_ref[...]


def _combine(dest3, ys, x1, gates, mods, g_final):
    b, s, d = x1.shape
    nt = s // TD
    return pl.pallas_call(
        _combine_kernel,
        grid=(b, nt),
        in_specs=[pl.BlockSpec((1, 1, 2 * TD), lambda bi, i: (bi * nt + i, 0, 0),
                               memory_space=pltpu.SMEM),
                  pl.BlockSpec((1, 1, 2 * TD),
                               lambda bi, i: (jnp.minimum(bi * nt + i + 1, b * nt - 1), 0, 0),
                               memory_space=pltpu.SMEM),
                  pl.BlockSpec(memory_space=pl.ANY),
                  pl.BlockSpec((1, TD, d), lambda bi, i: (bi, i, 0)),
                  pl.BlockSpec((TD, 128), lambda bi, i: (bi * nt + i, 0)),
                  pl.BlockSpec((1, N_MOD, d), lambda bi, i: (bi, 0, 0)),
                  pl.BlockSpec((1, d), lambda bi, i: (0, 0))],
        out_specs=pl.BlockSpec((1, TD, d), lambda bi, i: (bi, i, 0)),
        out_shape=jax.ShapeDtypeStruct((b, s, d), F32),
        scratch_shapes=[pltpu.VMEM((2, TD * ROW_TILE, 128), F32),
                        pltpu.VMEM((2, TD * ROW_TILE, 128), F32),
                        pltpu.SemaphoreType.DMA((2,))],
        compiler_params=_cparams("arbitrary", "arbitrary"),
        name="combine",
    )(dest3, dest3, ys, x1, gates, mods, g_final)


@functools.lru_cache(maxsize=None)
def _rope_tables(rows):
    r, col = np.meshgrid(np.arange(rows), np.arange(GRID_W), indexing='ij')
    pos = np.stack([r.reshape(-1), col.reshape(-1)], axis=-1).astype(np.float32)
    inv_freq = (np.float32(ROPE_THETA)
                ** (-np.arange(0, ROPE_AXIS, 2, dtype=np.float32) / np.float32(ROPE_AXIS))).astype(np.float32)
    ang = (pos[:, :, None] * inv_freq).astype(np.float32)
    ang = np.concatenate([ang, ang], axis=-1).astype(np.float64)
    n = ang.shape[0]
    cos = np.tile(np.cos(ang).reshape(n, HEAD_DIM), (1, 2)).astype(np.float32)
    sin = np.tile(np.sin(ang).reshape(n, HEAD_DIM), (1, 2)).astype(np.float32)
    upper = (np.arange(HEAD_W) % ROPE_AXIS) >= ROPE_HALF
    sa = np.where(upper, sin, np.float32(0.0))
    sb = np.where(upper, np.float32(0.0), -sin)
    return cos, sa, sb


@functools.lru_cache(maxsize=None)
def _dft_constants(n_pos):
    c = np.arange(FGROUP_DIM)
    ang_c = 2.0 * np.pi * ((c[:, None] * c[None, :]) % FGROUP_DIM) / FGROUP_DIM
    norm = 1.0 / math.sqrt(n_pos * FGROUP_DIM)
    cmat = (np.cos(ang_c) * norm).astype(np.float32)
    smat = (np.sin(ang_c) * norm).astype(np.float32)
    k1 = np.arange(DFT_N1)[None, :, None]
    n1 = np.arange(DFT_N1)[None, None, :]
    n2 = np.arange(DFT_N2)[:, None, None]
    ang_g = 2.0 * np.pi * ((k1 * (DFT_N2 * n1 + n2)) % n_pos) / n_pos
    gsmall = np.stack([np.cos(ang_g), np.sin(ang_g)], axis=1)
    gsmall = gsmall.reshape(DFT_N2 // F1_NB, F1_NB, 2, DFT_N1, DFT_N1)
    gmat = np.einsum('japkn,ab->jpkanb', gsmall, np.eye(F1_NB)).reshape(
        DFT_N2 // F1_NB, 2 * DFT_N1 * F1_NB, DFT_N1 * F1_NB).astype(BF16)
    k2 = np.arange(DFT_N2)
    ang_2 = 2.0 * np.pi * ((k2[:, None] * k2[None, :]) % DFT_N2) / DFT_N2
    c2 = np.cos(ang_2).astype(np.float32)
    s2 = np.sin(ang_2).astype(np.float32)
    return cmat, smat, gmat, c2, s2


def kernel(x, c, ctx, c_ctx, w_ada, b_ada, g_mix_norm, g_ffn_norm, w_in, lambda_q1, lambda_k1, lambda_q2, lambda_k2, g_subln, w_fourier, w_out, w_router_group, b_router_group, w_router_expert, b_router_expert, w_gate, w_up, w_down, g_final):
    b, s, d = x.shape
    t = b * s
    assert d == D_MODEL and s == DFT_N1 * DFT_N2 and s % GRID_W == 0 and b == 2

    cc = jnp.concatenate([c, c_ctx[None, :], jnp.zeros((8 - b - 1, d), F32)], axis=0)
    mods = _adaln(cc, w_ada[0], b_ada[0]).reshape(8, N_MOD, d)

    cmat, smat, gmat, c2, s2 = _dft_constants(s)
    wf = _wfold(jnp.asarray(cmat), jnp.asarray(smat), w_fourier[0])
    cos_t, sa_t, sb_t = _rope_tables(s // GRID_W)

    w_in_b = w_in[0].astype(BF16)
    g_mix = g_mix_norm[0].reshape(1, d)
    q, kl, vl, y = _inproj(x, mods, g_mix, w_in_b, wf, cos_t, sa_t, sb_t)
    kc, vc = _ctxproj(ctx, mods, g_mix, w_in_b)

    lam_p = jnp.stack([lambda_q1[0], lambda_k1[0], lambda_q2[0], lambda_k2[0]], axis=0)
    attn_o = _attention(q, kc, vc, kl, vl, lam_p, g_subln[0].reshape(1, HEAD_W))

    br, bi = _dft1(y.reshape(b, DFT_N1, DFT_N2, 2 * F_W), jnp.asarray(gmat))
    four = _dft2(br, bi, jnp.asarray(c2).astype(BF16),
                 jnp.asarray(s2).astype(BF16)).reshape(b, s, F_W)

    n_r = N_GROUPS + N_EXPERTS
    w_r = jnp.concatenate([w_router_group[0], w_router_expert[0],
                           jnp.zeros((d, 128 - n_r), F32)], axis=1)
    b_r = jnp.concatenate([b_router_group[0], b_router_expert[0],
                           jnp.zeros((128 - n_r,), F32)]).reshape(1, 128)
    w_r_hi = w_r.astype(BF16)
    w_r_lo = (w_r - w_r_hi.astype(F32)).astype(BF16)
    x1, h3, meta, gates, cnt = _outproj(attn_o, four, x, mods, g_ffn_norm[0].reshape(1, d),
                                        w_out[0].astype(BF16), jnp.stack([w_r_hi, w_r_lo]), b_r)

    e_flat = meta[:, 0:TOP_K].reshape(t * TOP_K)
    rank = meta[:, TOP_K:2 * TOP_K].reshape(t * TOP_K)
    counts = cnt[0, N_GROUPS:N_GROUPS + N_EXPERTS].astype(jnp.int32)
    nblk = (counts + MOE_BLK - 1) // MOE_BLK
    blk_end = jnp.cumsum(nblk)
    blk_start = blk_end - nblk
    dest = (jnp.take(blk_start * MOE_BLK, e_flat) + rank).astype(jnp.int32)
    n_blocks = t * TOP_K // MOE_BLK + N_EXPERTS
    blk_ids = jnp.arange(n_blocks, dtype=jnp.int32)
    blk_expert = jnp.minimum(
        jnp.sum((blk_end[None, :] <= blk_ids[:, None]).astype(jnp.int32), axis=1),
        N_EXPERTS - 1).astype(jnp.int32)
    n_used = blk_end[-1:].astype(jnp.int32)
    dest3 = dest.reshape(t // TD, 1, 2 * TD)

    xs = _dispatch(dest3, h3, jnp.zeros((n_blocks * MOE_BLK * ROW_TILE, 128), F32))
    ys = _experts(blk_expert, n_used, xs, w_gate[0], w_up[0], w_down[0])
    return _combine(dest3, ys, x1, gates, mods, g_final.reshape(1, d))
```

```python
import functools
import math

import numpy as np
import jax
import jax.numpy as jnp
from jax import lax
from jax.experimental import pallas as pl
from jax.experimental.pallas import tpu as pltpu

F32 = jnp.float32
BF16 = jnp.bfloat16

D_MODEL = 1024
GRID_W = 64
N_HEADS = 4
HEAD_DIM = 64
HEAD_W = 2 * HEAD_DIM
QK_W = N_HEADS * HEAD_W
N_FGROUPS = 4
FGROUP_DIM = 128
F_W = N_FGROUPS * FGROUP_DIM
ROPE_THETA = 10000.0
ROPE_AXIS = HEAD_DIM // 2
ROPE_HALF = ROPE_AXIS // 2
N_GROUPS = 4
EXPERTS_PER_GROUP = 8
N_EXPERTS = N_GROUPS * EXPERTS_PER_GROUP
TOP_K = 2
D_EXPERT = 512
N_MOD = 6
EPS = 1e-6
LAMBDA_INIT = 0.8 - 0.6 * math.exp(-0.3 * 0)
LOG2_E = 1.4426950408889634

DFT_N1 = 64
DFT_N2 = 128

TM_PROJ = 512
TQ = 1024
TK = 1024
F1_NB = 8
F2_KB = 8
MOE_BLK = 256
ROW_TILE = D_MODEL // 256
U32 = jnp.uint32
TD = 256
DMA_UNROLL = 16
VMEM_LIMIT = 48 * 1024 * 1024


def _cparams(*sem):
    return pltpu.CompilerParams(dimension_semantics=sem, vmem_limit_bytes=VMEM_LIMIT)


def _adaln_kernel(c_ref, w_ref, b_ref, o_ref):
    cc = c_ref[...]
    s = cc * jax.nn.sigmoid(cc)
    o_ref[...] = jnp.dot(s, w_ref[...], preferred_element_type=F32,
                         precision=lax.Precision.HIGHEST) + b_ref[...]


def _adaln(cc, w_ada, b_ada):
    n = w_ada.shape[1]
    tn = 1536
    return pl.pallas_call(
        _adaln_kernel,
        grid=(n // tn,),
        in_specs=[pl.BlockSpec((8, D_MODEL), lambda j: (0, 0)),
                  pl.BlockSpec((D_MODEL, tn), lambda j: (0, j)),
                  pl.BlockSpec((1, tn), lambda j: (0, j))],
        out_specs=pl.BlockSpec((8, tn), lambda j: (0, j)),
        out_shape=jax.ShapeDtypeStruct((8, n), F32),
        compiler_params=_cparams("arbitrary"),
        name="adaln",
    )(cc, w_ada, b_ada.reshape(1, n))


def _wfold_kernel(c_ref, s_ref, w_ref, o_ref):
    w = w_ref[0]
    o_ref[0, :, :FGROUP_DIM] = jnp.dot(c_ref[...], w, preferred_element_type=F32,
                                       precision=lax.Precision.HIGHEST).astype(BF16)
    o_ref[0, :, FGROUP_DIM:] = jnp.dot(s_ref[...], w, preferred_element_type=F32,
                                       precision=lax.Precision.HIGHEST).astype(BF16)


def _wfold(cmat, smat, w_fourier):
    return pl.pallas_call(
        _wfold_kernel,
        grid=(N_FGROUPS,),
        in_specs=[pl.BlockSpec((FGROUP_DIM, FGROUP_DIM), lambda g: (0, 0)),
                  pl.BlockSpec((FGROUP_DIM, FGROUP_DIM), lambda g: (0, 0)),
                  pl.BlockSpec((1, FGROUP_DIM, FGROUP_DIM), lambda g: (g, 0, 0))],
        out_specs=pl.BlockSpec((1, FGROUP_DIM, 2 * FGROUP_DIM), lambda g: (g, 0, 0)),
        out_shape=jax.ShapeDtypeStruct((N_FGROUPS, FGROUP_DIM, 2 * FGROUP_DIM), BF16),
        compiler_params=_cparams("arbitrary"),
        name="wfold",
    )(cmat, smat, w_fourier)


def _norm_mod(x, g, shift, scale):
    ms = jnp.mean(x * x, axis=-1, keepdims=True)
    y = x * lax.rsqrt(ms + EPS) * g
    return y * (1.0 + scale) + shift


def _rope_slab(p, cos, sa, sb):
    return (p * cos + pltpu.roll(p, ROPE_HALF, 1) * sa
            + pltpu.roll(p, HEAD_W - ROPE_HALF, 1) * sb)


def _inproj_kernel(x_ref, mod_ref, g_ref, w_ref, wf_ref, cos_ref, sa_ref, sb_ref,
                   q_ref, k_ref, v_ref, y_ref):
    m = mod_ref[0]
    h = _norm_mod(x_ref[0], g_ref[...], m[0:1], m[1:2]).astype(BF16)
    cos, sa, sb = cos_ref[...], sa_ref[...], sb_ref[...]
    scale = HEAD_DIM ** -0.5 * LOG2_E
    pq = jnp.dot(h, w_ref[:, 0:QK_W], preferred_element_type=F32)
    for hh in range(N_HEADS):
        sl = slice(hh * HEAD_W, (hh + 1) * HEAD_W)
        q_ref[0, :, sl] = (_rope_slab(pq[:, sl], cos, sa, sb) * scale).astype(BF16)
    pk = jnp.dot(h, w_ref[:, QK_W:2 * QK_W], preferred_element_type=F32)
    for hh in range(N_HEADS):
        sl = slice(hh * HEAD_W, (hh + 1) * HEAD_W)
        k_ref[0, :, sl] = _rope_slab(pk[:, sl], cos, sa, sb).astype(BF16)
    v_ref[0] = jnp.dot(h, w_ref[:, 2 * QK_W:3 * QK_W], preferred_element_type=F32).astype(BF16)
    pf = jnp.dot(h, w_ref[:, 3 * QK_W:], preferred_element_type=F32).astype(BF16)
    for g in range(N_FGROUPS):
        yy = jnp.dot(pf[:, g * FGROUP_DIM:(g + 1) * FGROUP_DIM], wf_ref[g],
                     preferred_element_type=F32)
        y_ref[0, :, g * FGROUP_DIM:(g + 1) * FGROUP_DIM] = yy[:, :FGROUP_DIM]
        y_ref[0, :, F_W + g * FGROUP_DIM:F_W + (g + 1) * FGROUP_DIM] = yy[:, FGROUP_DIM:]


def _inproj(x, mods, g_mix, w_in, wf, cos_t, sa_t, sb_t):
    b, s, d = x.shape
    tm = TM_PROJ
    tok = lambda bi, i: (bi, i, 0)
    return pl.pallas_call(
        _inproj_kernel,
        grid=(b, s // tm),
        in_specs=[pl.BlockSpec((1, tm, d), tok),
                  pl.BlockSpec((1, N_MOD, d), lambda bi, i: (bi, 0, 0)),
                  pl.BlockSpec((1, d), lambda bi, i: (0, 0)),
                  pl.BlockSpec(w_in.shape, lambda bi, i: (0, 0)),
                  pl.BlockSpec(wf.shape, lambda bi, i: (0, 0, 0)),
                  pl.BlockSpec((tm, HEAD_W), lambda bi, i: (i, 0)),
                  pl.BlockSpec((tm, HEAD_W), lambda bi, i: (i, 0)),
                  pl.BlockSpec((tm, HEAD_W), lambda bi, i: (i, 0))],
        out_specs=[pl.BlockSpec((1, tm, QK_W), tok),
                   pl.BlockSpec((1, tm, QK_W), tok),
                   pl.BlockSpec((1, tm, QK_W), tok),
                   pl.BlockSpec((1, tm, 2 * F_W), tok)],
        out_shape=[jax.ShapeDtypeStruct((b, s, QK_W), BF16),
                   jax.ShapeDtypeStruct((b, s, QK_W), BF16),
                   jax.ShapeDtypeStruct((b, s, QK_W), BF16),
                   jax.ShapeDtypeStruct((b, s, 2 * F_W), F32)],
        compiler_params=_cparams("parallel", "arbitrary"),
        name="inproj",
    )(x, mods, g_mix, w_in, wf, cos_t, sa_t, sb_t)


def _ctxproj_kernel(x_ref, mod_ref, g_ref, w_ref, k_ref, v_ref):
    m = mod_ref[0]
    h = _norm_mod(x_ref[0], g_ref[...], m[0:1], m[1:2]).astype(BF16)
    k_ref[0] = jnp.dot(h, w_ref[:, QK_W:2 * QK_W], preferred_element_type=F32).astype(BF16)
    v_ref[0] = jnp.dot(h, w_ref[:, 2 * QK_W:3 * QK_W], preferred_element_type=F32).astype(BF16)


def _ctxproj(ctx, mods, g_mix, w_in):
    b, n, d = ctx.shape
    return pl.pallas_call(
        _ctxproj_kernel,
        grid=(b,),
        in_specs=[pl.BlockSpec((1, n, d), lambda bi: (bi, 0, 0)),
                  pl.BlockSpec((1, N_MOD, d), lambda bi: (2, 0, 0)),
                  pl.BlockSpec((1, d), lambda bi: (0, 0)),
                  pl.BlockSpec(w_in.shape, lambda bi: (0, 0))],
        out_specs=[pl.BlockSpec((1, n, QK_W), lambda bi: (bi, 0, 0)),
                   pl.BlockSpec((1, n, QK_W), lambda bi: (bi, 0, 0))],
        out_shape=[jax.ShapeDtypeStruct((b, n, QK_W), BF16),
                   jax.ShapeDtypeStruct((b, n, QK_W), BF16)],
        compiler_params=_cparams("arbitrary"),
        name="ctxproj",
    )(ctx, mods, g_mix, w_in)


def _attn_kernel(q_ref, kc_ref, vc_ref, kl_ref, vl_ref, lam_ref, g_ref, o_ref,
                 m_ref, l_ref, acc_ref):
    q = q_ref[0]
    lane = lax.broadcasted_iota(jnp.int32, q.shape, 1)
    zero = jnp.zeros_like(q)
    q0 = jnp.where(lane < HEAD_DIM, q, zero)
    q1 = jnp.where(lane >= HEAD_DIM, q, zero)
    qs = (q0, q1)
    contract_last = (((1,), (1,)), ((), ()))
    m_ref[...] = jnp.full(m_ref.shape, -1e30, F32)
    l_ref[...] = jnp.zeros(l_ref.shape, F32)
    acc_ref[...] = jnp.zeros(acc_ref.shape, F32)

    def step(kb, vb):
        nk = kb.shape[0] // HEAD_W
        for mi in range(2):
            s = lax.dot_general(qs[mi], kb, contract_last, preferred_element_type=F32)
            m_old = m_ref[mi]
            m_new = jnp.maximum(m_old, jnp.max(s, axis=-1, keepdims=True))
            alpha = jnp.exp2(m_old - m_new)
            p = jnp.exp2(s - jnp.concatenate([m_new] * nk, axis=1))
            psum = p[:, 0:HEAD_W]
            for cblk in range(1, nk):
                psum = psum + p[:, cblk * HEAD_W:(cblk + 1) * HEAD_W]
            l_ref[mi] = alpha * l_ref[mi] + psum
            acc_ref[mi] = alpha * acc_ref[mi] + jnp.dot(p.astype(BF16), vb,
                                                       preferred_element_type=F32)
            m_ref[mi] = m_new

    step(kc_ref[0], vc_ref[0])

    def body(i, c):
        off = pl.multiple_of(i * TK, TK)
        step(kl_ref[0, pl.ds(off, TK), :], vl_ref[0, pl.ds(off, TK), :])
        return c

    lax.fori_loop(0, kl_ref.shape[1] // TK, body, 0)

    lp = lam_ref[...]
    t1 = jnp.sum(lp[0:1] * lp[1:2], axis=-1, keepdims=True)
    t2 = jnp.sum(lp[2:3] * lp[3:4], axis=-1, keepdims=True)
    lam = jnp.exp(t1) - jnp.exp(t2) + LAMBDA_INIT
    l0 = jnp.sum(l_ref[0], axis=-1, keepdims=True)
    l1 = jnp.sum(l_ref[1], axis=-1, keepdims=True)
    o = acc_ref[0] / l0 - lam * (acc_ref[1] / l1)
    ms = jnp.mean(o * o, axis=-1, keepdims=True)
    o = o * lax.rsqrt(ms + EPS) * g_ref[...] * (1.0 - LAMBDA_INIT)
    o_ref[0] = o.astype(BF16)


def _attention(q, kc, vc, kl, vl, lam_p, g_subln):
    b, s, _ = q.shape
    n_ctx = kc.shape[1]
    return pl.pallas_call(
        _attn_kernel,
        grid=(b, N_HEADS, s // TQ),
        in_specs=[pl.BlockSpec((1, TQ, HEAD_W), lambda bi, h, i: (bi, i, h)),
                  pl.BlockSpec((1, n_ctx, HEAD_W), lambda bi, h, i: (bi, 0, h)),
                  pl.BlockSpec((1, n_ctx, HEAD_W), lambda bi, h, i: (bi, 0, h)),
                  pl.BlockSpec((1, s, HEAD_W), lambda bi, h, i: (bi, 0, h)),
                  pl.BlockSpec((1, s, HEAD_W), lambda bi, h, i: (bi, 0, h)),
                  pl.BlockSpec((4, HEAD_DIM), lambda bi, h, i: (0, 0)),
                  pl.BlockSpec((1, HEAD_W), lambda bi, h, i: (0, 0))],
        out_specs=pl.BlockSpec((1, TQ, HEAD_W), lambda bi, h, i: (bi, i, h)),
        out_shape=jax.ShapeDtypeStruct((b, s, QK_W), BF16),
        scratch_shapes=[pltpu.VMEM((2, TQ, HEAD_W), F32)] * 3,
        compiler_params=_cparams("parallel", "parallel", "arbitrary"),
        name="diffattn",
    )(q, kc, vc, kl, vl, lam_p, g_subln)


def _dft1_kernel(y_ref, g_ref, br_ref, bi_ref):
    rows = DFT_N1 * F1_NB
    yb = y_ref[0].reshape(rows, 2 * F_W).astype(BF16)
    p = jnp.dot(g_ref[0], yb, preferred_element_type=F32)
    top, bot = p[:rows], p[rows:]
    br_ref[0] = (top[:, :F_W] - bot[:, F_W:]).reshape(DFT_N1, F1_NB, F_W)
    bi_ref[0] = (-top[:, F_W:] - bot[:, :F_W]).reshape(DFT_N1, F1_NB, F_W)


def _dft1(y4, gmat):
    b = y4.shape[0]
    rows = DFT_N1 * F1_NB
    return pl.pallas_call(
        _dft1_kernel,
        grid=(DFT_N2 // F1_NB, b),
        in_specs=[pl.BlockSpec((1, DFT_N1, F1_NB, 2 * F_W), lambda j, bi: (bi, 0, j, 0)),
                  pl.BlockSpec((1, 2 * rows, rows), lambda j, bi: (j, 0, 0))],
        out_specs=[pl.BlockSpec((1, DFT_N1, F1_NB, F_W), lambda j, bi: (bi, 0, j, 0)),
                   pl.BlockSpec((1, DFT_N1, F1_NB, F_W), lambda j, bi: (bi, 0, j, 0))],
        out_shape=[jax.ShapeDtypeStruct((b, DFT_N1, DFT_N2, F_W), F32),
                   jax.ShapeDtypeStruct((b, DFT_N1, DFT_N2, F_W), F32)],
        compiler_params=_cparams("arbitrary", "arbitrary"),
        name="dft1",
    )(y4, gmat)


def _dft2_kernel(br_ref, bi_ref, c_ref, s_ref, o_ref):
    for j in range(F2_KB):
        r = (jnp.dot(c_ref[...], br_ref[0, j].astype(BF16), preferred_element_type=F32)
             + jnp.dot(s_ref[...], bi_ref[0, j].astype(BF16), preferred_element_type=F32))
        o_ref[0, :, j, :] = r


def _dft2(br4, bi4, c2, s2):
    b = br4.shape[0]
    blk = (1, F2_KB, DFT_N2, F_W)
    return pl.pallas_call(
        _dft2_kernel,
        grid=(b, DFT_N1 // F2_KB),
        in_specs=[pl.BlockSpec(blk, lambda bi, j: (bi, j, 0, 0)),
                  pl.BlockSpec(blk, lambda bi, j: (bi, j, 0, 0)),
                  pl.BlockSpec((DFT_N2, DFT_N2), lambda bi, j: (0, 0)),
                  pl.BlockSpec((DFT_N2, DFT_N2), lambda bi, j: (0, 0))],
        out_specs=pl.BlockSpec((1, DFT_N2, F2_KB, F_W), lambda bi, j: (bi, 0, j, 0)),
        out_shape=jax.ShapeDtypeStruct((b, DFT_N2, DFT_N1, F_W), F32),
        compiler_params=_cparams("parallel", "arbitrary"),
        name="dft2",
    )(br4, bi4, c2, s2)


def _outproj_kernel(o_ref, f_ref, x_ref, mod_ref, g_ref, wo_ref, wr_ref, br_ref, tri_ref,
                    x1_ref, h3_ref, meta_ref, gate_ref, cnt_out_ref, cnt_ref):
    first = (pl.program_id(0) == 0) & (pl.program_id(1) == 0)

    @pl.when(first)
    def _():
        cnt_ref[...] = jnp.zeros(cnt_ref.shape, F32)

    m = mod_ref[0]
    mix = (jnp.dot(o_ref[0], wo_ref[0:QK_W, :], preferred_element_type=F32)
           + jnp.dot(f_ref[0].astype(BF16), wo_ref[QK_W:, :], preferred_element_type=F32))
    x1 = x_ref[0] + m[2:3] * mix
    x1_ref[0] = x1
    h2 = _norm_mod(x1, g_ref[...], m[3:4], m[4:5])
    _store_rows(h3_ref, h2)
    h_hi = h2.astype(BF16)
    h_lo = (h2 - h_hi.astype(F32)).astype(BF16)
    lg = (jnp.dot(h_hi, wr_ref[0], preferred_element_type=F32)
          + jnp.dot(h_lo, wr_ref[0], preferred_element_type=F32)
          + jnp.dot(h_hi, wr_ref[1], preferred_element_type=F32)
          + br_ref[...])
    lane = lax.broadcasted_iota(jnp.int32, lg.shape, 1)
    ninf = jnp.float32(-jnp.inf)
    big = jnp.int32(lg.shape[1])
    gl = jnp.where(lane < N_GROUPS, lg, ninf)
    gmax = jnp.max(gl, axis=-1, keepdims=True)
    grp = jnp.min(jnp.where(gl == gmax, lane, big), axis=-1, keepdims=True)
    pg = 1.0 / jnp.sum(jnp.exp(gl - gmax), axis=-1, keepdims=True)
    e_lane = lane - N_GROUPS
    emask = (e_lane >= 0) & (e_lane < N_EXPERTS) & ((e_lane >> 3) == grp)
    el = jnp.where(emask, lg, ninf)
    t1 = jnp.max(el, axis=-1, keepdims=True)
    i1 = jnp.min(jnp.where(el == t1, lane, big), axis=-1, keepdims=True)
    el2 = jnp.where(lane == i1, ninf, el)
    t2 = jnp.max(el2, axis=-1, keepdims=True)
    i2 = jnp.min(jnp.where(el2 == t2, lane, big), axis=-1, keepdims=True)
    dd = jnp.exp(t2 - t1)
    w1 = pg / (1.0 + dd)
    w2 = pg * dd / (1.0 + dd)
    gate_ref[...] = jnp.where(lane == 0, w1, jnp.where(lane == 1, w2, 0.0))
    hit1 = lane == i1
    hit2 = lane == i2
    oh = jnp.where(hit1 | hit2, 1.0, 0.0)
    before = jnp.dot(tri_ref[...], oh.astype(BF16), preferred_element_type=F32) + cnt_ref[0:1, :]
    r1 = jnp.sum(jnp.where(hit1, before, 0.0), axis=-1, keepdims=True).astype(jnp.int32)
    r2 = jnp.sum(jnp.where(hit2, before, 0.0), axis=-1, keepdims=True).astype(jnp.int32)
    cnt_ref[0:1, :] = cnt_ref[0:1, :] + jnp.sum(oh, axis=0, keepdims=True)
    cnt_out_ref[...] = cnt_ref[...]
    meta_ref[...] = jnp.where(lane == 0, i1 - N_GROUPS,
                              jnp.where(lane == 1, i2 - N_GROUPS,
                                        jnp.where(lane == 2, r1, jnp.where(lane == 3, r2, 0))))


def _outproj(attn_o, four, x, mods, g_ffn, w_out, w_r, b_r):
    b, s, d = x.shape
    tm = TM_PROJ
    nt = s // tm
    tok = lambda bi, i: (bi, i, 0)
    flat = lambda bi, i: (bi * nt + i, 0)
    tri = jnp.asarray(np.tril(np.ones((tm, tm), np.float32), -1)).astype(BF16)
    return pl.pallas_call(
        _outproj_kernel,
        grid=(b, nt),
        in_specs=[pl.BlockSpec((1, tm, QK_W), tok),
                  pl.BlockSpec((1, tm, F_W), tok),
                  pl.BlockSpec((1, tm, d), tok),
                  pl.BlockSpec((1, N_MOD, d), lambda bi, i: (bi, 0, 0)),
                  pl.BlockSpec((1, d), lambda bi, i: (0, 0)),
                  pl.BlockSpec(w_out.shape, lambda bi, i: (0, 0)),
                  pl.BlockSpec(w_r.shape, lambda bi, i: (0, 0, 0)),
                  pl.BlockSpec(b_r.shape, lambda bi, i: (0, 0)),
                  pl.BlockSpec((tm, tm), lambda bi, i: (0, 0))],
        out_specs=[pl.BlockSpec((1, tm, d), tok),
                   pl.BlockSpec((tm * ROW_TILE, 128), flat),
                   pl.BlockSpec((tm, 128), flat),
                   pl.BlockSpec((tm, 128), flat),
                   pl.BlockSpec((8, 128), lambda bi, i: (0, 0))],
        out_shape=[jax.ShapeDtypeStruct((b, s, d), F32),
                   jax.ShapeDtypeStruct((b * s * ROW_TILE, 128), U32),
                   jax.ShapeDtypeStruct((b * s, 128), jnp.int32),
                   jax.ShapeDtypeStruct((b * s, 128), F32),
                   jax.ShapeDtypeStruct((8, 128), F32)],
        scratch_shapes=[pltpu.VMEM((8, 128), F32)],
        compiler_params=_cparams("arbitrary", "arbitrary"),
        name="outproj",
    )(attn_o, four, x, mods, g_ffn, w_out, w_r, b_r, tri)


def _row_slice(row):
    return pl.ds(pl.multiple_of(row * ROW_TILE, ROW_TILE), ROW_TILE)


def _row_copy(src_ref, src_row, dst_ref, dst_row, sem):
    return pltpu.make_async_copy(src_ref.at[_row_slice(src_row)], dst_ref.at[_row_slice(dst_row)], sem)


def _bf16_bits(x):
    return lax.bitcast_convert_type(x.astype(BF16).astype(F32), U32)


def _rows_2d(ref, n_rows):
    hi, lo = [], []
    for cblk in range(ROW_TILE):
        w = ref[pl.ds(cblk, n_rows, stride=ROW_TILE), :]
        hi.append(lax.bitcast_convert_type(w & jnp.uint32(0xFFFF0000), F32).astype(BF16))
        lo.append(lax.bitcast_convert_type(w << 16, F32).astype(BF16))
    return jnp.concatenate(hi + lo, axis=1)


def _store_rows(ref, val):
    for cblk in range(ROW_TILE):
        hi = _bf16_bits(val[:, cblk * 128:(cblk + 1) * 128])
        lo = _bf16_bits(val[:, (cblk + ROW_TILE) * 128:(cblk + ROW_TILE + 1) * 128])
        ref[pl.ds(cblk, val.shape[0], stride=ROW_TILE), :] = hi | (lo >> 16)


def _dispatch_kernel(dest_ref, h_ref, zeros_ref, xs_ref, sem):
    del zeros_ref
    n = 2 * TD

    def issue(a, c):
        _row_copy(h_ref, a >> 1, xs_ref, dest_ref[0, 0, a], sem).start()
        return c

    lax.fori_loop(0, n, issue, 0, unroll=DMA_UNROLL)

    def drain(a, c):
        _row_copy(h_ref, 0, xs_ref, 0, sem).wait()
        return c

    lax.fori_loop(0, n, drain, 0, unroll=DMA_UNROLL)


def _dispatch(dest3, h3, xs_zeros):
    t = h3.shape[0] // ROW_TILE
    return pl.pallas_call(
        _dispatch_kernel,
        grid=(t // TD,),
        in_specs=[pl.BlockSpec((1, 1, 2 * TD), lambda i: (i, 0, 0), memory_space=pltpu.SMEM),
                  pl.BlockSpec((TD * ROW_TILE, 128), lambda i: (i, 0)),
                  pl.BlockSpec(memory_space=pl.ANY)],
        out_specs=pl.BlockSpec(memory_space=pl.ANY),
        out_shape=jax.ShapeDtypeStruct(xs_zeros.shape, xs_zeros.dtype),
        scratch_shapes=[pltpu.SemaphoreType.DMA(())],
        input_output_aliases={2: 0},
        compiler_params=_cparams("arbitrary"),
        name="dispatch",
    )(dest3, h3, xs_zeros)


def _experts_kernel(be_ref, nu_ref, xs_ref, wg_ref, wu_ref, wd_ref, ys_ref, wgb, wub, wdb):
    j = pl.program_id(0)
    used = j < nu_ref[0]
    new_expert = (j == 0) | (be_ref[j] != be_ref[jnp.maximum(j - 1, 0)])

    @pl.when(used & new_expert)
    def _():
        wgb[...] = wg_ref[0].astype(BF16)
        wub[...] = wu_ref[0].astype(BF16)
        wdb[...] = wd_ref[0].astype(BF16)

    @pl.when(used)
    def _():
        xb = _rows_2d(xs_ref, MOE_BLK)
        gate = jnp.dot(xb, wgb[...], preferred_element_type=F32)
        up = jnp.dot(xb, wub[...], preferred_element_type=F32)
        hid = (gate * jax.nn.sigmoid(gate) * up).astype(BF16)
        _store_rows(ys_ref, jnp.dot(hid, wdb[...], preferred_element_type=F32))

    @pl.when(pl.program_id(0) >= nu_ref[0])
    def _():
        ys_ref[...] = jnp.zeros_like(ys_ref)


def _experts(blk_expert, n_used, xs, wg, wu, wd):
    rows = xs.shape[0] // ROW_TILE
    d = wg.shape[1]
    nb = rows // MOE_BLK
    row_blk = lambda j, be, nu: (jnp.minimum(j, nu[0] - 1), 0)
    out_blk = lambda j, be, nu: (j, 0)
    w_blk = lambda j, be, nu: (be[jnp.minimum(j, nu[0] - 1)], 0, 0)
    return pl.pallas_call(
        _experts_kernel,
        grid_spec=pltpu.PrefetchScalarGridSpec(
            num_scalar_prefetch=2,
            grid=(nb,),
            in_specs=[pl.BlockSpec((MOE_BLK * ROW_TILE, 128), row_blk),
                      pl.BlockSpec((1, d, D_EXPERT), w_blk),
                      pl.BlockSpec((1, d, D_EXPERT), w_blk),
                      pl.BlockSpec((1, D_EXPERT, d), w_blk)],
            out_specs=pl.BlockSpec((MOE_BLK * ROW_TILE, 128), out_blk),
            scratch_shapes=[pltpu.VMEM((d, D_EXPERT), BF16), pltpu.VMEM((d, D_EXPERT), BF16),
                            pltpu.VMEM((D_EXPERT, d), BF16)]),
        out_shape=jax.ShapeDtypeStruct(xs.shape, xs.dtype),
        compiler_params=_cparams("arbitrary"),
        name="experts",
    )(blk_expert, n_used, xs, wg, wu, wd)


def _combine_kernel(dest_ref, dest_next_ref, ys_ref, x1_ref, gate_ref, mod_ref, g_ref, o_ref,
                    ya, yb, sem):
    step = pl.program_id(0) * pl.num_programs(1) + pl.program_id(1)
    n_steps = pl.num_programs(0) * pl.num_programs(1)
    slot = step % 2

    def start_gathers(idx_ref, to_slot):
        def issue(r, c):
            _row_copy(ys_ref, idx_ref[0, 0, 2 * r], ya.at[to_slot], r, sem.at[to_slot]).start()
            _row_copy(ys_ref, idx_ref[0, 0, 2 * r + 1], yb.at[to_slot], r, sem.at[to_slot]).start()
            return c

        lax.fori_loop(0, TD, issue, 0, unroll=DMA_UNROLL)

    @pl.when(step == 0)
    def _():
        start_gathers(dest_ref, 0)

    @pl.when(step + 1 < n_steps)
    def _():
        start_gathers(dest_next_ref, 1 - slot)

    def drain(r, c):
        _row_copy(ys_ref, 0, ya.at[slot], 0, sem.at[slot]).wait()
        _row_copy(ys_ref, 0, yb.at[slot], 0, sem.at[slot]).wait()
        return c

    lax.fori_loop(0, TD, drain, 0, unroll=DMA_UNROLL)
    gt = gate_ref[...]
    moe = (gt[:, 0:1] * _rows_2d(ya.at[slot], TD).astype(F32)
           + gt[:, 1:2] * _rows_2d(yb.at[slot], TD).astype(F32))
    x2 = x1_ref[0] + mod_ref[0][5:6] * moe
    ms = jnp.mean(x2 * x2, axis=-1, keepdims=True)
    o_ref[0] = x2 * lax.rsqrt(ms + EPS) * g_ref[...]


def _combine(dest3, ys, x1, gates, mods, g_final):
    b, s, d = x1.shape
    nt = s // TD
    return pl.pallas_call(
        _combine_kernel,
        grid=(b, nt),
        in_specs=[pl.BlockSpec((1, 1, 2 * TD), lambda bi, i: (bi * nt + i, 0, 0),
                               memory_space=pltpu.SMEM),
                  pl.BlockSpec((1, 1, 2 * TD),
                               lambda bi, i: (jnp.minimum(bi * nt + i + 1, b * nt - 1), 0, 0),
                               memory_space=pltpu.SMEM),
                  pl.BlockSpec(memory_space=pl.ANY),
                  pl.BlockSpec((1, TD, d), lambda bi, i: (bi, i, 0)),
                  pl.BlockSpec((TD, 128), lambda bi, i: (bi * nt + i, 0)),
                  pl.BlockSpec((1, N_MOD, d), lambda bi, i: (bi, 0, 0)),
                  pl.BlockSpec((1, d), lambda bi, i: (0, 0))],
        out_specs=pl.BlockSpec((1, TD, d), lambda bi, i: (bi, i, 0)),
        out_shape=jax.ShapeDtypeStruct((b, s, d), F32),
        scratch_shapes=[pltpu.VMEM((2, TD * ROW_TILE, 128), U32),
                        pltpu.VMEM((2, TD * ROW_TILE, 128), U32),
                        pltpu.SemaphoreType.DMA((2,))],
        compiler_params=_cparams("arbitrary", "arbitrary"),
        name="combine",
    )(dest3, dest3, ys, x1, gates, mods, g_final)


@functools.lru_cache(maxsize=None)
def _rope_tables(rows):
    r, col = np.meshgrid(np.arange(rows), np.arange(GRID_W), indexing='ij')
    pos = np.stack([r.reshape(-1), col.reshape(-1)], axis=-1).astype(np.float32)
    inv_freq = (np.float32(ROPE_THETA)
                ** (-np.arange(0, ROPE_AXIS, 2, dtype=np.float32) / np.float32(ROPE_AXIS))).astype(np.float32)
    ang = (pos[:, :, None] * inv_freq).astype(np.float32)
    ang = np.concatenate([ang, ang], axis=-1).astype(np.float64)
    n = ang.shape[0]
    cos = np.tile(np.cos(ang).reshape(n, HEAD_DIM), (1, 2)).astype(np.float32)
    sin = np.tile(np.sin(ang).reshape(n, HEAD_DIM), (1, 2)).astype(np.float32)
    upper = (np.arange(HEAD_W) % ROPE_AXIS) >= ROPE_HALF
    sa = np.where(upper, sin, np.float32(0.0))
    sb = np.where(upper, np.float32(0.0), -sin)
    return cos, sa, sb


@functools.lru_cache(maxsize=None)
def _dft_constants(n_pos):
    c = np.arange(FGROUP_DIM)
    ang_c = 2.0 * np.pi * ((c[:, None] * c[None, :]) % FGROUP_DIM) / FGROUP_DIM
    norm = 1.0 / math.sqrt(n_pos * FGROUP_DIM)
    cmat = (np.cos(ang_c) * norm).astype(np.float32)
    smat = (np.sin(ang_c) * norm).astype(np.float32)
    k1 = np.arange(DFT_N1)[None, :, None]
    n1 = np.arange(DFT_N1)[None, None, :]
    n2 = np.arange(DFT_N2)[:, None, None]
    ang_g = 2.0 * np.pi * ((k1 * (DFT_N2 * n1 + n2)) % n_pos) / n_pos
    gsmall = np.stack([np.cos(ang_g), np.sin(ang_g)], axis=1)
    gsmall = gsmall.reshape(DFT_N2 // F1_NB, F1_NB, 2, DFT_N1, DFT_N1)
    gmat = np.einsum('japkn,ab->jpkanb', gsmall, np.eye(F1_NB)).reshape(
        DFT_N2 // F1_NB, 2 * DFT_N1 * F1_NB, DFT_N1 * F1_NB).astype(BF16)
    k2 = np.arange(DFT_N2)
    ang_2 = 2.0 * np.pi * ((k2[:, None] * k2[None, :]) % DFT_N2) / DFT_N2
    c2 = np.cos(ang_2).astype(np.float32)
    s2 = np.sin(ang_2).astype(np.float32)
    return cmat, smat, gmat, c2, s2


def kernel(x, c, ctx, c_ctx, w_ada, b_ada, g_mix_norm, g_ffn_norm, w_in, lambda_q1, lambda_k1, lambda_q2, lambda_k2, g_subln, w_fourier, w_out, w_router_group, b_router_group, w_router_expert, b_router_expert, w_gate, w_up, w_down, g_final):
    b, s, d = x.shape
    t = b * s
    assert d == D_MODEL and s == DFT_N1 * DFT_N2 and s % GRID_W == 0 and b == 2

    cc = jnp.concatenate([c, c_ctx[None, :], jnp.zeros((8 - b - 1, d), F32)], axis=0)
    mods = _adaln(cc, w_ada[0], b_ada[0]).reshape(8, N_MOD, d)

    cmat, smat, gmat, c2, s2 = _dft_constants(s)
    wf = _wfold(jnp.asarray(cmat), jnp.asarray(smat), w_fourier[0])
    cos_t, sa_t, sb_t = _rope_tables(s // GRID_W)

    w_in_b = w_in[0].astype(BF16)
    g_mix = g_mix_norm[0].reshape(1, d)
    q, kl, vl, y = _inproj(x, mods, g_mix, w_in_b, wf, cos_t, sa_t, sb_t)
    kc, vc = _ctxproj(ctx, mods, g_mix, w_in_b)

    lam_p = jnp.stack([lambda_q1[0], lambda_k1[0], lambda_q2[0], lambda_k2[0]], axis=0)
    attn_o = _attention(q, kc, vc, kl, vl, lam_p, g_subln[0].reshape(1, HEAD_W))

    br, bi = _dft1(y.reshape(b, DFT_N1, DFT_N2, 2 * F_W), jnp.asarray(gmat))
    four = _dft2(br, bi, jnp.asarray(c2).astype(BF16),
                 jnp.asarray(s2).astype(BF16)).reshape(b, s, F_W)

    n_r = N_GROUPS + N_EXPERTS
    w_r = jnp.concatenate([w_router_group[0], w_router_expert[0],
                           jnp.zeros((d, 128 - n_r), F32)], axis=1)
    b_r = jnp.concatenate([b_router_group[0], b_router_expert[0],
                           jnp.zeros((128 - n_r,), F32)]).reshape(1, 128)
    w_r_hi = w_r.astype(BF16)
    w_r_lo = (w_r - w_r_hi.astype(F32)).astype(BF16)
    x1, h3, meta, gates, cnt = _outproj(attn_o, four, x, mods, g_ffn_norm[0].reshape(1, d),
                                        w_out[0].astype(BF16), jnp.stack([w_r_hi, w_r_lo]), b_r)

    e_flat = meta[:, 0:TOP_K].reshape(t * TOP_K)
    rank = meta[:, TOP_K:2 * TOP_K].reshape(t * TOP_K)
    counts = cnt[0, N_GROUPS:N_GROUPS + N_EXPERTS].astype(jnp.int32)
    nblk = (counts + MOE_BLK - 1) // MOE_BLK
    blk_end = jnp.cumsum(nblk)
    blk_start = blk_end - nblk
    dest = (jnp.take(blk_start * MOE_BLK, e_flat) + rank).astype(jnp.int32)
    n_blocks = t * TOP_K // MOE_BLK + N_EXPERTS
    blk_ids = jnp.arange(n_blocks, dtype=jnp.int32)
    blk_expert = jnp.minimum(
        jnp.sum((blk_end[None, :] <= blk_ids[:, None]).astype(jnp.int32), axis=1),
        N_EXPERTS - 1).astype(jnp.int32)
    n_used = blk_end[-1:].astype(jnp.int32)
    dest3 = dest.reshape(t // TD, 1, 2 * TD)

    xs = _dispatch(dest3, h3, jnp.zeros((n_blocks * MOE_BLK * ROW_TILE, 128), U32))
    ys = _experts(blk_expert, n_used, xs, w_gate[0], w_up[0], w_down[0])
    return _combine(dest3, ys, x1, gates, mods, g_final.reshape(1, d))
```

```python
import functools
import math

import numpy as np
import jax
import jax.numpy as jnp
from jax import lax
from jax.experimental import pallas as pl
from jax.experimental.pallas import tpu as pltpu

F32 = jnp.float32
BF16 = jnp.bfloat16

D_MODEL = 1024
GRID_W = 64
N_HEADS = 4
HEAD_DIM = 64
HEAD_W = 2 * HEAD_DIM
QK_W = N_HEADS * HEAD_W
N_FGROUPS = 4
FGROUP_DIM = 128
F_W = N_FGROUPS * FGROUP_DIM
ROPE_THETA = 10000.0
ROPE_AXIS = HEAD_DIM // 2
ROPE_HALF = ROPE_AXIS // 2
N_GROUPS = 4
EXPERTS_PER_GROUP = 8
N_EXPERTS = N_GROUPS * EXPERTS_PER_GROUP
TOP_K = 2
D_EXPERT = 512
N_MOD = 6
EPS = 1e-6
LAMBDA_INIT = 0.8 - 0.6 * math.exp(-0.3 * 0)
LOG2_E = 1.4426950408889634

DFT_N1 = 64
DFT_N2 = 128

TM_PROJ = 512
TQ = 1024
TK = 1024
F1_NB = 8
F2_KB = 8
MOE_BLK = 256
ROW_TILE = D_MODEL // 256
U32 = jnp.uint32
TD = 256
DMA_UNROLL = 16
VMEM_LIMIT = 48 * 1024 * 1024


def _cparams(*sem):
    return pltpu.CompilerParams(dimension_semantics=sem, vmem_limit_bytes=VMEM_LIMIT)


def _adaln_kernel(c_ref, w_ref, b_ref, o_ref):
    cc = c_ref[...]
    s = cc * jax.nn.sigmoid(cc)
    o_ref[...] = jnp.dot(s, w_ref[...], preferred_element_type=F32,
                         precision=lax.Precision.HIGHEST) + b_ref[...]


def _adaln(cc, w_ada, b_ada):
    n = w_ada.shape[1]
    tn = 1536
    return pl.pallas_call(
        _adaln_kernel,
        grid=(n // tn,),
        in_specs=[pl.BlockSpec((8, D_MODEL), lambda j: (0, 0)),
                  pl.BlockSpec((D_MODEL, tn), lambda j: (0, j)),
                  pl.BlockSpec((1, tn), lambda j: (0, j))],
        out_specs=pl.BlockSpec((8, tn), lambda j: (0, j)),
        out_shape=jax.ShapeDtypeStruct((8, n), F32),
        compiler_params=_cparams("arbitrary"),
        name="adaln",
    )(cc, w_ada, b_ada.reshape(1, n))


def _wfold_kernel(c_ref, s_ref, w_ref, o_ref):
    w = w_ref[0]
    o_ref[0, :, :FGROUP_DIM] = jnp.dot(c_ref[...], w, preferred_element_type=F32,
                                       precision=lax.Precision.HIGHEST).astype(BF16)
    o_ref[0, :, FGROUP_DIM:] = jnp.dot(s_ref[...], w, preferred_element_type=F32,
                                       precision=lax.Precision.HIGHEST).astype(BF16)


def _wfold(cmat, smat, w_fourier):
    return pl.pallas_call(
        _wfold_kernel,
        grid=(N_FGROUPS,),
        in_specs=[pl.BlockSpec((FGROUP_DIM, FGROUP_DIM), lambda g: (0, 0)),
                  pl.BlockSpec((FGROUP_DIM, FGROUP_DIM), lambda g: (0, 0)),
                  pl.BlockSpec((1, FGROUP_DIM, FGROUP_DIM), lambda g: (g, 0, 0))],
        out_specs=pl.BlockSpec((1, FGROUP_DIM, 2 * FGROUP_DIM), lambda g: (g, 0, 0)),
        out_shape=jax.ShapeDtypeStruct((N_FGROUPS, FGROUP_DIM, 2 * FGROUP_DIM), BF16),
        compiler_params=_cparams("arbitrary"),
        name="wfold",
    )(cmat, smat, w_fourier)


def _norm_mod(x, g, shift, scale):
    ms = jnp.mean(x * x, axis=-1, keepdims=True)
    y = x * lax.rsqrt(ms + EPS) * g
    return y * (1.0 + scale) + shift


def _rope_slab(p, cos, sa, sb):
    return (p * cos + pltpu.roll(p, ROPE_HALF, 1) * sa
            + pltpu.roll(p, HEAD_W - ROPE_HALF, 1) * sb)


def _inproj_kernel(x_ref, mod_ref, g_ref, w_ref, wf_ref, cos_ref, sa_ref, sb_ref,
                   q_ref, k_ref, v_ref, y_ref):
    m = mod_ref[0]
    h = _norm_mod(x_ref[0], g_ref[...], m[0:1], m[1:2]).astype(BF16)
    cos, sa, sb = cos_ref[...], sa_ref[...], sb_ref[...]
    scale = HEAD_DIM ** -0.5 * LOG2_E
    pq = jnp.dot(h, w_ref[:, 0:QK_W], preferred_element_type=F32)
    for hh in range(N_HEADS):
        sl = slice(hh * HEAD_W, (hh + 1) * HEAD_W)
        q_ref[0, :, sl] = (_rope_slab(pq[:, sl], cos, sa, sb) * scale).astype(BF16)
    pk = jnp.dot(h, w_ref[:, QK_W:2 * QK_W], preferred_element_type=F32)
    for hh in range(N_HEADS):
        sl = slice(hh * HEAD_W, (hh + 1) * HEAD_W)
        k_ref[0, :, sl] = _rope_slab(pk[:, sl], cos, sa, sb).astype(BF16)
    v_ref[0] = jnp.dot(h, w_ref[:, 2 * QK_W:3 * QK_W], preferred_element_type=F32).astype(BF16)
    pf = jnp.dot(h, w_ref[:, 3 * QK_W:], preferred_element_type=F32).astype(BF16)
    for g in range(N_FGROUPS):
        yy = jnp.dot(pf[:, g * FGROUP_DIM:(g + 1) * FGROUP_DIM], wf_ref[g],
                     preferred_element_type=F32)
        y_ref[0, :, g * FGROUP_DIM:(g + 1) * FGROUP_DIM] = yy[:, :FGROUP_DIM]
        y_ref[0, :, F_W + g * FGROUP_DIM:F_W + (g + 1) * FGROUP_DIM] = yy[:, FGROUP_DIM:]


def _inproj(x, mods, g_mix, w_in, wf, cos_t, sa_t, sb_t):
    b, s, d = x.shape
    tm = TM_PROJ
    tok = lambda bi, i: (bi, i, 0)
    return pl.pallas_call(
        _inproj_kernel,
        grid=(b, s // tm),
        in_specs=[pl.BlockSpec((1, tm, d), tok),
                  pl.BlockSpec((1, N_MOD, d), lambda bi, i: (bi, 0, 0)),
                  pl.BlockSpec((1, d), lambda bi, i: (0, 0)),
                  pl.BlockSpec(w_in.shape, lambda bi, i: (0, 0)),
                  pl.BlockSpec(wf.shape, lambda bi, i: (0, 0, 0)),
                  pl.BlockSpec((tm, HEAD_W), lambda bi, i: (i, 0)),
                  pl.BlockSpec((tm, HEAD_W), lambda bi, i: (i, 0)),
                  pl.BlockSpec((tm, HEAD_W), lambda bi, i: (i, 0))],
        out_specs=[pl.BlockSpec((1, tm, QK_W), tok),
                   pl.BlockSpec((1, tm, QK_W), tok),
                   pl.BlockSpec((1, tm, QK_W), tok),
                   pl.BlockSpec((1, tm, 2 * F_W), tok)],
        out_shape=[jax.ShapeDtypeStruct((b, s, QK_W), BF16),
                   jax.ShapeDtypeStruct((b, s, QK_W), BF16),
                   jax.ShapeDtypeStruct((b, s, QK_W), BF16),
                   jax.ShapeDtypeStruct((b, s, 2 * F_W), F32)],
        compiler_params=_cparams("parallel", "arbitrary"),
        name="inproj",
    )(x, mods, g_mix, w_in, wf, cos_t, sa_t, sb_t)


def _ctxproj_kernel(x_ref, mod_ref, g_ref, w_ref, k_ref, v_ref):
    m = mod_ref[0]
    h = _norm_mod(x_ref[0], g_ref[...], m[0:1], m[1:2]).astype(BF16)
    k_ref[0] = jnp.dot(h, w_ref[:, QK_W:2 * QK_W], preferred_element_type=F32).astype(BF16)
    v_ref[0] = jnp.dot(h, w_ref[:, 2 * QK_W:3 * QK_W], preferred_element_type=F32).astype(BF16)


def _ctxproj(ctx, mods, g_mix, w_in):
    b, n, d = ctx.shape
    return pl.pallas_call(
        _ctxproj_kernel,
        grid=(b,),
        in_specs=[pl.BlockSpec((1, n, d), lambda bi: (bi, 0, 0)),
                  pl.BlockSpec((1, N_MOD, d), lambda bi: (2, 0, 0)),
                  pl.BlockSpec((1, d), lambda bi: (0, 0)),
                  pl.BlockSpec(w_in.shape, lambda bi: (0, 0))],
        out_specs=[pl.BlockSpec((1, n, QK_W), lambda bi: (bi, 0, 0)),
                   pl.BlockSpec((1, n, QK_W), lambda bi: (bi, 0, 0))],
        out_shape=[jax.ShapeDtypeStruct((b, n, QK_W), BF16),
                   jax.ShapeDtypeStruct((b, n, QK_W), BF16)],
        compiler_params=_cparams("arbitrary"),
        name="ctxproj",
    )(ctx, mods, g_mix, w_in)


def _attn_kernel(q_ref, kc_ref, vc_ref, kl_ref, vl_ref, lam_ref, g_ref, o_ref,
                 m_ref, l_ref, acc_ref):
    q = q_ref[0]
    lane = lax.broadcasted_iota(jnp.int32, q.shape, 1)
    zero = jnp.zeros_like(q)
    q0 = jnp.where(lane < HEAD_DIM, q, zero)
    q1 = jnp.where(lane >= HEAD_DIM, q, zero)
    qs = (q0, q1)
    contract_last = (((1,), (1,)), ((), ()))
    m_ref[...] = jnp.full(m_ref.shape, -1e30, F32)
    l_ref[...] = jnp.zeros(l_ref.shape, F32)
    acc_ref[...] = jnp.zeros(acc_ref.shape, F32)

    def step(kb, vb):
        nk = kb.shape[0] // HEAD_W
        for mi in range(2):
            s = lax.dot_general(qs[mi], kb, contract_last, preferred_element_type=F32)
            m_old = m_ref[mi]
            m_new = jnp.maximum(m_old, jnp.max(s, axis=-1, keepdims=True))
            alpha = jnp.exp2(m_old - m_new)
            p = jnp.exp2(s - jnp.concatenate([m_new] * nk, axis=1))
            psum = p[:, 0:HEAD_W]
            for cblk in range(1, nk):
                psum = psum + p[:, cblk * HEAD_W:(cblk + 1) * HEAD_W]
            l_ref[mi] = alpha * l_ref[mi] + psum
            acc_ref[mi] = alpha * acc_ref[mi] + jnp.dot(p.astype(BF16), vb,
                                                       preferred_element_type=F32)
            m_ref[mi] = m_new

    step(kc_ref[0], vc_ref[0])

    def body(i, c):
        off = pl.multiple_of(i * TK, TK)
        step(kl_ref[0, pl.ds(off, TK), :], vl_ref[0, pl.ds(off, TK), :])
        return c

    lax.fori_loop(0, kl_ref.shape[1] // TK, body, 0)

    lp = lam_ref[...]
    t1 = jnp.sum(lp[0:1] * lp[1:2], axis=-1, keepdims=True)
    t2 = jnp.sum(lp[2:3] * lp[3:4], axis=-1, keepdims=True)
    lam = jnp.exp(t1) - jnp.exp(t2) + LAMBDA_INIT
    l0 = jnp.sum(l_ref[0], axis=-1, keepdims=True)
    l1 = jnp.sum(l_ref[1], axis=-1, keepdims=True)
    o = acc_ref[0] / l0 - lam * (acc_ref[1] / l1)
    ms = jnp.mean(o * o, axis=-1, keepdims=True)
    o = o * lax.rsqrt(ms + EPS) * g_ref[...] * (1.0 - LAMBDA_INIT)
    o_ref[0] = o.astype(BF16)


def _attention(q, kc, vc, kl, vl, lam_p, g_subln):
    b, s, _ = q.shape
    n_ctx = kc.shape[1]
    return pl.pallas_call(
        _attn_kernel,
        grid=(b, N_HEADS, s // TQ),
        in_specs=[pl.BlockSpec((1, TQ, HEAD_W), lambda bi, h, i: (bi, i, h)),
                  pl.BlockSpec((1, n_ctx, HEAD_W), lambda bi, h, i: (bi, 0, h)),
                  pl.BlockSpec((1, n_ctx, HEAD_W), lambda bi, h, i: (bi, 0, h)),
                  pl.BlockSpec((1, s, HEAD_W), lambda bi, h, i: (bi, 0, h)),
                  pl.BlockSpec((1, s, HEAD_W), lambda bi, h, i: (bi, 0, h)),
                  pl.BlockSpec((4, HEAD_DIM), lambda bi, h, i: (0, 0)),
                  pl.BlockSpec((1, HEAD_W), lambda bi, h, i: (0, 0))],
        out_specs=pl.BlockSpec((1, TQ, HEAD_W), lambda bi, h, i: (bi, i, h)),
        out_shape=jax.ShapeDtypeStruct((b, s, QK_W), BF16),
        scratch_shapes=[pltpu.VMEM((2, TQ, HEAD_W), F32)] * 3,
        compiler_params=_cparams("parallel", "parallel", "arbitrary"),
        name="diffattn",
    )(q, kc, vc, kl, vl, lam_p, g_subln)


def _dft1_kernel(y_ref, g_ref, br_ref, bi_ref):
    rows = DFT_N1 * F1_NB
    yb = y_ref[0].reshape(rows, 2 * F_W).astype(BF16)
    p = jnp.dot(g_ref[0], yb, preferred_element_type=F32)
    top, bot = p[:rows], p[rows:]
    br_ref[0] = (top[:, :F_W] - bot[:, F_W:]).reshape(DFT_N1, F1_NB, F_W)
    bi_ref[0] = (-top[:, F_W:] - bot[:, :F_W]).reshape(DFT_N1, F1_NB, F_W)


def _dft1(y4, gmat):
    b = y4.shape[0]
    rows = DFT_N1 * F1_NB
    return pl.pallas_call(
        _dft1_kernel,
        grid=(DFT_N2 // F1_NB, b),
        in_specs=[pl.BlockSpec((1, DFT_N1, F1_NB, 2 * F_W), lambda j, bi: (bi, 0, j, 0)),
                  pl.BlockSpec((1, 2 * rows, rows), lambda j, bi: (j, 0, 0))],
        out_specs=[pl.BlockSpec((1, DFT_N1, F1_NB, F_W), lambda j, bi: (bi, 0, j, 0)),
                   pl.BlockSpec((1, DFT_N1, F1_NB, F_W), lambda j, bi: (bi, 0, j, 0))],
        out_shape=[jax.ShapeDtypeStruct((b, DFT_N1, DFT_N2, F_W), F32),
                   jax.ShapeDtypeStruct((b, DFT_N1, DFT_N2, F_W), F32)],
        compiler_params=_cparams("arbitrary", "arbitrary"),
        name="dft1",
    )(y4, gmat)


def _dft2_kernel(br_ref, bi_ref, c_ref, s_ref, o_ref):
    for j in range(F2_KB):
        r = (jnp.dot(c_ref[...], br_ref[0, j].astype(BF16), preferred_element_type=F32)
             + jnp.dot(s_ref[...], bi_ref[0, j].astype(BF16), preferred_element_type=F32))
        o_ref[0, :, j, :] = r


def _dft2(br4, bi4, c2, s2):
    b = br4.shape[0]
    blk = (1, F2_KB, DFT_N2, F_W)
    return pl.pallas_call(
        _dft2_kernel,
        grid=(b, DFT_N1 // F2_KB),
        in_specs=[pl.BlockSpec(blk, lambda bi, j: (bi, j, 0, 0)),
                  pl.BlockSpec(blk, lambda bi, j: (bi, j, 0, 0)),
                  pl.BlockSpec((DFT_N2, DFT_N2), lambda bi, j: (0, 0)),
                  pl.BlockSpec((DFT_N2, DFT_N2), lambda bi, j: (0, 0))],
        out_specs=pl.BlockSpec((1, DFT_N2, F2_KB, F_W), lambda bi, j: (bi, 0, j, 0)),
        out_shape=jax.ShapeDtypeStruct((b, DFT_N2, DFT_N1, F_W), F32),
        compiler_params=_cparams("parallel", "arbitrary"),
        name="dft2",
    )(br4, bi4, c2, s2)


def _outproj_kernel(o_ref, f_ref, x_ref, mod_ref, g_ref, wo_ref, wr_ref, br_ref, tri_ref,
                    x1_ref, h3_ref, meta_ref, gate_ref, cnt_out_ref, cnt_ref):
    first = (pl.program_id(0) == 0) & (pl.program_id(1) == 0)

    @pl.when(first)
    def _():
        cnt_ref[...] = jnp.zeros(cnt_ref.shape, F32)

    m = mod_ref[0]
    mix = (jnp.dot(o_ref[0], wo_ref[0:QK_W, :], preferred_element_type=F32)
           + jnp.dot(f_ref[0].astype(BF16), wo_ref[QK_W:, :], preferred_element_type=F32))
    x1 = x_ref[0] + m[2:3] * mix
    x1_ref[0] = x1
    h2 = _norm_mod(x1, g_ref[...], m[3:4], m[4:5])
    _store_rows(h3_ref, h2)
    h_hi = h2.astype(BF16)
    h_lo = (h2 - h_hi.astype(F32)).astype(BF16)
    lg = (jnp.dot(h_hi, wr_ref[0], preferred_element_type=F32)
          + jnp.dot(h_lo, wr_ref[0], preferred_element_type=F32)
          + jnp.dot(h_hi, wr_ref[1], preferred_element_type=F32)
          + br_ref[...])
    lane = lax.broadcasted_iota(jnp.int32, lg.shape, 1)
    ninf = jnp.float32(-jnp.inf)
    big = jnp.int32(lg.shape[1])
    gl = jnp.where(lane < N_GROUPS, lg, ninf)
    gmax = jnp.max(gl, axis=-1, keepdims=True)
    grp = jnp.min(jnp.where(gl == gmax, lane, big), axis=-1, keepdims=True)
    pg = 1.0 / jnp.sum(jnp.exp(gl - gmax), axis=-1, keepdims=True)
    e_lane = lane - N_GROUPS
    emask = (e_lane >= 0) & (e_lane < N_EXPERTS) & ((e_lane >> 3) == grp)
    el = jnp.where(emask, lg, ninf)
    t1 = jnp.max(el, axis=-1, keepdims=True)
    i1 = jnp.min(jnp.where(el == t1, lane, big), axis=-1, keepdims=True)
    el2 = jnp.where(lane == i1, ninf, el)
    t2 = jnp.max(el2, axis=-1, keepdims=True)
    i2 = jnp.min(jnp.where(el2 == t2, lane, big), axis=-1, keepdims=True)
    dd = jnp.exp(t2 - t1)
    w1 = pg / (1.0 + dd)
    w2 = pg * dd / (1.0 + dd)
    gate_ref[...] = jnp.where(lane == 0, w1, jnp.where(lane == 1, w2, 0.0))
    hit1 = lane == i1
    hit2 = lane == i2
    oh = jnp.where(hit1 | hit2, 1.0, 0.0)
    before = jnp.dot(tri_ref[...], oh.astype(BF16), preferred_element_type=F32) + cnt_ref[0:1, :]
    r1 = jnp.sum(jnp.where(hit1, before, 0.0), axis=-1, keepdims=True).astype(jnp.int32)
    r2 = jnp.sum(jnp.where(hit2, before, 0.0), axis=-1, keepdims=True).astype(jnp.int32)
    cnt_ref[0:1, :] = cnt_ref[0:1, :] + jnp.sum(oh, axis=0, keepdims=True)
    cnt_out_ref[...] = cnt_ref[...]
    meta_ref[...] = jnp.where(lane == 0, i1 - N_GROUPS,
                              jnp.where(lane == 1, i2 - N_GROUPS,
                                        jnp.where(lane == 2, r1, jnp.where(lane == 3, r2, 0))))


def _outproj(attn_o, four, x, mods, g_ffn, w_out, w_r, b_r):
    b, s, d = x.shape
    tm = TM_PROJ
    nt = s // tm
    tok = lambda bi, i: (bi, i, 0)
    flat = lambda bi, i: (bi * nt + i, 0)
    tri = jnp.asarray(np.tril(np.ones((tm, tm), np.float32), -1)).astype(BF16)
    return pl.pallas_call(
        _outproj_kernel,
        grid=(b, nt),
        in_specs=[pl.BlockSpec((1, tm, QK_W), tok),
                  pl.BlockSpec((1, tm, F_W), tok),
                  pl.BlockSpec((1, tm, d), tok),
                  pl.BlockSpec((1, N_MOD, d), lambda bi, i: (bi, 0, 0)),
                  pl.BlockSpec((1, d), lambda bi, i: (0, 0)),
                  pl.BlockSpec(w_out.shape, lambda bi, i: (0, 0)),
                  pl.BlockSpec(w_r.shape, lambda bi, i: (0, 0, 0)),
                  pl.BlockSpec(b_r.shape, lambda bi, i: (0, 0)),
                  pl.BlockSpec((tm, tm), lambda bi, i: (0, 0))],
        out_specs=[pl.BlockSpec((1, tm, d), tok),
                   pl.BlockSpec((tm * ROW_TILE, 128), flat),
                   pl.BlockSpec((tm, 128), flat),
                   pl.BlockSpec((tm, 128), flat),
                   pl.BlockSpec((8, 128), lambda bi, i: (0, 0))],
        out_shape=[jax.ShapeDtypeStruct((b, s, d), F32),
                   jax.ShapeDtypeStruct((b * s * ROW_TILE, 128), U32),
                   jax.ShapeDtypeStruct((b * s, 128), jnp.int32),
                   jax.ShapeDtypeStruct((b * s, 128), F32),
                   jax.ShapeDtypeStruct((8, 128), F32)],
        scratch_shapes=[pltpu.VMEM((8, 128), F32)],
        compiler_params=_cparams("arbitrary", "arbitrary"),
        name="outproj",
    )(attn_o, four, x, mods, g_ffn, w_out, w_r, b_r, tri)


def _row_slice(row):
    return pl.ds(pl.multiple_of(row * ROW_TILE, ROW_TILE), ROW_TILE)


def _row_copy(src_ref, src_row, dst_ref, dst_row, sem):
    return pltpu.make_async_copy(src_ref.at[_row_slice(src_row)], dst_ref.at[_row_slice(dst_row)], sem)


def _bf16_bits(x):
    return lax.bitcast_convert_type(x.astype(BF16).astype(F32), U32)


def _rows_2d(ref, n_rows):
    hi, lo = [], []
    for cblk in range(ROW_TILE):
        w = ref[pl.ds(cblk, n_rows, stride=ROW_TILE), :]
        hi.append(lax.bitcast_convert_type(w & jnp.uint32(0xFFFF0000), F32).astype(BF16))
        lo.append(lax.bitcast_convert_type(w << 16, F32).astype(BF16))
    return jnp.concatenate(hi + lo, axis=1)


def _store_rows(ref, val):
    for cblk in range(ROW_TILE):
        hi = _bf16_bits(val[:, cblk * 128:(cblk + 1) * 128])
        lo = _bf16_bits(val[:, (cblk + ROW_TILE) * 128:(cblk + ROW_TILE + 1) * 128])
        ref[pl.ds(cblk, val.shape[0], stride=ROW_TILE), :] = hi | (lo >> 16)


def _dispatch_kernel(dest_ref, h_ref, zeros_ref, xs_ref, sem):
    del zeros_ref
    n = 2 * TD

    def issue(a, c):
        _row_copy(h_ref, a >> 1, xs_ref, dest_ref[0, 0, a], sem).start()
        return c

    lax.fori_loop(0, n, issue, 0, unroll=DMA_UNROLL)

    def drain(a, c):
        _row_copy(h_ref, 0, xs_ref, 0, sem).wait()
        return c

    lax.fori_loop(0, n, drain, 0, unroll=DMA_UNROLL)


def _dispatch(dest3, h3, xs_zeros):
    t = h3.shape[0] // ROW_TILE
    return pl.pallas_call(
        _dispatch_kernel,
        grid=(t // TD,),
        in_specs=[pl.BlockSpec((1, 1, 2 * TD), lambda i: (i, 0, 0), memory_space=pltpu.SMEM),
                  pl.BlockSpec((TD * ROW_TILE, 128), lambda i: (i, 0)),
                  pl.BlockSpec(memory_space=pl.ANY)],
        out_specs=pl.BlockSpec(memory_space=pl.ANY),
        out_shape=jax.ShapeDtypeStruct(xs_zeros.shape, xs_zeros.dtype),
        scratch_shapes=[pltpu.SemaphoreType.DMA(())],
        input_output_aliases={2: 0},
        compiler_params=_cparams("arbitrary"),
        name="dispatch",
    )(dest3, h3, xs_zeros)


def _experts_kernel(be_ref, nu_ref, nxt_ref, xs_ref, wg_hbm, wu_hbm, wd_hbm, ys_ref,
                    wgb, wub, wdb, sg, su, sd, sem):
    j = pl.program_id(0)
    used = j < nu_ref[0]
    e = be_ref[j]
    new_expert = (j == 0) | (e != be_ref[jnp.maximum(j - 1, 0)])
    e_next = nxt_ref[e]

    def weight_copies(idx):
        return (pltpu.make_async_copy(wg_hbm.at[idx], sg, sem.at[0]),
                pltpu.make_async_copy(wu_hbm.at[idx], su, sem.at[1]),
                pltpu.make_async_copy(wd_hbm.at[idx], sd, sem.at[2]))

    @pl.when(j == 0)
    def _():
        for cp in weight_copies(e):
            cp.start()

    @pl.when(used & new_expert)
    def _():
        for cp in weight_copies(e):
            cp.wait()
        wgb[...] = sg[...].astype(BF16)
        wub[...] = su[...].astype(BF16)
        wdb[...] = sd[...].astype(BF16)

    @pl.when(used & new_expert & (e_next != e))
    def _():
        for cp in weight_copies(e_next):
            cp.start()

    @pl.when(used)
    def _():
        xb = _rows_2d(xs_ref, MOE_BLK)
        gate = jnp.dot(xb, wgb[...], preferred_element_type=F32)
        up = jnp.dot(xb, wub[...], preferred_element_type=F32)
        hid = (gate * jax.nn.sigmoid(gate) * up).astype(BF16)
        _store_rows(ys_ref, jnp.dot(hid, wdb[...], preferred_element_type=F32))

    @pl.when(pl.program_id(0) >= nu_ref[0])
    def _():
        ys_ref[...] = jnp.zeros_like(ys_ref)


def _experts(blk_expert, n_used, next_expert, xs, wg, wu, wd):
    rows = xs.shape[0] // ROW_TILE
    d = wg.shape[1]
    nb = rows // MOE_BLK
    row_blk = lambda j, be, nu, nx: (jnp.minimum(j, nu[0] - 1), 0)
    out_blk = lambda j, be, nu, nx: (j, 0)
    return pl.pallas_call(
        _experts_kernel,
        grid_spec=pltpu.PrefetchScalarGridSpec(
            num_scalar_prefetch=3,
            grid=(nb,),
            in_specs=[pl.BlockSpec((MOE_BLK * ROW_TILE, 128), row_blk),
                      pl.BlockSpec(memory_space=pl.ANY),
                      pl.BlockSpec(memory_space=pl.ANY),
                      pl.BlockSpec(memory_space=pl.ANY)],
            out_specs=pl.BlockSpec((MOE_BLK * ROW_TILE, 128), out_blk),
            scratch_shapes=[pltpu.VMEM((d, D_EXPERT), BF16), pltpu.VMEM((d, D_EXPERT), BF16),
                            pltpu.VMEM((D_EXPERT, d), BF16),
                            pltpu.VMEM((d, D_EXPERT), F32), pltpu.VMEM((d, D_EXPERT), F32),
                            pltpu.VMEM((D_EXPERT, d), F32),
                            pltpu.SemaphoreType.DMA((3,))]),
        out_shape=jax.ShapeDtypeStruct(xs.shape, xs.dtype),
        compiler_params=_cparams("arbitrary"),
        name="experts",
    )(blk_expert, n_used, next_expert, xs, wg, wu, wd)


def _combine_kernel(dest_ref, dest_next_ref, ys_ref, x1_ref, gate_ref, mod_ref, g_ref, o_ref,
                    ya, yb, sem):
    step = pl.program_id(0) * pl.num_programs(1) + pl.program_id(1)
    n_steps = pl.num_programs(0) * pl.num_programs(1)
    slot = step % 2

    def start_gathers(idx_ref, to_slot):
        def issue(r, c):
            _row_copy(ys_ref, idx_ref[0, 0, 2 * r], ya.at[to_slot], r, sem.at[to_slot]).start()
            _row_copy(ys_ref, idx_ref[0, 0, 2 * r + 1], yb.at[to_slot], r, sem.at[to_slot]).start()
            return c

        lax.fori_loop(0, TD, issue, 0, unroll=DMA_UNROLL)

    @pl.when(step == 0)
    def _():
        start_gathers(dest_ref, 0)

    @pl.when(step + 1 < n_steps)
    def _():
        start_gathers(dest_next_ref, 1 - slot)

    def drain(r, c):
        _row_copy(ys_ref, 0, ya.at[slot], 0, sem.at[slot]).wait()
        _row_copy(ys_ref, 0, yb.at[slot], 0, sem.at[slot]).wait()
        return c

    lax.fori_loop(0, TD, drain, 0, unroll=DMA_UNROLL)
    gt = gate_ref[...]
    moe = (gt[:, 0:1] * _rows_2d(ya.at[slot], TD).astype(F32)
           + gt[:, 1:2] * _rows_2d(yb.at[slot], TD).astype(F32))
    x2 = x1_ref[0] + mod_ref[0][5:6] * moe
    ms = jnp.mean(x2 * x2, axis=-1, keepdims=True)
    o_ref[0] = x2 * lax.rsqrt(ms + EPS) * g_ref[...]


def _combine(dest3, ys, x1, gates, mods, g_final):
    b, s, d = x1.shape
    nt = s // TD
    return pl.pallas_call(
        _combine_kernel,
        grid=(b, nt),
        in_specs=[pl.BlockSpec((1, 1, 2 * TD), lambda bi, i: (bi * nt + i, 0, 0),
                               memory_space=pltpu.SMEM),
                  pl.BlockSpec((1, 1, 2 * TD),
                               lambda bi, i: (jnp.minimum(bi * nt + i + 1, b * nt - 1), 0, 0),
                               memory_space=pltpu.SMEM),
                  pl.BlockSpec(memory_space=pl.ANY),
                  pl.BlockSpec((1, TD, d), lambda bi, i: (bi, i, 0)),
                  pl.BlockSpec((TD, 128), lambda bi, i: (bi * nt + i, 0)),
                  pl.BlockSpec((1, N_MOD, d), lambda bi, i: (bi, 0, 0)),
                  pl.BlockSpec((1, d), lambda bi, i: (0, 0))],
        out_specs=pl.BlockSpec((1, TD, d), lambda bi, i: (bi, i, 0)),
        out_shape=jax.ShapeDtypeStruct((b, s, d), F32),
        scratch_shapes=[pltpu.VMEM((2, TD * ROW_TILE, 128), U32),
                        pltpu.VMEM((2, TD * ROW_TILE, 128), U32),
                        pltpu.SemaphoreType.DMA((2,))],
        compiler_params=_cparams("arbitrary", "arbitrary"),
        name="combine",
    )(dest3, dest3, ys, x1, gates, mods, g_final)


@functools.lru_cache(maxsize=None)
def _rope_tables(rows):
    r, col = np.meshgrid(np.arange(rows), np.arange(GRID_W), indexing='ij')
    pos = np.stack([r.reshape(-1), col.reshape(-1)], axis=-1).astype(np.float32)
    inv_freq = (np.float32(ROPE_THETA)
                ** (-np.arange(0, ROPE_AXIS, 2, dtype=np.float32) / np.float32(ROPE_AXIS))).astype(np.float32)
    ang = (pos[:, :, None] * inv_freq).astype(np.float32)
    ang = np.concatenate([ang, ang], axis=-1).astype(np.float64)
    n = ang.shape[0]
    cos = np.tile(np.cos(ang).reshape(n, HEAD_DIM), (1, 2)).astype(np.float32)
    sin = np.tile(np.sin(ang).reshape(n, HEAD_DIM), (1, 2)).astype(np.float32)
    upper = (np.arange(HEAD_W) % ROPE_AXIS) >= ROPE_HALF
    sa = np.where(upper, sin, np.float32(0.0))
    sb = np.where(upper, np.float32(0.0), -sin)
    return cos, sa, sb


@functools.lru_cache(maxsize=None)
def _dft_constants(n_pos):
    c = np.arange(FGROUP_DIM)
    ang_c = 2.0 * np.pi * ((c[:, None] * c[None, :]) % FGROUP_DIM) / FGROUP_DIM
    norm = 1.0 / math.sqrt(n_pos * FGROUP_DIM)
    cmat = (np.cos(ang_c) * norm).astype(np.float32)
    smat = (np.sin(ang_c) * norm).astype(np.float32)
    k1 = np.arange(DFT_N1)[None, :, None]
    n1 = np.arange(DFT_N1)[None, None, :]
    n2 = np.arange(DFT_N2)[:, None, None]
    ang_g = 2.0 * np.pi * ((k1 * (DFT_N2 * n1 + n2)) % n_pos) / n_pos
    gsmall = np.stack([np.cos(ang_g), np.sin(ang_g)], axis=1)
    gsmall = gsmall.reshape(DFT_N2 // F1_NB, F1_NB, 2, DFT_N1, DFT_N1)
    gmat = np.einsum('japkn,ab->jpkanb', gsmall, np.eye(F1_NB)).reshape(
        DFT_N2 // F1_NB, 2 * DFT_N1 * F1_NB, DFT_N1 * F1_NB).astype(BF16)
    k2 = np.arange(DFT_N2)
    ang_2 = 2.0 * np.pi * ((k2[:, None] * k2[None, :]) % DFT_N2) / DFT_N2
    c2 = np.cos(ang_2).astype(np.float32)
    s2 = np.sin(ang_2).astype(np.float32)
    return cmat, smat, gmat, c2, s2


def kernel(x, c, ctx, c_ctx, w_ada, b_ada, g_mix_norm, g_ffn_norm, w_in, lambda_q1, lambda_k1, lambda_q2, lambda_k2, g_subln, w_fourier, w_out, w_router_group, b_router_group, w_router_expert, b_router_expert, w_gate, w_up, w_down, g_final):
    b, s, d = x.shape
    t = b * s
    assert d == D_MODEL and s == DFT_N1 * DFT_N2 and s % GRID_W == 0 and b == 2

    cc = jnp.concatenate([c, c_ctx[None, :], jnp.zeros((8 - b - 1, d), F32)], axis=0)
    mods = _adaln(cc, w_ada[0], b_ada[0]).reshape(8, N_MOD, d)

    cmat, smat, gmat, c2, s2 = _dft_constants(s)
    wf = _wfold(jnp.asarray(cmat), jnp.asarray(smat), w_fourier[0])
    cos_t, sa_t, sb_t = _rope_tables(s // GRID_W)

    w_in_b = w_in[0].astype(BF16)
    g_mix = g_mix_norm[0].reshape(1, d)
    q, kl, vl, y = _inproj(x, mods, g_mix, w_in_b, wf, cos_t, sa_t, sb_t)
    kc, vc = _ctxproj(ctx, mods, g_mix, w_in_b)

    lam_p = jnp.stack([lambda_q1[0], lambda_k1[0], lambda_q2[0], lambda_k2[0]], axis=0)
    attn_o = _attention(q, kc, vc, kl, vl, lam_p, g_subln[0].reshape(1, HEAD_W))

    br, bi = _dft1(y.reshape(b, DFT_N1, DFT_N2, 2 * F_W), jnp.asarray(gmat))
    four = _dft2(br, bi, jnp.asarray(c2).astype(BF16),
                 jnp.asarray(s2).astype(BF16)).reshape(b, s, F_W)

    n_r = N_GROUPS + N_EXPERTS
    w_r = jnp.concatenate([w_router_group[0], w_router_expert[0],
                           jnp.zeros((d, 128 - n_r), F32)], axis=1)
    b_r = jnp.concatenate([b_router_group[0], b_router_expert[0],
                           jnp.zeros((128 - n_r,), F32)]).reshape(1, 128)
    w_r_hi = w_r.astype(BF16)
    w_r_lo = (w_r - w_r_hi.astype(F32)).astype(BF16)
    x1, h3, meta, gates, cnt = _outproj(attn_o, four, x, mods, g_ffn_norm[0].reshape(1, d),
                                        w_out[0].astype(BF16), jnp.stack([w_r_hi, w_r_lo]), b_r)

    e_flat = meta[:, 0:TOP_K].reshape(t * TOP_K)
    rank = meta[:, TOP_K:2 * TOP_K].reshape(t * TOP_K)
    counts = cnt[0, N_GROUPS:N_GROUPS + N_EXPERTS].astype(jnp.int32)
    nblk = (counts + MOE_BLK - 1) // MOE_BLK
    blk_end = jnp.cumsum(nblk)
    blk_start = blk_end - nblk
    dest = (jnp.take(blk_start * MOE_BLK, e_flat) + rank).astype(jnp.int32)
    n_blocks = t * TOP_K // MOE_BLK + N_EXPERTS
    blk_ids = jnp.arange(n_blocks, dtype=jnp.int32)
    blk_expert = jnp.minimum(
        jnp.sum((blk_end[None, :] <= blk_ids[:, None]).astype(jnp.int32), axis=1),
        N_EXPERTS - 1).astype(jnp.int32)
    n_used = blk_end[-1:].astype(jnp.int32)
    dest3 = dest.reshape(t // TD, 1, 2 * TD)

    xs = _dispatch(dest3, h3, jnp.zeros((n_blocks * MOE_BLK * ROW_TILE, 128), U32))
    e_ids = jnp.arange(N_EXPERTS, dtype=jnp.int32)
    later = (e_ids[None, :] > e_ids[:, None]) & (nblk[None, :] > 0)
    next_expert = jnp.min(jnp.where(later, e_ids[None, :], N_EXPERTS), axis=1)
    next_expert = jnp.where(next_expert == N_EXPERTS, e_ids, next_expert).astype(jnp.int32)
    ys = _experts(blk_expert, n_used, next_expert, xs, w_gate[0], w_up[0], w_down[0])
    return _combine(dest3, ys, x1, gates, mods, g_final.reshape(1, d))
```

```python
import functools
import math

import numpy as np
import jax
import jax.numpy as jnp
from jax import lax
from jax.experimental import pallas as pl
from jax.experimental.pallas import tpu as pltpu

F32 = jnp.float32
BF16 = jnp.bfloat16

D_MODEL = 1024
GRID_W = 64
N_HEADS = 4
HEAD_DIM = 64
HEAD_W = 2 * HEAD_DIM
QK_W = N_HEADS * HEAD_W
N_FGROUPS = 4
FGROUP_DIM = 128
F_W = N_FGROUPS * FGROUP_DIM
ROPE_THETA = 10000.0
ROPE_AXIS = HEAD_DIM // 2
ROPE_HALF = ROPE_AXIS // 2
N_GROUPS = 4
EXPERTS_PER_GROUP = 8
N_EXPERTS = N_GROUPS * EXPERTS_PER_GROUP
TOP_K = 2
D_EXPERT = 512
N_MOD = 6
EPS = 1e-6
LAMBDA_INIT = 0.8 - 0.6 * math.exp(-0.3 * 0)
LOG2_E = 1.4426950408889634

DFT_N1 = 64
DFT_N2 = 128

TM_PROJ = 512
TQ = 1024
TK = 1024
F1_NB = 8
F2_KB = 8
MOE_BLK = 256
ROW_TILE = D_MODEL // 256
U32 = jnp.uint32
TD = 256
DMA_UNROLL = 16
VMEM_LIMIT = 48 * 1024 * 1024


def _cparams(*sem):
    return pltpu.CompilerParams(dimension_semantics=sem, vmem_limit_bytes=VMEM_LIMIT)


def _adaln_kernel(c_ref, w_ref, b_ref, o_ref):
    cc = c_ref[...]
    s = cc * jax.nn.sigmoid(cc)
    o_ref[...] = jnp.dot(s, w_ref[...], preferred_element_type=F32,
                         precision=lax.Precision.HIGHEST) + b_ref[...]


def _adaln(cc, w_ada, b_ada):
    n = w_ada.shape[1]
    tn = 1536
    return pl.pallas_call(
        _adaln_kernel,
        grid=(n // tn,),
        in_specs=[pl.BlockSpec((8, D_MODEL), lambda j: (0, 0)),
                  pl.BlockSpec((D_MODEL, tn), lambda j: (0, j)),
                  pl.BlockSpec((1, tn), lambda j: (0, j))],
        out_specs=pl.BlockSpec((8, tn), lambda j: (0, j)),
        out_shape=jax.ShapeDtypeStruct((8, n), F32),
        compiler_params=_cparams("arbitrary"),
        name="adaln",
    )(cc, w_ada, b_ada.reshape(1, n))


def _wfold_kernel(c_ref, s_ref, w_ref, o_ref):
    w = w_ref[0]
    o_ref[0, :, :FGROUP_DIM] = jnp.dot(c_ref[...], w, preferred_element_type=F32,
                                       precision=lax.Precision.HIGHEST).astype(BF16)
    o_ref[0, :, FGROUP_DIM:] = jnp.dot(s_ref[...], w, preferred_element_type=F32,
                                       precision=lax.Precision.HIGHEST).astype(BF16)


def _wfold(cmat, smat, w_fourier):
    return pl.pallas_call(
        _wfold_kernel,
        grid=(N_FGROUPS,),
        in_specs=[pl.BlockSpec((FGROUP_DIM, FGROUP_DIM), lambda g: (0, 0)),
                  pl.BlockSpec((FGROUP_DIM, FGROUP_DIM), lambda g: (0, 0)),
                  pl.BlockSpec((1, FGROUP_DIM, FGROUP_DIM), lambda g: (g, 0, 0))],
        out_specs=pl.BlockSpec((1, FGROUP_DIM, 2 * FGROUP_DIM), lambda g: (g, 0, 0)),
        out_shape=jax.ShapeDtypeStruct((N_FGROUPS, FGROUP_DIM, 2 * FGROUP_DIM), BF16),
        compiler_params=_cparams("arbitrary"),
        name="wfold",
    )(cmat, smat, w_fourier)


def _norm_mod(x, g, shift, scale):
    ms = jnp.mean(x * x, axis=-1, keepdims=True)
    y = x * lax.rsqrt(ms + EPS) * g
    return y * (1.0 + scale) + shift


def _rope_slab(p, cos, sa, sb):
    return (p * cos + pltpu.roll(p, ROPE_HALF, 1) * sa
            + pltpu.roll(p, HEAD_W - ROPE_HALF, 1) * sb)


def _inproj_kernel(x_ref, mod_ref, g_ref, w_ref, wf_ref, cos_ref, sa_ref, sb_ref,
                   q_ref, k_ref, v_ref, y_ref):
    m = mod_ref[0]
    h = _norm_mod(x_ref[0], g_ref[...], m[0:1], m[1:2]).astype(BF16)
    cos, sa, sb = cos_ref[...], sa_ref[...], sb_ref[...]
    scale = HEAD_DIM ** -0.5 * LOG2_E
    pq = jnp.dot(h, w_ref[:, 0:QK_W], preferred_element_type=F32)
    for hh in range(N_HEADS):
        sl = slice(hh * HEAD_W, (hh + 1) * HEAD_W)
        q_ref[0, :, sl] = (_rope_slab(pq[:, sl], cos, sa, sb) * scale).astype(BF16)
    pk = jnp.dot(h, w_ref[:, QK_W:2 * QK_W], preferred_element_type=F32)
    for hh in range(N_HEADS):
        sl = slice(hh * HEAD_W, (hh + 1) * HEAD_W)
        k_ref[0, :, sl] = _rope_slab(pk[:, sl], cos, sa, sb).astype(BF16)
    v_ref[0] = jnp.dot(h, w_ref[:, 2 * QK_W:3 * QK_W], preferred_element_type=F32).astype(BF16)
    pf = jnp.dot(h, w_ref[:, 3 * QK_W:], preferred_element_type=F32).astype(BF16)
    for g in range(N_FGROUPS):
        yy = jnp.dot(pf[:, g * FGROUP_DIM:(g + 1) * FGROUP_DIM], wf_ref[g],
                     preferred_element_type=F32)
        y_ref[0, :, g * FGROUP_DIM:(g + 1) * FGROUP_DIM] = yy[:, :FGROUP_DIM]
        y_ref[0, :, F_W + g * FGROUP_DIM:F_W + (g + 1) * FGROUP_DIM] = yy[:, FGROUP_DIM:]


def _inproj(x, mods, g_mix, w_in, wf, cos_t, sa_t, sb_t):
    b, s, d = x.shape
    tm = TM_PROJ
    tok = lambda bi, i: (bi, i, 0)
    return pl.pallas_call(
        _inproj_kernel,
        grid=(b, s // tm),
        in_specs=[pl.BlockSpec((1, tm, d), tok),
                  pl.BlockSpec((1, N_MOD, d), lambda bi, i: (bi, 0, 0)),
                  pl.BlockSpec((1, d), lambda bi, i: (0, 0)),
                  pl.BlockSpec(w_in.shape, lambda bi, i: (0, 0)),
                  pl.BlockSpec(wf.shape, lambda bi, i: (0, 0, 0)),
                  pl.BlockSpec((tm, HEAD_W), lambda bi, i: (i, 0)),
                  pl.BlockSpec((tm, HEAD_W), lambda bi, i: (i, 0)),
                  pl.BlockSpec((tm, HEAD_W), lambda bi, i: (i, 0))],
        out_specs=[pl.BlockSpec((1, tm, QK_W), tok),
                   pl.BlockSpec((1, tm, QK_W), tok),
                   pl.BlockSpec((1, tm, QK_W), tok),
                   pl.BlockSpec((1, tm, 2 * F_W), tok)],
        out_shape=[jax.ShapeDtypeStruct((b, s, QK_W), BF16),
                   jax.ShapeDtypeStruct((b, s, QK_W), BF16),
                   jax.ShapeDtypeStruct((b, s, QK_W), BF16),
                   jax.ShapeDtypeStruct((b, s, 2 * F_W), F32)],
        compiler_params=_cparams("parallel", "arbitrary"),
        name="inproj",
    )(x, mods, g_mix, w_in, wf, cos_t, sa_t, sb_t)


def _ctxproj_kernel(x_ref, mod_ref, g_ref, w_ref, k_ref, v_ref):
    m = mod_ref[0]
    h = _norm_mod(x_ref[0], g_ref[...], m[0:1], m[1:2]).astype(BF16)
    k_ref[0] = jnp.dot(h, w_ref[:, QK_W:2 * QK_W], preferred_element_type=F32).astype(BF16)
    v_ref[0] = jnp.dot(h, w_ref[:, 2 * QK_W:3 * QK_W], preferred_element_type=F32).astype(BF16)


def _ctxproj(ctx, mods, g_mix, w_in):
    b, n, d = ctx.shape
    return pl.pallas_call(
        _ctxproj_kernel,
        grid=(b,),
        in_specs=[pl.BlockSpec((1, n, d), lambda bi: (bi, 0, 0)),
                  pl.BlockSpec((1, N_MOD, d), lambda bi: (2, 0, 0)),
                  pl.BlockSpec((1, d), lambda bi: (0, 0)),
                  pl.BlockSpec(w_in.shape, lambda bi: (0, 0))],
        out_specs=[pl.BlockSpec((1, n, QK_W), lambda bi: (bi, 0, 0)),
                   pl.BlockSpec((1, n, QK_W), lambda bi: (bi, 0, 0))],
        out_shape=[jax.ShapeDtypeStruct((b, n, QK_W), BF16),
                   jax.ShapeDtypeStruct((b, n, QK_W), BF16)],
        compiler_params=_cparams("arbitrary"),
        name="ctxproj",
    )(ctx, mods, g_mix, w_in)


def _attn_kernel(q_ref, kc_ref, vc_ref, kl_ref, vl_ref, lam_ref, g_ref, o_ref,
                 m_ref, l_ref, acc_ref):
    q = q_ref[0]
    lane = lax.broadcasted_iota(jnp.int32, q.shape, 1)
    zero = jnp.zeros_like(q)
    q0 = jnp.where(lane < HEAD_DIM, q, zero)
    q1 = jnp.where(lane >= HEAD_DIM, q, zero)
    qs = (q0, q1)
    contract_last = (((1,), (1,)), ((), ()))
    m_ref[...] = jnp.full(m_ref.shape, -1e30, F32)
    l_ref[...] = jnp.zeros(l_ref.shape, F32)
    acc_ref[...] = jnp.zeros(acc_ref.shape, F32)

    def step(kb, vb):
        nk = kb.shape[0] // HEAD_W
        for mi in range(2):
            s = lax.dot_general(qs[mi], kb, contract_last, preferred_element_type=F32)
            m_old = m_ref[mi]
            m_new = jnp.maximum(m_old, jnp.max(s, axis=-1, keepdims=True))
            alpha = jnp.exp2(m_old - m_new)
            p = jnp.exp2(s - jnp.concatenate([m_new] * nk, axis=1))
            psum = p[:, 0:HEAD_W]
            for cblk in range(1, nk):
                psum = psum + p[:, cblk * HEAD_W:(cblk + 1) * HEAD_W]
            l_ref[mi] = alpha * l_ref[mi] + psum
            acc_ref[mi] = alpha * acc_ref[mi] + jnp.dot(p.astype(BF16), vb,
                                                       preferred_element_type=F32)
            m_ref[mi] = m_new

    step(kc_ref[0], vc_ref[0])

    def body(i, c):
        off = pl.multiple_of(i * TK, TK)
        step(kl_ref[0, pl.ds(off, TK), :], vl_ref[0, pl.ds(off, TK), :])
        return c

    lax.fori_loop(0, kl_ref.shape[1] // TK, body, 0)

    lp = lam_ref[...]
    t1 = jnp.sum(lp[0:1] * lp[1:2], axis=-1, keepdims=True)
    t2 = jnp.sum(lp[2:3] * lp[3:4], axis=-1, keepdims=True)
    lam = jnp.exp(t1) - jnp.exp(t2) + LAMBDA_INIT
    l0 = jnp.sum(l_ref[0], axis=-1, keepdims=True)
    l1 = jnp.sum(l_ref[1], axis=-1, keepdims=True)
    o = acc_ref[0] / l0 - lam * (acc_ref[1] / l1)
    ms = jnp.mean(o * o, axis=-1, keepdims=True)
    o = o * lax.rsqrt(ms + EPS) * g_ref[...] * (1.0 - LAMBDA_INIT)
    o_ref[0] = o.astype(BF16)


def _attention(q, kc, vc, kl, vl, lam_p, g_subln):
    b, s, _ = q.shape
    n_ctx = kc.shape[1]
    return pl.pallas_call(
        _attn_kernel,
        grid=(b, N_HEADS, s // TQ),
        in_specs=[pl.BlockSpec((1, TQ, HEAD_W), lambda bi, h, i: (bi, i, h)),
                  pl.BlockSpec((1, n_ctx, HEAD_W), lambda bi, h, i: (bi, 0, h)),
                  pl.BlockSpec((1, n_ctx, HEAD_W), lambda bi, h, i: (bi, 0, h)),
                  pl.BlockSpec((1, s, HEAD_W), lambda bi, h, i: (bi, 0, h)),
                  pl.BlockSpec((1, s, HEAD_W), lambda bi, h, i: (bi, 0, h)),
                  pl.BlockSpec((4, HEAD_DIM), lambda bi, h, i: (0, 0)),
                  pl.BlockSpec((1, HEAD_W), lambda bi, h, i: (0, 0))],
        out_specs=pl.BlockSpec((1, TQ, HEAD_W), lambda bi, h, i: (bi, i, h)),
        out_shape=jax.ShapeDtypeStruct((b, s, QK_W), BF16),
        scratch_shapes=[pltpu.VMEM((2, TQ, HEAD_W), F32)] * 3,
        compiler_params=_cparams("parallel", "parallel", "arbitrary"),
        name="diffattn",
    )(q, kc, vc, kl, vl, lam_p, g_subln)


def _dft1_kernel(y_ref, g_ref, br_ref, bi_ref):
    rows = DFT_N1 * F1_NB
    yb = y_ref[0].reshape(rows, 2 * F_W).astype(BF16)
    p = jnp.dot(g_ref[0], yb, preferred_element_type=F32)
    top, bot = p[:rows], p[rows:]
    br_ref[0] = (top[:, :F_W] - bot[:, F_W:]).reshape(DFT_N1, F1_NB, F_W)
    bi_ref[0] = (-top[:, F_W:] - bot[:, :F_W]).reshape(DFT_N1, F1_NB, F_W)


def _dft1(y4, gmat):
    b = y4.shape[0]
    rows = DFT_N1 * F1_NB
    return pl.pallas_call(
        _dft1_kernel,
        grid=(DFT_N2 // F1_NB, b),
        in_specs=[pl.BlockSpec((1, DFT_N1, F1_NB, 2 * F_W), lambda j, bi: (bi, 0, j, 0)),
                  pl.BlockSpec((1, 2 * rows, rows), lambda j, bi: (j, 0, 0))],
        out_specs=[pl.BlockSpec((1, DFT_N1, F1_NB, F_W), lambda j, bi: (bi, 0, j, 0)),
                   pl.BlockSpec((1, DFT_N1, F1_NB, F_W), lambda j, bi: (bi, 0, j, 0))],
        out_shape=[jax.ShapeDtypeStruct((b, DFT_N1, DFT_N2, F_W), F32),
                   jax.ShapeDtypeStruct((b, DFT_N1, DFT_N2, F_W), F32)],
        compiler_params=_cparams("arbitrary", "arbitrary"),
        name="dft1",
    )(y4, gmat)


def _dft2_kernel(br_ref, bi_ref, c_ref, s_ref, o_ref):
    for j in range(F2_KB):
        r = (jnp.dot(c_ref[...], br_ref[0, j].astype(BF16), preferred_element_type=F32)
             + jnp.dot(s_ref[...], bi_ref[0, j].astype(BF16), preferred_element_type=F32))
        o_ref[0, :, j, :] = r


def _dft2(br4, bi4, c2, s2):
    b = br4.shape[0]
    blk = (1, F2_KB, DFT_N2, F_W)
    return pl.pallas_call(
        _dft2_kernel,
        grid=(b, DFT_N1 // F2_KB),
        in_specs=[pl.BlockSpec(blk, lambda bi, j: (bi, j, 0, 0)),
                  pl.BlockSpec(blk, lambda bi, j: (bi, j, 0, 0)),
                  pl.BlockSpec((DFT_N2, DFT_N2), lambda bi, j: (0, 0)),
                  pl.BlockSpec((DFT_N2, DFT_N2), lambda bi, j: (0, 0))],
        out_specs=pl.BlockSpec((1, DFT_N2, F2_KB, F_W), lambda bi, j: (bi, 0, j, 0)),
        out_shape=jax.ShapeDtypeStruct((b, DFT_N2, DFT_N1, F_W), F32),
        compiler_params=_cparams("parallel", "arbitrary"),
        name="dft2",
    )(br4, bi4, c2, s2)


def _outproj_kernel(o_ref, f_ref, x_ref, mod_ref, g_ref, wo_ref, wr_ref, br_ref, tri_ref,
                    x1_ref, h3_ref, meta_ref, gate_ref, cnt_out_ref, cnt_ref):
    first = (pl.program_id(0) == 0) & (pl.program_id(1) == 0)

    @pl.when(first)
    def _():
        cnt_ref[...] = jnp.zeros(cnt_ref.shape, F32)

    m = mod_ref[0]
    mix = (jnp.dot(o_ref[0], wo_ref[0:QK_W, :], preferred_element_type=F32)
           + jnp.dot(f_ref[0].astype(BF16), wo_ref[QK_W:, :], preferred_element_type=F32))
    x1 = x_ref[0] + m[2:3] * mix
    x1_ref[0] = x1
    h2 = _norm_mod(x1, g_ref[...], m[3:4], m[4:5])
    _store_rows(h3_ref, h2)
    h_hi = h2.astype(BF16)
    h_lo = (h2 - h_hi.astype(F32)).astype(BF16)
    lg = (jnp.dot(h_hi, wr_ref[0], preferred_element_type=F32)
          + jnp.dot(h_lo, wr_ref[0], preferred_element_type=F32)
          + jnp.dot(h_hi, wr_ref[1], preferred_element_type=F32)
          + br_ref[...])
    lane = lax.broadcasted_iota(jnp.int32, lg.shape, 1)
    ninf = jnp.float32(-jnp.inf)
    big = jnp.int32(lg.shape[1])
    gl = jnp.where(lane < N_GROUPS, lg, ninf)
    gmax = jnp.max(gl, axis=-1, keepdims=True)
    grp = jnp.min(jnp.where(gl == gmax, lane, big), axis=-1, keepdims=True)
    pg = 1.0 / jnp.sum(jnp.exp(gl - gmax), axis=-1, keepdims=True)
    e_lane = lane - N_GROUPS
    emask = (e_lane >= 0) & (e_lane < N_EXPERTS) & ((e_lane >> 3) == grp)
    el = jnp.where(emask, lg, ninf)
    t1 = jnp.max(el, axis=-1, keepdims=True)
    i1 = jnp.min(jnp.where(el == t1, lane, big), axis=-1, keepdims=True)
    el2 = jnp.where(lane == i1, ninf, el)
    t2 = jnp.max(el2, axis=-1, keepdims=True)
    i2 = jnp.min(jnp.where(el2 == t2, lane, big), axis=-1, keepdims=True)
    dd = jnp.exp(t2 - t1)
    w1 = pg / (1.0 + dd)
    w2 = pg * dd / (1.0 + dd)
    gate_ref[...] = jnp.where(lane == 0, w1, jnp.where(lane == 1, w2, 0.0))
    hit1 = lane == i1
    hit2 = lane == i2
    oh = jnp.where(hit1 | hit2, 1.0, 0.0)
    before = jnp.dot(tri_ref[...], oh.astype(BF16), preferred_element_type=F32) + cnt_ref[0:1, :]
    r1 = jnp.sum(jnp.where(hit1, before, 0.0), axis=-1, keepdims=True).astype(jnp.int32)
    r2 = jnp.sum(jnp.where(hit2, before, 0.0), axis=-1, keepdims=True).astype(jnp.int32)
    cnt_ref[0:1, :] = cnt_ref[0:1, :] + jnp.sum(oh, axis=0, keepdims=True)
    cnt_out_ref[...] = cnt_ref[...]
    meta = jnp.where(lane == 0, i1 - N_GROUPS,
                     jnp.where(lane == 1, i2 - N_GROUPS,
                               jnp.where(lane == 2, r1, jnp.where(lane == 3, r2, 0))))
    meta_ref[...] = jnp.transpose(meta)[0:8, :]


def _outproj(attn_o, four, x, mods, g_ffn, w_out, w_r, b_r):
    b, s, d = x.shape
    tm = TM_PROJ
    nt = s // tm
    tok = lambda bi, i: (bi, i, 0)
    flat = lambda bi, i: (bi * nt + i, 0)
    tri = jnp.asarray(np.tril(np.ones((tm, tm), np.float32), -1)).astype(BF16)
    return pl.pallas_call(
        _outproj_kernel,
        grid=(b, nt),
        in_specs=[pl.BlockSpec((1, tm, QK_W), tok),
                  pl.BlockSpec((1, tm, F_W), tok),
                  pl.BlockSpec((1, tm, d), tok),
                  pl.BlockSpec((1, N_MOD, d), lambda bi, i: (bi, 0, 0)),
                  pl.BlockSpec((1, d), lambda bi, i: (0, 0)),
                  pl.BlockSpec(w_out.shape, lambda bi, i: (0, 0)),
                  pl.BlockSpec(w_r.shape, lambda bi, i: (0, 0, 0)),
                  pl.BlockSpec(b_r.shape, lambda bi, i: (0, 0)),
                  pl.BlockSpec((tm, tm), lambda bi, i: (0, 0))],
        out_specs=[pl.BlockSpec((1, tm, d), tok),
                   pl.BlockSpec((tm * ROW_TILE, 128), flat),
                   pl.BlockSpec((8, tm), lambda bi, i: (0, bi * nt + i)),
                   pl.BlockSpec((tm, 128), flat),
                   pl.BlockSpec((8, 128), lambda bi, i: (0, 0))],
        out_shape=[jax.ShapeDtypeStruct((b, s, d), F32),
                   jax.ShapeDtypeStruct((b * s * ROW_TILE, 128), U32),
                   jax.ShapeDtypeStruct((8, b * s), jnp.int32),
                   jax.ShapeDtypeStruct((b * s, 128), F32),
                   jax.ShapeDtypeStruct((8, 128), F32)],
        scratch_shapes=[pltpu.VMEM((8, 128), F32)],
        compiler_params=_cparams("arbitrary", "arbitrary"),
        name="outproj",
    )(attn_o, four, x, mods, g_ffn, w_out, w_r, b_r, tri)


def _row_slice(row):
    return pl.ds(pl.multiple_of(row * ROW_TILE, ROW_TILE), ROW_TILE)


def _row_copy(src_ref, src_row, dst_ref, dst_row, sem):
    return pltpu.make_async_copy(src_ref.at[_row_slice(src_row)], dst_ref.at[_row_slice(dst_row)], sem)


def _bf16_bits(x):
    return lax.bitcast_convert_type(x.astype(BF16).astype(F32), U32)


def _rows_2d(ref, n_rows):
    hi, lo = [], []
    for cblk in range(ROW_TILE):
        w = ref[pl.ds(cblk, n_rows, stride=ROW_TILE), :]
        hi.append(lax.bitcast_convert_type(w & jnp.uint32(0xFFFF0000), F32).astype(BF16))
        lo.append(lax.bitcast_convert_type(w << 16, F32).astype(BF16))
    return jnp.concatenate(hi + lo, axis=1)


def _store_rows(ref, val):
    for cblk in range(ROW_TILE):
        hi = _bf16_bits(val[:, cblk * 128:(cblk + 1) * 128])
        lo = _bf16_bits(val[:, (cblk + ROW_TILE) * 128:(cblk + ROW_TILE + 1) * 128])
        ref[pl.ds(cblk, val.shape[0], stride=ROW_TILE), :] = hi | (lo >> 16)


def _dispatch_kernel(dest_ref, h_ref, zeros_ref, xs_ref, sem):
    del zeros_ref
    n = 2 * TD

    def issue(a, c):
        _row_copy(h_ref, a >> 1, xs_ref, dest_ref[0, 0, a], sem).start()
        return c

    lax.fori_loop(0, n, issue, 0, unroll=DMA_UNROLL)

    def drain(a, c):
        _row_copy(h_ref, 0, xs_ref, 0, sem).wait()
        return c

    lax.fori_loop(0, n, drain, 0, unroll=DMA_UNROLL)


def _dispatch(dest3, h3, xs_zeros):
    t = h3.shape[0] // ROW_TILE
    return pl.pallas_call(
        _dispatch_kernel,
        grid=(t // TD,),
        in_specs=[pl.BlockSpec((1, 1, 2 * TD), lambda i: (i, 0, 0), memory_space=pltpu.SMEM),
                  pl.BlockSpec((TD * ROW_TILE, 128), lambda i: (i, 0)),
                  pl.BlockSpec(memory_space=pl.ANY)],
        out_specs=pl.BlockSpec(memory_space=pl.ANY),
        out_shape=jax.ShapeDtypeStruct(xs_zeros.shape, xs_zeros.dtype),
        scratch_shapes=[pltpu.SemaphoreType.DMA(())],
        input_output_aliases={2: 0},
        compiler_params=_cparams("arbitrary"),
        name="dispatch",
    )(dest3, h3, xs_zeros)


def _experts_kernel(be_ref, nu_ref, nxt_ref, xs_ref, wg_hbm, wu_hbm, wd_hbm, ys_ref,
                    wgb, wub, wdb, sg, su, sd, sem):
    j = pl.program_id(0)
    used = j < nu_ref[0]
    e = be_ref[j]
    new_expert = (j == 0) | (e != be_ref[jnp.maximum(j - 1, 0)])
    e_next = nxt_ref[e]

    def weight_copies(idx):
        return (pltpu.make_async_copy(wg_hbm.at[idx], sg, sem.at[0]),
                pltpu.make_async_copy(wu_hbm.at[idx], su, sem.at[1]),
                pltpu.make_async_copy(wd_hbm.at[idx], sd, sem.at[2]))

    @pl.when(j == 0)
    def _():
        for cp in weight_copies(e):
            cp.start()

    @pl.when(used & new_expert)
    def _():
        for cp in weight_copies(e):
            cp.wait()
        wgb[...] = sg[...].astype(BF16)
        wub[...] = su[...].astype(BF16)
        wdb[...] = sd[...].astype(BF16)

    @pl.when(used & new_expert & (e_next != e))
    def _():
        for cp in weight_copies(e_next):
            cp.start()

    @pl.when(used)
    def _():
        xb = _rows_2d(xs_ref, MOE_BLK)
        gate = jnp.dot(xb, wgb[...], preferred_element_type=F32)
        up = jnp.dot(xb, wub[...], preferred_element_type=F32)
        hid = (gate * jax.nn.sigmoid(gate) * up).astype(BF16)
        _store_rows(ys_ref, jnp.dot(hid, wdb[...], preferred_element_type=F32))

    @pl.when(pl.program_id(0) >= nu_ref[0])
    def _():
        ys_ref[...] = jnp.zeros_like(ys_ref)


def _experts(blk_expert, n_used, next_expert, xs, wg, wu, wd):
    rows = xs.shape[0] // ROW_TILE
    d = wg.shape[1]
    nb = rows // MOE_BLK
    row_blk = lambda j, be, nu, nx: (jnp.minimum(j, nu[0] - 1), 0)
    out_blk = lambda j, be, nu, nx: (j, 0)
    return pl.pallas_call(
        _experts_kernel,
        grid_spec=pltpu.PrefetchScalarGridSpec(
            num_scalar_prefetch=3,
            grid=(nb,),
            in_specs=[pl.BlockSpec((MOE_BLK * ROW_TILE, 128), row_blk),
                      pl.BlockSpec(memory_space=pl.ANY),
                      pl.BlockSpec(memory_space=pl.ANY),
                      pl.BlockSpec(memory_space=pl.ANY)],
            out_specs=pl.BlockSpec((MOE_BLK * ROW_TILE, 128), out_blk),
            scratch_shapes=[pltpu.VMEM((d, D_EXPERT), BF16), pltpu.VMEM((d, D_EXPERT), BF16),
                            pltpu.VMEM((D_EXPERT, d), BF16),
                            pltpu.VMEM((d, D_EXPERT), F32), pltpu.VMEM((d, D_EXPERT), F32),
                            pltpu.VMEM((D_EXPERT, d), F32),
                            pltpu.SemaphoreType.DMA((3,))]),
        out_shape=jax.ShapeDtypeStruct(xs.shape, xs.dtype),
        compiler_params=_cparams("arbitrary"),
        name="experts",
    )(blk_expert, n_used, next_expert, xs, wg, wu, wd)


def _combine_kernel(dest_ref, dest_next_ref, ys_ref, x1_ref, gate_ref, mod_ref, g_ref, o_ref,
                    ya, yb, sem):
    step = pl.program_id(0) * pl.num_programs(1) + pl.program_id(1)
    n_steps = pl.num_programs(0) * pl.num_programs(1)
    slot = step % 2

    def start_gathers(idx_ref, to_slot):
        def issue(r, c):
            _row_copy(ys_ref, idx_ref[0, 0, 2 * r], ya.at[to_slot], r, sem.at[to_slot]).start()
            _row_copy(ys_ref, idx_ref[0, 0, 2 * r + 1], yb.at[to_slot], r, sem.at[to_slot]).start()
            return c

        lax.fori_loop(0, TD, issue, 0, unroll=DMA_UNROLL)

    @pl.when(step == 0)
    def _():
        start_gathers(dest_ref, 0)

    @pl.when(step + 1 < n_steps)
    def _():
        start_gathers(dest_next_ref, 1 - slot)

    def drain(r, c):
        _row_copy(ys_ref, 0, ya.at[slot], 0, sem.at[slot]).wait()
        _row_copy(ys_ref, 0, yb.at[slot], 0, sem.at[slot]).wait()
        return c

    lax.fori_loop(0, TD, drain, 0, unroll=DMA_UNROLL)
    gt = gate_ref[...]
    moe = (gt[:, 0:1] * _rows_2d(ya.at[slot], TD).astype(F32)
           + gt[:, 1:2] * _rows_2d(yb.at[slot], TD).astype(F32))
    x2 = x1_ref[0] + mod_ref[0][5:6] * moe
    ms = jnp.mean(x2 * x2, axis=-1, keepdims=True)
    o_ref[0] = x2 * lax.rsqrt(ms + EPS) * g_ref[...]


def _combine(dest3, ys, x1, gates, mods, g_final):
    b, s, d = x1.shape
    nt = s // TD
    return pl.pallas_call(
        _combine_kernel,
        grid=(b, nt),
        in_specs=[pl.BlockSpec((1, 1, 2 * TD), lambda bi, i: (bi * nt + i, 0, 0),
                               memory_space=pltpu.SMEM),
                  pl.BlockSpec((1, 1, 2 * TD),
                               lambda bi, i: (jnp.minimum(bi * nt + i + 1, b * nt - 1), 0, 0),
                               memory_space=pltpu.SMEM),
                  pl.BlockSpec(memory_space=pl.ANY),
                  pl.BlockSpec((1, TD, d), lambda bi, i: (bi, i, 0)),
                  pl.BlockSpec((TD, 128), lambda bi, i: (bi * nt + i, 0)),
                  pl.BlockSpec((1, N_MOD, d), lambda bi, i: (bi, 0, 0)),
                  pl.BlockSpec((1, d), lambda bi, i: (0, 0))],
        out_specs=pl.BlockSpec((1, TD, d), lambda bi, i: (bi, i, 0)),
        out_shape=jax.ShapeDtypeStruct((b, s, d), F32),
        scratch_shapes=[pltpu.VMEM((2, TD * ROW_TILE, 128), U32),
                        pltpu.VMEM((2, TD * ROW_TILE, 128), U32),
                        pltpu.SemaphoreType.DMA((2,))],
        compiler_params=_cparams("arbitrary", "arbitrary"),
        name="combine",
    )(dest3, dest3, ys, x1, gates, mods, g_final)


@functools.lru_cache(maxsize=None)
def _rope_tables(rows):
    r, col = np.meshgrid(np.arange(rows), np.arange(GRID_W), indexing='ij')
    pos = np.stack([r.reshape(-1), col.reshape(-1)], axis=-1).astype(np.float32)
    inv_freq = (np.float32(ROPE_THETA)
                ** (-np.arange(0, ROPE_AXIS, 2, dtype=np.float32) / np.float32(ROPE_AXIS))).astype(np.float32)
    ang = (pos[:, :, None] * inv_freq).astype(np.float32)
    ang = np.concatenate([ang, ang], axis=-1).astype(np.float64)
    n = ang.shape[0]
    cos = np.tile(np.cos(ang).reshape(n, HEAD_DIM), (1, 2)).astype(np.float32)
    sin = np.tile(np.sin(ang).reshape(n, HEAD_DIM), (1, 2)).astype(np.float32)
    upper = (np.arange(HEAD_W) % ROPE_AXIS) >= ROPE_HALF
    sa = np.where(upper, sin, np.float32(0.0))
    sb = np.where(upper, np.float32(0.0), -sin)
    return cos, sa, sb


@functools.lru_cache(maxsize=None)
def _dft_constants(n_pos):
    c = np.arange(FGROUP_DIM)
    ang_c = 2.0 * np.pi * ((c[:, None] * c[None, :]) % FGROUP_DIM) / FGROUP_DIM
    norm = 1.0 / math.sqrt(n_pos * FGROUP_DIM)
    cmat = (np.cos(ang_c) * norm).astype(np.float32)
    smat = (np.sin(ang_c) * norm).astype(np.float32)
    k1 = np.arange(DFT_N1)[None, :, None]
    n1 = np.arange(DFT_N1)[None, None, :]
    n2 = np.arange(DFT_N2)[:, None, None]
    ang_g = 2.0 * np.pi * ((k1 * (DFT_N2 * n1 + n2)) % n_pos) / n_pos
    gsmall = np.stack([np.cos(ang_g), np.sin(ang_g)], axis=1)
    gsmall = gsmall.reshape(DFT_N2 // F1_NB, F1_NB, 2, DFT_N1, DFT_N1)
    gmat = np.einsum('japkn,ab->jpkanb', gsmall, np.eye(F1_NB)).reshape(
        DFT_N2 // F1_NB, 2 * DFT_N1 * F1_NB, DFT_N1 * F1_NB).astype(BF16)
    k2 = np.arange(DFT_N2)
    ang_2 = 2.0 * np.pi * ((k2[:, None] * k2[None, :]) % DFT_N2) / DFT_N2
    c2 = np.cos(ang_2).astype(np.float32)
    s2 = np.sin(ang_2).astype(np.float32)
    return cmat, smat, gmat, c2, s2


def kernel(x, c, ctx, c_ctx, w_ada, b_ada, g_mix_norm, g_ffn_norm, w_in, lambda_q1, lambda_k1, lambda_q2, lambda_k2, g_subln, w_fourier, w_out, w_router_group, b_router_group, w_router_expert, b_router_expert, w_gate, w_up, w_down, g_final):
    b, s, d = x.shape
    t = b * s
    assert d == D_MODEL and s == DFT_N1 * DFT_N2 and s % GRID_W == 0 and b == 2

    cc = jnp.concatenate([c, c_ctx[None, :], jnp.zeros((8 - b - 1, d), F32)], axis=0)
    mods = _adaln(cc, w_ada[0], b_ada[0]).reshape(8, N_MOD, d)

    cmat, smat, gmat, c2, s2 = _dft_constants(s)
    wf = _wfold(jnp.asarray(cmat), jnp.asarray(smat), w_fourier[0])
    cos_t, sa_t, sb_t = _rope_tables(s // GRID_W)

    w_in_b = w_in[0].astype(BF16)
    g_mix = g_mix_norm[0].reshape(1, d)
    q, kl, vl, y = _inproj(x, mods, g_mix, w_in_b, wf, cos_t, sa_t, sb_t)
    kc, vc = _ctxproj(ctx, mods, g_mix, w_in_b)

    lam_p = jnp.stack([lambda_q1[0], lambda_k1[0], lambda_q2[0], lambda_k2[0]], axis=0)
    attn_o = _attention(q, kc, vc, kl, vl, lam_p, g_subln[0].reshape(1, HEAD_W))

    br, bi = _dft1(y.reshape(b, DFT_N1, DFT_N2, 2 * F_W), jnp.asarray(gmat))
    four = _dft2(br, bi, jnp.asarray(c2).astype(BF16),
                 jnp.asarray(s2).astype(BF16)).reshape(b, s, F_W)

    n_r = N_GROUPS + N_EXPERTS
    w_r = jnp.concatenate([w_router_group[0], w_router_expert[0],
                           jnp.zeros((d, 128 - n_r), F32)], axis=1)
    b_r = jnp.concatenate([b_router_group[0], b_router_expert[0],
                           jnp.zeros((128 - n_r,), F32)]).reshape(1, 128)
    w_r_hi = w_r.astype(BF16)
    w_r_lo = (w_r - w_r_hi.astype(F32)).astype(BF16)
    x1, h3, meta, gates, cnt = _outproj(attn_o, four, x, mods, g_ffn_norm[0].reshape(1, d),
                                        w_out[0].astype(BF16), jnp.stack([w_r_hi, w_r_lo]), b_r)

    e_tk = jnp.stack([meta[0], meta[1]], axis=1)
    rank_tk = jnp.stack([meta[2], meta[3]], axis=1)
    counts = cnt[0, N_GROUPS:N_GROUPS + N_EXPERTS].astype(jnp.int32)
    nblk = (counts + MOE_BLK - 1) // MOE_BLK
    blk_end = jnp.cumsum(nblk)
    blk_start = blk_end - nblk
    onehot = (e_tk[:, :, None] == jnp.arange(N_EXPERTS, dtype=jnp.int32)).astype(F32)
    first_blk = jnp.einsum('tke,e->tk', onehot, blk_start.astype(F32),
                           precision=lax.Precision.HIGHEST).astype(jnp.int32)
    dest = first_blk * MOE_BLK + rank_tk
    n_blocks = t * TOP_K // MOE_BLK + N_EXPERTS
    blk_ids = jnp.arange(n_blocks, dtype=jnp.int32)
    blk_expert = jnp.minimum(
        jnp.sum((blk_end[None, :] <= blk_ids[:, None]).astype(jnp.int32), axis=1),
        N_EXPERTS - 1).astype(jnp.int32)
    n_used = blk_end[-1:].astype(jnp.int32)
    dest3 = dest.reshape(t // TD, 1, 2 * TD)

    xs = _dispatch(dest3, h3, jnp.zeros((n_blocks * MOE_BLK * ROW_TILE, 128), U32))
    e_ids = jnp.arange(N_EXPERTS, dtype=jnp.int32)
    later = (e_ids[None, :] > e_ids[:, None]) & (nblk[None, :] > 0)
    next_expert = jnp.min(jnp.where(later, e_ids[None, :], N_EXPERTS), axis=1)
    next_expert = jnp.where(next_expert == N_EXPERTS, e_ids, next_expert).astype(jnp.int32)
    ys = _experts(blk_expert, n_used, next_expert, xs, w_gate[0], w_up[0], w_down[0])
    return _combine(dest3, ys, x1, gates, mods, g_final.reshape(1, d))
```

```python
import functools
import math

import numpy as np
import jax
import jax.numpy as jnp
from jax import lax
from jax.experimental import pallas as pl
from jax.experimental.pallas import tpu as pltpu

F32 = jnp.float32
BF16 = jnp.bfloat16

D_MODEL = 1024
GRID_W = 64
N_HEADS = 4
HEAD_DIM = 64
HEAD_W = 2 * HEAD_DIM
QK_W = N_HEADS * HEAD_W
N_FGROUPS = 4
FGROUP_DIM = 128
F_W = N_FGROUPS * FGROUP_DIM
ROPE_THETA = 10000.0
ROPE_AXIS = HEAD_DIM // 2
ROPE_HALF = ROPE_AXIS // 2
N_GROUPS = 4
EXPERTS_PER_GROUP = 8
N_EXPERTS = N_GROUPS * EXPERTS_PER_GROUP
TOP_K = 2
D_EXPERT = 512
N_MOD = 6
EPS = 1e-6
LAMBDA_INIT = 0.8 - 0.6 * math.exp(-0.3 * 0)
LOG2_E = 1.4426950408889634

DFT_N1 = 64
DFT_N2 = 128

TM_PROJ = 512
TQ = 1024
TK = 1024
F1_NB = 8
F2_KB = 8
MOE_BLK = 256
ROW_TILE = D_MODEL // 256
U32 = jnp.uint32
TD = 512
TD_DISPATCH = 1024
DMA_UNROLL = 16
VMEM_LIMIT = 48 * 1024 * 1024


def _cparams(*sem):
    return pltpu.CompilerParams(dimension_semantics=sem, vmem_limit_bytes=VMEM_LIMIT)


def _adaln_kernel(c_ref, w_ref, b_ref, o_ref):
    cc = c_ref[...]
    s = cc * jax.nn.sigmoid(cc)
    o_ref[...] = jnp.dot(s, w_ref[...], preferred_element_type=F32,
                         precision=lax.Precision.HIGHEST) + b_ref[...]


def _adaln(cc, w_ada, b_ada):
    n = w_ada.shape[1]
    tn = 1536
    return pl.pallas_call(
        _adaln_kernel,
        grid=(n // tn,),
        in_specs=[pl.BlockSpec((8, D_MODEL), lambda j: (0, 0)),
                  pl.BlockSpec((D_MODEL, tn), lambda j: (0, j)),
                  pl.BlockSpec((1, tn), lambda j: (0, j))],
        out_specs=pl.BlockSpec((8, tn), lambda j: (0, j)),
        out_shape=jax.ShapeDtypeStruct((8, n), F32),
        compiler_params=_cparams("arbitrary"),
        name="adaln",
    )(cc, w_ada, b_ada.reshape(1, n))


def _wfold_kernel(c_ref, s_ref, w_ref, o_ref):
    w = w_ref[0]
    o_ref[0, :, :FGROUP_DIM] = jnp.dot(c_ref[...], w, preferred_element_type=F32,
                                       precision=lax.Precision.HIGHEST).astype(BF16)
    o_ref[0, :, FGROUP_DIM:] = jnp.dot(s_ref[...], w, preferred_element_type=F32,
                                       precision=lax.Precision.HIGHEST).astype(BF16)


def _wfold(cmat, smat, w_fourier):
    return pl.pallas_call(
        _wfold_kernel,
        grid=(N_FGROUPS,),
        in_specs=[pl.BlockSpec((FGROUP_DIM, FGROUP_DIM), lambda g: (0, 0)),
                  pl.BlockSpec((FGROUP_DIM, FGROUP_DIM), lambda g: (0, 0)),
                  pl.BlockSpec((1, FGROUP_DIM, FGROUP_DIM), lambda g: (g, 0, 0))],
        out_specs=pl.BlockSpec((1, FGROUP_DIM, 2 * FGROUP_DIM), lambda g: (g, 0, 0)),
        out_shape=jax.ShapeDtypeStruct((N_FGROUPS, FGROUP_DIM, 2 * FGROUP_DIM), BF16),
        compiler_params=_cparams("arbitrary"),
        name="wfold",
    )(cmat, smat, w_fourier)


def _norm_mod(x, g, shift, scale):
    ms = jnp.mean(x * x, axis=-1, keepdims=True)
    y = x * lax.rsqrt(ms + EPS) * g
    return y * (1.0 + scale) + shift


def _rope_slab(p, cos, sa, sb):
    return (p * cos + pltpu.roll(p, ROPE_HALF, 1) * sa
            + pltpu.roll(p, HEAD_W - ROPE_HALF, 1) * sb)


def _inproj_kernel(x_ref, mod_ref, g_ref, w_ref, wf_ref, cos_ref, sa_ref, sb_ref,
                   q_ref, k_ref, v_ref, y_ref):
    m = mod_ref[0]
    h = _norm_mod(x_ref[0], g_ref[...], m[0:1], m[1:2]).astype(BF16)
    cos, sa, sb = cos_ref[...], sa_ref[...], sb_ref[...]
    scale = HEAD_DIM ** -0.5 * LOG2_E
    pq = jnp.dot(h, w_ref[:, 0:QK_W], preferred_element_type=F32)
    for hh in range(N_HEADS):
        sl = slice(hh * HEAD_W, (hh + 1) * HEAD_W)
        q_ref[0, :, sl] = (_rope_slab(pq[:, sl], cos, sa, sb) * scale).astype(BF16)
    pk = jnp.dot(h, w_ref[:, QK_W:2 * QK_W], preferred_element_type=F32)
    for hh in range(N_HEADS):
        sl = slice(hh * HEAD_W, (hh + 1) * HEAD_W)
        k_ref[0, :, sl] = _rope_slab(pk[:, sl], cos, sa, sb).astype(BF16)
    v_ref[0] = jnp.dot(h, w_ref[:, 2 * QK_W:3 * QK_W], preferred_element_type=F32).astype(BF16)
    pf = jnp.dot(h, w_ref[:, 3 * QK_W:], preferred_element_type=F32).astype(BF16)
    for g in range(N_FGROUPS):
        yy = jnp.dot(pf[:, g * FGROUP_DIM:(g + 1) * FGROUP_DIM], wf_ref[g],
                     preferred_element_type=F32)
        y_ref[0, :, g * FGROUP_DIM:(g + 1) * FGROUP_DIM] = yy[:, :FGROUP_DIM]
        y_ref[0, :, F_W + g * FGROUP_DIM:F_W + (g + 1) * FGROUP_DIM] = yy[:, FGROUP_DIM:]


def _inproj(x, mods, g_mix, w_in, wf, cos_t, sa_t, sb_t):
    b, s, d = x.shape
    tm = TM_PROJ
    tok = lambda bi, i: (bi, i, 0)
    return pl.pallas_call(
        _inproj_kernel,
        grid=(b, s // tm),
        in_specs=[pl.BlockSpec((1, tm, d), tok),
                  pl.BlockSpec((1, N_MOD, d), lambda bi, i: (bi, 0, 0)),
                  pl.BlockSpec((1, d), lambda bi, i: (0, 0)),
                  pl.BlockSpec(w_in.shape, lambda bi, i: (0, 0)),
                  pl.BlockSpec(wf.shape, lambda bi, i: (0, 0, 0)),
                  pl.BlockSpec((tm, HEAD_W), lambda bi, i: (i, 0)),
                  pl.BlockSpec((tm, HEAD_W), lambda bi, i: (i, 0)),
                  pl.BlockSpec((tm, HEAD_W), lambda bi, i: (i, 0))],
        out_specs=[pl.BlockSpec((1, tm, QK_W), tok),
                   pl.BlockSpec((1, tm, QK_W), tok),
                   pl.BlockSpec((1, tm, QK_W), tok),
                   pl.BlockSpec((1, tm, 2 * F_W), tok)],
        out_shape=[jax.ShapeDtypeStruct((b, s, QK_W), BF16),
                   jax.ShapeDtypeStruct((b, s, QK_W), BF16),
                   jax.ShapeDtypeStruct((b, s, QK_W), BF16),
                   jax.ShapeDtypeStruct((b, s, 2 * F_W), F32)],
        compiler_params=_cparams("parallel", "arbitrary"),
        name="inproj",
    )(x, mods, g_mix, w_in, wf, cos_t, sa_t, sb_t)


def _ctxproj_kernel(x_ref, mod_ref, g_ref, w_ref, k_ref, v_ref):
    m = mod_ref[0]
    h = _norm_mod(x_ref[0], g_ref[...], m[0:1], m[1:2]).astype(BF16)
    k_ref[0] = jnp.dot(h, w_ref[:, QK_W:2 * QK_W], preferred_element_type=F32).astype(BF16)
    v_ref[0] = jnp.dot(h, w_ref[:, 2 * QK_W:3 * QK_W], preferred_element_type=F32).astype(BF16)


def _ctxproj(ctx, mods, g_mix, w_in):
    b, n, d = ctx.shape
    return pl.pallas_call(
        _ctxproj_kernel,
        grid=(b,),
        in_specs=[pl.BlockSpec((1, n, d), lambda bi: (bi, 0, 0)),
                  pl.BlockSpec((1, N_MOD, d), lambda bi: (2, 0, 0)),
                  pl.BlockSpec((1, d), lambda bi: (0, 0)),
                  pl.BlockSpec(w_in.shape, lambda bi: (0, 0))],
        out_specs=[pl.BlockSpec((1, n, QK_W), lambda bi: (bi, 0, 0)),
                   pl.BlockSpec((1, n, QK_W), lambda bi: (bi, 0, 0))],
        out_shape=[jax.ShapeDtypeStruct((b, n, QK_W), BF16),
                   jax.ShapeDtypeStruct((b, n, QK_W), BF16)],
        compiler_params=_cparams("arbitrary"),
        name="ctxproj",
    )(ctx, mods, g_mix, w_in)


def _attn_kernel(q_ref, kc_ref, vc_ref, kl_ref, vl_ref, lam_ref, g_ref, o_ref,
                 m_ref, l_ref, acc_ref):
    q = q_ref[0]
    lane = lax.broadcasted_iota(jnp.int32, q.shape, 1)
    zero = jnp.zeros_like(q)
    q0 = jnp.where(lane < HEAD_DIM, q, zero)
    q1 = jnp.where(lane >= HEAD_DIM, q, zero)
    qs = (q0, q1)
    contract_last = (((1,), (1,)), ((), ()))
    m_ref[...] = jnp.full(m_ref.shape, -1e30, F32)
    l_ref[...] = jnp.zeros(l_ref.shape, F32)
    acc_ref[...] = jnp.zeros(acc_ref.shape, F32)

    def step(kb, vb):
        nk = kb.shape[0] // HEAD_W
        for mi in range(2):
            s = lax.dot_general(qs[mi], kb, contract_last, preferred_element_type=F32)
            m_old = m_ref[mi]
            m_new = jnp.maximum(m_old, jnp.max(s, axis=-1, keepdims=True))
            alpha = jnp.exp2(m_old - m_new)
            p = jnp.exp2(s - jnp.concatenate([m_new] * nk, axis=1))
            psum = p[:, 0:HEAD_W]
            for cblk in range(1, nk):
                psum = psum + p[:, cblk * HEAD_W:(cblk + 1) * HEAD_W]
            l_ref[mi] = alpha * l_ref[mi] + psum
            acc_ref[mi] = alpha * acc_ref[mi] + jnp.dot(p.astype(BF16), vb,
                                                       preferred_element_type=F32)
            m_ref[mi] = m_new

    step(kc_ref[0], vc_ref[0])

    def body(i, c):
        off = pl.multiple_of(i * TK, TK)
        step(kl_ref[0, pl.ds(off, TK), :], vl_ref[0, pl.ds(off, TK), :])
        return c

    lax.fori_loop(0, kl_ref.shape[1] // TK, body, 0)

    lp = lam_ref[...]
    t1 = jnp.sum(lp[0:1] * lp[1:2], axis=-1, keepdims=True)
    t2 = jnp.sum(lp[2:3] * lp[3:4], axis=-1, keepdims=True)
    lam = jnp.exp(t1) - jnp.exp(t2) + LAMBDA_INIT
    l0 = jnp.sum(l_ref[0], axis=-1, keepdims=True)
    l1 = jnp.sum(l_ref[1], axis=-1, keepdims=True)
    o = acc_ref[0] / l0 - lam * (acc_ref[1] / l1)
    ms = jnp.mean(o * o, axis=-1, keepdims=True)
    o = o * lax.rsqrt(ms + EPS) * g_ref[...] * (1.0 - LAMBDA_INIT)
    o_ref[0] = o.astype(BF16)


def _attention(q, kc, vc, kl, vl, lam_p, g_subln):
    b, s, _ = q.shape
    n_ctx = kc.shape[1]
    return pl.pallas_call(
        _attn_kernel,
        grid=(b, N_HEADS, s // TQ),
        in_specs=[pl.BlockSpec((1, TQ, HEAD_W), lambda bi, h, i: (bi, i, h)),
                  pl.BlockSpec((1, n_ctx, HEAD_W), lambda bi, h, i: (bi, 0, h)),
                  pl.BlockSpec((1, n_ctx, HEAD_W), lambda bi, h, i: (bi, 0, h)),
                  pl.BlockSpec((1, s, HEAD_W), lambda bi, h, i: (bi, 0, h)),
                  pl.BlockSpec((1, s, HEAD_W), lambda bi, h, i: (bi, 0, h)),
                  pl.BlockSpec((4, HEAD_DIM), lambda bi, h, i: (0, 0)),
                  pl.BlockSpec((1, HEAD_W), lambda bi, h, i: (0, 0))],
        out_specs=pl.BlockSpec((1, TQ, HEAD_W), lambda bi, h, i: (bi, i, h)),
        out_shape=jax.ShapeDtypeStruct((b, s, QK_W), BF16),
        scratch_shapes=[pltpu.VMEM((2, TQ, HEAD_W), F32)] * 3,
        compiler_params=_cparams("parallel", "parallel", "arbitrary"),
        name="diffattn",
    )(q, kc, vc, kl, vl, lam_p, g_subln)


def _dft1_kernel(y_ref, g_ref, br_ref, bi_ref):
    rows = DFT_N1 * F1_NB
    yb = y_ref[0].reshape(rows, 2 * F_W).astype(BF16)
    p = jnp.dot(g_ref[0], yb, preferred_element_type=F32)
    top, bot = p[:rows], p[rows:]
    br_ref[0] = (top[:, :F_W] - bot[:, F_W:]).reshape(DFT_N1, F1_NB, F_W)
    bi_ref[0] = (-top[:, F_W:] - bot[:, :F_W]).reshape(DFT_N1, F1_NB, F_W)


def _dft1(y4, gmat):
    b = y4.shape[0]
    rows = DFT_N1 * F1_NB
    return pl.pallas_call(
        _dft1_kernel,
        grid=(DFT_N2 // F1_NB, b),
        in_specs=[pl.BlockSpec((1, DFT_N1, F1_NB, 2 * F_W), lambda j, bi: (bi, 0, j, 0)),
                  pl.BlockSpec((1, 2 * rows, rows), lambda j, bi: (j, 0, 0))],
        out_specs=[pl.BlockSpec((1, DFT_N1, F1_NB, F_W), lambda j, bi: (bi, 0, j, 0)),
                   pl.BlockSpec((1, DFT_N1, F1_NB, F_W), lambda j, bi: (bi, 0, j, 0))],
        out_shape=[jax.ShapeDtypeStruct((b, DFT_N1, DFT_N2, F_W), F32),
                   jax.ShapeDtypeStruct((b, DFT_N1, DFT_N2, F_W), F32)],
        compiler_params=_cparams("arbitrary", "arbitrary"),
        name="dft1",
    )(y4, gmat)


def _dft2_kernel(br_ref, bi_ref, c_ref, s_ref, o_ref):
    for j in range(F2_KB):
        r = (jnp.dot(c_ref[...], br_ref[0, j].astype(BF16), preferred_element_type=F32)
             + jnp.dot(s_ref[...], bi_ref[0, j].astype(BF16), preferred_element_type=F32))
        o_ref[0, :, j, :] = r


def _dft2(br4, bi4, c2, s2):
    b = br4.shape[0]
    blk = (1, F2_KB, DFT_N2, F_W)
    return pl.pallas_call(
        _dft2_kernel,
        grid=(b, DFT_N1 // F2_KB),
        in_specs=[pl.BlockSpec(blk, lambda bi, j: (bi, j, 0, 0)),
                  pl.BlockSpec(blk, lambda bi, j: (bi, j, 0, 0)),
                  pl.BlockSpec((DFT_N2, DFT_N2), lambda bi, j: (0, 0)),
                  pl.BlockSpec((DFT_N2, DFT_N2), lambda bi, j: (0, 0))],
        out_specs=pl.BlockSpec((1, DFT_N2, F2_KB, F_W), lambda bi, j: (bi, 0, j, 0)),
        out_shape=jax.ShapeDtypeStruct((b, DFT_N2, DFT_N1, F_W), F32),
        compiler_params=_cparams("parallel", "arbitrary"),
        name="dft2",
    )(br4, bi4, c2, s2)


def _outproj_kernel(o_ref, f_ref, x_ref, mod_ref, g_ref, wo_ref, wr_ref, br_ref, tri_ref,
                    x1_ref, h3_ref, meta_ref, gate_ref, cnt_out_ref, cnt_ref):
    first = (pl.program_id(0) == 0) & (pl.program_id(1) == 0)

    @pl.when(first)
    def _():
        cnt_ref[...] = jnp.zeros(cnt_ref.shape, F32)

    m = mod_ref[0]
    mix = (jnp.dot(o_ref[0], wo_ref[0:QK_W, :], preferred_element_type=F32)
           + jnp.dot(f_ref[0].astype(BF16), wo_ref[QK_W:, :], preferred_element_type=F32))
    x1 = x_ref[0] + m[2:3] * mix
    x1_ref[0] = x1
    h2 = _norm_mod(x1, g_ref[...], m[3:4], m[4:5])
    _store_rows(h3_ref, h2)
    h_hi = h2.astype(BF16)
    h_lo = (h2 - h_hi.astype(F32)).astype(BF16)
    hw = jnp.dot(h_hi, wr_ref[...], preferred_element_type=F32)
    lg = (hw[:, :128] + hw[:, 128:]
          + jnp.dot(h_lo, wr_ref[:, 0:128], preferred_element_type=F32)
          + br_ref[...])
    lane = lax.broadcasted_iota(jnp.int32, lg.shape, 1)
    ninf = jnp.float32(-jnp.inf)
    big = jnp.int32(lg.shape[1])
    gl = jnp.where(lane < N_GROUPS, lg, ninf)
    gmax = jnp.max(gl, axis=-1, keepdims=True)
    grp = jnp.min(jnp.where(gl == gmax, lane, big), axis=-1, keepdims=True)
    pg = 1.0 / jnp.sum(jnp.exp(gl - gmax), axis=-1, keepdims=True)
    e_lane = lane - N_GROUPS
    emask = (e_lane >= 0) & (e_lane < N_EXPERTS) & ((e_lane >> 3) == grp)
    el = jnp.where(emask, lg, ninf)
    t1 = jnp.max(el, axis=-1, keepdims=True)
    i1 = jnp.min(jnp.where(el == t1, lane, big), axis=-1, keepdims=True)
    el2 = jnp.where(lane == i1, ninf, el)
    t2 = jnp.max(el2, axis=-1, keepdims=True)
    i2 = jnp.min(jnp.where(el2 == t2, lane, big), axis=-1, keepdims=True)
    dd = jnp.exp(t2 - t1)
    w1 = pg / (1.0 + dd)
    w2 = pg * dd / (1.0 + dd)
    gate_ref[...] = jnp.where(lane == 0, w1, jnp.where(lane == 1, w2, 0.0))
    hit1 = lane == i1
    hit2 = lane == i2
    oh = jnp.where(hit1 | hit2, 1.0, 0.0)
    before = jnp.dot(tri_ref[...], oh.astype(BF16), preferred_element_type=F32) + cnt_ref[0:1, :]
    r1 = jnp.sum(jnp.where(hit1, before, 0.0), axis=-1, keepdims=True).astype(jnp.int32)
    r2 = jnp.sum(jnp.where(hit2, before, 0.0), axis=-1, keepdims=True).astype(jnp.int32)
    cnt_ref[0:1, :] = cnt_ref[0:1, :] + jnp.sum(oh, axis=0, keepdims=True)
    cnt_out_ref[...] = cnt_ref[...]
    meta = jnp.where(lane == 0, i1 - N_GROUPS,
                     jnp.where(lane == 1, i2 - N_GROUPS,
                               jnp.where(lane == 2, r1, jnp.where(lane == 3, r2, 0))))
    meta_ref[...] = jnp.transpose(meta)[0:8, :]


def _outproj(attn_o, four, x, mods, g_ffn, w_out, w_r, b_r):
    b, s, d = x.shape
    tm = TM_PROJ
    nt = s // tm
    tok = lambda bi, i: (bi, i, 0)
    flat = lambda bi, i: (bi * nt + i, 0)
    tri = jnp.asarray(np.tril(np.ones((tm, tm), np.float32), -1)).astype(BF16)
    return pl.pallas_call(
        _outproj_kernel,
        grid=(b, nt),
        in_specs=[pl.BlockSpec((1, tm, QK_W), tok),
                  pl.BlockSpec((1, tm, F_W), tok),
                  pl.BlockSpec((1, tm, d), tok),
                  pl.BlockSpec((1, N_MOD, d), lambda bi, i: (bi, 0, 0)),
                  pl.BlockSpec((1, d), lambda bi, i: (0, 0)),
                  pl.BlockSpec(w_out.shape, lambda bi, i: (0, 0)),
                  pl.BlockSpec(w_r.shape, lambda bi, i: (0, 0)),
                  pl.BlockSpec(b_r.shape, lambda bi, i: (0, 0)),
                  pl.BlockSpec((tm, tm), lambda bi, i: (0, 0))],
        out_specs=[pl.BlockSpec((1, tm, d), tok),
                   pl.BlockSpec((tm * ROW_TILE, 128), flat),
                   pl.BlockSpec((8, tm), lambda bi, i: (0, bi * nt + i)),
                   pl.BlockSpec((tm, 128), flat),
                   pl.BlockSpec((8, 128), lambda bi, i: (0, 0))],
        out_shape=[jax.ShapeDtypeStruct((b, s, d), F32),
                   jax.ShapeDtypeStruct((b * s * ROW_TILE, 128), U32),
                   jax.ShapeDtypeStruct((8, b * s), jnp.int32),
                   jax.ShapeDtypeStruct((b * s, 128), F32),
                   jax.ShapeDtypeStruct((8, 128), F32)],
        scratch_shapes=[pltpu.VMEM((8, 128), F32)],
        compiler_params=_cparams("arbitrary", "arbitrary"),
        name="outproj",
    )(attn_o, four, x, mods, g_ffn, w_out, w_r, b_r, tri)


def _row_slice(row):
    return pl.ds(pl.multiple_of(row * ROW_TILE, ROW_TILE), ROW_TILE)


def _row_copy(src_ref, src_row, dst_ref, dst_row, sem):
    return pltpu.make_async_copy(src_ref.at[_row_slice(src_row)], dst_ref.at[_row_slice(dst_row)], sem)


def _bf16_bits(x):
    return lax.bitcast_convert_type(x.astype(BF16).astype(F32), U32)


def _rows_2d(ref, n_rows):
    hi, lo = [], []
    for cblk in range(ROW_TILE):
        w = ref[pl.ds(cblk, n_rows, stride=ROW_TILE), :]
        hi.append(lax.bitcast_convert_type(w & jnp.uint32(0xFFFF0000), F32).astype(BF16))
        lo.append(lax.bitcast_convert_type(w << 16, F32).astype(BF16))
    return jnp.concatenate(hi + lo, axis=1)


def _store_rows(ref, val):
    for cblk in range(ROW_TILE):
        hi = _bf16_bits(val[:, cblk * 128:(cblk + 1) * 128])
        lo = _bf16_bits(val[:, (cblk + ROW_TILE) * 128:(cblk + ROW_TILE + 1) * 128])
        ref[pl.ds(cblk, val.shape[0], stride=ROW_TILE), :] = hi | (lo >> 16)


def _dispatch_kernel(dest_ref, h_ref, zeros_ref, xs_ref, sem):
    del zeros_ref
    n = 2 * TD_DISPATCH

    def issue(a, c):
        _row_copy(h_ref, a >> 1, xs_ref, dest_ref[0, 0, a], sem).start()
        return c

    lax.fori_loop(0, n, issue, 0, unroll=DMA_UNROLL)

    def drain(a, c):
        _row_copy(h_ref, 0, xs_ref, 0, sem).wait()
        return c

    lax.fori_loop(0, n, drain, 0, unroll=DMA_UNROLL)


def _dispatch(dest3, h3, xs_zeros):
    t = h3.shape[0] // ROW_TILE
    return pl.pallas_call(
        _dispatch_kernel,
        grid=(t // TD_DISPATCH,),
        in_specs=[pl.BlockSpec((1, 1, 2 * TD_DISPATCH), lambda i: (i, 0, 0),
                               memory_space=pltpu.SMEM),
                  pl.BlockSpec((TD_DISPATCH * ROW_TILE, 128), lambda i: (i, 0)),
                  pl.BlockSpec(memory_space=pl.ANY)],
        out_specs=pl.BlockSpec(memory_space=pl.ANY),
        out_shape=jax.ShapeDtypeStruct(xs_zeros.shape, xs_zeros.dtype),
        scratch_shapes=[pltpu.SemaphoreType.DMA(())],
        input_output_aliases={2: 0},
        compiler_params=_cparams("arbitrary"),
        name="dispatch",
    )(dest3, h3, xs_zeros)


def _experts_kernel(be_ref, nu_ref, nxt_ref, xs_ref, wg_hbm, wu_hbm, wd_hbm, ys_ref,
                    wgb, wub, wdb, sg, su, sd, sem):
    j = pl.program_id(0)
    used = j < nu_ref[0]
    e = be_ref[j]
    new_expert = (j == 0) | (e != be_ref[jnp.maximum(j - 1, 0)])
    e_next = nxt_ref[e]

    def weight_copies(idx):
        return (pltpu.make_async_copy(wg_hbm.at[idx], sg, sem.at[0]),
                pltpu.make_async_copy(wu_hbm.at[idx], su, sem.at[1]),
                pltpu.make_async_copy(wd_hbm.at[idx], sd, sem.at[2]))

    @pl.when(j == 0)
    def _():
        for cp in weight_copies(e):
            cp.start()

    @pl.when(used & new_expert)
    def _():
        for cp in weight_copies(e):
            cp.wait()
        wgb[...] = sg[...].astype(BF16)
        wub[...] = su[...].astype(BF16)
        wdb[...] = sd[...].astype(BF16)

    @pl.when(used & new_expert & (e_next != e))
    def _():
        for cp in weight_copies(e_next):
            cp.start()

    @pl.when(used)
    def _():
        xb = _rows_2d(xs_ref, MOE_BLK)
        gate = jnp.dot(xb, wgb[...], preferred_element_type=F32)
        up = jnp.dot(xb, wub[...], preferred_element_type=F32)
        hid = (gate * jax.nn.sigmoid(gate) * up).astype(BF16)
        _store_rows(ys_ref, jnp.dot(hid, wdb[...], preferred_element_type=F32))

    @pl.when(pl.program_id(0) >= nu_ref[0])
    def _():
        ys_ref[...] = jnp.zeros_like(ys_ref)


def _experts(blk_expert, n_used, next_expert, xs, wg, wu, wd):
    rows = xs.shape[0] // ROW_TILE
    d = wg.shape[1]
    nb = rows // MOE_BLK
    row_blk = lambda j, be, nu, nx: (jnp.minimum(j, nu[0] - 1), 0)
    out_blk = lambda j, be, nu, nx: (j, 0)
    return pl.pallas_call(
        _experts_kernel,
        grid_spec=pltpu.PrefetchScalarGridSpec(
            num_scalar_prefetch=3,
            grid=(nb,),
            in_specs=[pl.BlockSpec((MOE_BLK * ROW_TILE, 128), row_blk),
                      pl.BlockSpec(memory_space=pl.ANY),
                      pl.BlockSpec(memory_space=pl.ANY),
                      pl.BlockSpec(memory_space=pl.ANY)],
            out_specs=pl.BlockSpec((MOE_BLK * ROW_TILE, 128), out_blk),
            scratch_shapes=[pltpu.VMEM((d, D_EXPERT), BF16), pltpu.VMEM((d, D_EXPERT), BF16),
                            pltpu.VMEM((D_EXPERT, d), BF16),
                            pltpu.VMEM((d, D_EXPERT), F32), pltpu.VMEM((d, D_EXPERT), F32),
                            pltpu.VMEM((D_EXPERT, d), F32),
                            pltpu.SemaphoreType.DMA((3,))]),
        out_shape=jax.ShapeDtypeStruct(xs.shape, xs.dtype),
        compiler_params=_cparams("arbitrary"),
        name="experts",
    )(blk_expert, n_used, next_expert, xs, wg, wu, wd)


def _combine_kernel(dest_ref, dest_next_ref, ys_ref, x1_ref, gate_ref, mod_ref, g_ref, o_ref,
                    ya, yb, sem):
    step = pl.program_id(0) * pl.num_programs(1) + pl.program_id(1)
    n_steps = pl.num_programs(0) * pl.num_programs(1)
    slot = step % 2

    def start_gathers(idx_ref, to_slot):
        def issue(r, c):
            _row_copy(ys_ref, idx_ref[0, 0, 2 * r], ya.at[to_slot], r, sem.at[to_slot]).start()
            _row_copy(ys_ref, idx_ref[0, 0, 2 * r + 1], yb.at[to_slot], r, sem.at[to_slot]).start()
            return c

        lax.fori_loop(0, TD, issue, 0, unroll=DMA_UNROLL)

    @pl.when(step == 0)
    def _():
        start_gathers(dest_ref, 0)

    @pl.when(step + 1 < n_steps)
    def _():
        start_gathers(dest_next_ref, 1 - slot)

    def drain(r, c):
        _row_copy(ys_ref, 0, ya.at[slot], 0, sem.at[slot]).wait()
        _row_copy(ys_ref, 0, yb.at[slot], 0, sem.at[slot]).wait()
        return c

    lax.fori_loop(0, TD, drain, 0, unroll=DMA_UNROLL)
    gt = gate_ref[...]
    moe = (gt[:, 0:1] * _rows_2d(ya.at[slot], TD).astype(F32)
           + gt[:, 1:2] * _rows_2d(yb.at[slot], TD).astype(F32))
    x2 = x1_ref[0] + mod_ref[0][5:6] * moe
    ms = jnp.mean(x2 * x2, axis=-1, keepdims=True)
    o_ref[0] = x2 * lax.rsqrt(ms + EPS) * g_ref[...]


def _combine(dest3, ys, x1, gates, mods, g_final):
    b, s, d = x1.shape
    nt = s // TD
    return pl.pallas_call(
        _combine_kernel,
        grid=(b, nt),
        in_specs=[pl.BlockSpec((1, 1, 2 * TD), lambda bi, i: (bi * nt + i, 0, 0),
                               memory_space=pltpu.SMEM),
                  pl.BlockSpec((1, 1, 2 * TD),
                               lambda bi, i: (jnp.minimum(bi * nt + i + 1, b * nt - 1), 0, 0),
                               memory_space=pltpu.SMEM),
                  pl.BlockSpec(memory_space=pl.ANY),
                  pl.BlockSpec((1, TD, d), lambda bi, i: (bi, i, 0)),
                  pl.BlockSpec((TD, 128), lambda bi, i: (bi * nt + i, 0)),
                  pl.BlockSpec((1, N_MOD, d), lambda bi, i: (bi, 0, 0)),
                  pl.BlockSpec((1, d), lambda bi, i: (0, 0))],
        out_specs=pl.BlockSpec((1, TD, d), lambda bi, i: (bi, i, 0)),
        out_shape=jax.ShapeDtypeStruct((b, s, d), F32),
        scratch_shapes=[pltpu.VMEM((2, TD * ROW_TILE, 128), U32),
                        pltpu.VMEM((2, TD * ROW_TILE, 128), U32),
                        pltpu.SemaphoreType.DMA((2,))],
        compiler_params=_cparams("arbitrary", "arbitrary"),
        name="combine",
    )(dest3, dest3, ys, x1, gates, mods, g_final)


@functools.lru_cache(maxsize=None)
def _rope_tables(rows):
    r, col = np.meshgrid(np.arange(rows), np.arange(GRID_W), indexing='ij')
    pos = np.stack([r.reshape(-1), col.reshape(-1)], axis=-1).astype(np.float32)
    inv_freq = (np.float32(ROPE_THETA)
                ** (-np.arange(0, ROPE_AXIS, 2, dtype=np.float32) / np.float32(ROPE_AXIS))).astype(np.float32)
    ang = (pos[:, :, None] * inv_freq).astype(np.float32)
    ang = np.concatenate([ang, ang], axis=-1).astype(np.float64)
    n = ang.shape[0]
    cos = np.tile(np.cos(ang).reshape(n, HEAD_DIM), (1, 2)).astype(np.float32)
    sin = np.tile(np.sin(ang).reshape(n, HEAD_DIM), (1, 2)).astype(np.float32)
    upper = (np.arange(HEAD_W) % ROPE_AXIS) >= ROPE_HALF
    sa = np.where(upper, sin, np.float32(0.0))
    sb = np.where(upper, np.float32(0.0), -sin)
    return cos, sa, sb


@functools.lru_cache(maxsize=None)
def _dft_constants(n_pos):
    c = np.arange(FGROUP_DIM)
    ang_c = 2.0 * np.pi * ((c[:, None] * c[None, :]) % FGROUP_DIM) / FGROUP_DIM
    norm = 1.0 / math.sqrt(n_pos * FGROUP_DIM)
    cmat = (np.cos(ang_c) * norm).astype(np.float32)
    smat = (np.sin(ang_c) * norm).astype(np.float32)
    k1 = np.arange(DFT_N1)[None, :, None]
    n1 = np.arange(DFT_N1)[None, None, :]
    n2 = np.arange(DFT_N2)[:, None, None]
    ang_g = 2.0 * np.pi * ((k1 * (DFT_N2 * n1 + n2)) % n_pos) / n_pos
    gsmall = np.stack([np.cos(ang_g), np.sin(ang_g)], axis=1)
    gsmall = gsmall.reshape(DFT_N2 // F1_NB, F1_NB, 2, DFT_N1, DFT_N1)
    gmat = np.einsum('japkn,ab->jpkanb', gsmall, np.eye(F1_NB)).reshape(
        DFT_N2 // F1_NB, 2 * DFT_N1 * F1_NB, DFT_N1 * F1_NB).astype(BF16)
    k2 = np.arange(DFT_N2)
    ang_2 = 2.0 * np.pi * ((k2[:, None] * k2[None, :]) % DFT_N2) / DFT_N2
    c2 = np.cos(ang_2).astype(np.float32)
    s2 = np.sin(ang_2).astype(np.float32)
    return cmat, smat, gmat, c2, s2


def kernel(x, c, ctx, c_ctx, w_ada, b_ada, g_mix_norm, g_ffn_norm, w_in, lambda_q1, lambda_k1, lambda_q2, lambda_k2, g_subln, w_fourier, w_out, w_router_group, b_router_group, w_router_expert, b_router_expert, w_gate, w_up, w_down, g_final):
    b, s, d = x.shape
    t = b * s
    assert d == D_MODEL and s == DFT_N1 * DFT_N2 and s % GRID_W == 0 and b == 2

    cc = jnp.concatenate([c, c_ctx[None, :], jnp.zeros((8 - b - 1, d), F32)], axis=0)
    mods = _adaln(cc, w_ada[0], b_ada[0]).reshape(8, N_MOD, d)

    cmat, smat, gmat, c2, s2 = _dft_constants(s)
    wf = _wfold(jnp.asarray(cmat), jnp.asarray(smat), w_fourier[0])
    cos_t, sa_t, sb_t = _rope_tables(s // GRID_W)

    w_in_b = w_in[0].astype(BF16)
    g_mix = g_mix_norm[0].reshape(1, d)
    q, kl, vl, y = _inproj(x, mods, g_mix, w_in_b, wf, cos_t, sa_t, sb_t)
    kc, vc = _ctxproj(ctx, mods, g_mix, w_in_b)

    lam_p = jnp.stack([lambda_q1[0], lambda_k1[0], lambda_q2[0], lambda_k2[0]], axis=0)
    attn_o = _attention(q, kc, vc, kl, vl, lam_p, g_subln[0].reshape(1, HEAD_W))

    br, bi = _dft1(y.reshape(b, DFT_N1, DFT_N2, 2 * F_W), jnp.asarray(gmat))
    four = _dft2(br, bi, jnp.asarray(c2).astype(BF16),
                 jnp.asarray(s2).astype(BF16)).reshape(b, s, F_W)

    n_r = N_GROUPS + N_EXPERTS
    w_r = jnp.concatenate([w_router_group[0], w_router_expert[0],
                           jnp.zeros((d, 128 - n_r), F32)], axis=1)
    b_r = jnp.concatenate([b_router_group[0], b_router_expert[0],
                           jnp.zeros((128 - n_r,), F32)]).reshape(1, 128)
    w_r_hi = w_r.astype(BF16)
    w_r_lo = (w_r - w_r_hi.astype(F32)).astype(BF16)
    x1, h3, meta, gates, cnt = _outproj(attn_o, four, x, mods, g_ffn_norm[0].reshape(1, d),
                                        w_out[0].astype(BF16),
                                        jnp.concatenate([w_r_hi, w_r_lo], axis=1), b_r)

    e_tk = jnp.stack([meta[0], meta[1]], axis=1)
    rank_tk = jnp.stack([meta[2], meta[3]], axis=1)
    counts = cnt[0, N_GROUPS:N_GROUPS + N_EXPERTS].astype(jnp.int32)
    nblk = (counts + MOE_BLK - 1) // MOE_BLK
    blk_end = jnp.cumsum(nblk)
    blk_start = blk_end - nblk
    onehot = (e_tk[:, :, None] == jnp.arange(N_EXPERTS, dtype=jnp.int32)).astype(F32)
    first_blk = jnp.einsum('tke,e->tk', onehot, blk_start.astype(F32),
                           precision=lax.Precision.HIGHEST).astype(jnp.int32)
    dest = first_blk * MOE_BLK + rank_tk
    n_blocks = t * TOP_K // MOE_BLK + N_EXPERTS
    blk_ids = jnp.arange(n_blocks, dtype=jnp.int32)
    blk_expert = jnp.minimum(
        jnp.sum((blk_end[None, :] <= blk_ids[:, None]).astype(jnp.int32), axis=1),
        N_EXPERTS - 1).astype(jnp.int32)
    n_used = blk_end[-1:].astype(jnp.int32)
    dest3 = dest.reshape(t // TD, 1, 2 * TD)

    xs = _dispatch(dest.reshape(t // TD_DISPATCH, 1, 2 * TD_DISPATCH), h3,
                   jnp.zeros((n_blocks * MOE_BLK * ROW_TILE, 128), U32))
    e_ids = jnp.arange(N_EXPERTS, dtype=jnp.int32)
    later = (e_ids[None, :] > e_ids[:, None]) & (nblk[None, :] > 0)
    next_expert = jnp.min(jnp.where(later, e_ids[None, :], N_EXPERTS), axis=1)
    next_expert = jnp.where(next_expert == N_EXPERTS, e_ids, next_expert).astype(jnp.int32)
    ys = _experts(blk_expert, n_used, next_expert, xs, w_gate[0], w_up[0], w_down[0])
    return _combine(dest3, ys, x1, gates, mods, g_final.reshape(1, d))
```

```python
import functools
import math

import numpy as np
import jax
import jax.numpy as jnp
from jax import lax
from jax.experimental import pallas as pl
from jax.experimental.pallas import tpu as pltpu

F32 = jnp.float32
BF16 = jnp.bfloat16

D_MODEL = 1024
GRID_W = 64
N_HEADS = 4
HEAD_DIM = 64
HEAD_W = 2 * HEAD_DIM
QK_W = N_HEADS * HEAD_W
N_FGROUPS = 4
FGROUP_DIM = 128
F_W = N_FGROUPS * FGROUP_DIM
ROPE_THETA = 10000.0
ROPE_AXIS = HEAD_DIM // 2
ROPE_HALF = ROPE_AXIS // 2
N_GROUPS = 4
EXPERTS_PER_GROUP = 8
N_EXPERTS = N_GROUPS * EXPERTS_PER_GROUP
TOP_K = 2
D_EXPERT = 512
N_MOD = 6
EPS = 1e-6
LAMBDA_INIT = 0.8 - 0.6 * math.exp(-0.3 * 0)
LOG2_E = 1.4426950408889634

DFT_N1 = 64
DFT_N2 = 128

TM_PROJ = 512
TQ = 1024
TK = 1024
F1_NB = 8
F2_KB = 8
MOE_BLK = 256
ROW_TILE = D_MODEL // 256
U32 = jnp.uint32
TD = 512
TD_DISPATCH = 1024
DMA_UNROLL = 16
VMEM_LIMIT = 48 * 1024 * 1024


def _cparams(*sem):
    return pltpu.CompilerParams(dimension_semantics=sem, vmem_limit_bytes=VMEM_LIMIT)


def _adaln_kernel(c_ref, w_ref, b_ref, o_ref):
    cc = c_ref[...]
    s = cc * jax.nn.sigmoid(cc)
    o_ref[...] = jnp.dot(s, w_ref[...], preferred_element_type=F32,
                         precision=lax.Precision.HIGHEST) + b_ref[...]


def _adaln(cc, w_ada, b_ada):
    n = w_ada.shape[1]
    tn = 1536
    return pl.pallas_call(
        _adaln_kernel,
        grid=(n // tn,),
        in_specs=[pl.BlockSpec((8, D_MODEL), lambda j: (0, 0)),
                  pl.BlockSpec((D_MODEL, tn), lambda j: (0, j)),
                  pl.BlockSpec((1, tn), lambda j: (0, j))],
        out_specs=pl.BlockSpec((8, tn), lambda j: (0, j)),
        out_shape=jax.ShapeDtypeStruct((8, n), F32),
        compiler_params=_cparams("arbitrary"),
        name="adaln",
    )(cc, w_ada, b_ada.reshape(1, n))


def _wfold_kernel(c_ref, s_ref, w_ref, o_ref):
    w = w_ref[0]
    o_ref[0, :, :FGROUP_DIM] = jnp.dot(c_ref[...], w, preferred_element_type=F32,
                                       precision=lax.Precision.HIGHEST).astype(BF16)
    o_ref[0, :, FGROUP_DIM:] = jnp.dot(s_ref[...], w, preferred_element_type=F32,
                                       precision=lax.Precision.HIGHEST).astype(BF16)


def _wfold(cmat, smat, w_fourier):
    return pl.pallas_call(
        _wfold_kernel,
        grid=(N_FGROUPS,),
        in_specs=[pl.BlockSpec((FGROUP_DIM, FGROUP_DIM), lambda g: (0, 0)),
                  pl.BlockSpec((FGROUP_DIM, FGROUP_DIM), lambda g: (0, 0)),
                  pl.BlockSpec((1, FGROUP_DIM, FGROUP_DIM), lambda g: (g, 0, 0))],
        out_specs=pl.BlockSpec((1, FGROUP_DIM, 2 * FGROUP_DIM), lambda g: (g, 0, 0)),
        out_shape=jax.ShapeDtypeStruct((N_FGROUPS, FGROUP_DIM, 2 * FGROUP_DIM), BF16),
        compiler_params=_cparams("arbitrary"),
        name="wfold",
    )(cmat, smat, w_fourier)


def _norm_mod(x, g, shift, scale):
    ms = jnp.mean(x * x, axis=-1, keepdims=True)
    y = x * lax.rsqrt(ms + EPS) * g
    return y * (1.0 + scale) + shift


def _rope_slab(p, cos, sa, sb):
    return (p * cos + pltpu.roll(p, ROPE_HALF, 1) * sa
            + pltpu.roll(p, HEAD_W - ROPE_HALF, 1) * sb)


def _inproj_kernel(x_ref, mod_ref, g_ref, w_ref, wf_ref, cos_ref, sa_ref, sb_ref,
                   q_ref, k_ref, v_ref, y_ref):
    m = mod_ref[0]
    h = _norm_mod(x_ref[0], g_ref[...], m[0:1], m[1:2]).astype(BF16)
    cos, sa, sb = cos_ref[...], sa_ref[...], sb_ref[...]
    scale = HEAD_DIM ** -0.5 * LOG2_E
    pq = jnp.dot(h, w_ref[:, 0:QK_W], preferred_element_type=F32)
    for hh in range(N_HEADS):
        sl = slice(hh * HEAD_W, (hh + 1) * HEAD_W)
        q_ref[0, :, sl] = (_rope_slab(pq[:, sl], cos, sa, sb) * scale).astype(BF16)
    pk = jnp.dot(h, w_ref[:, QK_W:2 * QK_W], preferred_element_type=F32)
    for hh in range(N_HEADS):
        sl = slice(hh * HEAD_W, (hh + 1) * HEAD_W)
        k_ref[0, :, sl] = _rope_slab(pk[:, sl], cos, sa, sb).astype(BF16)
    v_ref[0] = jnp.dot(h, w_ref[:, 2 * QK_W:3 * QK_W], preferred_element_type=F32).astype(BF16)
    pf = jnp.dot(h, w_ref[:, 3 * QK_W:], preferred_element_type=F32).astype(BF16)
    for g in range(N_FGROUPS):
        yy = jnp.dot(pf[:, g * FGROUP_DIM:(g + 1) * FGROUP_DIM], wf_ref[g],
                     preferred_element_type=F32)
        y_ref[0, :, g * FGROUP_DIM:(g + 1) * FGROUP_DIM] = yy[:, :FGROUP_DIM]
        y_ref[0, :, F_W + g * FGROUP_DIM:F_W + (g + 1) * FGROUP_DIM] = yy[:, FGROUP_DIM:]


def _inproj(x, mods, g_mix, w_in, wf, cos_t, sa_t, sb_t):
    b, s, d = x.shape
    tm = TM_PROJ
    tok = lambda bi, i: (bi, i, 0)
    return pl.pallas_call(
        _inproj_kernel,
        grid=(b, s // tm),
        in_specs=[pl.BlockSpec((1, tm, d), tok),
                  pl.BlockSpec((1, N_MOD, d), lambda bi, i: (bi, 0, 0)),
                  pl.BlockSpec((1, d), lambda bi, i: (0, 0)),
                  pl.BlockSpec(w_in.shape, lambda bi, i: (0, 0)),
                  pl.BlockSpec(wf.shape, lambda bi, i: (0, 0, 0)),
                  pl.BlockSpec((tm, HEAD_W), lambda bi, i: (i, 0)),
                  pl.BlockSpec((tm, HEAD_W), lambda bi, i: (i, 0)),
                  pl.BlockSpec((tm, HEAD_W), lambda bi, i: (i, 0))],
        out_specs=[pl.BlockSpec((1, tm, QK_W), tok),
                   pl.BlockSpec((1, tm, QK_W), tok),
                   pl.BlockSpec((1, tm, QK_W), tok),
                   pl.BlockSpec((1, tm, 2 * F_W), tok)],
        out_shape=[jax.ShapeDtypeStruct((b, s, QK_W), BF16),
                   jax.ShapeDtypeStruct((b, s, QK_W), BF16),
                   jax.ShapeDtypeStruct((b, s, QK_W), BF16),
                   jax.ShapeDtypeStruct((b, s, 2 * F_W), F32)],
        compiler_params=_cparams("parallel", "arbitrary"),
        name="inproj",
    )(x, mods, g_mix, w_in, wf, cos_t, sa_t, sb_t)


def _ctxproj_kernel(x_ref, mod_ref, g_ref, w_ref, k_ref, v_ref):
    m = mod_ref[0]
    h = _norm_mod(x_ref[0], g_ref[...], m[0:1], m[1:2]).astype(BF16)
    k_ref[0] = jnp.dot(h, w_ref[:, QK_W:2 * QK_W], preferred_element_type=F32).astype(BF16)
    v_ref[0] = jnp.dot(h, w_ref[:, 2 * QK_W:3 * QK_W], preferred_element_type=F32).astype(BF16)


def _ctxproj(ctx, mods, g_mix, w_in):
    b, n, d = ctx.shape
    return pl.pallas_call(
        _ctxproj_kernel,
        grid=(b,),
        in_specs=[pl.BlockSpec((1, n, d), lambda bi: (bi, 0, 0)),
                  pl.BlockSpec((1, N_MOD, d), lambda bi: (2, 0, 0)),
                  pl.BlockSpec((1, d), lambda bi: (0, 0)),
                  pl.BlockSpec(w_in.shape, lambda bi: (0, 0))],
        out_specs=[pl.BlockSpec((1, n, QK_W), lambda bi: (bi, 0, 0)),
                   pl.BlockSpec((1, n, QK_W), lambda bi: (bi, 0, 0))],
        out_shape=[jax.ShapeDtypeStruct((b, n, QK_W), BF16),
                   jax.ShapeDtypeStruct((b, n, QK_W), BF16)],
        compiler_params=_cparams("arbitrary"),
        name="ctxproj",
    )(ctx, mods, g_mix, w_in)


def _attn_kernel(q_ref, kc_ref, vc_ref, kl_ref, vl_ref, lam_ref, g_ref, o_ref,
                 m_ref, l_ref, acc_ref):
    q = q_ref[0]
    lane = lax.broadcasted_iota(jnp.int32, q.shape, 1)
    zero = jnp.zeros_like(q)
    q0 = jnp.where(lane < HEAD_DIM, q, zero)
    q1 = jnp.where(lane >= HEAD_DIM, q, zero)
    qs = (q0, q1)
    contract_last = (((1,), (1,)), ((), ()))
    m_ref[...] = jnp.full(m_ref.shape, -1e30, F32)
    l_ref[...] = jnp.zeros(l_ref.shape, F32)
    acc_ref[...] = jnp.zeros(acc_ref.shape, F32)

    def step(kb, vb):
        nk = kb.shape[0] // HEAD_W
        for mi in range(2):
            s = lax.dot_general(qs[mi], kb, contract_last, preferred_element_type=F32)
            m_old = m_ref[mi]
            m_new = jnp.maximum(m_old, jnp.max(s, axis=-1, keepdims=True))
            alpha = jnp.exp2(m_old - m_new)
            p = jnp.exp2(s - jnp.concatenate([m_new] * nk, axis=1))
            psum = p[:, 0:HEAD_W]
            for cblk in range(1, nk):
                psum = psum + p[:, cblk * HEAD_W:(cblk + 1) * HEAD_W]
            l_ref[mi] = alpha * l_ref[mi] + psum
            acc_ref[mi] = alpha * acc_ref[mi] + jnp.dot(p.astype(BF16), vb,
                                                       preferred_element_type=F32)
            m_ref[mi] = m_new

    step(kc_ref[0], vc_ref[0])

    def body(i, c):
        off = pl.multiple_of(i * TK, TK)
        step(kl_ref[0, pl.ds(off, TK), :], vl_ref[0, pl.ds(off, TK), :])
        return c

    lax.fori_loop(0, kl_ref.shape[1] // TK, body, 0)

    lp = lam_ref[...]
    t1 = jnp.sum(lp[0:1] * lp[1:2], axis=-1, keepdims=True)
    t2 = jnp.sum(lp[2:3] * lp[3:4], axis=-1, keepdims=True)
    lam = jnp.exp(t1) - jnp.exp(t2) + LAMBDA_INIT
    l0 = jnp.sum(l_ref[0], axis=-1, keepdims=True)
    l1 = jnp.sum(l_ref[1], axis=-1, keepdims=True)
    o = acc_ref[0] / l0 - lam * (acc_ref[1] / l1)
    ms = jnp.mean(o * o, axis=-1, keepdims=True)
    o = o * lax.rsqrt(ms + EPS) * g_ref[...] * (1.0 - LAMBDA_INIT)
    o_ref[0] = o.astype(BF16)


def _attention(q, kc, vc, kl, vl, lam_p, g_subln):
    b, s, _ = q.shape
    n_ctx = kc.shape[1]
    return pl.pallas_call(
        _attn_kernel,
        grid=(b, N_HEADS, s // TQ),
        in_specs=[pl.BlockSpec((1, TQ, HEAD_W), lambda bi, h, i: (bi, i, h)),
                  pl.BlockSpec((1, n_ctx, HEAD_W), lambda bi, h, i: (bi, 0, h)),
                  pl.BlockSpec((1, n_ctx, HEAD_W), lambda bi, h, i: (bi, 0, h)),
                  pl.BlockSpec((1, s, HEAD_W), lambda bi, h, i: (bi, 0, h)),
                  pl.BlockSpec((1, s, HEAD_W), lambda bi, h, i: (bi, 0, h)),
                  pl.BlockSpec((4, HEAD_DIM), lambda bi, h, i: (0, 0)),
                  pl.BlockSpec((1, HEAD_W), lambda bi, h, i: (0, 0))],
        out_specs=pl.BlockSpec((1, TQ, HEAD_W), lambda bi, h, i: (bi, i, h)),
        out_shape=jax.ShapeDtypeStruct((b, s, QK_W), BF16),
        scratch_shapes=[pltpu.VMEM((2, TQ, HEAD_W), F32)] * 3,
        compiler_params=_cparams("parallel", "parallel", "arbitrary"),
        name="diffattn",
    )(q, kc, vc, kl, vl, lam_p, g_subln)


def _dft1_kernel(y_ref, g_ref, br_ref, bi_ref):
    rows = DFT_N1 * F1_NB
    yb = y_ref[0].reshape(rows, 2 * F_W).astype(BF16)
    p = jnp.dot(g_ref[0], yb, preferred_element_type=F32)
    top, bot = p[:rows], p[rows:]
    br_ref[0] = (top[:, :F_W] - bot[:, F_W:]).reshape(DFT_N1, F1_NB, F_W)
    bi_ref[0] = (-top[:, F_W:] - bot[:, :F_W]).reshape(DFT_N1, F1_NB, F_W)


def _dft1(y4, gmat):
    b = y4.shape[0]
    rows = DFT_N1 * F1_NB
    return pl.pallas_call(
        _dft1_kernel,
        grid=(DFT_N2 // F1_NB, b),
        in_specs=[pl.BlockSpec((1, DFT_N1, F1_NB, 2 * F_W), lambda j, bi: (bi, 0, j, 0)),
                  pl.BlockSpec((1, 2 * rows, rows), lambda j, bi: (j, 0, 0))],
        out_specs=[pl.BlockSpec((1, DFT_N1, F1_NB, F_W), lambda j, bi: (bi, 0, j, 0)),
                   pl.BlockSpec((1, DFT_N1, F1_NB, F_W), lambda j, bi: (bi, 0, j, 0))],
        out_shape=[jax.ShapeDtypeStruct((b, DFT_N1, DFT_N2, F_W), F32),
                   jax.ShapeDtypeStruct((b, DFT_N1, DFT_N2, F_W), F32)],
        compiler_params=_cparams("arbitrary", "arbitrary"),
        name="dft1",
    )(y4, gmat)


def _dft2_kernel(br_ref, bi_ref, c_ref, s_ref, o_ref):
    for j in range(F2_KB):
        r = (jnp.dot(c_ref[...], br_ref[0, j].astype(BF16), preferred_element_type=F32)
             + jnp.dot(s_ref[...], bi_ref[0, j].astype(BF16), preferred_element_type=F32))
        o_ref[0, :, j, :] = r


def _dft2(br4, bi4, c2, s2):
    b = br4.shape[0]
    blk = (1, F2_KB, DFT_N2, F_W)
    return pl.pallas_call(
        _dft2_kernel,
        grid=(b, DFT_N1 // F2_KB),
        in_specs=[pl.BlockSpec(blk, lambda bi, j: (bi, j, 0, 0)),
                  pl.BlockSpec(blk, lambda bi, j: (bi, j, 0, 0)),
                  pl.BlockSpec((DFT_N2, DFT_N2), lambda bi, j: (0, 0)),
                  pl.BlockSpec((DFT_N2, DFT_N2), lambda bi, j: (0, 0))],
        out_specs=pl.BlockSpec((1, DFT_N2, F2_KB, F_W), lambda bi, j: (bi, 0, j, 0)),
        out_shape=jax.ShapeDtypeStruct((b, DFT_N2, DFT_N1, F_W), F32),
        compiler_params=_cparams("parallel", "arbitrary"),
        name="dft2",
    )(br4, bi4, c2, s2)


def _outproj_kernel(o_ref, f_ref, x_ref, mod_ref, g_ref, wo_ref, wr_ref, br_ref, tri_ref,
                    x1_ref, h3_ref, meta_ref, gate_ref, cnt_out_ref, cnt_ref):
    first = (pl.program_id(0) == 0) & (pl.program_id(1) == 0)

    @pl.when(first)
    def _():
        cnt_ref[...] = jnp.zeros(cnt_ref.shape, F32)

    m = mod_ref[0]
    mix = (jnp.dot(o_ref[0], wo_ref[0:QK_W, :], preferred_element_type=F32)
           + jnp.dot(f_ref[0].astype(BF16), wo_ref[QK_W:, :], preferred_element_type=F32))
    x1 = x_ref[0] + m[2:3] * mix
    x1_ref[0] = x1
    h2 = _norm_mod(x1, g_ref[...], m[3:4], m[4:5])
    _store_rows(h3_ref, h2)
    h_hi = h2.astype(BF16)
    h_lo = (h2 - h_hi.astype(F32)).astype(BF16)
    hw = jnp.dot(h_hi, wr_ref[...], preferred_element_type=F32)
    lg = (hw[:, :128] + hw[:, 128:]
          + jnp.dot(h_lo, wr_ref[:, 0:128], preferred_element_type=F32)
          + br_ref[...])
    lane = lax.broadcasted_iota(jnp.int32, lg.shape, 1)
    ninf = jnp.float32(-jnp.inf)
    big = jnp.int32(lg.shape[1])
    gl = jnp.where(lane < N_GROUPS, lg, ninf)
    gmax = jnp.max(gl, axis=-1, keepdims=True)
    grp = jnp.min(jnp.where(gl == gmax, lane, big), axis=-1, keepdims=True)
    pg = 1.0 / jnp.sum(jnp.exp(gl - gmax), axis=-1, keepdims=True)
    e_lane = lane - N_GROUPS
    emask = (e_lane >= 0) & (e_lane < N_EXPERTS) & ((e_lane >> 3) == grp)
    el = jnp.where(emask, lg, ninf)
    t1 = jnp.max(el, axis=-1, keepdims=True)
    i1 = jnp.min(jnp.where(el == t1, lane, big), axis=-1, keepdims=True)
    el2 = jnp.where(lane == i1, ninf, el)
    t2 = jnp.max(el2, axis=-1, keepdims=True)
    i2 = jnp.min(jnp.where(el2 == t2, lane, big), axis=-1, keepdims=True)
    dd = jnp.exp(t2 - t1)
    w1 = pg / (1.0 + dd)
    w2 = pg * dd / (1.0 + dd)
    gate_ref[...] = jnp.where(lane == 0, w1, jnp.where(lane == 1, w2, 0.0))
    hit1 = lane == i1
    hit2 = lane == i2
    oh = jnp.where(hit1 | hit2, 1.0, 0.0)
    before = jnp.dot(tri_ref[...], oh.astype(BF16), preferred_element_type=F32) + cnt_ref[0:1, :]
    r1 = jnp.sum(jnp.where(hit1, before, 0.0), axis=-1, keepdims=True).astype(jnp.int32)
    r2 = jnp.sum(jnp.where(hit2, before, 0.0), axis=-1, keepdims=True).astype(jnp.int32)
    cnt_ref[0:1, :] = cnt_ref[0:1, :] + jnp.sum(oh, axis=0, keepdims=True)
    cnt_out_ref[...] = cnt_ref[...]
    meta = jnp.where(lane == 0, i1 - N_GROUPS,
                     jnp.where(lane == 1, i2 - N_GROUPS,
                               jnp.where(lane == 2, r1, jnp.where(lane == 3, r2, 0))))
    meta_ref[...] = jnp.transpose(meta)[0:8, :]


def _outproj(attn_o, four, x, mods, g_ffn, w_out, w_r, b_r):
    b, s, d = x.shape
    tm = TM_PROJ
    nt = s // tm
    tok = lambda bi, i: (bi, i, 0)
    flat = lambda bi, i: (bi * nt + i, 0)
    tri = jnp.asarray(np.tril(np.ones((tm, tm), np.float32), -1)).astype(BF16)
    return pl.pallas_call(
        _outproj_kernel,
        grid=(b, nt),
        in_specs=[pl.BlockSpec((1, tm, QK_W), tok),
                  pl.BlockSpec((1, tm, F_W), tok),
                  pl.BlockSpec((1, tm, d), tok),
                  pl.BlockSpec((1, N_MOD, d), lambda bi, i: (bi, 0, 0)),
                  pl.BlockSpec((1, d), lambda bi, i: (0, 0)),
                  pl.BlockSpec(w_out.shape, lambda bi, i: (0, 0)),
                  pl.BlockSpec(w_r.shape, lambda bi, i: (0, 0)),
                  pl.BlockSpec(b_r.shape, lambda bi, i: (0, 0)),
                  pl.BlockSpec((tm, tm), lambda bi, i: (0, 0))],
        out_specs=[pl.BlockSpec((1, tm, d), tok),
                   pl.BlockSpec((tm * ROW_TILE, 128), flat),
                   pl.BlockSpec((8, tm), lambda bi, i: (0, bi * nt + i)),
                   pl.BlockSpec((tm, 128), flat),
                   pl.BlockSpec((8, 128), lambda bi, i: (0, 0))],
        out_shape=[jax.ShapeDtypeStruct((b, s, d), F32),
                   jax.ShapeDtypeStruct((b * s * ROW_TILE, 128), U32),
                   jax.ShapeDtypeStruct((8, b * s), jnp.int32),
                   jax.ShapeDtypeStruct((b * s, 128), F32),
                   jax.ShapeDtypeStruct((8, 128), F32)],
        scratch_shapes=[pltpu.VMEM((8, 128), F32)],
        compiler_params=_cparams("arbitrary", "arbitrary"),
        name="outproj",
    )(attn_o, four, x, mods, g_ffn, w_out, w_r, b_r, tri)


def _row_slice(row):
    return pl.ds(pl.multiple_of(row * ROW_TILE, ROW_TILE), ROW_TILE)


def _row_copy(src_ref, src_row, dst_ref, dst_row, sem):
    return pltpu.make_async_copy(src_ref.at[_row_slice(src_row)], dst_ref.at[_row_slice(dst_row)], sem)


def _bf16_bits(x):
    return lax.bitcast_convert_type(x.astype(BF16).astype(F32), U32)


def _rows_2d(ref, n_rows):
    hi, lo = [], []
    for cblk in range(ROW_TILE):
        w = ref[pl.ds(cblk, n_rows, stride=ROW_TILE), :]
        hi.append(lax.bitcast_convert_type(w & jnp.uint32(0xFFFF0000), F32).astype(BF16))
        lo.append(lax.bitcast_convert_type(w << 16, F32).astype(BF16))
    return jnp.concatenate(hi + lo, axis=1)


def _store_rows(ref, val):
    for cblk in range(ROW_TILE):
        hi = _bf16_bits(val[:, cblk * 128:(cblk + 1) * 128])
        lo = _bf16_bits(val[:, (cblk + ROW_TILE) * 128:(cblk + ROW_TILE + 1) * 128])
        ref[pl.ds(cblk, val.shape[0], stride=ROW_TILE), :] = hi | (lo >> 16)


def _dispatch_kernel(dest_ref, h_ref, zeros_ref, xs_ref, sem):
    del zeros_ref
    n = 2 * TD_DISPATCH

    def issue(r, c):
        _row_copy(h_ref, r, xs_ref, dest_ref[0, 0, 2 * r], sem).start(priority=0)
        _row_copy(h_ref, r, xs_ref, dest_ref[0, 0, 2 * r + 1], sem).start(priority=1)
        return c

    lax.fori_loop(0, TD_DISPATCH, issue, 0, unroll=DMA_UNROLL // 2)

    def drain(a, c):
        _row_copy(h_ref, 0, xs_ref, 0, sem).wait()
        return c

    lax.fori_loop(0, n, drain, 0, unroll=DMA_UNROLL)


def _dispatch(dest3, h3, xs_zeros):
    t = h3.shape[0] // ROW_TILE
    return pl.pallas_call(
        _dispatch_kernel,
        grid=(t // TD_DISPATCH,),
        in_specs=[pl.BlockSpec((1, 1, 2 * TD_DISPATCH), lambda i: (i, 0, 0),
                               memory_space=pltpu.SMEM),
                  pl.BlockSpec((TD_DISPATCH * ROW_TILE, 128), lambda i: (i, 0)),
                  pl.BlockSpec(memory_space=pl.ANY)],
        out_specs=pl.BlockSpec(memory_space=pl.ANY),
        out_shape=jax.ShapeDtypeStruct(xs_zeros.shape, xs_zeros.dtype),
        scratch_shapes=[pltpu.SemaphoreType.DMA(())],
        input_output_aliases={2: 0},
        compiler_params=_cparams("arbitrary"),
        name="dispatch",
    )(dest3, h3, xs_zeros)


def _experts_kernel(be_ref, nu_ref, nxt_ref, xs_ref, wg_hbm, wu_hbm, wd_hbm, ys_ref,
                    wgb, wub, wdb, sg, su, sd, sem):
    j = pl.program_id(0)
    used = j < nu_ref[0]
    e = be_ref[j]
    new_expert = (j == 0) | (e != be_ref[jnp.maximum(j - 1, 0)])
    e_next = nxt_ref[e]

    def weight_copies(idx):
        return (pltpu.make_async_copy(wg_hbm.at[idx], sg, sem.at[0]),
                pltpu.make_async_copy(wu_hbm.at[idx], su, sem.at[1]),
                pltpu.make_async_copy(wd_hbm.at[idx], sd, sem.at[2]))

    @pl.when(j == 0)
    def _():
        for cp in weight_copies(e):
            cp.start()

    @pl.when(used & new_expert)
    def _():
        for cp in weight_copies(e):
            cp.wait()
        wgb[...] = sg[...].astype(BF16)
        wub[...] = su[...].astype(BF16)
        wdb[...] = sd[...].astype(BF16)

    @pl.when(used & new_expert & (e_next != e))
    def _():
        for cp in weight_copies(e_next):
            cp.start()

    @pl.when(used)
    def _():
        xb = _rows_2d(xs_ref, MOE_BLK)
        gate = jnp.dot(xb, wgb[...], preferred_element_type=F32)
        up = jnp.dot(xb, wub[...], preferred_element_type=F32)
        hid = (gate * jax.nn.sigmoid(gate) * up).astype(BF16)
        _store_rows(ys_ref, jnp.dot(hid, wdb[...], preferred_element_type=F32))

    @pl.when(pl.program_id(0) >= nu_ref[0])
    def _():
        ys_ref[...] = jnp.zeros_like(ys_ref)


def _experts(blk_expert, n_used, next_expert, xs, wg, wu, wd):
    rows = xs.shape[0] // ROW_TILE
    d = wg.shape[1]
    nb = rows // MOE_BLK
    row_blk = lambda j, be, nu, nx: (jnp.minimum(j, nu[0] - 1), 0)
    out_blk = lambda j, be, nu, nx: (j, 0)
    return pl.pallas_call(
        _experts_kernel,
        grid_spec=pltpu.PrefetchScalarGridSpec(
            num_scalar_prefetch=3,
            grid=(nb,),
            in_specs=[pl.BlockSpec((MOE_BLK * ROW_TILE, 128), row_blk),
                      pl.BlockSpec(memory_space=pl.ANY),
                      pl.BlockSpec(memory_space=pl.ANY),
                      pl.BlockSpec(memory_space=pl.ANY)],
            out_specs=pl.BlockSpec((MOE_BLK * ROW_TILE, 128), out_blk),
            scratch_shapes=[pltpu.VMEM((d, D_EXPERT), BF16), pltpu.VMEM((d, D_EXPERT), BF16),
                            pltpu.VMEM((D_EXPERT, d), BF16),
                            pltpu.VMEM((d, D_EXPERT), F32), pltpu.VMEM((d, D_EXPERT), F32),
                            pltpu.VMEM((D_EXPERT, d), F32),
                            pltpu.SemaphoreType.DMA((3,))]),
        out_shape=jax.ShapeDtypeStruct(xs.shape, xs.dtype),
        compiler_params=_cparams("arbitrary"),
        name="experts",
    )(blk_expert, n_used, next_expert, xs, wg, wu, wd)


def _combine_kernel(dest_ref, dest_next_ref, ys_ref, x1_ref, gate_ref, mod_ref, g_ref, o_ref,
                    ya, yb, sem):
    step = pl.program_id(0) * pl.num_programs(1) + pl.program_id(1)
    n_steps = pl.num_programs(0) * pl.num_programs(1)
    slot = step % 2

    def start_gathers(idx_ref, to_slot):
        def issue(r, c):
            _row_copy(ys_ref, idx_ref[0, 0, 2 * r], ya.at[to_slot], r,
                      sem.at[to_slot]).start(priority=0)
            _row_copy(ys_ref, idx_ref[0, 0, 2 * r + 1], yb.at[to_slot], r,
                      sem.at[to_slot]).start(priority=1)
            return c

        lax.fori_loop(0, TD, issue, 0, unroll=DMA_UNROLL)

    @pl.when(step == 0)
    def _():
        start_gathers(dest_ref, 0)

    @pl.when(step + 1 < n_steps)
    def _():
        start_gathers(dest_next_ref, 1 - slot)

    def drain(r, c):
        _row_copy(ys_ref, 0, ya.at[slot], 0, sem.at[slot]).wait()
        _row_copy(ys_ref, 0, yb.at[slot], 0, sem.at[slot]).wait()
        return c

    lax.fori_loop(0, TD, drain, 0, unroll=DMA_UNROLL)
    gt = gate_ref[...]
    moe = (gt[:, 0:1] * _rows_2d(ya.at[slot], TD).astype(F32)
           + gt[:, 1:2] * _rows_2d(yb.at[slot], TD).astype(F32))
    x2 = x1_ref[0] + mod_ref[0][5:6] * moe
    ms = jnp.mean(x2 * x2, axis=-1, keepdims=True)
    o_ref[0] = x2 * lax.rsqrt(ms + EPS) * g_ref[...]


def _combine(dest3, ys, x1, gates, mods, g_final):
    b, s, d = x1.shape
    nt = s // TD
    return pl.pallas_call(
        _combine_kernel,
        grid=(b, nt),
        in_specs=[pl.BlockSpec((1, 1, 2 * TD), lambda bi, i: (bi * nt + i, 0, 0),
                               memory_space=pltpu.SMEM),
                  pl.BlockSpec((1, 1, 2 * TD),
                               lambda bi, i: (jnp.minimum(bi * nt + i + 1, b * nt - 1), 0, 0),
                               memory_space=pltpu.SMEM),
                  pl.BlockSpec(memory_space=pl.ANY),
                  pl.BlockSpec((1, TD, d), lambda bi, i: (bi, i, 0)),
                  pl.BlockSpec((TD, 128), lambda bi, i: (bi * nt + i, 0)),
                  pl.BlockSpec((1, N_MOD, d), lambda bi, i: (bi, 0, 0)),
                  pl.BlockSpec((1, d), lambda bi, i: (0, 0))],
        out_specs=pl.BlockSpec((1, TD, d), lambda bi, i: (bi, i, 0)),
        out_shape=jax.ShapeDtypeStruct((b, s, d), F32),
        scratch_shapes=[pltpu.VMEM((2, TD * ROW_TILE, 128), U32),
                        pltpu.VMEM((2, TD * ROW_TILE, 128), U32),
                        pltpu.SemaphoreType.DMA((2,))],
        compiler_params=_cparams("arbitrary", "arbitrary"),
        name="combine",
    )(dest3, dest3, ys, x1, gates, mods, g_final)


@functools.lru_cache(maxsize=None)
def _rope_tables(rows):
    r, col = np.meshgrid(np.arange(rows), np.arange(GRID_W), indexing='ij')
    pos = np.stack([r.reshape(-1), col.reshape(-1)], axis=-1).astype(np.float32)
    inv_freq = (np.float32(ROPE_THETA)
                ** (-np.arange(0, ROPE_AXIS, 2, dtype=np.float32) / np.float32(ROPE_AXIS))).astype(np.float32)
    ang = (pos[:, :, None] * inv_freq).astype(np.float32)
    ang = np.concatenate([ang, ang], axis=-1).astype(np.float64)
    n = ang.shape[0]
    cos = np.tile(np.cos(ang).reshape(n, HEAD_DIM), (1, 2)).astype(np.float32)
    sin = np.tile(np.sin(ang).reshape(n, HEAD_DIM), (1, 2)).astype(np.float32)
    upper = (np.arange(HEAD_W) % ROPE_AXIS) >= ROPE_HALF
    sa = np.where(upper, sin, np.float32(0.0))
    sb = np.where(upper, np.float32(0.0), -sin)
    return cos, sa, sb


@functools.lru_cache(maxsize=None)
def _dft_constants(n_pos):
    c = np.arange(FGROUP_DIM)
    ang_c = 2.0 * np.pi * ((c[:, None] * c[None, :]) % FGROUP_DIM) / FGROUP_DIM
    norm = 1.0 / math.sqrt(n_pos * FGROUP_DIM)
    cmat = (np.cos(ang_c) * norm).astype(np.float32)
    smat = (np.sin(ang_c) * norm).astype(np.float32)
    k1 = np.arange(DFT_N1)[None, :, None]
    n1 = np.arange(DFT_N1)[None, None, :]
    n2 = np.arange(DFT_N2)[:, None, None]
    ang_g = 2.0 * np.pi * ((k1 * (DFT_N2 * n1 + n2)) % n_pos) / n_pos
    gsmall = np.stack([np.cos(ang_g), np.sin(ang_g)], axis=1)
    gsmall = gsmall.reshape(DFT_N2 // F1_NB, F1_NB, 2, DFT_N1, DFT_N1)
    gmat = np.einsum('japkn,ab->jpkanb', gsmall, np.eye(F1_NB)).reshape(
        DFT_N2 // F1_NB, 2 * DFT_N1 * F1_NB, DFT_N1 * F1_NB).astype(BF16)
    k2 = np.arange(DFT_N2)
    ang_2 = 2.0 * np.pi * ((k2[:, None] * k2[None, :]) % DFT_N2) / DFT_N2
    c2 = np.cos(ang_2).astype(np.float32)
    s2 = np.sin(ang_2).astype(np.float32)
    return cmat, smat, gmat, c2, s2


def kernel(x, c, ctx, c_ctx, w_ada, b_ada, g_mix_norm, g_ffn_norm, w_in, lambda_q1, lambda_k1, lambda_q2, lambda_k2, g_subln, w_fourier, w_out, w_router_group, b_router_group, w_router_expert, b_router_expert, w_gate, w_up, w_down, g_final):
    b, s, d = x.shape
    t = b * s
    assert d == D_MODEL and s == DFT_N1 * DFT_N2 and s % GRID_W == 0 and b == 2

    cc = jnp.concatenate([c, c_ctx[None, :], jnp.zeros((8 - b - 1, d), F32)], axis=0)
    mods = _adaln(cc, w_ada[0], b_ada[0]).reshape(8, N_MOD, d)

    cmat, smat, gmat, c2, s2 = _dft_constants(s)
    wf = _wfold(jnp.asarray(cmat), jnp.asarray(smat), w_fourier[0])
    cos_t, sa_t, sb_t = _rope_tables(s // GRID_W)

    w_in_b = w_in[0].astype(BF16)
    g_mix = g_mix_norm[0].reshape(1, d)
    q, kl, vl, y = _inproj(x, mods, g_mix, w_in_b, wf, cos_t, sa_t, sb_t)
    kc, vc = _ctxproj(ctx, mods, g_mix, w_in_b)

    lam_p = jnp.stack([lambda_q1[0], lambda_k1[0], lambda_q2[0], lambda_k2[0]], axis=0)
    attn_o = _attention(q, kc, vc, kl, vl, lam_p, g_subln[0].reshape(1, HEAD_W))

    br, bi = _dft1(y.reshape(b, DFT_N1, DFT_N2, 2 * F_W), jnp.asarray(gmat))
    four = _dft2(br, bi, jnp.asarray(c2).astype(BF16),
                 jnp.asarray(s2).astype(BF16)).reshape(b, s, F_W)

    n_r = N_GROUPS + N_EXPERTS
    w_r = jnp.concatenate([w_router_group[0], w_router_expert[0],
                           jnp.zeros((d, 128 - n_r), F32)], axis=1)
    b_r = jnp.concatenate([b_router_group[0], b_router_expert[0],
                           jnp.zeros((128 - n_r,), F32)]).reshape(1, 128)
    w_r_hi = w_r.astype(BF16)
    w_r_lo = (w_r - w_r_hi.astype(F32)).astype(BF16)
    x1, h3, meta, gates, cnt = _outproj(attn_o, four, x, mods, g_ffn_norm[0].reshape(1, d),
                                        w_out[0].astype(BF16),
                                        jnp.concatenate([w_r_hi, w_r_lo], axis=1), b_r)

    e_tk = jnp.stack([meta[0], meta[1]], axis=1)
    rank_tk = jnp.stack([meta[2], meta[3]], axis=1)
    counts = cnt[0, N_GROUPS:N_GROUPS + N_EXPERTS].astype(jnp.int32)
    nblk = (counts + MOE_BLK - 1) // MOE_BLK
    blk_end = jnp.cumsum(nblk)
    blk_start = blk_end - nblk
    onehot = (e_tk[:, :, None] == jnp.arange(N_EXPERTS, dtype=jnp.int32)).astype(F32)
    first_blk = jnp.einsum('tke,e->tk', onehot, blk_start.astype(F32),
                           precision=lax.Precision.HIGHEST).astype(jnp.int32)
    dest = first_blk * MOE_BLK + rank_tk
    n_blocks = t * TOP_K // MOE_BLK + N_EXPERTS
    blk_ids = jnp.arange(n_blocks, dtype=jnp.int32)
    blk_expert = jnp.minimum(
        jnp.sum((blk_end[None, :] <= blk_ids[:, None]).astype(jnp.int32), axis=1),
        N_EXPERTS - 1).astype(jnp.int32)
    n_used = blk_end[-1:].astype(jnp.int32)
    dest3 = dest.reshape(t // TD, 1, 2 * TD)

    xs = _dispatch(dest.reshape(t // TD_DISPATCH, 1, 2 * TD_DISPATCH), h3,
                   jnp.zeros((n_blocks * MOE_BLK * ROW_TILE, 128), U32))
    e_ids = jnp.arange(N_EXPERTS, dtype=jnp.int32)
    later = (e_ids[None, :] > e_ids[:, None]) & (nblk[None, :] > 0)
    next_expert = jnp.min(jnp.where(later, e_ids[None, :], N_EXPERTS), axis=1)
    next_expert = jnp.where(next_expert == N_EXPERTS, e_ids, next_expert).astype(jnp.int32)
    ys = _experts(blk_expert, n_used, next_expert, xs, w_gate[0], w_up[0], w_down[0])
    return _combine(dest3, ys, x1, gates, mods, g_final.reshape(1, d))
```

```python
import functools
import math

import numpy as np
import jax
import jax.numpy as jnp
from jax import lax
from jax.experimental import pallas as pl
from jax.experimental.pallas import tpu as pltpu

F32 = jnp.float32
BF16 = jnp.bfloat16

D_MODEL = 1024
GRID_W = 64
N_HEADS = 4
HEAD_DIM = 64
HEAD_W = 2 * HEAD_DIM
QK_W = N_HEADS * HEAD_W
N_FGROUPS = 4
FGROUP_DIM = 128
F_W = N_FGROUPS * FGROUP_DIM
ROPE_THETA = 10000.0
ROPE_AXIS = HEAD_DIM // 2
ROPE_HALF = ROPE_AXIS // 2
N_GROUPS = 4
EXPERTS_PER_GROUP = 8
N_EXPERTS = N_GROUPS * EXPERTS_PER_GROUP
TOP_K = 2
D_EXPERT = 512
N_MOD = 6
EPS = 1e-6
LAMBDA_INIT = 0.8 - 0.6 * math.exp(-0.3 * 0)
LOG2_E = 1.4426950408889634

DFT_N1 = 64
DFT_N2 = 128

TM_PROJ = 512
TQ = 1024
TK = 2048
F1_NB = 8
F2_KB = 8
MOE_BLK = 256
ROW_TILE = D_MODEL // 256
U32 = jnp.uint32
TD = 512
TD_DISPATCH = 1024
DMA_UNROLL = 16
VMEM_LIMIT = 48 * 1024 * 1024


def _cparams(*sem):
    return pltpu.CompilerParams(dimension_semantics=sem, vmem_limit_bytes=VMEM_LIMIT)


def _adaln_kernel(c_ref, w_ref, b_ref, o_ref):
    cc = c_ref[...]
    s = cc * jax.nn.sigmoid(cc)
    o_ref[...] = jnp.dot(s, w_ref[...], preferred_element_type=F32,
                         precision=lax.Precision.HIGHEST) + b_ref[...]


def _adaln(cc, w_ada, b_ada):
    n = w_ada.shape[1]
    tn = 1536
    return pl.pallas_call(
        _adaln_kernel,
        grid=(n // tn,),
        in_specs=[pl.BlockSpec((8, D_MODEL), lambda j: (0, 0)),
                  pl.BlockSpec((D_MODEL, tn), lambda j: (0, j)),
                  pl.BlockSpec((1, tn), lambda j: (0, j))],
        out_specs=pl.BlockSpec((8, tn), lambda j: (0, j)),
        out_shape=jax.ShapeDtypeStruct((8, n), F32),
        compiler_params=_cparams("arbitrary"),
        name="adaln",
    )(cc, w_ada, b_ada.reshape(1, n))


def _wfold_kernel(c_ref, s_ref, w_ref, o_ref):
    w = w_ref[0]
    o_ref[0, :, :FGROUP_DIM] = jnp.dot(c_ref[...], w, preferred_element_type=F32,
                                       precision=lax.Precision.HIGHEST).astype(BF16)
    o_ref[0, :, FGROUP_DIM:] = jnp.dot(s_ref[...], w, preferred_element_type=F32,
                                       precision=lax.Precision.HIGHEST).astype(BF16)


def _wfold(cmat, smat, w_fourier):
    return pl.pallas_call(
        _wfold_kernel,
        grid=(N_FGROUPS,),
        in_specs=[pl.BlockSpec((FGROUP_DIM, FGROUP_DIM), lambda g: (0, 0)),
                  pl.BlockSpec((FGROUP_DIM, FGROUP_DIM), lambda g: (0, 0)),
                  pl.BlockSpec((1, FGROUP_DIM, FGROUP_DIM), lambda g: (g, 0, 0))],
        out_specs=pl.BlockSpec((1, FGROUP_DIM, 2 * FGROUP_DIM), lambda g: (g, 0, 0)),
        out_shape=jax.ShapeDtypeStruct((N_FGROUPS, FGROUP_DIM, 2 * FGROUP_DIM), BF16),
        compiler_params=_cparams("arbitrary"),
        name="wfold",
    )(cmat, smat, w_fourier)


def _norm_mod(x, g, shift, scale):
    ms = jnp.mean(x * x, axis=-1, keepdims=True)
    y = x * lax.rsqrt(ms + EPS) * g
    return y * (1.0 + scale) + shift


def _rope_slab(p, cos, sa, sb):
    return (p * cos + pltpu.roll(p, ROPE_HALF, 1) * sa
            + pltpu.roll(p, HEAD_W - ROPE_HALF, 1) * sb)


def _inproj_kernel(x_ref, mod_ref, g_ref, w_ref, wf_ref, cos_ref, sa_ref, sb_ref,
                   q_ref, k_ref, v_ref, y_ref):
    m = mod_ref[0]
    h = _norm_mod(x_ref[0], g_ref[...], m[0:1], m[1:2]).astype(BF16)
    cos, sa, sb = cos_ref[...], sa_ref[...], sb_ref[...]
    scale = HEAD_DIM ** -0.5 * LOG2_E
    pq = jnp.dot(h, w_ref[:, 0:QK_W], preferred_element_type=F32)
    for hh in range(N_HEADS):
        sl = slice(hh * HEAD_W, (hh + 1) * HEAD_W)
        q_ref[0, :, sl] = (_rope_slab(pq[:, sl], cos, sa, sb) * scale).astype(BF16)
    pk = jnp.dot(h, w_ref[:, QK_W:2 * QK_W], preferred_element_type=F32)
    for hh in range(N_HEADS):
        sl = slice(hh * HEAD_W, (hh + 1) * HEAD_W)
        k_ref[0, :, sl] = _rope_slab(pk[:, sl], cos, sa, sb).astype(BF16)
    v_ref[0] = jnp.dot(h, w_ref[:, 2 * QK_W:3 * QK_W], preferred_element_type=F32).astype(BF16)
    pf = jnp.dot(h, w_ref[:, 3 * QK_W:], preferred_element_type=F32).astype(BF16)
    for g in range(N_FGROUPS):
        yy = jnp.dot(pf[:, g * FGROUP_DIM:(g + 1) * FGROUP_DIM], wf_ref[g],
                     preferred_element_type=F32)
        y_ref[0, :, g * FGROUP_DIM:(g + 1) * FGROUP_DIM] = yy[:, :FGROUP_DIM]
        y_ref[0, :, F_W + g * FGROUP_DIM:F_W + (g + 1) * FGROUP_DIM] = yy[:, FGROUP_DIM:]


def _inproj(x, mods, g_mix, w_in, wf, cos_t, sa_t, sb_t):
    b, s, d = x.shape
    tm = TM_PROJ
    tok = lambda bi, i: (bi, i, 0)
    return pl.pallas_call(
        _inproj_kernel,
        grid=(b, s // tm),
        in_specs=[pl.BlockSpec((1, tm, d), tok),
                  pl.BlockSpec((1, N_MOD, d), lambda bi, i: (bi, 0, 0)),
                  pl.BlockSpec((1, d), lambda bi, i: (0, 0)),
                  pl.BlockSpec(w_in.shape, lambda bi, i: (0, 0)),
                  pl.BlockSpec(wf.shape, lambda bi, i: (0, 0, 0)),
                  pl.BlockSpec((tm, HEAD_W), lambda bi, i: (i, 0)),
                  pl.BlockSpec((tm, HEAD_W), lambda bi, i: (i, 0)),
                  pl.BlockSpec((tm, HEAD_W), lambda bi, i: (i, 0))],
        out_specs=[pl.BlockSpec((1, tm, QK_W), tok),
                   pl.BlockSpec((1, tm, QK_W), tok),
                   pl.BlockSpec((1, tm, QK_W), tok),
                   pl.BlockSpec((1, tm, 2 * F_W), tok)],
        out_shape=[jax.ShapeDtypeStruct((b, s, QK_W), BF16),
                   jax.ShapeDtypeStruct((b, s, QK_W), BF16),
                   jax.ShapeDtypeStruct((b, s, QK_W), BF16),
                   jax.ShapeDtypeStruct((b, s, 2 * F_W), F32)],
        compiler_params=_cparams("parallel", "arbitrary"),
        name="inproj",
    )(x, mods, g_mix, w_in, wf, cos_t, sa_t, sb_t)


def _ctxproj_kernel(x_ref, mod_ref, g_ref, w_ref, k_ref, v_ref):
    m = mod_ref[0]
    h = _norm_mod(x_ref[0], g_ref[...], m[0:1], m[1:2]).astype(BF16)
    k_ref[0] = jnp.dot(h, w_ref[:, QK_W:2 * QK_W], preferred_element_type=F32).astype(BF16)
    v_ref[0] = jnp.dot(h, w_ref[:, 2 * QK_W:3 * QK_W], preferred_element_type=F32).astype(BF16)


def _ctxproj(ctx, mods, g_mix, w_in):
    b, n, d = ctx.shape
    return pl.pallas_call(
        _ctxproj_kernel,
        grid=(b,),
        in_specs=[pl.BlockSpec((1, n, d), lambda bi: (bi, 0, 0)),
                  pl.BlockSpec((1, N_MOD, d), lambda bi: (2, 0, 0)),
                  pl.BlockSpec((1, d), lambda bi: (0, 0)),
                  pl.BlockSpec(w_in.shape, lambda bi: (0, 0))],
        out_specs=[pl.BlockSpec((1, n, QK_W), lambda bi: (bi, 0, 0)),
                   pl.BlockSpec((1, n, QK_W), lambda bi: (bi, 0, 0))],
        out_shape=[jax.ShapeDtypeStruct((b, n, QK_W), BF16),
                   jax.ShapeDtypeStruct((b, n, QK_W), BF16)],
        compiler_params=_cparams("arbitrary"),
        name="ctxproj",
    )(ctx, mods, g_mix, w_in)


def _attn_kernel(q_ref, kc_ref, vc_ref, kl_ref, vl_ref, lam_ref, g_ref, o_ref,
                 m_ref, l_ref, acc_ref):
    q = q_ref[0]
    lane = lax.broadcasted_iota(jnp.int32, q.shape, 1)
    zero = jnp.zeros_like(q)
    q0 = jnp.where(lane < HEAD_DIM, q, zero)
    q1 = jnp.where(lane >= HEAD_DIM, q, zero)
    qs = (q0, q1)
    contract_last = (((1,), (1,)), ((), ()))
    m_ref[...] = jnp.full(m_ref.shape, -1e30, F32)
    l_ref[...] = jnp.zeros(l_ref.shape, F32)
    acc_ref[...] = jnp.zeros(acc_ref.shape, F32)

    def step(kb, vb):
        nk = kb.shape[0] // HEAD_W
        for mi in range(2):
            s = lax.dot_general(qs[mi], kb, contract_last, preferred_element_type=F32)
            m_old = m_ref[mi]
            m_new = jnp.maximum(m_old, jnp.max(s, axis=-1, keepdims=True))
            alpha = jnp.exp2(m_old - m_new)
            p = jnp.exp2(s - jnp.concatenate([m_new] * nk, axis=1))
            psum = p[:, 0:HEAD_W]
            for cblk in range(1, nk):
                psum = psum + p[:, cblk * HEAD_W:(cblk + 1) * HEAD_W]
            l_ref[mi] = alpha * l_ref[mi] + psum
            acc_ref[mi] = alpha * acc_ref[mi] + jnp.dot(p.astype(BF16), vb,
                                                       preferred_element_type=F32)
            m_ref[mi] = m_new

    step(kc_ref[0], vc_ref[0])

    def body(i, c):
        off = pl.multiple_of(i * TK, TK)
        step(kl_ref[0, pl.ds(off, TK), :], vl_ref[0, pl.ds(off, TK), :])
        return c

    lax.fori_loop(0, kl_ref.shape[1] // TK, body, 0)

    lp = lam_ref[...]
    t1 = jnp.sum(lp[0:1] * lp[1:2], axis=-1, keepdims=True)
    t2 = jnp.sum(lp[2:3] * lp[3:4], axis=-1, keepdims=True)
    lam = jnp.exp(t1) - jnp.exp(t2) + LAMBDA_INIT
    l0 = jnp.sum(l_ref[0], axis=-1, keepdims=True)
    l1 = jnp.sum(l_ref[1], axis=-1, keepdims=True)
    o = acc_ref[0] / l0 - lam * (acc_ref[1] / l1)
    ms = jnp.mean(o * o, axis=-1, keepdims=True)
    o = o * lax.rsqrt(ms + EPS) * g_ref[...] * (1.0 - LAMBDA_INIT)
    o_ref[0] = o.astype(BF16)


def _attention(q, kc, vc, kl, vl, lam_p, g_subln):
    b, s, _ = q.shape
    n_ctx = kc.shape[1]
    return pl.pallas_call(
        _attn_kernel,
        grid=(b, N_HEADS, s // TQ),
        in_specs=[pl.BlockSpec((1, TQ, HEAD_W), lambda bi, h, i: (bi, i, h)),
                  pl.BlockSpec((1, n_ctx, HEAD_W), lambda bi, h, i: (bi, 0, h)),
                  pl.BlockSpec((1, n_ctx, HEAD_W), lambda bi, h, i: (bi, 0, h)),
                  pl.BlockSpec((1, s, HEAD_W), lambda bi, h, i: (bi, 0, h)),
                  pl.BlockSpec((1, s, HEAD_W), lambda bi, h, i: (bi, 0, h)),
                  pl.BlockSpec((4, HEAD_DIM), lambda bi, h, i: (0, 0)),
                  pl.BlockSpec((1, HEAD_W), lambda bi, h, i: (0, 0))],
        out_specs=pl.BlockSpec((1, TQ, HEAD_W), lambda bi, h, i: (bi, i, h)),
        out_shape=jax.ShapeDtypeStruct((b, s, QK_W), BF16),
        scratch_shapes=[pltpu.VMEM((2, TQ, HEAD_W), F32)] * 3,
        compiler_params=_cparams("parallel", "parallel", "arbitrary"),
        name="diffattn",
    )(q, kc, vc, kl, vl, lam_p, g_subln)


def _dft1_kernel(y_ref, g_ref, br_ref, bi_ref):
    rows = DFT_N1 * F1_NB
    yb = y_ref[0].reshape(rows, 2 * F_W).astype(BF16)
    p = jnp.dot(g_ref[0], yb, preferred_element_type=F32)
    top, bot = p[:rows], p[rows:]
    br_ref[0] = (top[:, :F_W] - bot[:, F_W:]).reshape(DFT_N1, F1_NB, F_W)
    bi_ref[0] = (-top[:, F_W:] - bot[:, :F_W]).reshape(DFT_N1, F1_NB, F_W)


def _dft1(y4, gmat):
    b = y4.shape[0]
    rows = DFT_N1 * F1_NB
    return pl.pallas_call(
        _dft1_kernel,
        grid=(DFT_N2 // F1_NB, b),
        in_specs=[pl.BlockSpec((1, DFT_N1, F1_NB, 2 * F_W), lambda j, bi: (bi, 0, j, 0)),
                  pl.BlockSpec((1, 2 * rows, rows), lambda j, bi: (j, 0, 0))],
        out_specs=[pl.BlockSpec((1, DFT_N1, F1_NB, F_W), lambda j, bi: (bi, 0, j, 0)),
                   pl.BlockSpec((1, DFT_N1, F1_NB, F_W), lambda j, bi: (bi, 0, j, 0))],
        out_shape=[jax.ShapeDtypeStruct((b, DFT_N1, DFT_N2, F_W), F32),
                   jax.ShapeDtypeStruct((b, DFT_N1, DFT_N2, F_W), F32)],
        compiler_params=_cparams("arbitrary", "arbitrary"),
        name="dft1",
    )(y4, gmat)


def _dft2_kernel(br_ref, bi_ref, c_ref, s_ref, o_ref):
    for j in range(F2_KB):
        r = (jnp.dot(c_ref[...], br_ref[0, j].astype(BF16), preferred_element_type=F32)
             + jnp.dot(s_ref[...], bi_ref[0, j].astype(BF16), preferred_element_type=F32))
        o_ref[0, :, j, :] = r


def _dft2(br4, bi4, c2, s2):
    b = br4.shape[0]
    blk = (1, F2_KB, DFT_N2, F_W)
    return pl.pallas_call(
        _dft2_kernel,
        grid=(b, DFT_N1 // F2_KB),
        in_specs=[pl.BlockSpec(blk, lambda bi, j: (bi, j, 0, 0)),
                  pl.BlockSpec(blk, lambda bi, j: (bi, j, 0, 0)),
                  pl.BlockSpec((DFT_N2, DFT_N2), lambda bi, j: (0, 0)),
                  pl.BlockSpec((DFT_N2, DFT_N2), lambda bi, j: (0, 0))],
        out_specs=pl.BlockSpec((1, DFT_N2, F2_KB, F_W), lambda bi, j: (bi, 0, j, 0)),
        out_shape=jax.ShapeDtypeStruct((b, DFT_N2, DFT_N1, F_W), F32),
        compiler_params=_cparams("parallel", "arbitrary"),
        name="dft2",
    )(br4, bi4, c2, s2)


def _outproj_kernel(o_ref, f_ref, x_ref, mod_ref, g_ref, wo_ref, wr_ref, br_ref, tri_ref,
                    x1_ref, h3_ref, meta_ref, gate_ref, cnt_out_ref, cnt_ref):
    first = (pl.program_id(0) == 0) & (pl.program_id(1) == 0)

    @pl.when(first)
    def _():
        cnt_ref[...] = jnp.zeros(cnt_ref.shape, F32)

    m = mod_ref[0]
    mix = (jnp.dot(o_ref[0], wo_ref[0:QK_W, :], preferred_element_type=F32)
           + jnp.dot(f_ref[0].astype(BF16), wo_ref[QK_W:, :], preferred_element_type=F32))
    x1 = x_ref[0] + m[2:3] * mix
    x1_ref[0] = x1
    h2 = _norm_mod(x1, g_ref[...], m[3:4], m[4:5])
    _store_rows(h3_ref, h2)
    h_hi = h2.astype(BF16)
    h_lo = (h2 - h_hi.astype(F32)).astype(BF16)
    hw = jnp.dot(h_hi, wr_ref[...], preferred_element_type=F32)
    lg = (hw[:, :128] + hw[:, 128:]
          + jnp.dot(h_lo, wr_ref[:, 0:128], preferred_element_type=F32)
          + br_ref[...])
    lane = lax.broadcasted_iota(jnp.int32, lg.shape, 1)
    ninf = jnp.float32(-jnp.inf)
    big = jnp.int32(lg.shape[1])
    gl = jnp.where(lane < N_GROUPS, lg, ninf)
    gmax = jnp.max(gl, axis=-1, keepdims=True)
    grp = jnp.min(jnp.where(gl == gmax, lane, big), axis=-1, keepdims=True)
    pg = 1.0 / jnp.sum(jnp.exp(gl - gmax), axis=-1, keepdims=True)
    e_lane = lane - N_GROUPS
    emask = (e_lane >= 0) & (e_lane < N_EXPERTS) & ((e_lane >> 3) == grp)
    el = jnp.where(emask, lg, ninf)
    t1 = jnp.max(el, axis=-1, keepdims=True)
    i1 = jnp.min(jnp.where(el == t1, lane, big), axis=-1, keepdims=True)
    el2 = jnp.where(lane == i1, ninf, el)
    t2 = jnp.max(el2, axis=-1, keepdims=True)
    i2 = jnp.min(jnp.where(el2 == t2, lane, big), axis=-1, keepdims=True)
    dd = jnp.exp(t2 - t1)
    w1 = pg / (1.0 + dd)
    w2 = pg * dd / (1.0 + dd)
    gate_ref[...] = jnp.where(lane == 0, w1, jnp.where(lane == 1, w2, 0.0))
    hit1 = lane == i1
    hit2 = lane == i2
    oh = jnp.where(hit1 | hit2, 1.0, 0.0)
    before = jnp.dot(tri_ref[...], oh.astype(BF16), preferred_element_type=F32) + cnt_ref[0:1, :]
    r1 = jnp.sum(jnp.where(hit1, before, 0.0), axis=-1, keepdims=True).astype(jnp.int32)
    r2 = jnp.sum(jnp.where(hit2, before, 0.0), axis=-1, keepdims=True).astype(jnp.int32)
    cnt_ref[0:1, :] = cnt_ref[0:1, :] + jnp.sum(oh, axis=0, keepdims=True)
    cnt_out_ref[...] = cnt_ref[...]
    meta = jnp.where(lane == 0, i1 - N_GROUPS,
                     jnp.where(lane == 1, i2 - N_GROUPS,
                               jnp.where(lane == 2, r1, jnp.where(lane == 3, r2, 0))))
    meta_ref[...] = jnp.transpose(meta)[0:8, :]


def _outproj(attn_o, four, x, mods, g_ffn, w_out, w_r, b_r):
    b, s, d = x.shape
    tm = TM_PROJ
    nt = s // tm
    tok = lambda bi, i: (bi, i, 0)
    flat = lambda bi, i: (bi * nt + i, 0)
    tri = jnp.asarray(np.tril(np.ones((tm, tm), np.float32), -1)).astype(BF16)
    return pl.pallas_call(
        _outproj_kernel,
        grid=(b, nt),
        in_specs=[pl.BlockSpec((1, tm, QK_W), tok),
                  pl.BlockSpec((1, tm, F_W), tok),
                  pl.BlockSpec((1, tm, d), tok),
                  pl.BlockSpec((1, N_MOD, d), lambda bi, i: (bi, 0, 0)),
                  pl.BlockSpec((1, d), lambda bi, i: (0, 0)),
                  pl.BlockSpec(w_out.shape, lambda bi, i: (0, 0)),
                  pl.BlockSpec(w_r.shape, lambda bi, i: (0, 0)),
                  pl.BlockSpec(b_r.shape, lambda bi, i: (0, 0)),
                  pl.BlockSpec((tm, tm), lambda bi, i: (0, 0))],
        out_specs=[pl.BlockSpec((1, tm, d), tok),
                   pl.BlockSpec((tm * ROW_TILE, 128), flat),
                   pl.BlockSpec((8, tm), lambda bi, i: (0, bi * nt + i)),
                   pl.BlockSpec((tm, 128), flat),
                   pl.BlockSpec((8, 128), lambda bi, i: (0, 0))],
        out_shape=[jax.ShapeDtypeStruct((b, s, d), F32),
                   jax.ShapeDtypeStruct((b * s * ROW_TILE, 128), U32),
                   jax.ShapeDtypeStruct((8, b * s), jnp.int32),
                   jax.ShapeDtypeStruct((b * s, 128), F32),
                   jax.ShapeDtypeStruct((8, 128), F32)],
        scratch_shapes=[pltpu.VMEM((8, 128), F32)],
        compiler_params=_cparams("arbitrary", "arbitrary"),
        name="outproj",
    )(attn_o, four, x, mods, g_ffn, w_out, w_r, b_r, tri)


def _row_slice(row):
    return pl.ds(pl.multiple_of(row * ROW_TILE, ROW_TILE), ROW_TILE)


def _row_copy(src_ref, src_row, dst_ref, dst_row, sem):
    return pltpu.make_async_copy(src_ref.at[_row_slice(src_row)], dst_ref.at[_row_slice(dst_row)], sem)


def _bf16_bits(x):
    return lax.bitcast_convert_type(x.astype(BF16).astype(F32), U32)


def _rows_2d(ref, n_rows):
    hi, lo = [], []
    for cblk in range(ROW_TILE):
        w = ref[pl.ds(cblk, n_rows, stride=ROW_TILE), :]
        hi.append(lax.bitcast_convert_type(w & jnp.uint32(0xFFFF0000), F32).astype(BF16))
        lo.append(lax.bitcast_convert_type(w << 16, F32).astype(BF16))
    return jnp.concatenate(hi + lo, axis=1)


def _store_rows(ref, val):
    for cblk in range(ROW_TILE):
        hi = _bf16_bits(val[:, cblk * 128:(cblk + 1) * 128])
        lo = _bf16_bits(val[:, (cblk + ROW_TILE) * 128:(cblk + ROW_TILE + 1) * 128])
        ref[pl.ds(cblk, val.shape[0], stride=ROW_TILE), :] = hi | (lo >> 16)


def _dispatch_kernel(dest_ref, h_ref, zeros_ref, xs_ref, sem):
    del zeros_ref
    n = 2 * TD_DISPATCH

    def issue(r, c):
        _row_copy(h_ref, r, xs_ref, dest_ref[0, 0, 2 * r], sem).start(priority=0)
        _row_copy(h_ref, r, xs_ref, dest_ref[0, 0, 2 * r + 1], sem).start(priority=1)
        return c

    lax.fori_loop(0, TD_DISPATCH, issue, 0, unroll=DMA_UNROLL // 2)

    def drain(a, c):
        _row_copy(h_ref, 0, xs_ref, 0, sem).wait()
        return c

    lax.fori_loop(0, n, drain, 0, unroll=DMA_UNROLL)


def _dispatch(dest3, h3, xs_zeros):
    t = h3.shape[0] // ROW_TILE
    return pl.pallas_call(
        _dispatch_kernel,
        grid=(t // TD_DISPATCH,),
        in_specs=[pl.BlockSpec((1, 1, 2 * TD_DISPATCH), lambda i: (i, 0, 0),
                               memory_space=pltpu.SMEM),
                  pl.BlockSpec((TD_DISPATCH * ROW_TILE, 128), lambda i: (i, 0)),
                  pl.BlockSpec(memory_space=pl.ANY)],
        out_specs=pl.BlockSpec(memory_space=pl.ANY),
        out_shape=jax.ShapeDtypeStruct(xs_zeros.shape, xs_zeros.dtype),
        scratch_shapes=[pltpu.SemaphoreType.DMA(())],
        input_output_aliases={2: 0},
        compiler_params=_cparams("arbitrary"),
        name="dispatch",
    )(dest3, h3, xs_zeros)


def _experts_kernel(be_ref, nu_ref, nxt_ref, xs_ref, wg_hbm, wu_hbm, wd_hbm, ys_ref,
                    wgb, wub, wdb, sg, su, sd, sem):
    j = pl.program_id(0)
    used = j < nu_ref[0]
    e = be_ref[j]
    new_expert = (j == 0) | (e != be_ref[jnp.maximum(j - 1, 0)])
    e_next = nxt_ref[e]

    def weight_copies(idx):
        return (pltpu.make_async_copy(wg_hbm.at[idx], sg, sem.at[0]),
                pltpu.make_async_copy(wu_hbm.at[idx], su, sem.at[1]),
                pltpu.make_async_copy(wd_hbm.at[idx], sd, sem.at[2]))

    @pl.when(j == 0)
    def _():
        for cp in weight_copies(e):
            cp.start(priority=1)

    @pl.when(used & new_expert)
    def _():
        for cp in weight_copies(e):
            cp.wait()
        wgb[...] = sg[...].astype(BF16)
        wub[...] = su[...].astype(BF16)
        wdb[...] = sd[...].astype(BF16)

    @pl.when(used & new_expert & (e_next != e))
    def _():
        for cp in weight_copies(e_next):
            cp.start(priority=1)

    @pl.when(used)
    def _():
        xb = _rows_2d(xs_ref, MOE_BLK)
        gate = jnp.dot(xb, wgb[...], preferred_element_type=F32)
        up = jnp.dot(xb, wub[...], preferred_element_type=F32)
        hid = (gate * jax.nn.sigmoid(gate) * up).astype(BF16)
        _store_rows(ys_ref, jnp.dot(hid, wdb[...], preferred_element_type=F32))

    @pl.when(pl.program_id(0) >= nu_ref[0])
    def _():
        ys_ref[...] = jnp.zeros_like(ys_ref)


def _experts(blk_expert, n_used, next_expert, xs, wg, wu, wd):
    rows = xs.shape[0] // ROW_TILE
    d = wg.shape[1]
    nb = rows // MOE_BLK
    row_blk = lambda j, be, nu, nx: (jnp.minimum(j, nu[0] - 1), 0)
    out_blk = lambda j, be, nu, nx: (j, 0)
    return pl.pallas_call(
        _experts_kernel,
        grid_spec=pltpu.PrefetchScalarGridSpec(
            num_scalar_prefetch=3,
            grid=(nb,),
            in_specs=[pl.BlockSpec((MOE_BLK * ROW_TILE, 128), row_blk),
                      pl.BlockSpec(memory_space=pl.ANY),
                      pl.BlockSpec(memory_space=pl.ANY),
                      pl.BlockSpec(memory_space=pl.ANY)],
            out_specs=pl.BlockSpec((MOE_BLK * ROW_TILE, 128), out_blk),
            scratch_shapes=[pltpu.VMEM((d, D_EXPERT), BF16), pltpu.VMEM((d, D_EXPERT), BF16),
                            pltpu.VMEM((D_EXPERT, d), BF16),
                            pltpu.VMEM((d, D_EXPERT), F32), pltpu.VMEM((d, D_EXPERT), F32),
                            pltpu.VMEM((D_EXPERT, d), F32),
                            pltpu.SemaphoreType.DMA((3,))]),
        out_shape=jax.ShapeDtypeStruct(xs.shape, xs.dtype),
        compiler_params=_cparams("arbitrary"),
        name="experts",
    )(blk_expert, n_used, next_expert, xs, wg, wu, wd)


def _combine_kernel(dest_ref, dest_next_ref, ys_ref, x1_ref, gate_ref, mod_ref, g_ref, o_ref,
                    ya, yb, sem):
    step = pl.program_id(0) * pl.num_programs(1) + pl.program_id(1)
    n_steps = pl.num_programs(0) * pl.num_programs(1)
    slot = step % 2

    def start_gathers(idx_ref, to_slot):
        def issue(r, c):
            _row_copy(ys_ref, idx_ref[0, 0, 2 * r], ya.at[to_slot], r,
                      sem.at[to_slot]).start(priority=0)
            _row_copy(ys_ref, idx_ref[0, 0, 2 * r + 1], yb.at[to_slot], r,
                      sem.at[to_slot]).start(priority=1)
            return c

        lax.fori_loop(0, TD, issue, 0, unroll=DMA_UNROLL)

    @pl.when(step == 0)
    def _():
        start_gathers(dest_ref, 0)

    @pl.when(step + 1 < n_steps)
    def _():
        start_gathers(dest_next_ref, 1 - slot)

    def drain(r, c):
        _row_copy(ys_ref, 0, ya.at[slot], 0, sem.at[slot]).wait()
        _row_copy(ys_ref, 0, yb.at[slot], 0, sem.at[slot]).wait()
        return c

    lax.fori_loop(0, TD, drain, 0, unroll=DMA_UNROLL)
    gt = gate_ref[...]
    moe = (gt[:, 0:1] * _rows_2d(ya.at[slot], TD).astype(F32)
           + gt[:, 1:2] * _rows_2d(yb.at[slot], TD).astype(F32))
    x2 = x1_ref[0] + mod_ref[0][5:6] * moe
    ms = jnp.mean(x2 * x2, axis=-1, keepdims=True)
    o_ref[0] = x2 * lax.rsqrt(ms + EPS) * g_ref[...]


def _combine(dest3, ys, x1, gates, mods, g_final):
    b, s, d = x1.shape
    nt = s // TD
    return pl.pallas_call(
        _combine_kernel,
        grid=(b, nt),
        in_specs=[pl.BlockSpec((1, 1, 2 * TD), lambda bi, i: (bi * nt + i, 0, 0),
                               memory_space=pltpu.SMEM),
                  pl.BlockSpec((1, 1, 2 * TD),
                               lambda bi, i: (jnp.minimum(bi * nt + i + 1, b * nt - 1), 0, 0),
                               memory_space=pltpu.SMEM),
                  pl.BlockSpec(memory_space=pl.ANY),
                  pl.BlockSpec((1, TD, d), lambda bi, i: (bi, i, 0)),
                  pl.BlockSpec((TD, 128), lambda bi, i: (bi * nt + i, 0)),
                  pl.BlockSpec((1, N_MOD, d), lambda bi, i: (bi, 0, 0)),
                  pl.BlockSpec((1, d), lambda bi, i: (0, 0))],
        out_specs=pl.BlockSpec((1, TD, d), lambda bi, i: (bi, i, 0)),
        out_shape=jax.ShapeDtypeStruct((b, s, d), F32),
        scratch_shapes=[pltpu.VMEM((2, TD * ROW_TILE, 128), U32),
                        pltpu.VMEM((2, TD * ROW_TILE, 128), U32),
                        pltpu.SemaphoreType.DMA((2,))],
        compiler_params=_cparams("arbitrary", "arbitrary"),
        name="combine",
    )(dest3, dest3, ys, x1, gates, mods, g_final)


@functools.lru_cache(maxsize=None)
def _rope_tables(rows):
    r, col = np.meshgrid(np.arange(rows), np.arange(GRID_W), indexing='ij')
    pos = np.stack([r.reshape(-1), col.reshape(-1)], axis=-1).astype(np.float32)
    inv_freq = (np.float32(ROPE_THETA)
                ** (-np.arange(0, ROPE_AXIS, 2, dtype=np.float32) / np.float32(ROPE_AXIS))).astype(np.float32)
    ang = (pos[:, :, None] * inv_freq).astype(np.float32)
    ang = np.concatenate([ang, ang], axis=-1).astype(np.float64)
    n = ang.shape[0]
    cos = np.tile(np.cos(ang).reshape(n, HEAD_DIM), (1, 2)).astype(np.float32)
    sin = np.tile(np.sin(ang).reshape(n, HEAD_DIM), (1, 2)).astype(np.float32)
    upper = (np.arange(HEAD_W) % ROPE_AXIS) >= ROPE_HALF
    sa = np.where(upper, sin, np.float32(0.0))
    sb = np.where(upper, np.float32(0.0), -sin)
    return cos, sa, sb


@functools.lru_cache(maxsize=None)
def _dft_constants(n_pos):
    c = np.arange(FGROUP_DIM)
    ang_c = 2.0 * np.pi * ((c[:, None] * c[None, :]) % FGROUP_DIM) / FGROUP_DIM
    norm = 1.0 / math.sqrt(n_pos * FGROUP_DIM)
    cmat = (np.cos(ang_c) * norm).astype(np.float32)
    smat = (np.sin(ang_c) * norm).astype(np.float32)
    k1 = np.arange(DFT_N1)[None, :, None]
    n1 = np.arange(DFT_N1)[None, None, :]
    n2 = np.arange(DFT_N2)[:, None, None]
    ang_g = 2.0 * np.pi * ((k1 * (DFT_N2 * n1 + n2)) % n_pos) / n_pos
    gsmall = np.stack([np.cos(ang_g), np.sin(ang_g)], axis=1)
    gsmall = gsmall.reshape(DFT_N2 // F1_NB, F1_NB, 2, DFT_N1, DFT_N1)
    gmat = np.einsum('japkn,ab->jpkanb', gsmall, np.eye(F1_NB)).reshape(
        DFT_N2 // F1_NB, 2 * DFT_N1 * F1_NB, DFT_N1 * F1_NB).astype(BF16)
    k2 = np.arange(DFT_N2)
    ang_2 = 2.0 * np.pi * ((k2[:, None] * k2[None, :]) % DFT_N2) / DFT_N2
    c2 = np.cos(ang_2).astype(np.float32)
    s2 = np.sin(ang_2).astype(np.float32)
    return cmat, smat, gmat, c2, s2


def kernel(x, c, ctx, c_ctx, w_ada, b_ada, g_mix_norm, g_ffn_norm, w_in, lambda_q1, lambda_k1, lambda_q2, lambda_k2, g_subln, w_fourier, w_out, w_router_group, b_router_group, w_router_expert, b_router_expert, w_gate, w_up, w_down, g_final):
    b, s, d = x.shape
    t = b * s
    assert d == D_MODEL and s == DFT_N1 * DFT_N2 and s % GRID_W == 0 and b == 2

    cc = jnp.concatenate([c, c_ctx[None, :], jnp.zeros((8 - b - 1, d), F32)], axis=0)
    mods = _adaln(cc, w_ada[0], b_ada[0]).reshape(8, N_MOD, d)

    cmat, smat, gmat, c2, s2 = _dft_constants(s)
    wf = _wfold(jnp.asarray(cmat), jnp.asarray(smat), w_fourier[0])
    cos_t, sa_t, sb_t = _rope_tables(s // GRID_W)

    w_in_b = w_in[0].astype(BF16)
    g_mix = g_mix_norm[0].reshape(1, d)
    q, kl, vl, y = _inproj(x, mods, g_mix, w_in_b, wf, cos_t, sa_t, sb_t)
    kc, vc = _ctxproj(ctx, mods, g_mix, w_in_b)

    lam_p = jnp.stack([lambda_q1[0], lambda_k1[0], lambda_q2[0], lambda_k2[0]], axis=0)
    attn_o = _attention(q, kc, vc, kl, vl, lam_p, g_subln[0].reshape(1, HEAD_W))

    br, bi = _dft1(y.reshape(b, DFT_N1, DFT_N2, 2 * F_W), jnp.asarray(gmat))
    four = _dft2(br, bi, jnp.asarray(c2).astype(BF16),
                 jnp.asarray(s2).astype(BF16)).reshape(b, s, F_W)

    n_r = N_GROUPS + N_EXPERTS
    w_r = jnp.concatenate([w_router_group[0], w_router_expert[0],
                           jnp.zeros((d, 128 - n_r), F32)], axis=1)
    b_r = jnp.concatenate([b_router_group[0], b_router_expert[0],
                           jnp.zeros((128 - n_r,), F32)]).reshape(1, 128)
    w_r_hi = w_r.astype(BF16)
    w_r_lo = (w_r - w_r_hi.astype(F32)).astype(BF16)
    x1, h3, meta, gates, cnt = _outproj(attn_o, four, x, mods, g_ffn_norm[0].reshape(1, d),
                                        w_out[0].astype(BF16),
                                        jnp.concatenate([w_r_hi, w_r_lo], axis=1), b_r)

    e_tk = jnp.stack([meta[0], meta[1]], axis=1)
    rank_tk = jnp.stack([meta[2], meta[3]], axis=1)
    counts = cnt[0, N_GROUPS:N_GROUPS + N_EXPERTS].astype(jnp.int32)
    nblk = (counts + MOE_BLK - 1) // MOE_BLK
    blk_end = jnp.cumsum(nblk)
    blk_start = blk_end - nblk
    onehot = (e_tk[:, :, None] == jnp.arange(N_EXPERTS, dtype=jnp.int32)).astype(F32)
    first_blk = jnp.einsum('tke,e->tk', onehot, blk_start.astype(F32),
                           precision=lax.Precision.HIGHEST).astype(jnp.int32)
    dest = first_blk * MOE_BLK + rank_tk
    n_blocks = t * TOP_K // MOE_BLK + N_EXPERTS
    blk_ids = jnp.arange(n_blocks, dtype=jnp.int32)
    blk_expert = jnp.minimum(
        jnp.sum((blk_end[None, :] <= blk_ids[:, None]).astype(jnp.int32), axis=1),
        N_EXPERTS - 1).astype(jnp.int32)
    n_used = blk_end[-1:].astype(jnp.int32)
    dest3 = dest.reshape(t // TD, 1, 2 * TD)

    xs = _dispatch(dest.reshape(t // TD_DISPATCH, 1, 2 * TD_DISPATCH), h3,
                   jnp.zeros((n_blocks * MOE_BLK * ROW_TILE, 128), U32))
    e_ids = jnp.arange(N_EXPERTS, dtype=jnp.int32)
    later = (e_ids[None, :] > e_ids[:, None]) & (nblk[None, :] > 0)
    next_expert = jnp.min(jnp.where(later, e_ids[None, :], N_EXPERTS), axis=1)
    next_expert = jnp.where(next_expert == N_EXPERTS, e_ids, next_expert).astype(jnp.int32)
    ys = _experts(blk_expert, n_used, next_expert, xs, w_gate[0], w_up[0], w_down[0])
    return _combine(dest3, ys, x1, gates, mods, g_final.reshape(1, d))
```

```python
import functools
import math

import numpy as np
import jax
import jax.numpy as jnp
from jax import lax
from jax.experimental import pallas as pl
from jax.experimental.pallas import tpu as pltpu

F32 = jnp.float32
BF16 = jnp.bfloat16

D_MODEL = 1024
GRID_W = 64
N_HEADS = 4
HEAD_DIM = 64
HEAD_W = 2 * HEAD_DIM
QK_W = N_HEADS * HEAD_W
N_FGROUPS = 4
FGROUP_DIM = 128
F_W = N_FGROUPS * FGROUP_DIM
ROPE_THETA = 10000.0
ROPE_AXIS = HEAD_DIM // 2
ROPE_HALF = ROPE_AXIS // 2
N_GROUPS = 4
EXPERTS_PER_GROUP = 8
N_EXPERTS = N_GROUPS * EXPERTS_PER_GROUP
TOP_K = 2
D_EXPERT = 512
N_MOD = 6
EPS = 1e-6
LAMBDA_INIT = 0.8 - 0.6 * math.exp(-0.3 * 0)
LOG2_E = 1.4426950408889634

DFT_N1 = 64
DFT_N2 = 128

TM_PROJ = 512
TQ = 1024
TK = 2048
F1_NB = 8
F2_KB = 8
MOE_BLK = 512
ROW_TILE = D_MODEL // 256
U32 = jnp.uint32
TD = 512
TD_DISPATCH = 1024
DMA_UNROLL = 16
VMEM_LIMIT = 48 * 1024 * 1024


def _cparams(*sem):
    return pltpu.CompilerParams(dimension_semantics=sem, vmem_limit_bytes=VMEM_LIMIT)


def _adaln_kernel(c_ref, w_ref, b_ref, o_ref):
    cc = c_ref[...]
    s = cc * jax.nn.sigmoid(cc)
    o_ref[...] = jnp.dot(s, w_ref[...], preferred_element_type=F32,
                         precision=lax.Precision.HIGHEST) + b_ref[...]


def _adaln(cc, w_ada, b_ada):
    n = w_ada.shape[1]
    tn = 1536
    return pl.pallas_call(
        _adaln_kernel,
        grid=(n // tn,),
        in_specs=[pl.BlockSpec((8, D_MODEL), lambda j: (0, 0)),
                  pl.BlockSpec((D_MODEL, tn), lambda j: (0, j)),
                  pl.BlockSpec((1, tn), lambda j: (0, j))],
        out_specs=pl.BlockSpec((8, tn), lambda j: (0, j)),
        out_shape=jax.ShapeDtypeStruct((8, n), F32),
        compiler_params=_cparams("arbitrary"),
        name="adaln",
    )(cc, w_ada, b_ada.reshape(1, n))


def _wfold_kernel(c_ref, s_ref, w_ref, o_ref):
    w = w_ref[0]
    o_ref[0, :, :FGROUP_DIM] = jnp.dot(c_ref[...], w, preferred_element_type=F32,
                                       precision=lax.Precision.HIGHEST).astype(BF16)
    o_ref[0, :, FGROUP_DIM:] = jnp.dot(s_ref[...], w, preferred_element_type=F32,
                                       precision=lax.Precision.HIGHEST).astype(BF16)


def _wfold(cmat, smat, w_fourier):
    return pl.pallas_call(
        _wfold_kernel,
        grid=(N_FGROUPS,),
        in_specs=[pl.BlockSpec((FGROUP_DIM, FGROUP_DIM), lambda g: (0, 0)),
                  pl.BlockSpec((FGROUP_DIM, FGROUP_DIM), lambda g: (0, 0)),
                  pl.BlockSpec((1, FGROUP_DIM, FGROUP_DIM), lambda g: (g, 0, 0))],
        out_specs=pl.BlockSpec((1, FGROUP_DIM, 2 * FGROUP_DIM), lambda g: (g, 0, 0)),
        out_shape=jax.ShapeDtypeStruct((N_FGROUPS, FGROUP_DIM, 2 * FGROUP_DIM), BF16),
        compiler_params=_cparams("arbitrary"),
        name="wfold",
    )(cmat, smat, w_fourier)


def _norm_mod(x, g, shift, scale):
    ms = jnp.mean(x * x, axis=-1, keepdims=True)
    y = x * lax.rsqrt(ms + EPS) * g
    return y * (1.0 + scale) + shift


def _rope_slab(p, cos, sa, sb):
    return (p * cos + pltpu.roll(p, ROPE_HALF, 1) * sa
            + pltpu.roll(p, HEAD_W - ROPE_HALF, 1) * sb)


def _inproj_kernel(x_ref, mod_ref, g_ref, w_ref, wf_ref, cos_ref, sa_ref, sb_ref,
                   q_ref, k_ref, v_ref, y_ref):
    m = mod_ref[0]
    h = _norm_mod(x_ref[0], g_ref[...], m[0:1], m[1:2]).astype(BF16)
    cos, sa, sb = cos_ref[...], sa_ref[...], sb_ref[...]
    scale = HEAD_DIM ** -0.5 * LOG2_E
    pq = jnp.dot(h, w_ref[:, 0:QK_W], preferred_element_type=F32)
    for hh in range(N_HEADS):
        sl = slice(hh * HEAD_W, (hh + 1) * HEAD_W)
        q_ref[0, :, sl] = (_rope_slab(pq[:, sl], cos, sa, sb) * scale).astype(BF16)
    pk = jnp.dot(h, w_ref[:, QK_W:2 * QK_W], preferred_element_type=F32)
    for hh in range(N_HEADS):
        sl = slice(hh * HEAD_W, (hh + 1) * HEAD_W)
        k_ref[0, :, sl] = _rope_slab(pk[:, sl], cos, sa, sb).astype(BF16)
    v_ref[0] = jnp.dot(h, w_ref[:, 2 * QK_W:3 * QK_W], preferred_element_type=F32).astype(BF16)
    pf = jnp.dot(h, w_ref[:, 3 * QK_W:], preferred_element_type=F32).astype(BF16)
    for g in range(N_FGROUPS):
        yy = jnp.dot(pf[:, g * FGROUP_DIM:(g + 1) * FGROUP_DIM], wf_ref[g],
                     preferred_element_type=F32)
        y_ref[0, :, g * FGROUP_DIM:(g + 1) * FGROUP_DIM] = yy[:, :FGROUP_DIM]
        y_ref[0, :, F_W + g * FGROUP_DIM:F_W + (g + 1) * FGROUP_DIM] = yy[:, FGROUP_DIM:]


def _inproj(x, mods, g_mix, w_in, wf, cos_t, sa_t, sb_t):
    b, s, d = x.shape
    tm = TM_PROJ
    tok = lambda bi, i: (bi, i, 0)
    return pl.pallas_call(
        _inproj_kernel,
        grid=(b, s // tm),
        in_specs=[pl.BlockSpec((1, tm, d), tok),
                  pl.BlockSpec((1, N_MOD, d), lambda bi, i: (bi, 0, 0)),
                  pl.BlockSpec((1, d), lambda bi, i: (0, 0)),
                  pl.BlockSpec(w_in.shape, lambda bi, i: (0, 0)),
                  pl.BlockSpec(wf.shape, lambda bi, i: (0, 0, 0)),
                  pl.BlockSpec((tm, HEAD_W), lambda bi, i: (i, 0)),
                  pl.BlockSpec((tm, HEAD_W), lambda bi, i: (i, 0)),
                  pl.BlockSpec((tm, HEAD_W), lambda bi, i: (i, 0))],
        out_specs=[pl.BlockSpec((1, tm, QK_W), tok),
                   pl.BlockSpec((1, tm, QK_W), tok),
                   pl.BlockSpec((1, tm, QK_W), tok),
                   pl.BlockSpec((1, tm, 2 * F_W), tok)],
        out_shape=[jax.ShapeDtypeStruct((b, s, QK_W), BF16),
                   jax.ShapeDtypeStruct((b, s, QK_W), BF16),
                   jax.ShapeDtypeStruct((b, s, QK_W), BF16),
                   jax.ShapeDtypeStruct((b, s, 2 * F_W), F32)],
        compiler_params=_cparams("parallel", "arbitrary"),
        name="inproj",
    )(x, mods, g_mix, w_in, wf, cos_t, sa_t, sb_t)


def _ctxproj_kernel(x_ref, mod_ref, g_ref, w_ref, k_ref, v_ref):
    m = mod_ref[0]
    h = _norm_mod(x_ref[0], g_ref[...], m[0:1], m[1:2]).astype(BF16)
    k_ref[0] = jnp.dot(h, w_ref[:, QK_W:2 * QK_W], preferred_element_type=F32).astype(BF16)
    v_ref[0] = jnp.dot(h, w_ref[:, 2 * QK_W:3 * QK_W], preferred_element_type=F32).astype(BF16)


def _ctxproj(ctx, mods, g_mix, w_in):
    b, n, d = ctx.shape
    return pl.pallas_call(
        _ctxproj_kernel,
        grid=(b,),
        in_specs=[pl.BlockSpec((1, n, d), lambda bi: (bi, 0, 0)),
                  pl.BlockSpec((1, N_MOD, d), lambda bi: (2, 0, 0)),
                  pl.BlockSpec((1, d), lambda bi: (0, 0)),
                  pl.BlockSpec(w_in.shape, lambda bi: (0, 0))],
        out_specs=[pl.BlockSpec((1, n, QK_W), lambda bi: (bi, 0, 0)),
                   pl.BlockSpec((1, n, QK_W), lambda bi: (bi, 0, 0))],
        out_shape=[jax.ShapeDtypeStruct((b, n, QK_W), BF16),
                   jax.ShapeDtypeStruct((b, n, QK_W), BF16)],
        compiler_params=_cparams("arbitrary"),
        name="ctxproj",
    )(ctx, mods, g_mix, w_in)


def _attn_kernel(q_ref, kc_ref, vc_ref, kl_ref, vl_ref, lam_ref, g_ref, o_ref,
                 m_ref, l_ref, acc_ref):
    q = q_ref[0]
    lane = lax.broadcasted_iota(jnp.int32, q.shape, 1)
    zero = jnp.zeros_like(q)
    q0 = jnp.where(lane < HEAD_DIM, q, zero)
    q1 = jnp.where(lane >= HEAD_DIM, q, zero)
    qs = (q0, q1)
    contract_last = (((1,), (1,)), ((), ()))
    m_ref[...] = jnp.full(m_ref.shape, -1e30, F32)
    l_ref[...] = jnp.zeros(l_ref.shape, F32)
    acc_ref[...] = jnp.zeros(acc_ref.shape, F32)

    def step(kb, vb):
        nk = kb.shape[0] // HEAD_W
        for mi in range(2):
            s = lax.dot_general(qs[mi], kb, contract_last, preferred_element_type=F32)
            m_old = m_ref[mi]
            m_new = jnp.maximum(m_old, jnp.max(s, axis=-1, keepdims=True))
            alpha = jnp.exp2(m_old - m_new)
            p = jnp.exp2(s - jnp.concatenate([m_new] * nk, axis=1))
            psum = p[:, 0:HEAD_W]
            for cblk in range(1, nk):
                psum = psum + p[:, cblk * HEAD_W:(cblk + 1) * HEAD_W]
            l_ref[mi] = alpha * l_ref[mi] + psum
            acc_ref[mi] = alpha * acc_ref[mi] + jnp.dot(p.astype(BF16), vb,
                                                       preferred_element_type=F32)
            m_ref[mi] = m_new

    step(kc_ref[0], vc_ref[0])

    def body(i, c):
        off = pl.multiple_of(i * TK, TK)
        step(kl_ref[0, pl.ds(off, TK), :], vl_ref[0, pl.ds(off, TK), :])
        return c

    lax.fori_loop(0, kl_ref.shape[1] // TK, body, 0)

    lp = lam_ref[...]
    t1 = jnp.sum(lp[0:1] * lp[1:2], axis=-1, keepdims=True)
    t2 = jnp.sum(lp[2:3] * lp[3:4], axis=-1, keepdims=True)
    lam = jnp.exp(t1) - jnp.exp(t2) + LAMBDA_INIT
    l0 = jnp.sum(l_ref[0], axis=-1, keepdims=True)
    l1 = jnp.sum(l_ref[1], axis=-1, keepdims=True)
    o = acc_ref[0] / l0 - lam * (acc_ref[1] / l1)
    ms = jnp.mean(o * o, axis=-1, keepdims=True)
    o = o * lax.rsqrt(ms + EPS) * g_ref[...] * (1.0 - LAMBDA_INIT)
    o_ref[0] = o.astype(BF16)


def _attention(q, kc, vc, kl, vl, lam_p, g_subln):
    b, s, _ = q.shape
    n_ctx = kc.shape[1]
    return pl.pallas_call(
        _attn_kernel,
        grid=(b, N_HEADS, s // TQ),
        in_specs=[pl.BlockSpec((1, TQ, HEAD_W), lambda bi, h, i: (bi, i, h)),
                  pl.BlockSpec((1, n_ctx, HEAD_W), lambda bi, h, i: (bi, 0, h)),
                  pl.BlockSpec((1, n_ctx, HEAD_W), lambda bi, h, i: (bi, 0, h)),
                  pl.BlockSpec((1, s, HEAD_W), lambda bi, h, i: (bi, 0, h)),
                  pl.BlockSpec((1, s, HEAD_W), lambda bi, h, i: (bi, 0, h)),
                  pl.BlockSpec((4, HEAD_DIM), lambda bi, h, i: (0, 0)),
                  pl.BlockSpec((1, HEAD_W), lambda bi, h, i: (0, 0))],
        out_specs=pl.BlockSpec((1, TQ, HEAD_W), lambda bi, h, i: (bi, i, h)),
        out_shape=jax.ShapeDtypeStruct((b, s, QK_W), BF16),
        scratch_shapes=[pltpu.VMEM((2, TQ, HEAD_W), F32)] * 3,
        compiler_params=_cparams("parallel", "parallel", "arbitrary"),
        name="diffattn",
    )(q, kc, vc, kl, vl, lam_p, g_subln)


def _dft1_kernel(y_ref, g_ref, br_ref, bi_ref):
    rows = DFT_N1 * F1_NB
    yb = y_ref[0].reshape(rows, 2 * F_W).astype(BF16)
    p = jnp.dot(g_ref[0], yb, preferred_element_type=F32)
    top, bot = p[:rows], p[rows:]
    br_ref[0] = (top[:, :F_W] - bot[:, F_W:]).reshape(DFT_N1, F1_NB, F_W)
    bi_ref[0] = (-top[:, F_W:] - bot[:, :F_W]).reshape(DFT_N1, F1_NB, F_W)


def _dft1(y4, gmat):
    b = y4.shape[0]
    rows = DFT_N1 * F1_NB
    return pl.pallas_call(
        _dft1_kernel,
        grid=(DFT_N2 // F1_NB, b),
        in_specs=[pl.BlockSpec((1, DFT_N1, F1_NB, 2 * F_W), lambda j, bi: (bi, 0, j, 0)),
                  pl.BlockSpec((1, 2 * rows, rows), lambda j, bi: (j, 0, 0))],
        out_specs=[pl.BlockSpec((1, DFT_N1, F1_NB, F_W), lambda j, bi: (bi, 0, j, 0)),
                   pl.BlockSpec((1, DFT_N1, F1_NB, F_W), lambda j, bi: (bi, 0, j, 0))],
        out_shape=[jax.ShapeDtypeStruct((b, DFT_N1, DFT_N2, F_W), F32),
                   jax.ShapeDtypeStruct((b, DFT_N1, DFT_N2, F_W), F32)],
        compiler_params=_cparams("arbitrary", "arbitrary"),
        name="dft1",
    )(y4, gmat)


def _dft2_kernel(br_ref, bi_ref, c_ref, s_ref, o_ref):
    for j in range(F2_KB):
        r = (jnp.dot(c_ref[...], br_ref[0, j].astype(BF16), preferred_element_type=F32)
             + jnp.dot(s_ref[...], bi_ref[0, j].astype(BF16), preferred_element_type=F32))
        o_ref[0, :, j, :] = r


def _dft2(br4, bi4, c2, s2):
    b = br4.shape[0]
    blk = (1, F2_KB, DFT_N2, F_W)
    return pl.pallas_call(
        _dft2_kernel,
        grid=(b, DFT_N1 // F2_KB),
        in_specs=[pl.BlockSpec(blk, lambda bi, j: (bi, j, 0, 0)),
                  pl.BlockSpec(blk, lambda bi, j: (bi, j, 0, 0)),
                  pl.BlockSpec((DFT_N2, DFT_N2), lambda bi, j: (0, 0)),
                  pl.BlockSpec((DFT_N2, DFT_N2), lambda bi, j: (0, 0))],
        out_specs=pl.BlockSpec((1, DFT_N2, F2_KB, F_W), lambda bi, j: (bi, 0, j, 0)),
        out_shape=jax.ShapeDtypeStruct((b, DFT_N2, DFT_N1, F_W), F32),
        compiler_params=_cparams("parallel", "arbitrary"),
        name="dft2",
    )(br4, bi4, c2, s2)


def _outproj_kernel(o_ref, f_ref, x_ref, mod_ref, g_ref, wo_ref, wr_ref, br_ref, tri_ref,
                    x1_ref, h3_ref, meta_ref, gate_ref, cnt_out_ref, cnt_ref):
    first = (pl.program_id(0) == 0) & (pl.program_id(1) == 0)

    @pl.when(first)
    def _():
        cnt_ref[...] = jnp.zeros(cnt_ref.shape, F32)

    m = mod_ref[0]
    mix = (jnp.dot(o_ref[0], wo_ref[0:QK_W, :], preferred_element_type=F32)
           + jnp.dot(f_ref[0].astype(BF16), wo_ref[QK_W:, :], preferred_element_type=F32))
    x1 = x_ref[0] + m[2:3] * mix
    x1_ref[0] = x1
    h2 = _norm_mod(x1, g_ref[...], m[3:4], m[4:5])
    _store_rows(h3_ref, h2)
    h_hi = h2.astype(BF16)
    h_lo = (h2 - h_hi.astype(F32)).astype(BF16)
    hw = jnp.dot(h_hi, wr_ref[...], preferred_element_type=F32)
    lg = (hw[:, :128] + hw[:, 128:]
          + jnp.dot(h_lo, wr_ref[:, 0:128], preferred_element_type=F32)
          + br_ref[...])
    lane = lax.broadcasted_iota(jnp.int32, lg.shape, 1)
    ninf = jnp.float32(-jnp.inf)
    big = jnp.int32(lg.shape[1])
    gl = jnp.where(lane < N_GROUPS, lg, ninf)
    gmax = jnp.max(gl, axis=-1, keepdims=True)
    grp = jnp.min(jnp.where(gl == gmax, lane, big), axis=-1, keepdims=True)
    pg = 1.0 / jnp.sum(jnp.exp(gl - gmax), axis=-1, keepdims=True)
    e_lane = lane - N_GROUPS
    emask = (e_lane >= 0) & (e_lane < N_EXPERTS) & ((e_lane >> 3) == grp)
    el = jnp.where(emask, lg, ninf)
    t1 = jnp.max(el, axis=-1, keepdims=True)
    i1 = jnp.min(jnp.where(el == t1, lane, big), axis=-1, keepdims=True)
    el2 = jnp.where(lane == i1, ninf, el)
    t2 = jnp.max(el2, axis=-1, keepdims=True)
    i2 = jnp.min(jnp.where(el2 == t2, lane, big), axis=-1, keepdims=True)
    dd = jnp.exp(t2 - t1)
    w1 = pg / (1.0 + dd)
    w2 = pg * dd / (1.0 + dd)
    gate_ref[...] = jnp.where(lane == 0, w1, jnp.where(lane == 1, w2, 0.0))
    hit1 = lane == i1
    hit2 = lane == i2
    oh = jnp.where(hit1 | hit2, 1.0, 0.0)
    before = jnp.dot(tri_ref[...], oh.astype(BF16), preferred_element_type=F32) + cnt_ref[0:1, :]
    r1 = jnp.sum(jnp.where(hit1, before, 0.0), axis=-1, keepdims=True).astype(jnp.int32)
    r2 = jnp.sum(jnp.where(hit2, before, 0.0), axis=-1, keepdims=True).astype(jnp.int32)
    cnt_ref[0:1, :] = cnt_ref[0:1, :] + jnp.sum(oh, axis=0, keepdims=True)
    cnt_out_ref[...] = cnt_ref[...]
    meta = jnp.where(lane == 0, i1 - N_GROUPS,
                     jnp.where(lane == 1, i2 - N_GROUPS,
                               jnp.where(lane == 2, r1, jnp.where(lane == 3, r2, 0))))
    meta_ref[...] = jnp.transpose(meta)[0:8, :]


def _outproj(attn_o, four, x, mods, g_ffn, w_out, w_r, b_r):
    b, s, d = x.shape
    tm = TM_PROJ
    nt = s // tm
    tok = lambda bi, i: (bi, i, 0)
    flat = lambda bi, i: (bi * nt + i, 0)
    tri = jnp.asarray(np.tril(np.ones((tm, tm), np.float32), -1)).astype(BF16)
    return pl.pallas_call(
        _outproj_kernel,
        grid=(b, nt),
        in_specs=[pl.BlockSpec((1, tm, QK_W), tok),
                  pl.BlockSpec((1, tm, F_W), tok),
                  pl.BlockSpec((1, tm, d), tok),
                  pl.BlockSpec((1, N_MOD, d), lambda bi, i: (bi, 0, 0)),
                  pl.BlockSpec((1, d), lambda bi, i: (0, 0)),
                  pl.BlockSpec(w_out.shape, lambda bi, i: (0, 0)),
                  pl.BlockSpec(w_r.shape, lambda bi, i: (0, 0)),
                  pl.BlockSpec(b_r.shape, lambda bi, i: (0, 0)),
                  pl.BlockSpec((tm, tm), lambda bi, i: (0, 0))],
        out_specs=[pl.BlockSpec((1, tm, d), tok),
                   pl.BlockSpec((tm * ROW_TILE, 128), flat),
                   pl.BlockSpec((8, tm), lambda bi, i: (0, bi * nt + i)),
                   pl.BlockSpec((tm, 128), flat),
                   pl.BlockSpec((8, 128), lambda bi, i: (0, 0))],
        out_shape=[jax.ShapeDtypeStruct((b, s, d), F32),
                   jax.ShapeDtypeStruct((b * s * ROW_TILE, 128), U32),
                   jax.ShapeDtypeStruct((8, b * s), jnp.int32),
                   jax.ShapeDtypeStruct((b * s, 128), F32),
                   jax.ShapeDtypeStruct((8, 128), F32)],
        scratch_shapes=[pltpu.VMEM((8, 128), F32)],
        compiler_params=_cparams("arbitrary", "arbitrary"),
        name="outproj",
    )(attn_o, four, x, mods, g_ffn, w_out, w_r, b_r, tri)


def _row_slice(row):
    return pl.ds(pl.multiple_of(row * ROW_TILE, ROW_TILE), ROW_TILE)


def _row_copy(src_ref, src_row, dst_ref, dst_row, sem):
    return pltpu.make_async_copy(src_ref.at[_row_slice(src_row)], dst_ref.at[_row_slice(dst_row)], sem)


def _bf16_bits(x):
    return lax.bitcast_convert_type(x.astype(BF16).astype(F32), U32)


def _rows_2d(ref, n_rows):
    hi, lo = [], []
    for cblk in range(ROW_TILE):
        w = ref[pl.ds(cblk, n_rows, stride=ROW_TILE), :]
        hi.append(lax.bitcast_convert_type(w & jnp.uint32(0xFFFF0000), F32).astype(BF16))
        lo.append(lax.bitcast_convert_type(w << 16, F32).astype(BF16))
    return jnp.concatenate(hi + lo, axis=1)


def _store_rows(ref, val):
    for cblk in range(ROW_TILE):
        hi = _bf16_bits(val[:, cblk * 128:(cblk + 1) * 128])
        lo = _bf16_bits(val[:, (cblk + ROW_TILE) * 128:(cblk + ROW_TILE + 1) * 128])
        ref[pl.ds(cblk, val.shape[0], stride=ROW_TILE), :] = hi | (lo >> 16)


def _dispatch_kernel(dest_ref, h_ref, zeros_ref, xs_ref, sem):
    del zeros_ref
    n = 2 * TD_DISPATCH

    def issue(r, c):
        _row_copy(h_ref, r, xs_ref, dest_ref[0, 0, 2 * r], sem).start(priority=0)
        _row_copy(h_ref, r, xs_ref, dest_ref[0, 0, 2 * r + 1], sem).start(priority=1)
        return c

    lax.fori_loop(0, TD_DISPATCH, issue, 0, unroll=DMA_UNROLL // 2)

    def drain(a, c):
        _row_copy(h_ref, 0, xs_ref, 0, sem).wait()
        return c

    lax.fori_loop(0, n, drain, 0, unroll=DMA_UNROLL)


def _dispatch(dest3, h3, xs_zeros):
    t = h3.shape[0] // ROW_TILE
    return pl.pallas_call(
        _dispatch_kernel,
        grid=(t // TD_DISPATCH,),
        in_specs=[pl.BlockSpec((1, 1, 2 * TD_DISPATCH), lambda i: (i, 0, 0),
                               memory_space=pltpu.SMEM),
                  pl.BlockSpec((TD_DISPATCH * ROW_TILE, 128), lambda i: (i, 0)),
                  pl.BlockSpec(memory_space=pl.ANY)],
        out_specs=pl.BlockSpec(memory_space=pl.ANY),
        out_shape=jax.ShapeDtypeStruct(xs_zeros.shape, xs_zeros.dtype),
        scratch_shapes=[pltpu.SemaphoreType.DMA(())],
        input_output_aliases={2: 0},
        compiler_params=_cparams("arbitrary"),
        name="dispatch",
    )(dest3, h3, xs_zeros)


def _experts_kernel(be_ref, nu_ref, nxt_ref, xs_ref, wg_hbm, wu_hbm, wd_hbm, ys_ref,
                    wgb, wub, wdb, sg, su, sd, sem):
    j = pl.program_id(0)
    used = j < nu_ref[0]
    e = be_ref[j]
    new_expert = (j == 0) | (e != be_ref[jnp.maximum(j - 1, 0)])
    e_next = nxt_ref[e]

    def weight_copies(idx):
        return (pltpu.make_async_copy(wg_hbm.at[idx], sg, sem.at[0]),
                pltpu.make_async_copy(wu_hbm.at[idx], su, sem.at[1]),
                pltpu.make_async_copy(wd_hbm.at[idx], sd, sem.at[2]))

    @pl.when(j == 0)
    def _():
        for cp in weight_copies(e):
            cp.start(priority=1)

    @pl.when(used & new_expert)
    def _():
        for cp in weight_copies(e):
            cp.wait()
        wgb[...] = sg[...].astype(BF16)
        wub[...] = su[...].astype(BF16)
        wdb[...] = sd[...].astype(BF16)

    @pl.when(used & new_expert & (e_next != e))
    def _():
        for cp in weight_copies(e_next):
            cp.start(priority=1)

    @pl.when(used)
    def _():
        xb = _rows_2d(xs_ref, MOE_BLK)
        gate = jnp.dot(xb, wgb[...], preferred_element_type=F32)
        up = jnp.dot(xb, wub[...], preferred_element_type=F32)
        hid = (gate * jax.nn.sigmoid(gate) * up).astype(BF16)
        _store_rows(ys_ref, jnp.dot(hid, wdb[...], preferred_element_type=F32))

    @pl.when(pl.program_id(0) >= nu_ref[0])
    def _():
        ys_ref[...] = jnp.zeros_like(ys_ref)


def _experts(blk_expert, n_used, next_expert, xs, wg, wu, wd):
    rows = xs.shape[0] // ROW_TILE
    d = wg.shape[1]
    nb = rows // MOE_BLK
    row_blk = lambda j, be, nu, nx: (jnp.minimum(j, nu[0] - 1), 0)
    out_blk = lambda j, be, nu, nx: (j, 0)
    return pl.pallas_call(
        _experts_kernel,
        grid_spec=pltpu.PrefetchScalarGridSpec(
            num_scalar_prefetch=3,
            grid=(nb,),
            in_specs=[pl.BlockSpec((MOE_BLK * ROW_TILE, 128), row_blk),
                      pl.BlockSpec(memory_space=pl.ANY),
                      pl.BlockSpec(memory_space=pl.ANY),
                      pl.BlockSpec(memory_space=pl.ANY)],
            out_specs=pl.BlockSpec((MOE_BLK * ROW_TILE, 128), out_blk),
            scratch_shapes=[pltpu.VMEM((d, D_EXPERT), BF16), pltpu.VMEM((d, D_EXPERT), BF16),
                            pltpu.VMEM((D_EXPERT, d), BF16),
                            pltpu.VMEM((d, D_EXPERT), F32), pltpu.VMEM((d, D_EXPERT), F32),
                            pltpu.VMEM((D_EXPERT, d), F32),
                            pltpu.SemaphoreType.DMA((3,))]),
        out_shape=jax.ShapeDtypeStruct(xs.shape, xs.dtype),
        compiler_params=_cparams("arbitrary"),
        name="experts",
    )(blk_expert, n_used, next_expert, xs, wg, wu, wd)


def _combine_kernel(dest_ref, dest_next_ref, ys_ref, x1_ref, gate_ref, mod_ref, g_ref, o_ref,
                    ya, yb, sem):
    step = pl.program_id(0) * pl.num_programs(1) + pl.program_id(1)
    n_steps = pl.num_programs(0) * pl.num_programs(1)
    slot = step % 2

    def start_gathers(idx_ref, to_slot):
        def issue(r, c):
            _row_copy(ys_ref, idx_ref[0, 0, 2 * r], ya.at[to_slot], r,
                      sem.at[to_slot]).start(priority=0)
            _row_copy(ys_ref, idx_ref[0, 0, 2 * r + 1], yb.at[to_slot], r,
                      sem.at[to_slot]).start(priority=1)
            return c

        lax.fori_loop(0, TD, issue, 0, unroll=DMA_UNROLL)

    @pl.when(step == 0)
    def _():
        start_gathers(dest_ref, 0)

    @pl.when(step + 1 < n_steps)
    def _():
        start_gathers(dest_next_ref, 1 - slot)

    def drain(r, c):
        _row_copy(ys_ref, 0, ya.at[slot], 0, sem.at[slot]).wait()
        _row_copy(ys_ref, 0, yb.at[slot], 0, sem.at[slot]).wait()
        return c

    lax.fori_loop(0, TD, drain, 0, unroll=DMA_UNROLL)
    gt = gate_ref[...]
    moe = (gt[:, 0:1] * _rows_2d(ya.at[slot], TD).astype(F32)
           + gt[:, 1:2] * _rows_2d(yb.at[slot], TD).astype(F32))
    x2 = x1_ref[0] + mod_ref[0][5:6] * moe
    ms = jnp.mean(x2 * x2, axis=-1, keepdims=True)
    o_ref[0] = x2 * lax.rsqrt(ms + EPS) * g_ref[...]


def _combine(dest3, ys, x1, gates, mods, g_final):
    b, s, d = x1.shape
    nt = s // TD
    return pl.pallas_call(
        _combine_kernel,
        grid=(b, nt),
        in_specs=[pl.BlockSpec((1, 1, 2 * TD), lambda bi, i: (bi * nt + i, 0, 0),
                               memory_space=pltpu.SMEM),
                  pl.BlockSpec((1, 1, 2 * TD),
                               lambda bi, i: (jnp.minimum(bi * nt + i + 1, b * nt - 1), 0, 0),
                               memory_space=pltpu.SMEM),
                  pl.BlockSpec(memory_space=pl.ANY),
                  pl.BlockSpec((1, TD, d), lambda bi, i: (bi, i, 0)),
                  pl.BlockSpec((TD, 128), lambda bi, i: (bi * nt + i, 0)),
                  pl.BlockSpec((1, N_MOD, d), lambda bi, i: (bi, 0, 0)),
                  pl.BlockSpec((1, d), lambda bi, i: (0, 0))],
        out_specs=pl.BlockSpec((1, TD, d), lambda bi, i: (bi, i, 0)),
        out_shape=jax.ShapeDtypeStruct((b, s, d), F32),
        scratch_shapes=[pltpu.VMEM((2, TD * ROW_TILE, 128), U32),
                        pltpu.VMEM((2, TD * ROW_TILE, 128), U32),
                        pltpu.SemaphoreType.DMA((2,))],
        compiler_params=_cparams("arbitrary", "arbitrary"),
        name="combine",
    )(dest3, dest3, ys, x1, gates, mods, g_final)


@functools.lru_cache(maxsize=None)
def _rope_tables(rows):
    r, col = np.meshgrid(np.arange(rows), np.arange(GRID_W), indexing='ij')
    pos = np.stack([r.reshape(-1), col.reshape(-1)], axis=-1).astype(np.float32)
    inv_freq = (np.float32(ROPE_THETA)
                ** (-np.arange(0, ROPE_AXIS, 2, dtype=np.float32) / np.float32(ROPE_AXIS))).astype(np.float32)
    ang = (pos[:, :, None] * inv_freq).astype(np.float32)
    ang = np.concatenate([ang, ang], axis=-1).astype(np.float64)
    n = ang.shape[0]
    cos = np.tile(np.cos(ang).reshape(n, HEAD_DIM), (1, 2)).astype(np.float32)
    sin = np.tile(np.sin(ang).reshape(n, HEAD_DIM), (1, 2)).astype(np.float32)
    upper = (np.arange(HEAD_W) % ROPE_AXIS) >= ROPE_HALF
    sa = np.where(upper, sin, np.float32(0.0))
    sb = np.where(upper, np.float32(0.0), -sin)
    return cos, sa, sb


@functools.lru_cache(maxsize=None)
def _dft_constants(n_pos):
    c = np.arange(FGROUP_DIM)
    ang_c = 2.0 * np.pi * ((c[:, None] * c[None, :]) % FGROUP_DIM) / FGROUP_DIM
    norm = 1.0 / math.sqrt(n_pos * FGROUP_DIM)
    cmat = (np.cos(ang_c) * norm).astype(np.float32)
    smat = (np.sin(ang_c) * norm).astype(np.float32)
    k1 = np.arange(DFT_N1)[None, :, None]
    n1 = np.arange(DFT_N1)[None, None, :]
    n2 = np.arange(DFT_N2)[:, None, None]
    ang_g = 2.0 * np.pi * ((k1 * (DFT_N2 * n1 + n2)) % n_pos) / n_pos
    gsmall = np.stack([np.cos(ang_g), np.sin(ang_g)], axis=1)
    gsmall = gsmall.reshape(DFT_N2 // F1_NB, F1_NB, 2, DFT_N1, DFT_N1)
    gmat = np.einsum('japkn,ab->jpkanb', gsmall, np.eye(F1_NB)).reshape(
        DFT_N2 // F1_NB, 2 * DFT_N1 * F1_NB, DFT_N1 * F1_NB).astype(BF16)
    k2 = np.arange(DFT_N2)
    ang_2 = 2.0 * np.pi * ((k2[:, None] * k2[None, :]) % DFT_N2) / DFT_N2
    c2 = np.cos(ang_2).astype(np.float32)
    s2 = np.sin(ang_2).astype(np.float32)
    return cmat, smat, gmat, c2, s2


def kernel(x, c, ctx, c_ctx, w_ada, b_ada, g_mix_norm, g_ffn_norm, w_in, lambda_q1, lambda_k1, lambda_q2, lambda_k2, g_subln, w_fourier, w_out, w_router_group, b_router_group, w_router_expert, b_router_expert, w_gate, w_up, w_down, g_final):
    b, s, d = x.shape
    t = b * s
    assert d == D_MODEL and s == DFT_N1 * DFT_N2 and s % GRID_W == 0 and b == 2

    cc = jnp.concatenate([c, c_ctx[None, :], jnp.zeros((8 - b - 1, d), F32)], axis=0)
    mods = _adaln(cc, w_ada[0], b_ada[0]).reshape(8, N_MOD, d)

    cmat, smat, gmat, c2, s2 = _dft_constants(s)
    wf = _wfold(jnp.asarray(cmat), jnp.asarray(smat), w_fourier[0])
    cos_t, sa_t, sb_t = _rope_tables(s // GRID_W)

    w_in_b = w_in[0].astype(BF16)
    g_mix = g_mix_norm[0].reshape(1, d)
    q, kl, vl, y = _inproj(x, mods, g_mix, w_in_b, wf, cos_t, sa_t, sb_t)
    kc, vc = _ctxproj(ctx, mods, g_mix, w_in_b)

    lam_p = jnp.stack([lambda_q1[0], lambda_k1[0], lambda_q2[0], lambda_k2[0]], axis=0)
    attn_o = _attention(q, kc, vc, kl, vl, lam_p, g_subln[0].reshape(1, HEAD_W))

    br, bi = _dft1(y.reshape(b, DFT_N1, DFT_N2, 2 * F_W), jnp.asarray(gmat))
    four = _dft2(br, bi, jnp.asarray(c2).astype(BF16),
                 jnp.asarray(s2).astype(BF16)).reshape(b, s, F_W)

    n_r = N_GROUPS + N_EXPERTS
    w_r = jnp.concatenate([w_router_group[0], w_router_expert[0],
                           jnp.zeros((d, 128 - n_r), F32)], axis=1)
    b_r = jnp.concatenate([b_router_group[0], b_router_expert[0],
                           jnp.zeros((128 - n_r,), F32)]).reshape(1, 128)
    w_r_hi = w_r.astype(BF16)
    w_r_lo = (w_r - w_r_hi.astype(F32)).astype(BF16)
    x1, h3, meta, gates, cnt = _outproj(attn_o, four, x, mods, g_ffn_norm[0].reshape(1, d),
                                        w_out[0].astype(BF16),
                                        jnp.concatenate([w_r_hi, w_r_lo], axis=1), b_r)

    e_tk = jnp.stack([meta[0], meta[1]], axis=1)
    rank_tk = jnp.stack([meta[2], meta[3]], axis=1)
    counts = cnt[0, N_GROUPS:N_GROUPS + N_EXPERTS].astype(jnp.int32)
    nblk = (counts + MOE_BLK - 1) // MOE_BLK
    blk_end = jnp.cumsum(nblk)
    blk_start = blk_end - nblk
    onehot = (e_tk[:, :, None] == jnp.arange(N_EXPERTS, dtype=jnp.int32)).astype(F32)
    first_blk = jnp.einsum('tke,e->tk', onehot, blk_start.astype(F32),
                           precision=lax.Precision.HIGHEST).astype(jnp.int32)
    dest = first_blk * MOE_BLK + rank_tk
    n_blocks = t * TOP_K // MOE_BLK + N_EXPERTS
    blk_ids = jnp.arange(n_blocks, dtype=jnp.int32)
    blk_expert = jnp.minimum(
        jnp.sum((blk_end[None, :] <= blk_ids[:, None]).astype(jnp.int32), axis=1),
        N_EXPERTS - 1).astype(jnp.int32)
    n_used = blk_end[-1:].astype(jnp.int32)
    dest3 = dest.reshape(t // TD, 1, 2 * TD)

    xs = _dispatch(dest.reshape(t // TD_DISPATCH, 1, 2 * TD_DISPATCH), h3,
                   jnp.zeros((n_blocks * MOE_BLK * ROW_TILE, 128), U32))
    e_ids = jnp.arange(N_EXPERTS, dtype=jnp.int32)
    later = (e_ids[None, :] > e_ids[:, None]) & (nblk[None, :] > 0)
    next_expert = jnp.min(jnp.where(later, e_ids[None, :], N_EXPERTS), axis=1)
    next_expert = jnp.where(next_expert == N_EXPERTS, e_ids, next_expert).astype(jnp.int32)
    ys = _experts(blk_expert, n_used, next_expert, xs, w_gate[0], w_up[0], w_down[0])
    return _combine(dest3, ys, x1, gates, mods, g_final.reshape(1, d))
```

```python
import functools
import math

import numpy as np
import jax
import jax.numpy as jnp
from jax import lax
from jax.experimental import pallas as pl
from jax.experimental.pallas import tpu as pltpu

F32 = jnp.float32
BF16 = jnp.bfloat16

D_MODEL = 1024
GRID_W = 64
N_HEADS = 4
HEAD_DIM = 64
HEAD_W = 2 * HEAD_DIM
QK_W = N_HEADS * HEAD_W
N_FGROUPS = 4
FGROUP_DIM = 128
F_W = N_FGROUPS * FGROUP_DIM
ROPE_THETA = 10000.0
ROPE_AXIS = HEAD_DIM // 2
ROPE_HALF = ROPE_AXIS // 2
N_GROUPS = 4
EXPERTS_PER_GROUP = 8
N_EXPERTS = N_GROUPS * EXPERTS_PER_GROUP
TOP_K = 2
D_EXPERT = 512
N_MOD = 6
EPS = 1e-6
LAMBDA_INIT = 0.8 - 0.6 * math.exp(-0.3 * 0)
LOG2_E = 1.4426950408889634

DFT_N1 = 64
DFT_N2 = 128

TM_PROJ = 512
TQ = 1024
TK = 2048
F1_NB = 8
F2_KB = 8
MOE_BLK = 512
ROW_TILE = D_MODEL // 256
U32 = jnp.uint32
TD = 512
TD_DISPATCH = 1024
DMA_UNROLL = 16
VMEM_LIMIT = 48 * 1024 * 1024


def _cparams(*sem):
    return pltpu.CompilerParams(dimension_semantics=sem, vmem_limit_bytes=VMEM_LIMIT)


def _adaln_kernel(c_ref, w_ref, b_ref, o_ref):
    cc = c_ref[...]
    s = cc * jax.nn.sigmoid(cc)
    o_ref[...] = jnp.dot(s, w_ref[...], preferred_element_type=F32,
                         precision=lax.Precision.HIGHEST) + b_ref[...]


def _adaln(cc, w_ada, b_ada):
    n = w_ada.shape[1]
    tn = 1536
    return pl.pallas_call(
        _adaln_kernel,
        grid=(n // tn,),
        in_specs=[pl.BlockSpec((8, D_MODEL), lambda j: (0, 0)),
                  pl.BlockSpec((D_MODEL, tn), lambda j: (0, j)),
                  pl.BlockSpec((1, tn), lambda j: (0, j))],
        out_specs=pl.BlockSpec((8, tn), lambda j: (0, j)),
        out_shape=jax.ShapeDtypeStruct((8, n), F32),
        compiler_params=_cparams("arbitrary"),
        name="adaln",
    )(cc, w_ada, b_ada.reshape(1, n))


def _wfold_kernel(c_ref, s_ref, w_ref, o_ref):
    w = w_ref[0]
    o_ref[0, :, :FGROUP_DIM] = jnp.dot(c_ref[...], w, preferred_element_type=F32,
                                       precision=lax.Precision.HIGHEST).astype(BF16)
    o_ref[0, :, FGROUP_DIM:] = jnp.dot(s_ref[...], w, preferred_element_type=F32,
                                       precision=lax.Precision.HIGHEST).astype(BF16)


def _wfold(cmat, smat, w_fourier):
    return pl.pallas_call(
        _wfold_kernel,
        grid=(N_FGROUPS,),
        in_specs=[pl.BlockSpec((FGROUP_DIM, FGROUP_DIM), lambda g: (0, 0)),
                  pl.BlockSpec((FGROUP_DIM, FGROUP_DIM), lambda g: (0, 0)),
                  pl.BlockSpec((1, FGROUP_DIM, FGROUP_DIM), lambda g: (g, 0, 0))],
        out_specs=pl.BlockSpec((1, FGROUP_DIM, 2 * FGROUP_DIM), lambda g: (g, 0, 0)),
        out_shape=jax.ShapeDtypeStruct((N_FGROUPS, FGROUP_DIM, 2 * FGROUP_DIM), BF16),
        compiler_params=_cparams("arbitrary"),
        name="wfold",
    )(cmat, smat, w_fourier)


def _norm_mod(x, g, shift, scale):
    ms = jnp.mean(x * x, axis=-1, keepdims=True)
    y = x * lax.rsqrt(ms + EPS) * g
    return y * (1.0 + scale) + shift


def _rope_slab(p, cos, sa, sb):
    return (p * cos + pltpu.roll(p, ROPE_HALF, 1) * sa
            + pltpu.roll(p, HEAD_W - ROPE_HALF, 1) * sb)


def _inproj_kernel(x_ref, mod_ref, g_ref, w_ref, wf_ref, cos_ref, sa_ref, sb_ref,
                   q_ref, k_ref, v_ref, y_ref):
    m = mod_ref[0]
    h = _norm_mod(x_ref[0], g_ref[...], m[0:1], m[1:2]).astype(BF16)
    cos, sa, sb = cos_ref[...], sa_ref[...], sb_ref[...]
    scale = HEAD_DIM ** -0.5 * LOG2_E
    pq = jnp.dot(h, w_ref[:, 0:QK_W], preferred_element_type=F32)
    for hh in range(N_HEADS):
        sl = slice(hh * HEAD_W, (hh + 1) * HEAD_W)
        q_ref[0, :, sl] = (_rope_slab(pq[:, sl], cos, sa, sb) * scale).astype(BF16)
    pk = jnp.dot(h, w_ref[:, QK_W:2 * QK_W], preferred_element_type=F32)
    for hh in range(N_HEADS):
        sl = slice(hh * HEAD_W, (hh + 1) * HEAD_W)
        k_ref[0, :, sl] = _rope_slab(pk[:, sl], cos, sa, sb).astype(BF16)
    v_ref[0] = jnp.dot(h, w_ref[:, 2 * QK_W:3 * QK_W], preferred_element_type=F32).astype(BF16)
    pf = jnp.dot(h, w_ref[:, 3 * QK_W:], preferred_element_type=F32).astype(BF16)
    for g in range(N_FGROUPS):
        yy = jnp.dot(pf[:, g * FGROUP_DIM:(g + 1) * FGROUP_DIM], wf_ref[g],
                     preferred_element_type=F32)
        y_ref[0, :, g * FGROUP_DIM:(g + 1) * FGROUP_DIM] = yy[:, :FGROUP_DIM]
        y_ref[0, :, F_W + g * FGROUP_DIM:F_W + (g + 1) * FGROUP_DIM] = yy[:, FGROUP_DIM:]


def _inproj(x, mods, g_mix, w_in, wf, cos_t, sa_t, sb_t):
    b, s, d = x.shape
    tm = TM_PROJ
    tok = lambda bi, i: (bi, i, 0)
    return pl.pallas_call(
        _inproj_kernel,
        grid=(b, s // tm),
        in_specs=[pl.BlockSpec((1, tm, d), tok),
                  pl.BlockSpec((1, N_MOD, d), lambda bi, i: (bi, 0, 0)),
                  pl.BlockSpec((1, d), lambda bi, i: (0, 0)),
                  pl.BlockSpec(w_in.shape, lambda bi, i: (0, 0)),
                  pl.BlockSpec(wf.shape, lambda bi, i: (0, 0, 0)),
                  pl.BlockSpec((tm, HEAD_W), lambda bi, i: (i, 0)),
                  pl.BlockSpec((tm, HEAD_W), lambda bi, i: (i, 0)),
                  pl.BlockSpec((tm, HEAD_W), lambda bi, i: (i, 0))],
        out_specs=[pl.BlockSpec((1, tm, QK_W), tok),
                   pl.BlockSpec((1, tm, QK_W), tok),
                   pl.BlockSpec((1, tm, QK_W), tok),
                   pl.BlockSpec((1, tm, 2 * F_W), tok)],
        out_shape=[jax.ShapeDtypeStruct((b, s, QK_W), BF16),
                   jax.ShapeDtypeStruct((b, s, QK_W), BF16),
                   jax.ShapeDtypeStruct((b, s, QK_W), BF16),
                   jax.ShapeDtypeStruct((b, s, 2 * F_W), F32)],
        compiler_params=_cparams("parallel", "arbitrary"),
        name="inproj",
    )(x, mods, g_mix, w_in, wf, cos_t, sa_t, sb_t)


def _ctxproj_kernel(x_ref, mod_ref, g_ref, w_ref, k_ref, v_ref):
    m = mod_ref[0]
    h = _norm_mod(x_ref[0], g_ref[...], m[0:1], m[1:2]).astype(BF16)
    k_ref[0] = jnp.dot(h, w_ref[:, QK_W:2 * QK_W], preferred_element_type=F32).astype(BF16)
    v_ref[0] = jnp.dot(h, w_ref[:, 2 * QK_W:3 * QK_W], preferred_element_type=F32).astype(BF16)


def _ctxproj(ctx, mods, g_mix, w_in):
    b, n, d = ctx.shape
    return pl.pallas_call(
        _ctxproj_kernel,
        grid=(b,),
        in_specs=[pl.BlockSpec((1, n, d), lambda bi: (bi, 0, 0)),
                  pl.BlockSpec((1, N_MOD, d), lambda bi: (2, 0, 0)),
                  pl.BlockSpec((1, d), lambda bi: (0, 0)),
                  pl.BlockSpec(w_in.shape, lambda bi: (0, 0))],
        out_specs=[pl.BlockSpec((1, n, QK_W), lambda bi: (bi, 0, 0)),
                   pl.BlockSpec((1, n, QK_W), lambda bi: (bi, 0, 0))],
        out_shape=[jax.ShapeDtypeStruct((b, n, QK_W), BF16),
                   jax.ShapeDtypeStruct((b, n, QK_W), BF16)],
        compiler_params=_cparams("arbitrary"),
        name="ctxproj",
    )(ctx, mods, g_mix, w_in)


def _attn_kernel(q_ref, kc_ref, vc_ref, kl_ref, vl_ref, lam_ref, g_ref, o_ref,
                 m_ref, l_ref, acc_ref):
    q = q_ref[0]
    lane = lax.broadcasted_iota(jnp.int32, q.shape, 1)
    zero = jnp.zeros_like(q)
    q0 = jnp.where(lane < HEAD_DIM, q, zero)
    q1 = jnp.where(lane >= HEAD_DIM, q, zero)
    qs = (q0, q1)
    contract_last = (((1,), (1,)), ((), ()))
    m_ref[...] = jnp.full(m_ref.shape, -1e30, F32)
    l_ref[...] = jnp.zeros(l_ref.shape, F32)
    acc_ref[...] = jnp.zeros(acc_ref.shape, F32)

    def step(kb, vb):
        nk = kb.shape[0] // HEAD_W
        for mi in range(2):
            s = lax.dot_general(qs[mi], kb, contract_last, preferred_element_type=F32)
            m_old = m_ref[mi]
            m_new = jnp.maximum(m_old, jnp.max(s, axis=-1, keepdims=True))
            alpha = jnp.exp2(m_old - m_new)
            p = jnp.exp2(s - jnp.concatenate([m_new] * nk, axis=1))
            psum = p[:, 0:HEAD_W]
            for cblk in range(1, nk):
                psum = psum + p[:, cblk * HEAD_W:(cblk + 1) * HEAD_W]
            l_ref[mi] = alpha * l_ref[mi] + psum
            acc_ref[mi] = alpha * acc_ref[mi] + jnp.dot(p.astype(BF16), vb,
                                                       preferred_element_type=F32)
            m_ref[mi] = m_new

    step(kc_ref[0], vc_ref[0])

    def body(i, c):
        off = pl.multiple_of(i * TK, TK)
        step(kl_ref[0, pl.ds(off, TK), :], vl_ref[0, pl.ds(off, TK), :])
        return c

    lax.fori_loop(0, kl_ref.shape[1] // TK, body, 0)

    lp = lam_ref[...]
    t1 = jnp.sum(lp[0:1] * lp[1:2], axis=-1, keepdims=True)
    t2 = jnp.sum(lp[2:3] * lp[3:4], axis=-1, keepdims=True)
    lam = jnp.exp(t1) - jnp.exp(t2) + LAMBDA_INIT
    l0 = jnp.sum(l_ref[0], axis=-1, keepdims=True)
    l1 = jnp.sum(l_ref[1], axis=-1, keepdims=True)
    o = acc_ref[0] / l0 - lam * (acc_ref[1] / l1)
    ms = jnp.mean(o * o, axis=-1, keepdims=True)
    o = o * lax.rsqrt(ms + EPS) * g_ref[...] * (1.0 - LAMBDA_INIT)
    o_ref[0] = o.astype(BF16)


def _attention(q, kc, vc, kl, vl, lam_p, g_subln):
    b, s, _ = q.shape
    n_ctx = kc.shape[1]
    return pl.pallas_call(
        _attn_kernel,
        grid=(b, N_HEADS, s // TQ),
        in_specs=[pl.BlockSpec((1, TQ, HEAD_W), lambda bi, h, i: (bi, i, h)),
                  pl.BlockSpec((1, n_ctx, HEAD_W), lambda bi, h, i: (bi, 0, h)),
                  pl.BlockSpec((1, n_ctx, HEAD_W), lambda bi, h, i: (bi, 0, h)),
                  pl.BlockSpec((1, s, HEAD_W), lambda bi, h, i: (bi, 0, h)),
                  pl.BlockSpec((1, s, HEAD_W), lambda bi, h, i: (bi, 0, h)),
                  pl.BlockSpec((4, HEAD_DIM), lambda bi, h, i: (0, 0)),
                  pl.BlockSpec((1, HEAD_W), lambda bi, h, i: (0, 0))],
        out_specs=pl.BlockSpec((1, TQ, HEAD_W), lambda bi, h, i: (bi, i, h)),
        out_shape=jax.ShapeDtypeStruct((b, s, QK_W), BF16),
        scratch_shapes=[pltpu.VMEM((2, TQ, HEAD_W), F32)] * 3,
        compiler_params=_cparams("parallel", "parallel", "arbitrary"),
        name="diffattn",
    )(q, kc, vc, kl, vl, lam_p, g_subln)


def _dft1_kernel(y_ref, g_ref, br_ref, bi_ref):
    rows = DFT_N1 * F1_NB
    yb = y_ref[0].reshape(rows, 2 * F_W).astype(BF16)
    p = jnp.dot(g_ref[0], yb, preferred_element_type=F32)
    top, bot = p[:rows], p[rows:]
    br_ref[0] = (top[:, :F_W] - bot[:, F_W:]).reshape(DFT_N1, F1_NB, F_W)
    bi_ref[0] = (-top[:, F_W:] - bot[:, :F_W]).reshape(DFT_N1, F1_NB, F_W)


def _dft1(y4, gmat):
    b = y4.shape[0]
    rows = DFT_N1 * F1_NB
    return pl.pallas_call(
        _dft1_kernel,
        grid=(DFT_N2 // F1_NB, b),
        in_specs=[pl.BlockSpec((1, DFT_N1, F1_NB, 2 * F_W), lambda j, bi: (bi, 0, j, 0)),
                  pl.BlockSpec((1, 2 * rows, rows), lambda j, bi: (j, 0, 0))],
        out_specs=[pl.BlockSpec((1, DFT_N1, F1_NB, F_W), lambda j, bi: (bi, 0, j, 0)),
                   pl.BlockSpec((1, DFT_N1, F1_NB, F_W), lambda j, bi: (bi, 0, j, 0))],
        out_shape=[jax.ShapeDtypeStruct((b, DFT_N1, DFT_N2, F_W), F32),
                   jax.ShapeDtypeStruct((b, DFT_N1, DFT_N2, F_W), F32)],
        compiler_params=_cparams("arbitrary", "arbitrary"),
        name="dft1",
    )(y4, gmat)


def _dft2_kernel(br_ref, bi_ref, c_ref, s_ref, o_ref):
    for j in range(F2_KB):
        r = (jnp.dot(c_ref[...], br_ref[0, j].astype(BF16), preferred_element_type=F32)
             + jnp.dot(s_ref[...], bi_ref[0, j].astype(BF16), preferred_element_type=F32))
        o_ref[0, :, j, :] = r


def _dft2(br4, bi4, c2, s2):
    b = br4.shape[0]
    blk = (1, F2_KB, DFT_N2, F_W)
    return pl.pallas_call(
        _dft2_kernel,
        grid=(b, DFT_N1 // F2_KB),
        in_specs=[pl.BlockSpec(blk, lambda bi, j: (bi, j, 0, 0)),
                  pl.BlockSpec(blk, lambda bi, j: (bi, j, 0, 0)),
                  pl.BlockSpec((DFT_N2, DFT_N2), lambda bi, j: (0, 0)),
                  pl.BlockSpec((DFT_N2, DFT_N2), lambda bi, j: (0, 0))],
        out_specs=pl.BlockSpec((1, DFT_N2, F2_KB, F_W), lambda bi, j: (bi, 0, j, 0)),
        out_shape=jax.ShapeDtypeStruct((b, DFT_N2, DFT_N1, F_W), F32),
        compiler_params=_cparams("parallel", "arbitrary"),
        name="dft2",
    )(br4, bi4, c2, s2)


def _outproj_kernel(o_ref, f_ref, x_ref, mod_ref, g_ref, wo_ref, wr_ref, br_ref, tri_ref,
                    x1_ref, h3_ref, meta_ref, gate_ref, cnt_out_ref, cnt_ref):
    first = (pl.program_id(0) == 0) & (pl.program_id(1) == 0)

    @pl.when(first)
    def _():
        cnt_ref[...] = jnp.zeros(cnt_ref.shape, F32)

    m = mod_ref[0]
    mix = (jnp.dot(o_ref[0], wo_ref[0:QK_W, :], preferred_element_type=F32)
           + jnp.dot(f_ref[0].astype(BF16), wo_ref[QK_W:, :], preferred_element_type=F32))
    x1 = x_ref[0] + m[2:3] * mix
    x1_ref[0] = x1
    h2 = _norm_mod(x1, g_ref[...], m[3:4], m[4:5])
    _store_rows(h3_ref, h2)
    h_hi = h2.astype(BF16)
    h_lo = (h2 - h_hi.astype(F32)).astype(BF16)
    hw = jnp.dot(h_hi, wr_ref[...], preferred_element_type=F32)
    lg = (hw[:, :128] + hw[:, 128:]
          + jnp.dot(h_lo, wr_ref[:, 0:128], preferred_element_type=F32)
          + br_ref[...])
    lane = lax.broadcasted_iota(jnp.int32, lg.shape, 1)
    lanef = lane.astype(F32)
    ninf = jnp.float32(-jnp.inf)
    big = jnp.float32(lg.shape[1])
    gl = jnp.where(lane < N_GROUPS, lg, ninf)
    gmax = jnp.max(gl, axis=-1, keepdims=True)
    grp = jnp.min(jnp.where(gl == gmax, lanef, big), axis=-1, keepdims=True)
    pg = 1.0 / jnp.sum(jnp.exp(gl - gmax), axis=-1, keepdims=True)
    e_lane = lane - N_GROUPS
    lane_grp = (e_lane >> 3).astype(F32)
    emask = (e_lane >= 0) & (e_lane < N_EXPERTS) & (lane_grp == grp)
    el = jnp.where(emask, lg, ninf)
    t1 = jnp.max(el, axis=-1, keepdims=True)
    i1 = jnp.min(jnp.where(el == t1, lanef, big), axis=-1, keepdims=True)
    el2 = jnp.where(lanef == i1, ninf, el)
    t2 = jnp.max(el2, axis=-1, keepdims=True)
    i2 = jnp.min(jnp.where(el2 == t2, lanef, big), axis=-1, keepdims=True)
    dd = jnp.exp(t2 - t1)
    w1 = pg / (1.0 + dd)
    w2 = pg * dd / (1.0 + dd)
    gate_ref[...] = jnp.where(lane == 0, w1, jnp.where(lane == 1, w2, 0.0))
    hit1 = lanef == i1
    hit2 = lanef == i2
    oh = jnp.where(hit1 | hit2, 1.0, 0.0)
    before = jnp.dot(tri_ref[...], oh.astype(BF16), preferred_element_type=F32) + cnt_ref[0:1, :]
    r1 = jnp.sum(jnp.where(hit1, before, 0.0), axis=-1, keepdims=True)
    r2 = jnp.sum(jnp.where(hit2, before, 0.0), axis=-1, keepdims=True)
    cnt_ref[0:1, :] = cnt_ref[0:1, :] + jnp.sum(oh, axis=0, keepdims=True)
    cnt_out_ref[...] = cnt_ref[...]
    meta = jnp.where(lane == 0, i1 - N_GROUPS,
                     jnp.where(lane == 1, i2 - N_GROUPS,
                               jnp.where(lane == 2, r1, jnp.where(lane == 3, r2, 0.0))))
    meta_ref[...] = jnp.transpose(meta)[0:8, :].astype(jnp.int32)


def _outproj(attn_o, four, x, mods, g_ffn, w_out, w_r, b_r):
    b, s, d = x.shape
    tm = TM_PROJ
    nt = s // tm
    tok = lambda bi, i: (bi, i, 0)
    flat = lambda bi, i: (bi * nt + i, 0)
    tri = jnp.asarray(np.tril(np.ones((tm, tm), np.float32), -1)).astype(BF16)
    return pl.pallas_call(
        _outproj_kernel,
        grid=(b, nt),
        in_specs=[pl.BlockSpec((1, tm, QK_W), tok),
                  pl.BlockSpec((1, tm, F_W), tok),
                  pl.BlockSpec((1, tm, d), tok),
                  pl.BlockSpec((1, N_MOD, d), lambda bi, i: (bi, 0, 0)),
                  pl.BlockSpec((1, d), lambda bi, i: (0, 0)),
                  pl.BlockSpec(w_out.shape, lambda bi, i: (0, 0)),
                  pl.BlockSpec(w_r.shape, lambda bi, i: (0, 0)),
                  pl.BlockSpec(b_r.shape, lambda bi, i: (0, 0)),
                  pl.BlockSpec((tm, tm), lambda bi, i: (0, 0))],
        out_specs=[pl.BlockSpec((1, tm, d), tok),
                   pl.BlockSpec((tm * ROW_TILE, 128), flat),
                   pl.BlockSpec((8, tm), lambda bi, i: (0, bi * nt + i)),
                   pl.BlockSpec((tm, 128), flat),
                   pl.BlockSpec((8, 128), lambda bi, i: (0, 0))],
        out_shape=[jax.ShapeDtypeStruct((b, s, d), F32),
                   jax.ShapeDtypeStruct((b * s * ROW_TILE, 128), U32),
                   jax.ShapeDtypeStruct((8, b * s), jnp.int32),
                   jax.ShapeDtypeStruct((b * s, 128), F32),
                   jax.ShapeDtypeStruct((8, 128), F32)],
        scratch_shapes=[pltpu.VMEM((8, 128), F32)],
        compiler_params=_cparams("arbitrary", "arbitrary"),
        name="outproj",
    )(attn_o, four, x, mods, g_ffn, w_out, w_r, b_r, tri)


def _destrows_kernel(start_ref, meta_ref, o_ref):
    meta = meta_ref[...]
    row = lax.broadcasted_iota(jnp.int32, meta.shape, 0)
    base = jnp.zeros_like(meta)
    for e in range(N_EXPERTS):
        base = jnp.where(meta == e, start_ref[e] * MOE_BLK, base)
    ranks = pltpu.roll(meta, meta.shape[0] - TOP_K, 0)
    o_ref[...] = jnp.where(row < TOP_K, base + ranks, 0)


def _destrows(blk_start, meta):
    return pl.pallas_call(
        _destrows_kernel,
        grid_spec=pltpu.PrefetchScalarGridSpec(
            num_scalar_prefetch=1,
            grid=(1,),
            in_specs=[pl.BlockSpec(meta.shape, lambda i, st: (0, 0))],
            out_specs=pl.BlockSpec(meta.shape, lambda i, st: (0, 0))),
        out_shape=jax.ShapeDtypeStruct(meta.shape, jnp.int32),
        compiler_params=_cparams("arbitrary"),
        name="destrows",
    )(blk_start, meta)


def _row_slice(row):
    return pl.ds(pl.multiple_of(row * ROW_TILE, ROW_TILE), ROW_TILE)


def _row_copy(src_ref, src_row, dst_ref, dst_row, sem):
    return pltpu.make_async_copy(src_ref.at[_row_slice(src_row)], dst_ref.at[_row_slice(dst_row)], sem)


def _bf16_bits(x):
    return lax.bitcast_convert_type(x.astype(BF16).astype(F32), U32)


def _rows_2d(ref, n_rows):
    hi, lo = [], []
    for cblk in range(ROW_TILE):
        w = ref[pl.ds(cblk, n_rows, stride=ROW_TILE), :]
        hi.append(lax.bitcast_convert_type(w & jnp.uint32(0xFFFF0000), F32).astype(BF16))
        lo.append(lax.bitcast_convert_type(w << 16, F32).astype(BF16))
    return jnp.concatenate(hi + lo, axis=1)


def _store_rows(ref, val):
    for cblk in range(ROW_TILE):
        hi = _bf16_bits(val[:, cblk * 128:(cblk + 1) * 128])
        lo = _bf16_bits(val[:, (cblk + ROW_TILE) * 128:(cblk + ROW_TILE + 1) * 128])
        ref[pl.ds(cblk, val.shape[0], stride=ROW_TILE), :] = hi | (lo >> 16)


def _dispatch_kernel(dest_ref, h_ref, zeros_ref, xs_ref, sem):
    del zeros_ref
    n = 2 * TD_DISPATCH

    def issue(r, c):
        _row_copy(h_ref, r, xs_ref, dest_ref[0, 0, 2 * r], sem).start(priority=0)
        _row_copy(h_ref, r, xs_ref, dest_ref[0, 0, 2 * r + 1], sem).start(priority=1)
        return c

    lax.fori_loop(0, TD_DISPATCH, issue, 0, unroll=DMA_UNROLL // 2)

    def drain(a, c):
        _row_copy(h_ref, 0, xs_ref, 0, sem).wait()
        return c

    lax.fori_loop(0, n, drain, 0, unroll=DMA_UNROLL)


def _dispatch(dest3, h3, xs_zeros):
    t = h3.shape[0] // ROW_TILE
    return pl.pallas_call(
        _dispatch_kernel,
        grid=(t // TD_DISPATCH,),
        in_specs=[pl.BlockSpec((1, 1, 2 * TD_DISPATCH), lambda i: (i, 0, 0),
                               memory_space=pltpu.SMEM),
                  pl.BlockSpec((TD_DISPATCH * ROW_TILE, 128), lambda i: (i, 0)),
                  pl.BlockSpec(memory_space=pl.ANY)],
        out_specs=pl.BlockSpec(memory_space=pl.ANY),
        out_shape=jax.ShapeDtypeStruct(xs_zeros.shape, xs_zeros.dtype),
        scratch_shapes=[pltpu.SemaphoreType.DMA(())],
        input_output_aliases={2: 0},
        compiler_params=_cparams("arbitrary"),
        name="dispatch",
    )(dest3, h3, xs_zeros)


def _experts_kernel(be_ref, nu_ref, nxt_ref, xs_ref, wg_hbm, wu_hbm, wd_hbm, ys_ref,
                    wgb, wub, wdb, sg, su, sd, sem):
    j = pl.program_id(0)
    used = j < nu_ref[0]
    e = be_ref[j]
    new_expert = (j == 0) | (e != be_ref[jnp.maximum(j - 1, 0)])
    e_next = nxt_ref[e]

    def weight_copies(idx):
        return (pltpu.make_async_copy(wg_hbm.at[idx], sg, sem.at[0]),
                pltpu.make_async_copy(wu_hbm.at[idx], su, sem.at[1]),
                pltpu.make_async_copy(wd_hbm.at[idx], sd, sem.at[2]))

    @pl.when(j == 0)
    def _():
        for cp in weight_copies(e):
            cp.start(priority=1)

    @pl.when(used & new_expert)
    def _():
        for cp in weight_copies(e):
            cp.wait()
        wgb[...] = sg[...].astype(BF16)
        wub[...] = su[...].astype(BF16)
        wdb[...] = sd[...].astype(BF16)

    @pl.when(used & new_expert & (e_next != e))
    def _():
        for cp in weight_copies(e_next):
            cp.start(priority=1)

    @pl.when(used)
    def _():
        xb = _rows_2d(xs_ref, MOE_BLK)
        gate = jnp.dot(xb, wgb[...], preferred_element_type=F32)
        up = jnp.dot(xb, wub[...], preferred_element_type=F32)
        hid = (gate * jax.nn.sigmoid(gate) * up).astype(BF16)
        _store_rows(ys_ref, jnp.dot(hid, wdb[...], preferred_element_type=F32))

    @pl.when(pl.program_id(0) >= nu_ref[0])
    def _():
        ys_ref[...] = jnp.zeros_like(ys_ref)


def _experts(blk_expert, n_used, next_expert, xs, wg, wu, wd):
    rows = xs.shape[0] // ROW_TILE
    d = wg.shape[1]
    nb = rows // MOE_BLK
    row_blk = lambda j, be, nu, nx: (jnp.minimum(j, nu[0] - 1), 0)
    out_blk = lambda j, be, nu, nx: (j, 0)
    return pl.pallas_call(
        _experts_kernel,
        grid_spec=pltpu.PrefetchScalarGridSpec(
            num_scalar_prefetch=3,
            grid=(nb,),
            in_specs=[pl.BlockSpec((MOE_BLK * ROW_TILE, 128), row_blk),
                      pl.BlockSpec(memory_space=pl.ANY),
                      pl.BlockSpec(memory_space=pl.ANY),
                      pl.BlockSpec(memory_space=pl.ANY)],
            out_specs=pl.BlockSpec((MOE_BLK * ROW_TILE, 128), out_blk),
            scratch_shapes=[pltpu.VMEM((d, D_EXPERT), BF16), pltpu.VMEM((d, D_EXPERT), BF16),
                            pltpu.VMEM((D_EXPERT, d), BF16),
                            pltpu.VMEM((d, D_EXPERT), F32), pltpu.VMEM((d, D_EXPERT), F32),
                            pltpu.VMEM((D_EXPERT, d), F32),
                            pltpu.SemaphoreType.DMA((3,))]),
        out_shape=jax.ShapeDtypeStruct(xs.shape, xs.dtype),
        compiler_params=_cparams("arbitrary"),
        name="experts",
    )(blk_expert, n_used, next_expert, xs, wg, wu, wd)


def _combine_kernel(dest_ref, dest_next_ref, ys_ref, x1_ref, gate_ref, mod_ref, g_ref, o_ref,
                    ya, yb, sem):
    step = pl.program_id(0) * pl.num_programs(1) + pl.program_id(1)
    n_steps = pl.num_programs(0) * pl.num_programs(1)
    slot = step % 2

    def start_gathers(idx_ref, to_slot):
        def issue(r, c):
            _row_copy(ys_ref, idx_ref[0, 0, 2 * r], ya.at[to_slot], r,
                      sem.at[to_slot]).start(priority=0)
            _row_copy(ys_ref, idx_ref[0, 0, 2 * r + 1], yb.at[to_slot], r,
                      sem.at[to_slot]).start(priority=1)
            return c

        lax.fori_loop(0, TD, issue, 0, unroll=DMA_UNROLL)

    @pl.when(step == 0)
    def _():
        start_gathers(dest_ref, 0)

    @pl.when(step + 1 < n_steps)
    def _():
        start_gathers(dest_next_ref, 1 - slot)

    def drain(r, c):
        _row_copy(ys_ref, 0, ya.at[slot], 0, sem.at[slot]).wait()
        _row_copy(ys_ref, 0, yb.at[slot], 0, sem.at[slot]).wait()
        return c

    lax.fori_loop(0, TD, drain, 0, unroll=DMA_UNROLL)
    gt = gate_ref[...]
    moe = (gt[:, 0:1] * _rows_2d(ya.at[slot], TD).astype(F32)
           + gt[:, 1:2] * _rows_2d(yb.at[slot], TD).astype(F32))
    x2 = x1_ref[0] + mod_ref[0][5:6] * moe
    ms = jnp.mean(x2 * x2, axis=-1, keepdims=True)
    o_ref[0] = x2 * lax.rsqrt(ms + EPS) * g_ref[...]


def _combine(dest3, ys, x1, gates, mods, g_final):
    b, s, d = x1.shape
    nt = s // TD
    return pl.pallas_call(
        _combine_kernel,
        grid=(b, nt),
        in_specs=[pl.BlockSpec((1, 1, 2 * TD), lambda bi, i: (bi * nt + i, 0, 0),
                               memory_space=pltpu.SMEM),
                  pl.BlockSpec((1, 1, 2 * TD),
                               lambda bi, i: (jnp.minimum(bi * nt + i + 1, b * nt - 1), 0, 0),
                               memory_space=pltpu.SMEM),
                  pl.BlockSpec(memory_space=pl.ANY),
                  pl.BlockSpec((1, TD, d), lambda bi, i: (bi, i, 0)),
                  pl.BlockSpec((TD, 128), lambda bi, i: (bi * nt + i, 0)),
                  pl.BlockSpec((1, N_MOD, d), lambda bi, i: (bi, 0, 0)),
                  pl.BlockSpec((1, d), lambda bi, i: (0, 0))],
        out_specs=pl.BlockSpec((1, TD, d), lambda bi, i: (bi, i, 0)),
        out_shape=jax.ShapeDtypeStruct((b, s, d), F32),
        scratch_shapes=[pltpu.VMEM((2, TD * ROW_TILE, 128), U32),
                        pltpu.VMEM((2, TD * ROW_TILE, 128), U32),
                        pltpu.SemaphoreType.DMA((2,))],
        compiler_params=_cparams("arbitrary", "arbitrary"),
        name="combine",
    )(dest3, dest3, ys, x1, gates, mods, g_final)


@functools.lru_cache(maxsize=None)
def _rope_tables(rows):
    r, col = np.meshgrid(np.arange(rows), np.arange(GRID_W), indexing='ij')
    pos = np.stack([r.reshape(-1), col.reshape(-1)], axis=-1).astype(np.float32)
    inv_freq = (np.float32(ROPE_THETA)
                ** (-np.arange(0, ROPE_AXIS, 2, dtype=np.float32) / np.float32(ROPE_AXIS))).astype(np.float32)
    ang = (pos[:, :, None] * inv_freq).astype(np.float32)
    ang = np.concatenate([ang, ang], axis=-1).astype(np.float64)
    n = ang.shape[0]
    cos = np.tile(np.cos(ang).reshape(n, HEAD_DIM), (1, 2)).astype(np.float32)
    sin = np.tile(np.sin(ang).reshape(n, HEAD_DIM), (1, 2)).astype(np.float32)
    upper = (np.arange(HEAD_W) % ROPE_AXIS) >= ROPE_HALF
    sa = np.where(upper, sin, np.float32(0.0))
    sb = np.where(upper, np.float32(0.0), -sin)
    return cos, sa, sb


@functools.lru_cache(maxsize=None)
def _dft_constants(n_pos):
    c = np.arange(FGROUP_DIM)
    ang_c = 2.0 * np.pi * ((c[:, None] * c[None, :]) % FGROUP_DIM) / FGROUP_DIM
    norm = 1.0 / math.sqrt(n_pos * FGROUP_DIM)
    cmat = (np.cos(ang_c) * norm).astype(np.float32)
    smat = (np.sin(ang_c) * norm).astype(np.float32)
    k1 = np.arange(DFT_N1)[None, :, None]
    n1 = np.arange(DFT_N1)[None, None, :]
    n2 = np.arange(DFT_N2)[:, None, None]
    ang_g = 2.0 * np.pi * ((k1 * (DFT_N2 * n1 + n2)) % n_pos) / n_pos
    gsmall = np.stack([np.cos(ang_g), np.sin(ang_g)], axis=1)
    gsmall = gsmall.reshape(DFT_N2 // F1_NB, F1_NB, 2, DFT_N1, DFT_N1)
    gmat = np.einsum('japkn,ab->jpkanb', gsmall, np.eye(F1_NB)).reshape(
        DFT_N2 // F1_NB, 2 * DFT_N1 * F1_NB, DFT_N1 * F1_NB).astype(BF16)
    k2 = np.arange(DFT_N2)
    ang_2 = 2.0 * np.pi * ((k2[:, None] * k2[None, :]) % DFT_N2) / DFT_N2
    c2 = np.cos(ang_2).astype(np.float32)
    s2 = np.sin(ang_2).astype(np.float32)
    return cmat, smat, gmat, c2, s2


def kernel(x, c, ctx, c_ctx, w_ada, b_ada, g_mix_norm, g_ffn_norm, w_in, lambda_q1, lambda_k1, lambda_q2, lambda_k2, g_subln, w_fourier, w_out, w_router_group, b_router_group, w_router_expert, b_router_expert, w_gate, w_up, w_down, g_final):
    b, s, d = x.shape
    t = b * s
    assert d == D_MODEL and s == DFT_N1 * DFT_N2 and s % GRID_W == 0 and b == 2

    cc = jnp.concatenate([c, c_ctx[None, :], jnp.zeros((8 - b - 1, d), F32)], axis=0)
    mods = _adaln(cc, w_ada[0], b_ada[0]).reshape(8, N_MOD, d)

    cmat, smat, gmat, c2, s2 = _dft_constants(s)
    wf = _wfold(jnp.asarray(cmat), jnp.asarray(smat), w_fourier[0])
    cos_t, sa_t, sb_t = _rope_tables(s // GRID_W)

    w_in_b = w_in[0].astype(BF16)
    g_mix = g_mix_norm[0].reshape(1, d)
    q, kl, vl, y = _inproj(x, mods, g_mix, w_in_b, wf, cos_t, sa_t, sb_t)
    kc, vc = _ctxproj(ctx, mods, g_mix, w_in_b)

    lam_p = jnp.stack([lambda_q1[0], lambda_k1[0], lambda_q2[0], lambda_k2[0]], axis=0)
    attn_o = _attention(q, kc, vc, kl, vl, lam_p, g_subln[0].reshape(1, HEAD_W))

    br, bi = _dft1(y.reshape(b, DFT_N1, DFT_N2, 2 * F_W), jnp.asarray(gmat))
    four = _dft2(br, bi, jnp.asarray(c2).astype(BF16),
                 jnp.asarray(s2).astype(BF16)).reshape(b, s, F_W)

    n_r = N_GROUPS + N_EXPERTS
    w_r = jnp.concatenate([w_router_group[0], w_router_expert[0],
                           jnp.zeros((d, 128 - n_r), F32)], axis=1)
    b_r = jnp.concatenate([b_router_group[0], b_router_expert[0],
                           jnp.zeros((128 - n_r,), F32)]).reshape(1, 128)
    w_r_hi = w_r.astype(BF16)
    w_r_lo = (w_r - w_r_hi.astype(F32)).astype(BF16)
    x1, h3, meta, gates, cnt = _outproj(attn_o, four, x, mods, g_ffn_norm[0].reshape(1, d),
                                        w_out[0].astype(BF16),
                                        jnp.concatenate([w_r_hi, w_r_lo], axis=1), b_r)

    counts = cnt[0, N_GROUPS:N_GROUPS + N_EXPERTS].astype(jnp.int32)
    nblk = (counts + MOE_BLK - 1) // MOE_BLK
    blk_end = jnp.cumsum(nblk)
    blk_start = (blk_end - nblk).astype(jnp.int32)
    drows = _destrows(blk_start, meta)
    dest = jnp.stack([drows[0], drows[1]], axis=1)
    n_blocks = t * TOP_K // MOE_BLK + N_EXPERTS
    blk_ids = jnp.arange(n_blocks, dtype=jnp.int32)
    blk_expert = jnp.minimum(
        jnp.sum((blk_end[None, :] <= blk_ids[:, None]).astype(jnp.int32), axis=1),
        N_EXPERTS - 1).astype(jnp.int32)
    n_used = blk_end[-1:].astype(jnp.int32)
    dest3 = dest.reshape(t // TD, 1, 2 * TD)

    xs = _dispatch(dest.reshape(t // TD_DISPATCH, 1, 2 * TD_DISPATCH), h3,
                   jnp.zeros((n_blocks * MOE_BLK * ROW_TILE, 128), U32))
    e_ids = jnp.arange(N_EXPERTS, dtype=jnp.int32)
    later = (e_ids[None, :] > e_ids[:, None]) & (nblk[None, :] > 0)
    next_expert = jnp.min(jnp.where(later, e_ids[None, :], N_EXPERTS), axis=1)
    next_expert = jnp.where(next_expert == N_EXPERTS, e_ids, next_expert).astype(jnp.int32)
    ys = _experts(blk_expert, n_used, next_expert, xs, w_gate[0], w_up[0], w_down[0])
    return _combine(dest3, ys, x1, gates, mods, g_final.reshape(1, d))
```

```python
import functools
import math

import numpy as np
import jax
import jax.numpy as jnp
from jax import lax
from jax.experimental import pallas as pl
from jax.experimental.pallas import tpu as pltpu

F32 = jnp.float32
BF16 = jnp.bfloat16

D_MODEL = 1024
GRID_W = 64
N_HEADS = 4
HEAD_DIM = 64
HEAD_W = 2 * HEAD_DIM
QK_W = N_HEADS * HEAD_W
N_FGROUPS = 4
FGROUP_DIM = 128
F_W = N_FGROUPS * FGROUP_DIM
ROPE_THETA = 10000.0
ROPE_AXIS = HEAD_DIM // 2
ROPE_HALF = ROPE_AXIS // 2
N_GROUPS = 4
EXPERTS_PER_GROUP = 8
N_EXPERTS = N_GROUPS * EXPERTS_PER_GROUP
TOP_K = 2
D_EXPERT = 512
N_MOD = 6
N_COND = 3
EPS = 1e-6
LAMBDA_INIT = 0.8 - 0.6 * math.exp(-0.3 * 0)
LOG2_E = 1.4426950408889634

DFT_N1 = 64
DFT_N2 = 128

TM_PROJ = 512
TM_INPROJ = 1024
TQ = 1024
TK = 2048
F1_NB = 8
F2_KB = 8
MOE_BLK = 512
ROW_TILE = D_MODEL // 256
U32 = jnp.uint32
TD = 1024
TD_DISPATCH = 1024
DMA_UNROLL = 16
VMEM_LIMIT = 48 * 1024 * 1024


def _cparams(*sem):
    return pltpu.CompilerParams(dimension_semantics=sem, vmem_limit_bytes=VMEM_LIMIT)


def _adaln_kernel(c_ref, w_ref, b_ref, o_ref):
    cc = c_ref[...]
    s = cc * jax.nn.sigmoid(cc)
    w = w_ref[...]
    o_ref[...] = jnp.zeros(o_ref.shape, F32)
    for r in range(N_COND):
        o_ref[r:r + 1, :] = jnp.sum(w * s[:, r:r + 1], axis=0, keepdims=True) + b_ref[...]


def _adaln(cc, w_ada, b_ada):
    n = w_ada.shape[1]
    tn = 1536
    return pl.pallas_call(
        _adaln_kernel,
        grid=(n // tn,),
        in_specs=[pl.BlockSpec((D_MODEL, 8), lambda j: (0, 0)),
                  pl.BlockSpec((D_MODEL, tn), lambda j: (0, j)),
                  pl.BlockSpec((1, tn), lambda j: (0, j))],
        out_specs=pl.BlockSpec((8, tn), lambda j: (0, j)),
        out_shape=jax.ShapeDtypeStruct((8, n), F32),
        compiler_params=_cparams("arbitrary"),
        name="adaln",
    )(cc, w_ada, b_ada.reshape(1, n))


def _wfold_kernel(c_ref, s_ref, w_ref, o_ref):
    w = w_ref[0]
    o_ref[0, :, :FGROUP_DIM] = jnp.dot(c_ref[...], w, preferred_element_type=F32,
                                       precision=lax.Precision.HIGHEST).astype(BF16)
    o_ref[0, :, FGROUP_DIM:] = jnp.dot(s_ref[...], w, preferred_element_type=F32,
                                       precision=lax.Precision.HIGHEST).astype(BF16)


def _wfold(cmat, smat, w_fourier):
    return pl.pallas_call(
        _wfold_kernel,
        grid=(N_FGROUPS,),
        in_specs=[pl.BlockSpec((FGROUP_DIM, FGROUP_DIM), lambda g: (0, 0)),
                  pl.BlockSpec((FGROUP_DIM, FGROUP_DIM), lambda g: (0, 0)),
                  pl.BlockSpec((1, FGROUP_DIM, FGROUP_DIM), lambda g: (g, 0, 0))],
        out_specs=pl.BlockSpec((1, FGROUP_DIM, 2 * FGROUP_DIM), lambda g: (g, 0, 0)),
        out_shape=jax.ShapeDtypeStruct((N_FGROUPS, FGROUP_DIM, 2 * FGROUP_DIM), BF16),
        compiler_params=_cparams("arbitrary"),
        name="wfold",
    )(cmat, smat, w_fourier)


def _norm_mod(x, g, shift, scale):
    ms = jnp.mean(x * x, axis=-1, keepdims=True)
    y = x * lax.rsqrt(ms + EPS) * g
    return y * (1.0 + scale) + shift


def _rope_slab(p, cos, sa, sb):
    return (p * cos + pltpu.roll(p, ROPE_HALF, 1) * sa
            + pltpu.roll(p, HEAD_W - ROPE_HALF, 1) * sb)


def _inproj_kernel(x_ref, mod_ref, g_ref, w_ref, wf_ref, cos_ref, sa_ref, sb_ref,
                   q_ref, k_ref, v_ref, y_ref):
    m = mod_ref[0]
    h = _norm_mod(x_ref[0], g_ref[...], m[0:1], m[1:2]).astype(BF16)
    cos, sa, sb = cos_ref[...], sa_ref[...], sb_ref[...]
    scale = HEAD_DIM ** -0.5 * LOG2_E
    pq = jnp.dot(h, w_ref[:, 0:QK_W], preferred_element_type=F32)
    for hh in range(N_HEADS):
        sl = slice(hh * HEAD_W, (hh + 1) * HEAD_W)
        q_ref[0, :, sl] = (_rope_slab(pq[:, sl], cos, sa, sb) * scale).astype(BF16)
    pk = jnp.dot(h, w_ref[:, QK_W:2 * QK_W], preferred_element_type=F32)
    for hh in range(N_HEADS):
        sl = slice(hh * HEAD_W, (hh + 1) * HEAD_W)
        k_ref[0, :, sl] = _rope_slab(pk[:, sl], cos, sa, sb).astype(BF16)
    v_ref[0] = jnp.dot(h, w_ref[:, 2 * QK_W:3 * QK_W], preferred_element_type=F32).astype(BF16)
    pf = jnp.dot(h, w_ref[:, 3 * QK_W:], preferred_element_type=F32).astype(BF16)
    for g in range(N_FGROUPS):
        yy = jnp.dot(pf[:, g * FGROUP_DIM:(g + 1) * FGROUP_DIM], wf_ref[g],
                     preferred_element_type=F32)
        y_ref[0, :, g * FGROUP_DIM:(g + 1) * FGROUP_DIM] = yy[:, :FGROUP_DIM]
        y_ref[0, :, F_W + g * FGROUP_DIM:F_W + (g + 1) * FGROUP_DIM] = yy[:, FGROUP_DIM:]


def _inproj(x, mods, g_mix, w_in, wf, cos_t, sa_t, sb_t):
    b, s, d = x.shape
    tm = TM_INPROJ
    tok = lambda bi, i: (bi, i, 0)
    return pl.pallas_call(
        _inproj_kernel,
        grid=(b, s // tm),
        in_specs=[pl.BlockSpec((1, tm, d), tok),
                  pl.BlockSpec((1, N_MOD, d), lambda bi, i: (bi, 0, 0)),
                  pl.BlockSpec((1, d), lambda bi, i: (0, 0)),
                  pl.BlockSpec(w_in.shape, lambda bi, i: (0, 0)),
                  pl.BlockSpec(wf.shape, lambda bi, i: (0, 0, 0)),
                  pl.BlockSpec((tm, HEAD_W), lambda bi, i: (i, 0)),
                  pl.BlockSpec((tm, HEAD_W), lambda bi, i: (i, 0)),
                  pl.BlockSpec((tm, HEAD_W), lambda bi, i: (i, 0))],
        out_specs=[pl.BlockSpec((1, tm, QK_W), tok),
                   pl.BlockSpec((1, tm, QK_W), tok),
                   pl.BlockSpec((1, tm, QK_W), tok),
                   pl.BlockSpec((1, tm, 2 * F_W), tok)],
        out_shape=[jax.ShapeDtypeStruct((b, s, QK_W), BF16),
                   jax.ShapeDtypeStruct((b, s, QK_W), BF16),
                   jax.ShapeDtypeStruct((b, s, QK_W), BF16),
                   jax.ShapeDtypeStruct((b, s, 2 * F_W), F32)],
        compiler_params=_cparams("parallel", "arbitrary"),
        name="inproj",
    )(x, mods, g_mix, w_in, wf, cos_t, sa_t, sb_t)


def _ctxproj_kernel(x_ref, mod_ref, g_ref, w_ref, k_ref, v_ref):
    m = mod_ref[0]
    h = _norm_mod(x_ref[0], g_ref[...], m[0:1], m[1:2]).astype(BF16)
    k_ref[0] = jnp.dot(h, w_ref[:, QK_W:2 * QK_W], preferred_element_type=F32).astype(BF16)
    v_ref[0] = jnp.dot(h, w_ref[:, 2 * QK_W:3 * QK_W], preferred_element_type=F32).astype(BF16)


def _ctxproj(ctx, mods, g_mix, w_in):
    b, n, d = ctx.shape
    return pl.pallas_call(
        _ctxproj_kernel,
        grid=(b,),
        in_specs=[pl.BlockSpec((1, n, d), lambda bi: (bi, 0, 0)),
                  pl.BlockSpec((1, N_MOD, d), lambda bi: (2, 0, 0)),
                  pl.BlockSpec((1, d), lambda bi: (0, 0)),
                  pl.BlockSpec(w_in.shape, lambda bi: (0, 0))],
        out_specs=[pl.BlockSpec((1, n, QK_W), lambda bi: (bi, 0, 0)),
                   pl.BlockSpec((1, n, QK_W), lambda bi: (bi, 0, 0))],
        out_shape=[jax.ShapeDtypeStruct((b, n, QK_W), BF16),
                   jax.ShapeDtypeStruct((b, n, QK_W), BF16)],
        compiler_params=_cparams("arbitrary"),
        name="ctxproj",
    )(ctx, mods, g_mix, w_in)


def _attn_kernel(q_ref, kc_ref, vc_ref, kl_ref, vl_ref, lam_ref, g_ref, o_ref,
                 m_ref, l_ref, acc_ref):
    q = q_ref[0]
    lane = lax.broadcasted_iota(jnp.int32, q.shape, 1)
    zero = jnp.zeros_like(q)
    q0 = jnp.where(lane < HEAD_DIM, q, zero)
    q1 = jnp.where(lane >= HEAD_DIM, q, zero)
    qs = (q0, q1)
    contract_last = (((1,), (1,)), ((), ()))
    m_ref[...] = jnp.full(m_ref.shape, -1e30, F32)
    l_ref[...] = jnp.zeros(l_ref.shape, F32)
    acc_ref[...] = jnp.zeros(acc_ref.shape, F32)

    def step(kb, vb):
        nk = kb.shape[0] // HEAD_W
        for mi in range(2):
            s = lax.dot_general(qs[mi], kb, contract_last, preferred_element_type=F32)
            m_old = m_ref[mi]
            m_new = jnp.maximum(m_old, jnp.max(s, axis=-1, keepdims=True))
            alpha = jnp.exp2(m_old - m_new)
            p = jnp.exp2(s - jnp.concatenate([m_new] * nk, axis=1))
            psum = p[:, 0:HEAD_W]
            for cblk in range(1, nk):
                psum = psum + p[:, cblk * HEAD_W:(cblk + 1) * HEAD_W]
            l_ref[mi] = alpha * l_ref[mi] + psum
            acc_ref[mi] = alpha * acc_ref[mi] + jnp.dot(p.astype(BF16), vb,
                                                       preferred_element_type=F32)
            m_ref[mi] = m_new

    step(kc_ref[0], vc_ref[0])

    def body(i, c):
        off = pl.multiple_of(i * TK, TK)
        step(kl_ref[0, pl.ds(off, TK), :], vl_ref[0, pl.ds(off, TK), :])
        return c

    lax.fori_loop(0, kl_ref.shape[1] // TK, body, 0)

    lp = lam_ref[...]
    t1 = jnp.sum(lp[0:1] * lp[1:2], axis=-1, keepdims=True)
    t2 = jnp.sum(lp[2:3] * lp[3:4], axis=-1, keepdims=True)
    lam = jnp.exp(t1) - jnp.exp(t2) + LAMBDA_INIT
    l0 = jnp.sum(l_ref[0], axis=-1, keepdims=True)
    l1 = jnp.sum(l_ref[1], axis=-1, keepdims=True)
    o = acc_ref[0] / l0 - lam * (acc_ref[1] / l1)
    ms = jnp.mean(o * o, axis=-1, keepdims=True)
    o = o * lax.rsqrt(ms + EPS) * g_ref[...] * (1.0 - LAMBDA_INIT)
    o_ref[0] = o.astype(BF16)


def _attention(q, kc, vc, kl, vl, lam_p, g_subln):
    b, s, _ = q.shape
    n_ctx = kc.shape[1]
    return pl.pallas_call(
        _attn_kernel,
        grid=(b, N_HEADS, s // TQ),
        in_specs=[pl.BlockSpec((1, TQ, HEAD_W), lambda bi, h, i: (bi, i, h)),
                  pl.BlockSpec((1, n_ctx, HEAD_W), lambda bi, h, i: (bi, 0, h)),
                  pl.BlockSpec((1, n_ctx, HEAD_W), lambda bi, h, i: (bi, 0, h)),
                  pl.BlockSpec((1, s, HEAD_W), lambda bi, h, i: (bi, 0, h)),
                  pl.BlockSpec((1, s, HEAD_W), lambda bi, h, i: (bi, 0, h)),
                  pl.BlockSpec((4, HEAD_DIM), lambda bi, h, i: (0, 0)),
                  pl.BlockSpec((1, HEAD_W), lambda bi, h, i: (0, 0))],
        out_specs=pl.BlockSpec((1, TQ, HEAD_W), lambda bi, h, i: (bi, i, h)),
        out_shape=jax.ShapeDtypeStruct((b, s, QK_W), BF16),
        scratch_shapes=[pltpu.VMEM((2, TQ, HEAD_W), F32)] * 3,
        compiler_params=_cparams("parallel", "parallel", "arbitrary"),
        name="diffattn",
    )(q, kc, vc, kl, vl, lam_p, g_subln)


def _dft1_kernel(y_ref, g_ref, br_ref, bi_ref):
    rows = DFT_N1 * F1_NB
    yb = y_ref[0].reshape(rows, 2 * F_W).astype(BF16)
    p = jnp.dot(g_ref[0], yb, preferred_element_type=F32)
    top, bot = p[:rows], p[rows:]
    br_ref[0] = (top[:, :F_W] - bot[:, F_W:]).reshape(DFT_N1, F1_NB, F_W)
    bi_ref[0] = (-top[:, F_W:] - bot[:, :F_W]).reshape(DFT_N1, F1_NB, F_W)


def _dft1(y4, gmat):
    b = y4.shape[0]
    rows = DFT_N1 * F1_NB
    return pl.pallas_call(
        _dft1_kernel,
        grid=(DFT_N2 // F1_NB, b),
        in_specs=[pl.BlockSpec((1, DFT_N1, F1_NB, 2 * F_W), lambda j, bi: (bi, 0, j, 0)),
                  pl.BlockSpec((1, 2 * rows, rows), lambda j, bi: (j, 0, 0))],
        out_specs=[pl.BlockSpec((1, DFT_N1, F1_NB, F_W), lambda j, bi: (bi, 0, j, 0)),
                   pl.BlockSpec((1, DFT_N1, F1_NB, F_W), lambda j, bi: (bi, 0, j, 0))],
        out_shape=[jax.ShapeDtypeStruct((b, DFT_N1, DFT_N2, F_W), F32),
                   jax.ShapeDtypeStruct((b, DFT_N1, DFT_N2, F_W), F32)],
        compiler_params=_cparams("arbitrary", "arbitrary"),
        name="dft1",
    )(y4, gmat)


def _dft2_kernel(br_ref, bi_ref, c_ref, s_ref, o_ref):
    for j in range(F2_KB):
        r = (jnp.dot(c_ref[...], br_ref[0, j].astype(BF16), preferred_element_type=F32)
             + jnp.dot(s_ref[...], bi_ref[0, j].astype(BF16), preferred_element_type=F32))
        o_ref[0, :, j, :] = r


def _dft2(br4, bi4, c2, s2):
    b = br4.shape[0]
    blk = (1, F2_KB, DFT_N2, F_W)
    return pl.pallas_call(
        _dft2_kernel,
        grid=(b, DFT_N1 // F2_KB),
        in_specs=[pl.BlockSpec(blk, lambda bi, j: (bi, j, 0, 0)),
                  pl.BlockSpec(blk, lambda bi, j: (bi, j, 0, 0)),
                  pl.BlockSpec((DFT_N2, DFT_N2), lambda bi, j: (0, 0)),
                  pl.BlockSpec((DFT_N2, DFT_N2), lambda bi, j: (0, 0))],
        out_specs=pl.BlockSpec((1, DFT_N2, F2_KB, F_W), lambda bi, j: (bi, 0, j, 0)),
        out_shape=jax.ShapeDtypeStruct((b, DFT_N2, DFT_N1, F_W), F32),
        compiler_params=_cparams("parallel", "arbitrary"),
        name="dft2",
    )(br4, bi4, c2, s2)


def _outproj_kernel(o_ref, f_ref, x_ref, mod_ref, g_ref, wo_ref, wr_ref, br_ref, tri_ref,
                    x1_ref, h3_ref, meta_ref, gate_ref, cnt_out_ref, cnt_ref):
    first = (pl.program_id(0) == 0) & (pl.program_id(1) == 0)

    @pl.when(first)
    def _():
        cnt_ref[...] = jnp.zeros(cnt_ref.shape, F32)

    m = mod_ref[0]
    mix = (jnp.dot(o_ref[0], wo_ref[0:QK_W, :], preferred_element_type=F32)
           + jnp.dot(f_ref[0].astype(BF16), wo_ref[QK_W:, :], preferred_element_type=F32))
    x1 = x_ref[0] + m[2:3] * mix
    x1_ref[0] = x1
    h2 = _norm_mod(x1, g_ref[...], m[3:4], m[4:5])
    _store_rows(h3_ref, h2)
    h_hi = h2.astype(BF16)
    h_lo = (h2 - h_hi.astype(F32)).astype(BF16)
    hw = jnp.dot(h_hi, wr_ref[...], preferred_element_type=F32)
    lg = (hw[:, :128] + hw[:, 128:]
          + jnp.dot(h_lo, wr_ref[:, 0:128], preferred_element_type=F32)
          + br_ref[...])
    lane = lax.broadcasted_iota(jnp.int32, lg.shape, 1)
    lanef = lane.astype(F32)
    ninf = jnp.float32(-jnp.inf)
    big = jnp.float32(lg.shape[1])
    gl = jnp.where(lane < N_GROUPS, lg, ninf)
    gmax = jnp.max(gl, axis=-1, keepdims=True)
    grp = jnp.min(jnp.where(gl == gmax, lanef, big), axis=-1, keepdims=True)
    pg = 1.0 / jnp.sum(jnp.exp(gl - gmax), axis=-1, keepdims=True)
    e_lane = lane - N_GROUPS
    lane_grp = (e_lane >> 3).astype(F32)
    emask = (e_lane >= 0) & (e_lane < N_EXPERTS) & (lane_grp == grp)
    el = jnp.where(emask, lg, ninf)
    t1 = jnp.max(el, axis=-1, keepdims=True)
    i1 = jnp.min(jnp.where(el == t1, lanef, big), axis=-1, keepdims=True)
    el2 = jnp.where(lanef == i1, ninf, el)
    t2 = jnp.max(el2, axis=-1, keepdims=True)
    i2 = jnp.min(jnp.where(el2 == t2, lanef, big), axis=-1, keepdims=True)
    dd = jnp.exp(t2 - t1)
    w1 = pg / (1.0 + dd)
    w2 = pg * dd / (1.0 + dd)
    gate_ref[...] = jnp.where(lane == 0, w1, jnp.where(lane == 1, w2, 0.0))
    hit1 = lanef == i1
    hit2 = lanef == i2
    oh = jnp.where(hit1 | hit2, 1.0, 0.0)
    before = jnp.dot(tri_ref[...], oh.astype(BF16), preferred_element_type=F32) + cnt_ref[0:1, :]
    r1 = jnp.sum(jnp.where(hit1, before, 0.0), axis=-1, keepdims=True)
    r2 = jnp.sum(jnp.where(hit2, before, 0.0), axis=-1, keepdims=True)
    cnt_ref[0:1, :] = cnt_ref[0:1, :] + jnp.sum(oh, axis=0, keepdims=True)
    cnt_out_ref[...] = cnt_ref[...]
    meta = jnp.where(lane == 0, i1 - N_GROUPS,
                     jnp.where(lane == 1, i2 - N_GROUPS,
                               jnp.where(lane == 2, r1, jnp.where(lane == 3, r2, 0.0))))
    meta_ref[...] = jnp.transpose(meta)[0:8, :].astype(jnp.int32)


def _outproj(attn_o, four, x, mods, g_ffn, w_out, w_r, b_r):
    b, s, d = x.shape
    tm = TM_PROJ
    nt = s // tm
    tok = lambda bi, i: (bi, i, 0)
    flat = lambda bi, i: (bi * nt + i, 0)
    tri = jnp.asarray(np.tril(np.ones((tm, tm), np.float32), -1)).astype(BF16)
    return pl.pallas_call(
        _outproj_kernel,
        grid=(b, nt),
        in_specs=[pl.BlockSpec((1, tm, QK_W), tok),
                  pl.BlockSpec((1, tm, F_W), tok),
                  pl.BlockSpec((1, tm, d), tok),
                  pl.BlockSpec((1, N_MOD, d), lambda bi, i: (bi, 0, 0)),
                  pl.BlockSpec((1, d), lambda bi, i: (0, 0)),
                  pl.BlockSpec(w_out.shape, lambda bi, i: (0, 0)),
                  pl.BlockSpec(w_r.shape, lambda bi, i: (0, 0)),
                  pl.BlockSpec(b_r.shape, lambda bi, i: (0, 0)),
                  pl.BlockSpec((tm, tm), lambda bi, i: (0, 0))],
        out_specs=[pl.BlockSpec((1, tm, d), tok),
                   pl.BlockSpec((tm * ROW_TILE, 128), flat),
                   pl.BlockSpec((8, tm), lambda bi, i: (0, bi * nt + i)),
                   pl.BlockSpec((tm, 128), flat),
                   pl.BlockSpec((8, 128), lambda bi, i: (0, 0))],
        out_shape=[jax.ShapeDtypeStruct((b, s, d), F32),
                   jax.ShapeDtypeStruct((b * s * ROW_TILE, 128), U32),
                   jax.ShapeDtypeStruct((8, b * s), jnp.int32),
                   jax.ShapeDtypeStruct((b * s, 128), F32),
                   jax.ShapeDtypeStruct((8, 128), F32)],
        scratch_shapes=[pltpu.VMEM((8, 128), F32)],
        compiler_params=_cparams("arbitrary", "arbitrary"),
        name="outproj",
    )(attn_o, four, x, mods, g_ffn, w_out, w_r, b_r, tri)


def _destrows_kernel(start_ref, meta_ref, o_ref):
    meta = meta_ref[...]
    row = lax.broadcasted_iota(jnp.int32, meta.shape, 0)
    base = jnp.zeros_like(meta)
    for e in range(N_EXPERTS):
        base = jnp.where(meta == e, start_ref[e] * MOE_BLK, base)
    ranks = pltpu.roll(meta, meta.shape[0] - TOP_K, 0)
    o_ref[...] = jnp.where(row < TOP_K, base + ranks, 0)


def _destrows(blk_start, meta):
    return pl.pallas_call(
        _destrows_kernel,
        grid_spec=pltpu.PrefetchScalarGridSpec(
            num_scalar_prefetch=1,
            grid=(1,),
            in_specs=[pl.BlockSpec(meta.shape, lambda i, st: (0, 0))],
            out_specs=pl.BlockSpec(meta.shape, lambda i, st: (0, 0))),
        out_shape=jax.ShapeDtypeStruct(meta.shape, jnp.int32),
        compiler_params=_cparams("arbitrary"),
        name="destrows",
    )(blk_start, meta)


def _row_slice(row):
    return pl.ds(pl.multiple_of(row * ROW_TILE, ROW_TILE), ROW_TILE)


def _row_copy(src_ref, src_row, dst_ref, dst_row, sem):
    return pltpu.make_async_copy(src_ref.at[_row_slice(src_row)], dst_ref.at[_row_slice(dst_row)], sem)


def _bf16_bits(x):
    return lax.bitcast_convert_type(x.astype(BF16).astype(F32), U32)


def _rows_2d(ref, n_rows):
    hi, lo = [], []
    for cblk in range(ROW_TILE):
        w = ref[pl.ds(cblk, n_rows, stride=ROW_TILE), :]
        hi.append(lax.bitcast_convert_type(w & jnp.uint32(0xFFFF0000), F32).astype(BF16))
        lo.append(lax.bitcast_convert_type(w << 16, F32).astype(BF16))
    return jnp.concatenate(hi + lo, axis=1)


def _store_rows(ref, val):
    for cblk in range(ROW_TILE):
        hi = _bf16_bits(val[:, cblk * 128:(cblk + 1) * 128])
        lo = _bf16_bits(val[:, (cblk + ROW_TILE) * 128:(cblk + ROW_TILE + 1) * 128])
        ref[pl.ds(cblk, val.shape[0], stride=ROW_TILE), :] = hi | (lo >> 16)


def _dispatch_kernel(dest_ref, h_ref, zeros_ref, xs_ref, sem):
    del zeros_ref
    n = 2 * TD_DISPATCH

    def issue(r, c):
        _row_copy(h_ref, r, xs_ref, dest_ref[0, 0, 2 * r], sem).start(priority=0)
        _row_copy(h_ref, r, xs_ref, dest_ref[0, 0, 2 * r + 1], sem).start(priority=1)
        return c

    lax.fori_loop(0, TD_DISPATCH, issue, 0, unroll=DMA_UNROLL // 2)

    def drain(a, c):
        _row_copy(h_ref, 0, xs_ref, 0, sem).wait()
        return c

    lax.fori_loop(0, n, drain, 0, unroll=DMA_UNROLL)


def _dispatch(dest3, h3, xs_zeros):
    t = h3.shape[0] // ROW_TILE
    return pl.pallas_call(
        _dispatch_kernel,
        grid=(t // TD_DISPATCH,),
        in_specs=[pl.BlockSpec((1, 1, 2 * TD_DISPATCH), lambda i: (i, 0, 0),
                               memory_space=pltpu.SMEM),
                  pl.BlockSpec((TD_DISPATCH * ROW_TILE, 128), lambda i: (i, 0)),
                  pl.BlockSpec(memory_space=pl.ANY)],
        out_specs=pl.BlockSpec(memory_space=pl.ANY),
        out_shape=jax.ShapeDtypeStruct(xs_zeros.shape, xs_zeros.dtype),
        scratch_shapes=[pltpu.SemaphoreType.DMA(())],
        input_output_aliases={2: 0},
        compiler_params=_cparams("arbitrary"),
        name="dispatch",
    )(dest3, h3, xs_zeros)


def _experts_kernel(be_ref, nu_ref, nxt_ref, xs_ref, wg_hbm, wu_hbm, wd_hbm, ys_ref,
                    wgb, wub, wdb, sg, su, sd, sem):
    j = pl.program_id(0)
    used = j < nu_ref[0]
    e = be_ref[j]
    new_expert = (j == 0) | (e != be_ref[jnp.maximum(j - 1, 0)])
    e_next = nxt_ref[e]

    def weight_copies(idx):
        return (pltpu.make_async_copy(wg_hbm.at[idx], sg, sem.at[0]),
                pltpu.make_async_copy(wu_hbm.at[idx], su, sem.at[1]),
                pltpu.make_async_copy(wd_hbm.at[idx], sd, sem.at[2]))

    @pl.when(j == 0)
    def _():
        for cp in weight_copies(e):
            cp.start(priority=1)

    @pl.when(used & new_expert)
    def _():
        for cp in weight_copies(e):
            cp.wait()
        wgb[...] = sg[...].astype(BF16)
        wub[...] = su[...].astype(BF16)
        wdb[...] = sd[...].astype(BF16)

    @pl.when(used & new_expert & (e_next != e))
    def _():
        for cp in weight_copies(e_next):
            cp.start(priority=1)

    @pl.when(used)
    def _():
        xb = _rows_2d(xs_ref, MOE_BLK)
        gate = jnp.dot(xb, wgb[...], preferred_element_type=F32)
        up = jnp.dot(xb, wub[...], preferred_element_type=F32)
        hid = (gate * jax.nn.sigmoid(gate) * up).astype(BF16)
        _store_rows(ys_ref, jnp.dot(hid, wdb[...], preferred_element_type=F32))

    @pl.when(pl.program_id(0) >= nu_ref[0])
    def _():
        ys_ref[...] = jnp.zeros_like(ys_ref)


def _experts(blk_expert, n_used, next_expert, xs, wg, wu, wd):
    rows = xs.shape[0] // ROW_TILE
    d = wg.shape[1]
    nb = rows // MOE_BLK
    row_blk = lambda j, be, nu, nx: (jnp.minimum(j, nu[0] - 1), 0)
    out_blk = lambda j, be, nu, nx: (j, 0)
    return pl.pallas_call(
        _experts_kernel,
        grid_spec=pltpu.PrefetchScalarGridSpec(
            num_scalar_prefetch=3,
            grid=(nb,),
            in_specs=[pl.BlockSpec((MOE_BLK * ROW_TILE, 128), row_blk),
                      pl.BlockSpec(memory_space=pl.ANY),
                      pl.BlockSpec(memory_space=pl.ANY),
                      pl.BlockSpec(memory_space=pl.ANY)],
            out_specs=pl.BlockSpec((MOE_BLK * ROW_TILE, 128), out_blk),
            scratch_shapes=[pltpu.VMEM((d, D_EXPERT), BF16), pltpu.VMEM((d, D_EXPERT), BF16),
                            pltpu.VMEM((D_EXPERT, d), BF16),
                            pltpu.VMEM((d, D_EXPERT), F32), pltpu.VMEM((d, D_EXPERT), F32),
                            pltpu.VMEM((D_EXPERT, d), F32),
                            pltpu.SemaphoreType.DMA((3,))]),
        out_shape=jax.ShapeDtypeStruct(xs.shape, xs.dtype),
        compiler_params=_cparams("arbitrary"),
        name="experts",
    )(blk_expert, n_used, next_expert, xs, wg, wu, wd)


def _combine_kernel(dest_ref, dest_next_ref, ys_ref, x1_ref, gate_ref, mod_ref, g_ref, o_ref,
                    ya, yb, sem):
    step = pl.program_id(0) * pl.num_programs(1) + pl.program_id(1)
    n_steps = pl.num_programs(0) * pl.num_programs(1)
    slot = step % 2

    def start_gathers(idx_ref, to_slot):
        def issue(r, c):
            _row_copy(ys_ref, idx_ref[0, 0, 2 * r], ya.at[to_slot], r,
                      sem.at[to_slot]).start(priority=0)
            _row_copy(ys_ref, idx_ref[0, 0, 2 * r + 1], yb.at[to_slot], r,
                      sem.at[to_slot]).start(priority=1)
            return c

        lax.fori_loop(0, TD, issue, 0, unroll=DMA_UNROLL)

    @pl.when(step == 0)
    def _():
        start_gathers(dest_ref, 0)

    @pl.when(step + 1 < n_steps)
    def _():
        start_gathers(dest_next_ref, 1 - slot)

    def drain(r, c):
        _row_copy(ys_ref, 0, ya.at[slot], 0, sem.at[slot]).wait()
        _row_copy(ys_ref, 0, yb.at[slot], 0, sem.at[slot]).wait()
        return c

    lax.fori_loop(0, TD, drain, 0, unroll=DMA_UNROLL)
    gt = gate_ref[...]
    moe = (gt[:, 0:1] * _rows_2d(ya.at[slot], TD).astype(F32)
           + gt[:, 1:2] * _rows_2d(yb.at[slot], TD).astype(F32))
    x2 = x1_ref[0] + mod_ref[0][5:6] * moe
    ms = jnp.mean(x2 * x2, axis=-1, keepdims=True)
    o_ref[0] = x2 * lax.rsqrt(ms + EPS) * g_ref[...]


def _combine(dest3, ys, x1, gates, mods, g_final):
    b, s, d = x1.shape
    nt = s // TD
    return pl.pallas_call(
        _combine_kernel,
        grid=(b, nt),
        in_specs=[pl.BlockSpec((1, 1, 2 * TD), lambda bi, i: (bi * nt + i, 0, 0),
                               memory_space=pltpu.SMEM),
                  pl.BlockSpec((1, 1, 2 * TD),
                               lambda bi, i: (jnp.minimum(bi * nt + i + 1, b * nt - 1), 0, 0),
                               memory_space=pltpu.SMEM),
                  pl.BlockSpec(memory_space=pl.ANY),
                  pl.BlockSpec((1, TD, d), lambda bi, i: (bi, i, 0)),
                  pl.BlockSpec((TD, 128), lambda bi, i: (bi * nt + i, 0)),
                  pl.BlockSpec((1, N_MOD, d), lambda bi, i: (bi, 0, 0)),
                  pl.BlockSpec((1, d), lambda bi, i: (0, 0))],
        out_specs=pl.BlockSpec((1, TD, d), lambda bi, i: (bi, i, 0)),
        out_shape=jax.ShapeDtypeStruct((b, s, d), F32),
        scratch_shapes=[pltpu.VMEM((2, TD * ROW_TILE, 128), U32),
                        pltpu.VMEM((2, TD * ROW_TILE, 128), U32),
                        pltpu.SemaphoreType.DMA((2,))],
        compiler_params=_cparams("arbitrary", "arbitrary"),
        name="combine",
    )(dest3, dest3, ys, x1, gates, mods, g_final)


@functools.lru_cache(maxsize=None)
def _rope_tables(rows):
    r, col = np.meshgrid(np.arange(rows), np.arange(GRID_W), indexing='ij')
    pos = np.stack([r.reshape(-1), col.reshape(-1)], axis=-1).astype(np.float32)
    inv_freq = (np.float32(ROPE_THETA)
                ** (-np.arange(0, ROPE_AXIS, 2, dtype=np.float32) / np.float32(ROPE_AXIS))).astype(np.float32)
    ang = (pos[:, :, None] * inv_freq).astype(np.float32)
    ang = np.concatenate([ang, ang], axis=-1).astype(np.float64)
    n = ang.shape[0]
    cos = np.tile(np.cos(ang).reshape(n, HEAD_DIM), (1, 2)).astype(np.float32)
    sin = np.tile(np.sin(ang).reshape(n, HEAD_DIM), (1, 2)).astype(np.float32)
    upper = (np.arange(HEAD_W) % ROPE_AXIS) >= ROPE_HALF
    sa = np.where(upper, sin, np.float32(0.0))
    sb = np.where(upper, np.float32(0.0), -sin)
    return cos, sa, sb


@functools.lru_cache(maxsize=None)
def _dft_constants(n_pos):
    c = np.arange(FGROUP_DIM)
    ang_c = 2.0 * np.pi * ((c[:, None] * c[None, :]) % FGROUP_DIM) / FGROUP_DIM
    norm = 1.0 / math.sqrt(n_pos * FGROUP_DIM)
    cmat = (np.cos(ang_c) * norm).astype(np.float32)
    smat = (np.sin(ang_c) * norm).astype(np.float32)
    k1 = np.arange(DFT_N1)[None, :, None]
    n1 = np.arange(DFT_N1)[None, None, :]
    n2 = np.arange(DFT_N2)[:, None, None]
    ang_g = 2.0 * np.pi * ((k1 * (DFT_N2 * n1 + n2)) % n_pos) / n_pos
    gsmall = np.stack([np.cos(ang_g), np.sin(ang_g)], axis=1)
    gsmall = gsmall.reshape(DFT_N2 // F1_NB, F1_NB, 2, DFT_N1, DFT_N1)
    gmat = np.einsum('japkn,ab->jpkanb', gsmall, np.eye(F1_NB)).reshape(
        DFT_N2 // F1_NB, 2 * DFT_N1 * F1_NB, DFT_N1 * F1_NB).astype(BF16)
    k2 = np.arange(DFT_N2)
    ang_2 = 2.0 * np.pi * ((k2[:, None] * k2[None, :]) % DFT_N2) / DFT_N2
    c2 = np.cos(ang_2).astype(np.float32)
    s2 = np.sin(ang_2).astype(np.float32)
    return cmat, smat, gmat, c2, s2


def kernel(x, c, ctx, c_ctx, w_ada, b_ada, g_mix_norm, g_ffn_norm, w_in, lambda_q1, lambda_k1, lambda_q2, lambda_k2, g_subln, w_fourier, w_out, w_router_group, b_router_group, w_router_expert, b_router_expert, w_gate, w_up, w_down, g_final):
    b, s, d = x.shape
    t = b * s
    assert d == D_MODEL and s == DFT_N1 * DFT_N2 and s % GRID_W == 0 and b == 2

    cc = jnp.concatenate([c, c_ctx[None, :], jnp.zeros((8 - b - 1, d), F32)], axis=0)
    mods = _adaln(cc.T, w_ada[0], b_ada[0]).reshape(8, N_MOD, d)

    cmat, smat, gmat, c2, s2 = _dft_constants(s)
    wf = _wfold(jnp.asarray(cmat), jnp.asarray(smat), w_fourier[0])
    cos_t, sa_t, sb_t = _rope_tables(s // GRID_W)

    w_in_b = w_in[0].astype(BF16)
    g_mix = g_mix_norm[0].reshape(1, d)
    q, kl, vl, y = _inproj(x, mods, g_mix, w_in_b, wf, cos_t, sa_t, sb_t)
    kc, vc = _ctxproj(ctx, mods, g_mix, w_in_b)

    lam_p = jnp.stack([lambda_q1[0], lambda_k1[0], lambda_q2[0], lambda_k2[0]], axis=0)
    attn_o = _attention(q, kc, vc, kl, vl, lam_p, g_subln[0].reshape(1, HEAD_W))

    br, bi = _dft1(y.reshape(b, DFT_N1, DFT_N2, 2 * F_W), jnp.asarray(gmat))
    four = _dft2(br, bi, jnp.asarray(c2).astype(BF16),
                 jnp.asarray(s2).astype(BF16)).reshape(b, s, F_W)

    n_r = N_GROUPS + N_EXPERTS
    w_r = jnp.concatenate([w_router_group[0], w_router_expert[0],
                           jnp.zeros((d, 128 - n_r), F32)], axis=1)
    b_r = jnp.concatenate([b_router_group[0], b_router_expert[0],
                           jnp.zeros((128 - n_r,), F32)]).reshape(1, 128)
    w_r_hi = w_r.astype(BF16)
    w_r_lo = (w_r - w_r_hi.astype(F32)).astype(BF16)
    x1, h3, meta, gates, cnt = _outproj(attn_o, four, x, mods, g_ffn_norm[0].reshape(1, d),
                                        w_out[0].astype(BF16),
                                        jnp.concatenate([w_r_hi, w_r_lo], axis=1), b_r)

    counts = cnt[0, N_GROUPS:N_GROUPS + N_EXPERTS].astype(jnp.int32)
    nblk = (counts + MOE_BLK - 1) // MOE_BLK
    blk_end = jnp.cumsum(nblk)
    blk_start = (blk_end - nblk).astype(jnp.int32)
    drows = _destrows(blk_start, meta)
    dest = jnp.stack([drows[0], drows[1]], axis=1)
    n_blocks = t * TOP_K // MOE_BLK + N_EXPERTS
    blk_ids = jnp.arange(n_blocks, dtype=jnp.int32)
    blk_expert = jnp.minimum(
        jnp.sum((blk_end[None, :] <= blk_ids[:, None]).astype(jnp.int32), axis=1),
        N_EXPERTS - 1).astype(jnp.int32)
    n_used = blk_end[-1:].astype(jnp.int32)
    dest3 = dest.reshape(t // TD, 1, 2 * TD)

    xs = _dispatch(dest.reshape(t // TD_DISPATCH, 1, 2 * TD_DISPATCH), h3,
                   jnp.zeros((n_blocks * MOE_BLK * ROW_TILE, 128), U32))
    e_ids = jnp.arange(N_EXPERTS, dtype=jnp.int32)
    later = (e_ids[None, :] > e_ids[:, None]) & (nblk[None, :] > 0)
    next_expert = jnp.min(jnp.where(later, e_ids[None, :], N_EXPERTS), axis=1)
    next_expert = jnp.where(next_expert == N_EXPERTS, e_ids, next_expert).astype(jnp.int32)
    ys = _experts(blk_expert, n_used, next_expert, xs, w_gate[0], w_up[0], w_down[0])
    return _combine(dest3, ys, x1, gates, mods, g_final.reshape(1, d))
```

```python
import functools
import math

import numpy as np
import jax
import jax.numpy as jnp
from jax import lax
from jax.experimental import pallas as pl
from jax.experimental.pallas import tpu as pltpu

F32 = jnp.float32
BF16 = jnp.bfloat16

D_MODEL = 1024
GRID_W = 64
N_HEADS = 4
HEAD_DIM = 64
HEAD_W = 2 * HEAD_DIM
QK_W = N_HEADS * HEAD_W
N_FGROUPS = 4
FGROUP_DIM = 128
F_W = N_FGROUPS * FGROUP_DIM
ROPE_THETA = 10000.0
ROPE_AXIS = HEAD_DIM // 2
ROPE_HALF = ROPE_AXIS // 2
N_GROUPS = 4
EXPERTS_PER_GROUP = 8
N_EXPERTS = N_GROUPS * EXPERTS_PER_GROUP
TOP_K = 2
D_EXPERT = 512
N_MOD = 6
N_COND = 3
EPS = 1e-6
LAMBDA_INIT = 0.8 - 0.6 * math.exp(-0.3 * 0)
LOG2_E = 1.4426950408889634

DFT_N1 = 64
DFT_N2 = 128

TM_PROJ = 512
TM_INPROJ = 1024
TQ = 1024
TK = 2048
F1_NB = 8
F2_KB = 8
MOE_BLK = 512
ROW_TILE = D_MODEL // 256
U32 = jnp.uint32
TD = 512
TD_DISPATCH = 1024
DMA_UNROLL = 16
VMEM_LIMIT = 48 * 1024 * 1024


def _cparams(*sem):
    return pltpu.CompilerParams(dimension_semantics=sem, vmem_limit_bytes=VMEM_LIMIT)


def _adaln_kernel(c_ref, w_ref, b_ref, o_ref):
    cc = c_ref[...]
    s = cc * jax.nn.sigmoid(cc)
    w = w_ref[...]
    o_ref[...] = jnp.zeros(o_ref.shape, F32)
    for r in range(N_COND):
        o_ref[r:r + 1, :] = jnp.sum(w * s[:, r:r + 1], axis=0, keepdims=True) + b_ref[...]


def _adaln(cc, w_ada, b_ada):
    n = w_ada.shape[1]
    tn = 1536
    return pl.pallas_call(
        _adaln_kernel,
        grid=(n // tn,),
        in_specs=[pl.BlockSpec((D_MODEL, 8), lambda j: (0, 0)),
                  pl.BlockSpec((D_MODEL, tn), lambda j: (0, j)),
                  pl.BlockSpec((1, tn), lambda j: (0, j))],
        out_specs=pl.BlockSpec((8, tn), lambda j: (0, j)),
        out_shape=jax.ShapeDtypeStruct((8, n), F32),
        compiler_params=_cparams("arbitrary"),
        name="adaln",
    )(cc, w_ada, b_ada.reshape(1, n))


def _wfold_kernel(c_ref, s_ref, w_ref, o_ref):
    w = w_ref[0]
    o_ref[0, :, :FGROUP_DIM] = jnp.dot(c_ref[...], w, preferred_element_type=F32,
                                       precision=lax.Precision.HIGHEST).astype(BF16)
    o_ref[0, :, FGROUP_DIM:] = jnp.dot(s_ref[...], w, preferred_element_type=F32,
                                       precision=lax.Precision.HIGHEST).astype(BF16)


def _wfold(cmat, smat, w_fourier):
    return pl.pallas_call(
        _wfold_kernel,
        grid=(N_FGROUPS,),
        in_specs=[pl.BlockSpec((FGROUP_DIM, FGROUP_DIM), lambda g: (0, 0)),
                  pl.BlockSpec((FGROUP_DIM, FGROUP_DIM), lambda g: (0, 0)),
                  pl.BlockSpec((1, FGROUP_DIM, FGROUP_DIM), lambda g: (g, 0, 0))],
        out_specs=pl.BlockSpec((1, FGROUP_DIM, 2 * FGROUP_DIM), lambda g: (g, 0, 0)),
        out_shape=jax.ShapeDtypeStruct((N_FGROUPS, FGROUP_DIM, 2 * FGROUP_DIM), BF16),
        compiler_params=_cparams("arbitrary"),
        name="wfold",
    )(cmat, smat, w_fourier)


def _norm_mod(x, g, shift, scale):
    ms = jnp.mean(x * x, axis=-1, keepdims=True)
    y = x * lax.rsqrt(ms + EPS) * g
    return y * (1.0 + scale) + shift


def _rope_slab(p, cos, sa, sb):
    return (p * cos + pltpu.roll(p, ROPE_HALF, 1) * sa
            + pltpu.roll(p, HEAD_W - ROPE_HALF, 1) * sb)


def _inproj_kernel(x_ref, mod_ref, g_ref, w_ref, wf_ref, cos_ref, sa_ref, sb_ref,
                   q_ref, k_ref, v_ref, y_ref):
    m = mod_ref[0]
    h = _norm_mod(x_ref[0], g_ref[...], m[0:1], m[1:2]).astype(BF16)
    cos, sa, sb = cos_ref[...], sa_ref[...], sb_ref[...]
    scale = HEAD_DIM ** -0.5 * LOG2_E
    pq = jnp.dot(h, w_ref[:, 0:QK_W], preferred_element_type=F32)
    for hh in range(N_HEADS):
        sl = slice(hh * HEAD_W, (hh + 1) * HEAD_W)
        q_ref[0, :, sl] = (_rope_slab(pq[:, sl], cos, sa, sb) * scale).astype(BF16)
    pk = jnp.dot(h, w_ref[:, QK_W:2 * QK_W], preferred_element_type=F32)
    for hh in range(N_HEADS):
        sl = slice(hh * HEAD_W, (hh + 1) * HEAD_W)
        k_ref[0, :, sl] = _rope_slab(pk[:, sl], cos, sa, sb).astype(BF16)
    v_ref[0] = jnp.dot(h, w_ref[:, 2 * QK_W:3 * QK_W], preferred_element_type=F32).astype(BF16)
    pf = jnp.dot(h, w_ref[:, 3 * QK_W:], preferred_element_type=F32).astype(BF16)
    for g in range(N_FGROUPS):
        yy = jnp.dot(pf[:, g * FGROUP_DIM:(g + 1) * FGROUP_DIM], wf_ref[g],
                     preferred_element_type=F32)
        y_ref[0, :, g * FGROUP_DIM:(g + 1) * FGROUP_DIM] = yy[:, :FGROUP_DIM]
        y_ref[0, :, F_W + g * FGROUP_DIM:F_W + (g + 1) * FGROUP_DIM] = yy[:, FGROUP_DIM:]


def _inproj(x, mods, g_mix, w_in, wf, cos_t, sa_t, sb_t):
    b, s, d = x.shape
    tm = TM_INPROJ
    tok = lambda bi, i: (bi, i, 0)
    return pl.pallas_call(
        _inproj_kernel,
        grid=(b, s // tm),
        in_specs=[pl.BlockSpec((1, tm, d), tok),
                  pl.BlockSpec((1, N_MOD, d), lambda bi, i: (bi, 0, 0)),
                  pl.BlockSpec((1, d), lambda bi, i: (0, 0)),
                  pl.BlockSpec(w_in.shape, lambda bi, i: (0, 0)),
                  pl.BlockSpec(wf.shape, lambda bi, i: (0, 0, 0)),
                  pl.BlockSpec((tm, HEAD_W), lambda bi, i: (i, 0)),
                  pl.BlockSpec((tm, HEAD_W), lambda bi, i: (i, 0)),
                  pl.BlockSpec((tm, HEAD_W), lambda bi, i: (i, 0))],
        out_specs=[pl.BlockSpec((1, tm, QK_W), tok),
                   pl.BlockSpec((1, tm, QK_W), tok),
                   pl.BlockSpec((1, tm, QK_W), tok),
                   pl.BlockSpec((1, tm, 2 * F_W), tok)],
        out_shape=[jax.ShapeDtypeStruct((b, s, QK_W), BF16),
                   jax.ShapeDtypeStruct((b, s, QK_W), BF16),
                   jax.ShapeDtypeStruct((b, s, QK_W), BF16),
                   jax.ShapeDtypeStruct((b, s, 2 * F_W), F32)],
        compiler_params=_cparams("parallel", "arbitrary"),
        name="inproj",
    )(x, mods, g_mix, w_in, wf, cos_t, sa_t, sb_t)


def _ctxproj_kernel(x_ref, mod_ref, g_ref, w_ref, k_ref, v_ref):
    m = mod_ref[0]
    h = _norm_mod(x_ref[0], g_ref[...], m[0:1], m[1:2]).astype(BF16)
    k_ref[0] = jnp.dot(h, w_ref[:, QK_W:2 * QK_W], preferred_element_type=F32).astype(BF16)
    v_ref[0] = jnp.dot(h, w_ref[:, 2 * QK_W:3 * QK_W], preferred_element_type=F32).astype(BF16)


def _ctxproj(ctx, mods, g_mix, w_in):
    b, n, d = ctx.shape
    return pl.pallas_call(
        _ctxproj_kernel,
        grid=(b,),
        in_specs=[pl.BlockSpec((1, n, d), lambda bi: (bi, 0, 0)),
                  pl.BlockSpec((1, N_MOD, d), lambda bi: (2, 0, 0)),
                  pl.BlockSpec((1, d), lambda bi: (0, 0)),
                  pl.BlockSpec(w_in.shape, lambda bi: (0, 0))],
        out_specs=[pl.BlockSpec((1, n, QK_W), lambda bi: (bi, 0, 0)),
                   pl.BlockSpec((1, n, QK_W), lambda bi: (bi, 0, 0))],
        out_shape=[jax.ShapeDtypeStruct((b, n, QK_W), BF16),
                   jax.ShapeDtypeStruct((b, n, QK_W), BF16)],
        compiler_params=_cparams("arbitrary"),
        name="ctxproj",
    )(ctx, mods, g_mix, w_in)


def _attn_kernel(q_ref, kc_ref, vc_ref, kl_ref, vl_ref, lam_ref, g_ref, o_ref, z_ref,
                 m_ref, l_ref, acc_ref):
    q = q_ref[0]
    lane = lax.broadcasted_iota(jnp.int32, q.shape, 1)
    zero = jnp.zeros_like(q)
    q0 = jnp.where(lane < HEAD_DIM, q, zero)
    q1 = jnp.where(lane >= HEAD_DIM, q, zero)
    qs = (q0, q1)
    contract_last = (((1,), (1,)), ((), ()))
    m_ref[...] = jnp.full(m_ref.shape, -1e30, F32)
    l_ref[...] = jnp.zeros(l_ref.shape, F32)
    acc_ref[...] = jnp.zeros(acc_ref.shape, F32)

    def step(kb, vb):
        nk = kb.shape[0] // HEAD_W
        for mi in range(2):
            s = lax.dot_general(qs[mi], kb, contract_last, preferred_element_type=F32)
            m_old = m_ref[mi]
            m_new = jnp.maximum(m_old, jnp.max(s, axis=-1, keepdims=True))
            alpha = jnp.exp2(m_old - m_new)
            p = jnp.exp2(s - jnp.concatenate([m_new] * nk, axis=1))
            psum = p[:, 0:HEAD_W]
            for cblk in range(1, nk):
                psum = psum + p[:, cblk * HEAD_W:(cblk + 1) * HEAD_W]
            l_ref[mi] = alpha * l_ref[mi] + psum
            acc_ref[mi] = alpha * acc_ref[mi] + jnp.dot(p.astype(BF16), vb,
                                                       preferred_element_type=F32)
            m_ref[mi] = m_new

    step(kc_ref[0], vc_ref[0])

    def body(i, c):
        off = pl.multiple_of(i * TK, TK)
        step(kl_ref[0, pl.ds(off, TK), :], vl_ref[0, pl.ds(off, TK), :])
        return c

    lax.fori_loop(0, kl_ref.shape[1] // TK, body, 0)

    lp = lam_ref[...]
    t1 = jnp.sum(lp[0:1] * lp[1:2], axis=-1, keepdims=True)
    t2 = jnp.sum(lp[2:3] * lp[3:4], axis=-1, keepdims=True)
    lam = jnp.exp(t1) - jnp.exp(t2) + LAMBDA_INIT
    l0 = jnp.sum(l_ref[0], axis=-1, keepdims=True)
    l1 = jnp.sum(l_ref[1], axis=-1, keepdims=True)
    o = acc_ref[0] / l0 - lam * (acc_ref[1] / l1)
    ms = jnp.mean(o * o, axis=-1, keepdims=True)
    o = o * lax.rsqrt(ms + EPS) * g_ref[...] * (1.0 - LAMBDA_INIT)
    o_ref[0] = o.astype(BF16)
    z_ref[...] = jnp.zeros(z_ref.shape, U32)


def _attention(q, kc, vc, kl, vl, lam_p, g_subln, zero_rows):
    b, s, _ = q.shape
    n_ctx = kc.shape[1]
    nq = s // TQ
    n_steps = b * N_HEADS * nq
    zrows = zero_rows // n_steps
    assert zrows * n_steps == zero_rows and zrows % 8 == 0
    return pl.pallas_call(
        _attn_kernel,
        grid=(b, N_HEADS, nq),
        in_specs=[pl.BlockSpec((1, TQ, HEAD_W), lambda bi, h, i: (bi, i, h)),
                  pl.BlockSpec((1, n_ctx, HEAD_W), lambda bi, h, i: (bi, 0, h)),
                  pl.BlockSpec((1, n_ctx, HEAD_W), lambda bi, h, i: (bi, 0, h)),
                  pl.BlockSpec((1, s, HEAD_W), lambda bi, h, i: (bi, 0, h)),
                  pl.BlockSpec((1, s, HEAD_W), lambda bi, h, i: (bi, 0, h)),
                  pl.BlockSpec((4, HEAD_DIM), lambda bi, h, i: (0, 0)),
                  pl.BlockSpec((1, HEAD_W), lambda bi, h, i: (0, 0))],
        out_specs=[pl.BlockSpec((1, TQ, HEAD_W), lambda bi, h, i: (bi, i, h)),
                   pl.BlockSpec((zrows, 128), lambda bi, h, i: ((bi * N_HEADS + h) * nq + i, 0))],
        out_shape=[jax.ShapeDtypeStruct((b, s, QK_W), BF16),
                   jax.ShapeDtypeStruct((zero_rows, 128), U32)],
        scratch_shapes=[pltpu.VMEM((2, TQ, HEAD_W), F32)] * 3,
        compiler_params=_cparams("parallel", "parallel", "arbitrary"),
        name="diffattn",
    )(q, kc, vc, kl, vl, lam_p, g_subln)


def _dft1_kernel(y_ref, g_ref, br_ref, bi_ref):
    rows = DFT_N1 * F1_NB
    yb = y_ref[0].reshape(rows, 2 * F_W).astype(BF16)
    p = jnp.dot(g_ref[0], yb, preferred_element_type=F32)
    top, bot = p[:rows], p[rows:]
    br_ref[0] = (top[:, :F_W] - bot[:, F_W:]).reshape(DFT_N1, F1_NB, F_W)
    bi_ref[0] = (-top[:, F_W:] - bot[:, :F_W]).reshape(DFT_N1, F1_NB, F_W)


def _dft1(y4, gmat):
    b = y4.shape[0]
    rows = DFT_N1 * F1_NB
    return pl.pallas_call(
        _dft1_kernel,
        grid=(DFT_N2 // F1_NB, b),
        in_specs=[pl.BlockSpec((1, DFT_N1, F1_NB, 2 * F_W), lambda j, bi: (bi, 0, j, 0)),
                  pl.BlockSpec((1, 2 * rows, rows), lambda j, bi: (j, 0, 0))],
        out_specs=[pl.BlockSpec((1, DFT_N1, F1_NB, F_W), lambda j, bi: (bi, 0, j, 0)),
                   pl.BlockSpec((1, DFT_N1, F1_NB, F_W), lambda j, bi: (bi, 0, j, 0))],
        out_shape=[jax.ShapeDtypeStruct((b, DFT_N1, DFT_N2, F_W), F32),
                   jax.ShapeDtypeStruct((b, DFT_N1, DFT_N2, F_W), F32)],
        compiler_params=_cparams("arbitrary", "arbitrary"),
        name="dft1",
    )(y4, gmat)


def _dft2_kernel(br_ref, bi_ref, c_ref, s_ref, o_ref):
    for j in range(F2_KB):
        r = (jnp.dot(c_ref[...], br_ref[0, j].astype(BF16), preferred_element_type=F32)
             + jnp.dot(s_ref[...], bi_ref[0, j].astype(BF16), preferred_element_type=F32))
        o_ref[0, :, j, :] = r


def _dft2(br4, bi4, c2, s2):
    b = br4.shape[0]
    blk = (1, F2_KB, DFT_N2, F_W)
    return pl.pallas_call(
        _dft2_kernel,
        grid=(b, DFT_N1 // F2_KB),
        in_specs=[pl.BlockSpec(blk, lambda bi, j: (bi, j, 0, 0)),
                  pl.BlockSpec(blk, lambda bi, j: (bi, j, 0, 0)),
                  pl.BlockSpec((DFT_N2, DFT_N2), lambda bi, j: (0, 0)),
                  pl.BlockSpec((DFT_N2, DFT_N2), lambda bi, j: (0, 0))],
        out_specs=pl.BlockSpec((1, DFT_N2, F2_KB, F_W), lambda bi, j: (bi, 0, j, 0)),
        out_shape=jax.ShapeDtypeStruct((b, DFT_N2, DFT_N1, F_W), F32),
        compiler_params=_cparams("parallel", "arbitrary"),
        name="dft2",
    )(br4, bi4, c2, s2)


def _outproj_kernel(o_ref, f_ref, x_ref, mod_ref, g_ref, wo_ref, wr_ref, br_ref, tri_ref,
                    x1_ref, h3_ref, meta_ref, gate_ref, cnt_out_ref, cnt_ref):
    first = (pl.program_id(0) == 0) & (pl.program_id(1) == 0)

    @pl.when(first)
    def _():
        cnt_ref[...] = jnp.zeros(cnt_ref.shape, F32)

    m = mod_ref[0]
    mix = (jnp.dot(o_ref[0], wo_ref[0:QK_W, :], preferred_element_type=F32)
           + jnp.dot(f_ref[0].astype(BF16), wo_ref[QK_W:, :], preferred_element_type=F32))
    x1 = x_ref[0] + m[2:3] * mix
    x1_ref[0] = x1
    h2 = _norm_mod(x1, g_ref[...], m[3:4], m[4:5])
    _store_rows(h3_ref, h2)
    h_hi = h2.astype(BF16)
    h_lo = (h2 - h_hi.astype(F32)).astype(BF16)
    hw = jnp.dot(h_hi, wr_ref[...], preferred_element_type=F32)
    lg = (hw[:, :128] + hw[:, 128:]
          + jnp.dot(h_lo, wr_ref[:, 0:128], preferred_element_type=F32)
          + br_ref[...])
    lane = lax.broadcasted_iota(jnp.int32, lg.shape, 1)
    lanef = lane.astype(F32)
    ninf = jnp.float32(-jnp.inf)
    big = jnp.float32(lg.shape[1])
    gl = jnp.where(lane < N_GROUPS, lg, ninf)
    gmax = jnp.max(gl, axis=-1, keepdims=True)
    grp = jnp.min(jnp.where(gl == gmax, lanef, big), axis=-1, keepdims=True)
    pg = 1.0 / jnp.sum(jnp.exp(gl - gmax), axis=-1, keepdims=True)
    e_lane = lane - N_GROUPS
    lane_grp = (e_lane >> 3).astype(F32)
    emask = (e_lane >= 0) & (e_lane < N_EXPERTS) & (lane_grp == grp)
    el = jnp.where(emask, lg, ninf)
    t1 = jnp.max(el, axis=-1, keepdims=True)
    i1 = jnp.min(jnp.where(el == t1, lanef, big), axis=-1, keepdims=True)
    el2 = jnp.where(lanef == i1, ninf, el)
    t2 = jnp.max(el2, axis=-1, keepdims=True)
    i2 = jnp.min(jnp.where(el2 == t2, lanef, big), axis=-1, keepdims=True)
    dd = jnp.exp(t2 - t1)
    w1 = pg / (1.0 + dd)
    w2 = pg * dd / (1.0 + dd)
    gate_ref[...] = jnp.where(lane == 0, w1, jnp.where(lane == 1, w2, 0.0))
    hit1 = lanef == i1
    hit2 = lanef == i2
    oh = jnp.where(hit1 | hit2, 1.0, 0.0)
    before = jnp.dot(tri_ref[...], oh.astype(BF16), preferred_element_type=F32) + cnt_ref[0:1, :]
    r1 = jnp.sum(jnp.where(hit1, before, 0.0), axis=-1, keepdims=True)
    r2 = jnp.sum(jnp.where(hit2, before, 0.0), axis=-1, keepdims=True)
    cnt_ref[0:1, :] = cnt_ref[0:1, :] + jnp.sum(oh, axis=0, keepdims=True)
    cnt_out_ref[...] = cnt_ref[...]
    meta = jnp.where(lane == 0, i1 - N_GROUPS,
                     jnp.where(lane == 1, i2 - N_GROUPS,
                               jnp.where(lane == 2, r1, jnp.where(lane == 3, r2, 0.0))))
    meta_ref[...] = jnp.transpose(meta)[0:8, :].astype(jnp.int32)


def _outproj(attn_o, four, x, mods, g_ffn, w_out, w_r, b_r):
    b, s, d = x.shape
    tm = TM_PROJ
    nt = s // tm
    tok = lambda bi, i: (bi, i, 0)
    flat = lambda bi, i: (bi * nt + i, 0)
    tri = jnp.asarray(np.tril(np.ones((tm, tm), np.float32), -1)).astype(BF16)
    return pl.pallas_call(
        _outproj_kernel,
        grid=(b, nt),
        in_specs=[pl.BlockSpec((1, tm, QK_W), tok),
                  pl.BlockSpec((1, tm, F_W), tok),
                  pl.BlockSpec((1, tm, d), tok),
                  pl.BlockSpec((1, N_MOD, d), lambda bi, i: (bi, 0, 0)),
                  pl.BlockSpec((1, d), lambda bi, i: (0, 0)),
                  pl.BlockSpec(w_out.shape, lambda bi, i: (0, 0)),
                  pl.BlockSpec(w_r.shape, lambda bi, i: (0, 0)),
                  pl.BlockSpec(b_r.shape, lambda bi, i: (0, 0)),
                  pl.BlockSpec((tm, tm), lambda bi, i: (0, 0))],
        out_specs=[pl.BlockSpec((1, tm, d), tok),
                   pl.BlockSpec((tm * ROW_TILE, 128), flat),
                   pl.BlockSpec((8, tm), lambda bi, i: (0, bi * nt + i)),
                   pl.BlockSpec((tm, 128), flat),
                   pl.BlockSpec((8, 128), lambda bi, i: (0, 0))],
        out_shape=[jax.ShapeDtypeStruct((b, s, d), F32),
                   jax.ShapeDtypeStruct((b * s * ROW_TILE, 128), U32),
                   jax.ShapeDtypeStruct((8, b * s), jnp.int32),
                   jax.ShapeDtypeStruct((b * s, 128), F32),
                   jax.ShapeDtypeStruct((8, 128), F32)],
        scratch_shapes=[pltpu.VMEM((8, 128), F32)],
        compiler_params=_cparams("arbitrary", "arbitrary"),
        name="outproj",
    )(attn_o, four, x, mods, g_ffn, w_out, w_r, b_r, tri)


def _destrows_kernel(start_ref, meta_ref, o_ref):
    meta = meta_ref[...]
    row = lax.broadcasted_iota(jnp.int32, meta.shape, 0)
    base = jnp.zeros_like(meta)
    for e in range(N_EXPERTS):
        base = jnp.where(meta == e, start_ref[e] * MOE_BLK, base)
    ranks = pltpu.roll(meta, meta.shape[0] - TOP_K, 0)
    o_ref[...] = jnp.where(row < TOP_K, base + ranks, 0)


def _destrows(blk_start, meta):
    return pl.pallas_call(
        _destrows_kernel,
        grid_spec=pltpu.PrefetchScalarGridSpec(
            num_scalar_prefetch=1,
            grid=(1,),
            in_specs=[pl.BlockSpec(meta.shape, lambda i, st: (0, 0))],
            out_specs=pl.BlockSpec(meta.shape, lambda i, st: (0, 0))),
        out_shape=jax.ShapeDtypeStruct(meta.shape, jnp.int32),
        compiler_params=_cparams("arbitrary"),
        name="destrows",
    )(blk_start, meta)


def _row_slice(row):
    return pl.ds(pl.multiple_of(row * ROW_TILE, ROW_TILE), ROW_TILE)


def _row_copy(src_ref, src_row, dst_ref, dst_row, sem):
    return pltpu.make_async_copy(src_ref.at[_row_slice(src_row)], dst_ref.at[_row_slice(dst_row)], sem)


def _bf16_bits(x):
    return lax.bitcast_convert_type(x.astype(BF16).astype(F32), U32)


def _rows_2d(ref, n_rows):
    hi, lo = [], []
    for cblk in range(ROW_TILE):
        w = ref[pl.ds(cblk, n_rows, stride=ROW_TILE), :]
        hi.append(lax.bitcast_convert_type(w & jnp.uint32(0xFFFF0000), F32).astype(BF16))
        lo.append(lax.bitcast_convert_type(w << 16, F32).astype(BF16))
    return jnp.concatenate(hi + lo, axis=1)


def _store_rows(ref, val):
    for cblk in range(ROW_TILE):
        hi = _bf16_bits(val[:, cblk * 128:(cblk + 1) * 128])
        lo = _bf16_bits(val[:, (cblk + ROW_TILE) * 128:(cblk + ROW_TILE + 1) * 128])
        ref[pl.ds(cblk, val.shape[0], stride=ROW_TILE), :] = hi | (lo >> 16)


def _dispatch_kernel(dest_ref, h_ref, zeros_ref, xs_ref, sem):
    del zeros_ref
    n = 2 * TD_DISPATCH

    def issue(r, c):
        _row_copy(h_ref, r, xs_ref, dest_ref[0, 0, 2 * r], sem).start(priority=0)
        _row_copy(h_ref, r, xs_ref, dest_ref[0, 0, 2 * r + 1], sem).start(priority=1)
        return c

    lax.fori_loop(0, TD_DISPATCH, issue, 0, unroll=DMA_UNROLL // 2)

    def drain(a, c):
        _row_copy(h_ref, 0, xs_ref, 0, sem).wait()
        return c

    lax.fori_loop(0, n, drain, 0, unroll=DMA_UNROLL)


def _dispatch(dest3, h3, xs_zeros):
    t = h3.shape[0] // ROW_TILE
    return pl.pallas_call(
        _dispatch_kernel,
        grid=(t // TD_DISPATCH,),
        in_specs=[pl.BlockSpec((1, 1, 2 * TD_DISPATCH), lambda i: (i, 0, 0),
                               memory_space=pltpu.SMEM),
                  pl.BlockSpec((TD_DISPATCH * ROW_TILE, 128), lambda i: (i, 0)),
                  pl.BlockSpec(memory_space=pl.ANY)],
        out_specs=pl.BlockSpec(memory_space=pl.ANY),
        out_shape=jax.ShapeDtypeStruct(xs_zeros.shape, xs_zeros.dtype),
        scratch_shapes=[pltpu.SemaphoreType.DMA(())],
        input_output_aliases={2: 0},
        compiler_params=_cparams("arbitrary"),
        name="dispatch",
    )(dest3, h3, xs_zeros)


def _experts_kernel(be_ref, nu_ref, nxt_ref, xs_ref, wg_hbm, wu_hbm, wd_hbm, ys_ref,
                    wgb, wub, wdb, sg, su, sd, sem):
    j = pl.program_id(0)
    used = j < nu_ref[0]
    e = be_ref[j]
    new_expert = (j == 0) | (e != be_ref[jnp.maximum(j - 1, 0)])
    e_next = nxt_ref[e]

    def weight_copies(idx):
        return (pltpu.make_async_copy(wg_hbm.at[idx], sg, sem.at[0]),
                pltpu.make_async_copy(wu_hbm.at[idx], su, sem.at[1]),
                pltpu.make_async_copy(wd_hbm.at[idx], sd, sem.at[2]))

    @pl.when(j == 0)
    def _():
        for cp in weight_copies(e):
            cp.start(priority=1)

    @pl.when(used & new_expert)
    def _():
        for cp in weight_copies(e):
            cp.wait()
        wgb[...] = sg[...].astype(BF16)
        wub[...] = su[...].astype(BF16)
        wdb[...] = sd[...].astype(BF16)

    @pl.when(used & new_expert & (e_next != e))
    def _():
        for cp in weight_copies(e_next):
            cp.start(priority=1)

    @pl.when(used)
    def _():
        xb = _rows_2d(xs_ref, MOE_BLK)
        gate = jnp.dot(xb, wgb[...], preferred_element_type=F32)
        up = jnp.dot(xb, wub[...], preferred_element_type=F32)
        hid = (gate * jax.nn.sigmoid(gate) * up).astype(BF16)
        _store_rows(ys_ref, jnp.dot(hid, wdb[...], preferred_element_type=F32))

    @pl.when(pl.program_id(0) >= nu_ref[0])
    def _():
        ys_ref[...] = jnp.zeros_like(ys_ref)


def _experts(blk_expert, n_used, next_expert, xs, wg, wu, wd):
    rows = xs.shape[0] // ROW_TILE
    d = wg.shape[1]
    nb = rows // MOE_BLK
    row_blk = lambda j, be, nu, nx: (jnp.minimum(j, nu[0] - 1), 0)
    out_blk = lambda j, be, nu, nx: (j, 0)
    return pl.pallas_call(
        _experts_kernel,
        grid_spec=pltpu.PrefetchScalarGridSpec(
            num_scalar_prefetch=3,
            grid=(nb,),
            in_specs=[pl.BlockSpec((MOE_BLK * ROW_TILE, 128), row_blk),
                      pl.BlockSpec(memory_space=pl.ANY),
                      pl.BlockSpec(memory_space=pl.ANY),
                      pl.BlockSpec(memory_space=pl.ANY)],
            out_specs=pl.BlockSpec((MOE_BLK * ROW_TILE, 128), out_blk),
            scratch_shapes=[pltpu.VMEM((d, D_EXPERT), BF16), pltpu.VMEM((d, D_EXPERT), BF16),
                            pltpu.VMEM((D_EXPERT, d), BF16),
                            pltpu.VMEM((d, D_EXPERT), F32), pltpu.VMEM((d, D_EXPERT), F32),
                            pltpu.VMEM((D_EXPERT, d), F32),
                            pltpu.SemaphoreType.DMA((3,))]),
        out_shape=jax.ShapeDtypeStruct(xs.shape, xs.dtype),
        compiler_params=_cparams("arbitrary"),
        name="experts",
    )(blk_expert, n_used, next_expert, xs, wg, wu, wd)


def _combine_kernel(dest_ref, dest_next_ref, ys_ref, x1_ref, gate_ref, mod_ref, g_ref, o_ref,
                    ya, yb, sem):
    step = pl.program_id(0) * pl.num_programs(1) + pl.program_id(1)
    n_steps = pl.num_programs(0) * pl.num_programs(1)
    slot = step % 2

    def start_gathers(idx_ref, to_slot):
        def issue(r, c):
            _row_copy(ys_ref, idx_ref[0, 0, 2 * r], ya.at[to_slot], r,
                      sem.at[to_slot]).start(priority=0)
            _row_copy(ys_ref, idx_ref[0, 0, 2 * r + 1], yb.at[to_slot], r,
                      sem.at[to_slot]).start(priority=1)
            return c

        lax.fori_loop(0, TD, issue, 0, unroll=DMA_UNROLL)

    @pl.when(step == 0)
    def _():
        start_gathers(dest_ref, 0)

    @pl.when(step + 1 < n_steps)
    def _():
        start_gathers(dest_next_ref, 1 - slot)

    def drain(r, c):
        _row_copy(ys_ref, 0, ya.at[slot], 0, sem.at[slot]).wait()
        _row_copy(ys_ref, 0, yb.at[slot], 0, sem.at[slot]).wait()
        return c

    lax.fori_loop(0, TD, drain, 0, unroll=DMA_UNROLL)
    gt = gate_ref[...]
    moe = (gt[:, 0:1] * _rows_2d(ya.at[slot], TD).astype(F32)
           + gt[:, 1:2] * _rows_2d(yb.at[slot], TD).astype(F32))
    x2 = x1_ref[0] + mod_ref[0][5:6] * moe
    ms = jnp.mean(x2 * x2, axis=-1, keepdims=True)
    o_ref[0] = x2 * lax.rsqrt(ms + EPS) * g_ref[...]


def _combine(dest3, ys, x1, gates, mods, g_final):
    b, s, d = x1.shape
    nt = s // TD
    return pl.pallas_call(
        _combine_kernel,
        grid=(b, nt),
        in_specs=[pl.BlockSpec((1, 1, 2 * TD), lambda bi, i: (bi * nt + i, 0, 0),
                               memory_space=pltpu.SMEM),
                  pl.BlockSpec((1, 1, 2 * TD),
                               lambda bi, i: (jnp.minimum(bi * nt + i + 1, b * nt - 1), 0, 0),
                               memory_space=pltpu.SMEM),
                  pl.BlockSpec(memory_space=pl.ANY),
                  pl.BlockSpec((1, TD, d), lambda bi, i: (bi, i, 0)),
                  pl.BlockSpec((TD, 128), lambda bi, i: (bi * nt + i, 0)),
                  pl.BlockSpec((1, N_MOD, d), lambda bi, i: (bi, 0, 0)),
                  pl.BlockSpec((1, d), lambda bi, i: (0, 0))],
        out_specs=pl.BlockSpec((1, TD, d), lambda bi, i: (bi, i, 0)),
        out_shape=jax.ShapeDtypeStruct((b, s, d), F32),
        scratch_shapes=[pltpu.VMEM((2, TD * ROW_TILE, 128), U32),
                        pltpu.VMEM((2, TD * ROW_TILE, 128), U32),
                        pltpu.SemaphoreType.DMA((2,))],
        compiler_params=_cparams("arbitrary", "arbitrary"),
        name="combine",
    )(dest3, dest3, ys, x1, gates, mods, g_final)


@functools.lru_cache(maxsize=None)
def _rope_tables(rows):
    r, col = np.meshgrid(np.arange(rows), np.arange(GRID_W), indexing='ij')
    pos = np.stack([r.reshape(-1), col.reshape(-1)], axis=-1).astype(np.float32)
    inv_freq = (np.float32(ROPE_THETA)
                ** (-np.arange(0, ROPE_AXIS, 2, dtype=np.float32) / np.float32(ROPE_AXIS))).astype(np.float32)
    ang = (pos[:, :, None] * inv_freq).astype(np.float32)
    ang = np.concatenate([ang, ang], axis=-1).astype(np.float64)
    n = ang.shape[0]
    cos = np.tile(np.cos(ang).reshape(n, HEAD_DIM), (1, 2)).astype(np.float32)
    sin = np.tile(np.sin(ang).reshape(n, HEAD_DIM), (1, 2)).astype(np.float32)
    upper = (np.arange(HEAD_W) % ROPE_AXIS) >= ROPE_HALF
    sa = np.where(upper, sin, np.float32(0.0))
    sb = np.where(upper, np.float32(0.0), -sin)
    return cos, sa, sb


@functools.lru_cache(maxsize=None)
def _dft_constants(n_pos):
    c = np.arange(FGROUP_DIM)
    ang_c = 2.0 * np.pi * ((c[:, None] * c[None, :]) % FGROUP_DIM) / FGROUP_DIM
    norm = 1.0 / math.sqrt(n_pos * FGROUP_DIM)
    cmat = (np.cos(ang_c) * norm).astype(np.float32)
    smat = (np.sin(ang_c) * norm).astype(np.float32)
    k1 = np.arange(DFT_N1)[None, :, None]
    n1 = np.arange(DFT_N1)[None, None, :]
    n2 = np.arange(DFT_N2)[:, None, None]
    ang_g = 2.0 * np.pi * ((k1 * (DFT_N2 * n1 + n2)) % n_pos) / n_pos
    gsmall = np.stack([np.cos(ang_g), np.sin(ang_g)], axis=1)
    gsmall = gsmall.reshape(DFT_N2 // F1_NB, F1_NB, 2, DFT_N1, DFT_N1)
    gmat = np.einsum('japkn,ab->jpkanb', gsmall, np.eye(F1_NB)).reshape(
        DFT_N2 // F1_NB, 2 * DFT_N1 * F1_NB, DFT_N1 * F1_NB).astype(BF16)
    k2 = np.arange(DFT_N2)
    ang_2 = 2.0 * np.pi * ((k2[:, None] * k2[None, :]) % DFT_N2) / DFT_N2
    c2 = np.cos(ang_2).astype(np.float32)
    s2 = np.sin(ang_2).astype(np.float32)
    return cmat, smat, gmat, c2, s2


def kernel(x, c, ctx, c_ctx, w_ada, b_ada, g_mix_norm, g_ffn_norm, w_in, lambda_q1, lambda_k1, lambda_q2, lambda_k2, g_subln, w_fourier, w_out, w_router_group, b_router_group, w_router_expert, b_router_expert, w_gate, w_up, w_down, g_final):
    b, s, d = x.shape
    t = b * s
    assert d == D_MODEL and s == DFT_N1 * DFT_N2 and s % GRID_W == 0 and b == 2

    cc = jnp.concatenate([c, c_ctx[None, :], jnp.zeros((8 - b - 1, d), F32)], axis=0)
    mods = _adaln(cc.T, w_ada[0], b_ada[0]).reshape(8, N_MOD, d)

    cmat, smat, gmat, c2, s2 = _dft_constants(s)
    wf = _wfold(jnp.asarray(cmat), jnp.asarray(smat), w_fourier[0])
    cos_t, sa_t, sb_t = _rope_tables(s // GRID_W)

    w_in_b = w_in[0].astype(BF16)
    g_mix = g_mix_norm[0].reshape(1, d)
    q, kl, vl, y = _inproj(x, mods, g_mix, w_in_b, wf, cos_t, sa_t, sb_t)
    kc, vc = _ctxproj(ctx, mods, g_mix, w_in_b)

    lam_p = jnp.stack([lambda_q1[0], lambda_k1[0], lambda_q2[0], lambda_k2[0]], axis=0)
    n_blocks = t * TOP_K // MOE_BLK + N_EXPERTS
    attn_o, xs_zeros = _attention(q, kc, vc, kl, vl, lam_p, g_subln[0].reshape(1, HEAD_W),
                                  n_blocks * MOE_BLK * ROW_TILE)

    br, bi = _dft1(y.reshape(b, DFT_N1, DFT_N2, 2 * F_W), jnp.asarray(gmat))
    four = _dft2(br, bi, jnp.asarray(c2).astype(BF16),
                 jnp.asarray(s2).astype(BF16)).reshape(b, s, F_W)

    n_r = N_GROUPS + N_EXPERTS
    w_r = jnp.concatenate([w_router_group[0], w_router_expert[0],
                           jnp.zeros((d, 128 - n_r), F32)], axis=1)
    b_r = jnp.concatenate([b_router_group[0], b_router_expert[0],
                           jnp.zeros((128 - n_r,), F32)]).reshape(1, 128)
    w_r_hi = w_r.astype(BF16)
    w_r_lo = (w_r - w_r_hi.astype(F32)).astype(BF16)
    x1, h3, meta, gates, cnt = _outproj(attn_o, four, x, mods, g_ffn_norm[0].reshape(1, d),
                                        w_out[0].astype(BF16),
                                        jnp.concatenate([w_r_hi, w_r_lo], axis=1), b_r)

    counts = cnt[0, N_GROUPS:N_GROUPS + N_EXPERTS].astype(jnp.int32)
    nblk = (counts + MOE_BLK - 1) // MOE_BLK
    blk_end = jnp.cumsum(nblk)
    blk_start = (blk_end - nblk).astype(jnp.int32)
    drows = _destrows(blk_start, meta)
    dest = jnp.stack([drows[0], drows[1]], axis=1)
    blk_ids = jnp.arange(n_blocks, dtype=jnp.int32)
    blk_expert = jnp.minimum(
        jnp.sum((blk_end[None, :] <= blk_ids[:, None]).astype(jnp.int32), axis=1),
        N_EXPERTS - 1).astype(jnp.int32)
    n_used = blk_end[-1:].astype(jnp.int32)
    dest3 = dest.reshape(t // TD, 1, 2 * TD)

    xs = _dispatch(dest.reshape(t // TD_DISPATCH, 1, 2 * TD_DISPATCH), h3, xs_zeros)
    e_ids = jnp.arange(N_EXPERTS, dtype=jnp.int32)
    later = (e_ids[None, :] > e_ids[:, None]) & (nblk[None, :] > 0)
    next_expert = jnp.min(jnp.where(later, e_ids[None, :], N_EXPERTS), axis=1)
    next_expert = jnp.where(next_expert == N_EXPERTS, e_ids, next_expert).astype(jnp.int32)
    ys = _experts(blk_expert, n_used, next_expert, xs, w_gate[0], w_up[0], w_down[0])
    return _combine(dest3, ys, x1, gates, mods, g_final.reshape(1, d))
```

```python
import functools
import math

import numpy as np
import jax
import jax.numpy as jnp
from jax import lax
from jax.experimental import pallas as pl
from jax.experimental.pallas import tpu as pltpu

F32 = jnp.float32
BF16 = jnp.bfloat16

D_MODEL = 1024
GRID_W = 64
N_HEADS = 4
HEAD_DIM = 64
HEAD_W = 2 * HEAD_DIM
QK_W = N_HEADS * HEAD_W
N_FGROUPS = 4
FGROUP_DIM = 128
F_W = N_FGROUPS * FGROUP_DIM
ROPE_THETA = 10000.0
ROPE_AXIS = HEAD_DIM // 2
ROPE_HALF = ROPE_AXIS // 2
N_GROUPS = 4
EXPERTS_PER_GROUP = 8
N_EXPERTS = N_GROUPS * EXPERTS_PER_GROUP
TOP_K = 2
D_EXPERT = 512
N_MOD = 6
N_COND = 3
EPS = 1e-6
LAMBDA_INIT = 0.8 - 0.6 * math.exp(-0.3 * 0)
LOG2_E = 1.4426950408889634

DFT_N1 = 64
DFT_N2 = 128

TM_PROJ = 512
TM_INPROJ = 1024
TQ = 1024
TK = 2048
F1_NB = 8
F2_KB = 16
MOE_BLK = 512
ROW_TILE = D_MODEL // 256
U32 = jnp.uint32
TD = 512
TD_DISPATCH = 1024
DMA_UNROLL = 16
VMEM_LIMIT = 48 * 1024 * 1024


def _cparams(*sem):
    return pltpu.CompilerParams(dimension_semantics=sem, vmem_limit_bytes=VMEM_LIMIT)


def _adaln_kernel(c_ref, w_ref, b_ref, o_ref):
    cc = c_ref[...]
    s = cc * jax.nn.sigmoid(cc)
    w = w_ref[...]
    o_ref[...] = jnp.zeros(o_ref.shape, F32)
    for r in range(N_COND):
        o_ref[r:r + 1, :] = jnp.sum(w * s[:, r:r + 1], axis=0, keepdims=True) + b_ref[...]


def _adaln(cc, w_ada, b_ada):
    n = w_ada.shape[1]
    tn = 1536
    return pl.pallas_call(
        _adaln_kernel,
        grid=(n // tn,),
        in_specs=[pl.BlockSpec((D_MODEL, 8), lambda j: (0, 0)),
                  pl.BlockSpec((D_MODEL, tn), lambda j: (0, j)),
                  pl.BlockSpec((1, tn), lambda j: (0, j))],
        out_specs=pl.BlockSpec((8, tn), lambda j: (0, j)),
        out_shape=jax.ShapeDtypeStruct((8, n), F32),
        compiler_params=_cparams("arbitrary"),
        name="adaln",
    )(cc, w_ada, b_ada.reshape(1, n))


def _wfold_kernel(c_ref, s_ref, w_ref, o_ref):
    w = w_ref[0]
    o_ref[0, :, :FGROUP_DIM] = jnp.dot(c_ref[...], w, preferred_element_type=F32,
                                       precision=lax.Precision.HIGHEST).astype(BF16)
    o_ref[0, :, FGROUP_DIM:] = jnp.dot(s_ref[...], w, preferred_element_type=F32,
                                       precision=lax.Precision.HIGHEST).astype(BF16)


def _wfold(cmat, smat, w_fourier):
    return pl.pallas_call(
        _wfold_kernel,
        grid=(N_FGROUPS,),
        in_specs=[pl.BlockSpec((FGROUP_DIM, FGROUP_DIM), lambda g: (0, 0)),
                  pl.BlockSpec((FGROUP_DIM, FGROUP_DIM), lambda g: (0, 0)),
                  pl.BlockSpec((1, FGROUP_DIM, FGROUP_DIM), lambda g: (g, 0, 0))],
        out_specs=pl.BlockSpec((1, FGROUP_DIM, 2 * FGROUP_DIM), lambda g: (g, 0, 0)),
        out_shape=jax.ShapeDtypeStruct((N_FGROUPS, FGROUP_DIM, 2 * FGROUP_DIM), BF16),
        compiler_params=_cparams("arbitrary"),
        name="wfold",
    )(cmat, smat, w_fourier)


def _norm_mod(x, g, shift, scale):
    ms = jnp.mean(x * x, axis=-1, keepdims=True)
    y = x * lax.rsqrt(ms + EPS) * g
    return y * (1.0 + scale) + shift


def _rope_slab(p, cos, sa, sb):
    return (p * cos + pltpu.roll(p, ROPE_HALF, 1) * sa
            + pltpu.roll(p, HEAD_W - ROPE_HALF, 1) * sb)


def _inproj_kernel(x_ref, mod_ref, g_ref, w_ref, wf_ref, cos_ref, sa_ref, sb_ref,
                   q_ref, k_ref, v_ref, y_ref):
    m = mod_ref[0]
    h = _norm_mod(x_ref[0], g_ref[...], m[0:1], m[1:2]).astype(BF16)
    cos, sa, sb = cos_ref[...], sa_ref[...], sb_ref[...]
    scale = HEAD_DIM ** -0.5 * LOG2_E
    pq = jnp.dot(h, w_ref[:, 0:QK_W], preferred_element_type=F32)
    for hh in range(N_HEADS):
        sl = slice(hh * HEAD_W, (hh + 1) * HEAD_W)
        q_ref[0, :, sl] = (_rope_slab(pq[:, sl], cos, sa, sb) * scale).astype(BF16)
    pk = jnp.dot(h, w_ref[:, QK_W:2 * QK_W], preferred_element_type=F32)
    for hh in range(N_HEADS):
        sl = slice(hh * HEAD_W, (hh + 1) * HEAD_W)
        k_ref[0, :, sl] = _rope_slab(pk[:, sl], cos, sa, sb).astype(BF16)
    v_ref[0] = jnp.dot(h, w_ref[:, 2 * QK_W:3 * QK_W], preferred_element_type=F32).astype(BF16)
    pf = jnp.dot(h, w_ref[:, 3 * QK_W:], preferred_element_type=F32).astype(BF16)
    for g in range(N_FGROUPS):
        yy = jnp.dot(pf[:, g * FGROUP_DIM:(g + 1) * FGROUP_DIM], wf_ref[g],
                     preferred_element_type=F32)
        y_ref[0, :, g * FGROUP_DIM:(g + 1) * FGROUP_DIM] = yy[:, :FGROUP_DIM]
        y_ref[0, :, F_W + g * FGROUP_DIM:F_W + (g + 1) * FGROUP_DIM] = yy[:, FGROUP_DIM:]


def _inproj(x, mods, g_mix, w_in, wf, cos_t, sa_t, sb_t):
    b, s, d = x.shape
    tm = TM_INPROJ
    tok = lambda bi, i: (bi, i, 0)
    return pl.pallas_call(
        _inproj_kernel,
        grid=(b, s // tm),
        in_specs=[pl.BlockSpec((1, tm, d), tok),
                  pl.BlockSpec((1, N_MOD, d), lambda bi, i: (bi, 0, 0)),
                  pl.BlockSpec((1, d), lambda bi, i: (0, 0)),
                  pl.BlockSpec(w_in.shape, lambda bi, i: (0, 0)),
                  pl.BlockSpec(wf.shape, lambda bi, i: (0, 0, 0)),
                  pl.BlockSpec((tm, HEAD_W), lambda bi, i: (i, 0)),
                  pl.BlockSpec((tm, HEAD_W), lambda bi, i: (i, 0)),
                  pl.BlockSpec((tm, HEAD_W), lambda bi, i: (i, 0))],
        out_specs=[pl.BlockSpec((1, tm, QK_W), tok),
                   pl.BlockSpec((1, tm, QK_W), tok),
                   pl.BlockSpec((1, tm, QK_W), tok),
                   pl.BlockSpec((1, tm, 2 * F_W), tok)],
        out_shape=[jax.ShapeDtypeStruct((b, s, QK_W), BF16),
                   jax.ShapeDtypeStruct((b, s, QK_W), BF16),
                   jax.ShapeDtypeStruct((b, s, QK_W), BF16),
                   jax.ShapeDtypeStruct((b, s, 2 * F_W), F32)],
        compiler_params=_cparams("parallel", "arbitrary"),
        name="inproj",
    )(x, mods, g_mix, w_in, wf, cos_t, sa_t, sb_t)


def _ctxproj_kernel(x_ref, mod_ref, g_ref, w_ref, k_ref, v_ref):
    m = mod_ref[0]
    h = _norm_mod(x_ref[0], g_ref[...], m[0:1], m[1:2]).astype(BF16)
    k_ref[0] = jnp.dot(h, w_ref[:, QK_W:2 * QK_W], preferred_element_type=F32).astype(BF16)
    v_ref[0] = jnp.dot(h, w_ref[:, 2 * QK_W:3 * QK_W], preferred_element_type=F32).astype(BF16)


def _ctxproj(ctx, mods, g_mix, w_in):
    b, n, d = ctx.shape
    return pl.pallas_call(
        _ctxproj_kernel,
        grid=(b,),
        in_specs=[pl.BlockSpec((1, n, d), lambda bi: (bi, 0, 0)),
                  pl.BlockSpec((1, N_MOD, d), lambda bi: (2, 0, 0)),
                  pl.BlockSpec((1, d), lambda bi: (0, 0)),
                  pl.BlockSpec(w_in.shape, lambda bi: (0, 0))],
        out_specs=[pl.BlockSpec((1, n, QK_W), lambda bi: (bi, 0, 0)),
                   pl.BlockSpec((1, n, QK_W), lambda bi: (bi, 0, 0))],
        out_shape=[jax.ShapeDtypeStruct((b, n, QK_W), BF16),
                   jax.ShapeDtypeStruct((b, n, QK_W), BF16)],
        compiler_params=_cparams("arbitrary"),
        name="ctxproj",
    )(ctx, mods, g_mix, w_in)


def _attn_kernel(q_ref, kc_ref, vc_ref, kl_ref, vl_ref, lam_ref, g_ref, o_ref, z_ref,
                 m_ref, l_ref, acc_ref):
    q = q_ref[0]
    lane = lax.broadcasted_iota(jnp.int32, q.shape, 1)
    zero = jnp.zeros_like(q)
    q0 = jnp.where(lane < HEAD_DIM, q, zero)
    q1 = jnp.where(lane >= HEAD_DIM, q, zero)
    qs = (q0, q1)
    contract_last = (((1,), (1,)), ((), ()))
    m_ref[...] = jnp.full(m_ref.shape, -1e30, F32)
    l_ref[...] = jnp.zeros(l_ref.shape, F32)
    acc_ref[...] = jnp.zeros(acc_ref.shape, F32)

    def step(kb, vb):
        nk = kb.shape[0] // HEAD_W
        for mi in range(2):
            s = lax.dot_general(qs[mi], kb, contract_last, preferred_element_type=F32)
            m_old = m_ref[mi]
            m_new = jnp.maximum(m_old, jnp.max(s, axis=-1, keepdims=True))
            alpha = jnp.exp2(m_old - m_new)
            p = jnp.exp2(s - jnp.concatenate([m_new] * nk, axis=1))
            psum = p[:, 0:HEAD_W]
            for cblk in range(1, nk):
                psum = psum + p[:, cblk * HEAD_W:(cblk + 1) * HEAD_W]
            l_ref[mi] = alpha * l_ref[mi] + psum
            acc_ref[mi] = alpha * acc_ref[mi] + jnp.dot(p.astype(BF16), vb,
                                                       preferred_element_type=F32)
            m_ref[mi] = m_new

    step(kc_ref[0], vc_ref[0])

    def body(i, c):
        off = pl.multiple_of(i * TK, TK)
        step(kl_ref[0, pl.ds(off, TK), :], vl_ref[0, pl.ds(off, TK), :])
        return c

    lax.fori_loop(0, kl_ref.shape[1] // TK, body, 0)

    lp = lam_ref[...]
    t1 = jnp.sum(lp[0:1] * lp[1:2], axis=-1, keepdims=True)
    t2 = jnp.sum(lp[2:3] * lp[3:4], axis=-1, keepdims=True)
    lam = jnp.exp(t1) - jnp.exp(t2) + LAMBDA_INIT
    l0 = jnp.sum(l_ref[0], axis=-1, keepdims=True)
    l1 = jnp.sum(l_ref[1], axis=-1, keepdims=True)
    o = acc_ref[0] / l0 - lam * (acc_ref[1] / l1)
    ms = jnp.mean(o * o, axis=-1, keepdims=True)
    o = o * lax.rsqrt(ms + EPS) * g_ref[...] * (1.0 - LAMBDA_INIT)
    o_ref[0] = o.astype(BF16)
    z_ref[...] = jnp.zeros(z_ref.shape, U32)


def _attention(q, kc, vc, kl, vl, lam_p, g_subln, zero_rows):
    b, s, _ = q.shape
    n_ctx = kc.shape[1]
    nq = s // TQ
    n_steps = b * N_HEADS * nq
    zrows = zero_rows // n_steps
    assert zrows * n_steps == zero_rows and zrows % 8 == 0
    return pl.pallas_call(
        _attn_kernel,
        grid=(b, N_HEADS, nq),
        in_specs=[pl.BlockSpec((1, TQ, HEAD_W), lambda bi, h, i: (bi, i, h)),
                  pl.BlockSpec((1, n_ctx, HEAD_W), lambda bi, h, i: (bi, 0, h)),
                  pl.BlockSpec((1, n_ctx, HEAD_W), lambda bi, h, i: (bi, 0, h)),
                  pl.BlockSpec((1, s, HEAD_W), lambda bi, h, i: (bi, 0, h)),
                  pl.BlockSpec((1, s, HEAD_W), lambda bi, h, i: (bi, 0, h)),
                  pl.BlockSpec((4, HEAD_DIM), lambda bi, h, i: (0, 0)),
                  pl.BlockSpec((1, HEAD_W), lambda bi, h, i: (0, 0))],
        out_specs=[pl.BlockSpec((1, TQ, HEAD_W), lambda bi, h, i: (bi, i, h)),
                   pl.BlockSpec((zrows, 128), lambda bi, h, i: ((bi * N_HEADS + h) * nq + i, 0))],
        out_shape=[jax.ShapeDtypeStruct((b, s, QK_W), BF16),
                   jax.ShapeDtypeStruct((zero_rows, 128), U32)],
        scratch_shapes=[pltpu.VMEM((2, TQ, HEAD_W), F32)] * 3,
        compiler_params=_cparams("parallel", "parallel", "arbitrary"),
        name="diffattn",
    )(q, kc, vc, kl, vl, lam_p, g_subln)


def _dft1_kernel(y_ref, g_ref, br_ref, bi_ref):
    rows = DFT_N1 * F1_NB
    yb = y_ref[0].reshape(rows, 2 * F_W).astype(BF16)
    p = jnp.dot(g_ref[0], yb, preferred_element_type=F32)
    top, bot = p[:rows], p[rows:]
    br_ref[0] = (top[:, :F_W] - bot[:, F_W:]).reshape(DFT_N1, F1_NB, F_W)
    bi_ref[0] = (-top[:, F_W:] - bot[:, :F_W]).reshape(DFT_N1, F1_NB, F_W)


def _dft1(y4, gmat):
    b = y4.shape[0]
    rows = DFT_N1 * F1_NB
    return pl.pallas_call(
        _dft1_kernel,
        grid=(DFT_N2 // F1_NB, b),
        in_specs=[pl.BlockSpec((1, DFT_N1, F1_NB, 2 * F_W), lambda j, bi: (bi, 0, j, 0)),
                  pl.BlockSpec((1, 2 * rows, rows), lambda j, bi: (j, 0, 0))],
        out_specs=[pl.BlockSpec((1, DFT_N1, F1_NB, F_W), lambda j, bi: (bi, 0, j, 0)),
                   pl.BlockSpec((1, DFT_N1, F1_NB, F_W), lambda j, bi: (bi, 0, j, 0))],
        out_shape=[jax.ShapeDtypeStruct((b, DFT_N1, DFT_N2, F_W), F32),
                   jax.ShapeDtypeStruct((b, DFT_N1, DFT_N2, F_W), F32)],
        compiler_params=_cparams("arbitrary", "arbitrary"),
        name="dft1",
    )(y4, gmat)


def _dft2_kernel(br_ref, bi_ref, c_ref, s_ref, o_ref):
    for j in range(F2_KB):
        r = (jnp.dot(c_ref[...], br_ref[0, j].astype(BF16), preferred_element_type=F32)
             + jnp.dot(s_ref[...], bi_ref[0, j].astype(BF16), preferred_element_type=F32))
        o_ref[0, :, j, :] = r


def _dft2(br4, bi4, c2, s2):
    b = br4.shape[0]
    blk = (1, F2_KB, DFT_N2, F_W)
    return pl.pallas_call(
        _dft2_kernel,
        grid=(b, DFT_N1 // F2_KB),
        in_specs=[pl.BlockSpec(blk, lambda bi, j: (bi, j, 0, 0)),
                  pl.BlockSpec(blk, lambda bi, j: (bi, j, 0, 0)),
                  pl.BlockSpec((DFT_N2, DFT_N2), lambda bi, j: (0, 0)),
                  pl.BlockSpec((DFT_N2, DFT_N2), lambda bi, j: (0, 0))],
        out_specs=pl.BlockSpec((1, DFT_N2, F2_KB, F_W), lambda bi, j: (bi, 0, j, 0)),
        out_shape=jax.ShapeDtypeStruct((b, DFT_N2, DFT_N1, F_W), F32),
        compiler_params=_cparams("parallel", "arbitrary"),
        name="dft2",
    )(br4, bi4, c2, s2)


def _outproj_kernel(o_ref, f_ref, x_ref, mod_ref, g_ref, wo_ref, wr_ref, br_ref, tri_ref,
                    x1_ref, h3_ref, meta_ref, gate_ref, cnt_out_ref, cnt_ref):
    first = (pl.program_id(0) == 0) & (pl.program_id(1) == 0)

    @pl.when(first)
    def _():
        cnt_ref[...] = jnp.zeros(cnt_ref.shape, F32)

    m = mod_ref[0]
    mix = (jnp.dot(o_ref[0], wo_ref[0:QK_W, :], preferred_element_type=F32)
           + jnp.dot(f_ref[0].astype(BF16), wo_ref[QK_W:, :], preferred_element_type=F32))
    x1 = x_ref[0] + m[2:3] * mix
    x1_ref[0] = x1
    h2 = _norm_mod(x1, g_ref[...], m[3:4], m[4:5])
    _store_rows(h3_ref, h2)
    h_hi = h2.astype(BF16)
    h_lo = (h2 - h_hi.astype(F32)).astype(BF16)
    hw = jnp.dot(h_hi, wr_ref[...], preferred_element_type=F32)
    lg = (hw[:, :128] + hw[:, 128:]
          + jnp.dot(h_lo, wr_ref[:, 0:128], preferred_element_type=F32)
          + br_ref[...])
    lane = lax.broadcasted_iota(jnp.int32, lg.shape, 1)
    lanef = lane.astype(F32)
    ninf = jnp.float32(-jnp.inf)
    big = jnp.float32(lg.shape[1])
    gl = jnp.where(lane < N_GROUPS, lg, ninf)
    gmax = jnp.max(gl, axis=-1, keepdims=True)
    grp = jnp.min(jnp.where(gl == gmax, lanef, big), axis=-1, keepdims=True)
    pg = 1.0 / jnp.sum(jnp.exp(gl - gmax), axis=-1, keepdims=True)
    e_lane = lane - N_GROUPS
    lane_grp = (e_lane >> 3).astype(F32)
    emask = (e_lane >= 0) & (e_lane < N_EXPERTS) & (lane_grp == grp)
    el = jnp.where(emask, lg, ninf)
    t1 = jnp.max(el, axis=-1, keepdims=True)
    i1 = jnp.min(jnp.where(el == t1, lanef, big), axis=-1, keepdims=True)
    el2 = jnp.where(lanef == i1, ninf, el)
    t2 = jnp.max(el2, axis=-1, keepdims=True)
    i2 = jnp.min(jnp.where(el2 == t2, lanef, big), axis=-1, keepdims=True)
    dd = jnp.exp(t2 - t1)
    w1 = pg / (1.0 + dd)
    w2 = pg * dd / (1.0 + dd)
    gate_ref[...] = jnp.where(lane == 0, w1, jnp.where(lane == 1, w2, 0.0))
    hit1 = lanef == i1
    hit2 = lanef == i2
    oh = jnp.where(hit1 | hit2, 1.0, 0.0)
    before = jnp.dot(tri_ref[...], oh.astype(BF16), preferred_element_type=F32) + cnt_ref[0:1, :]
    r1 = jnp.sum(jnp.where(hit1, before, 0.0), axis=-1, keepdims=True)
    r2 = jnp.sum(jnp.where(hit2, before, 0.0), axis=-1, keepdims=True)
    cnt_ref[0:1, :] = cnt_ref[0:1, :] + jnp.sum(oh, axis=0, keepdims=True)
    cnt_out_ref[...] = cnt_ref[...]
    meta = jnp.where(lane == 0, i1 - N_GROUPS,
                     jnp.where(lane == 1, i2 - N_GROUPS,
                               jnp.where(lane == 2, r1, jnp.where(lane == 3, r2, 0.0))))
    meta_ref[...] = jnp.transpose(meta)[0:8, :].astype(jnp.int32)


def _outproj(attn_o, four, x, mods, g_ffn, w_out, w_r, b_r):
    b, s, d = x.shape
    tm = TM_PROJ
    nt = s // tm
    tok = lambda bi, i: (bi, i, 0)
    flat = lambda bi, i: (bi * nt + i, 0)
    tri = jnp.asarray(np.tril(np.ones((tm, tm), np.float32), -1)).astype(BF16)
    return pl.pallas_call(
        _outproj_kernel,
        grid=(b, nt),
        in_specs=[pl.BlockSpec((1, tm, QK_W), tok),
                  pl.BlockSpec((1, tm, F_W), tok),
                  pl.BlockSpec((1, tm, d), tok),
                  pl.BlockSpec((1, N_MOD, d), lambda bi, i: (bi, 0, 0)),
                  pl.BlockSpec((1, d), lambda bi, i: (0, 0)),
                  pl.BlockSpec(w_out.shape, lambda bi, i: (0, 0)),
                  pl.BlockSpec(w_r.shape, lambda bi, i: (0, 0)),
                  pl.BlockSpec(b_r.shape, lambda bi, i: (0, 0)),
                  pl.BlockSpec((tm, tm), lambda bi, i: (0, 0))],
        out_specs=[pl.BlockSpec((1, tm, d), tok),
                   pl.BlockSpec((tm * ROW_TILE, 128), flat),
                   pl.BlockSpec((8, tm), lambda bi, i: (0, bi * nt + i)),
                   pl.BlockSpec((tm, 128), flat),
                   pl.BlockSpec((8, 128), lambda bi, i: (0, 0))],
        out_shape=[jax.ShapeDtypeStruct((b, s, d), F32),
                   jax.ShapeDtypeStruct((b * s * ROW_TILE, 128), U32),
                   jax.ShapeDtypeStruct((8, b * s), jnp.int32),
                   jax.ShapeDtypeStruct((b * s, 128), F32),
                   jax.ShapeDtypeStruct((8, 128), F32)],
        scratch_shapes=[pltpu.VMEM((8, 128), F32)],
        compiler_params=_cparams("arbitrary", "arbitrary"),
        name="outproj",
    )(attn_o, four, x, mods, g_ffn, w_out, w_r, b_r, tri)


def _destrows_kernel(start_ref, meta_ref, o_ref):
    meta = meta_ref[...]
    row = lax.broadcasted_iota(jnp.int32, meta.shape, 0)
    base = jnp.zeros_like(meta)
    for e in range(N_EXPERTS):
        base = jnp.where(meta == e, start_ref[e] * MOE_BLK, base)
    ranks = pltpu.roll(meta, meta.shape[0] - TOP_K, 0)
    o_ref[...] = jnp.where(row < TOP_K, base + ranks, 0)


def _destrows(blk_start, meta):
    return pl.pallas_call(
        _destrows_kernel,
        grid_spec=pltpu.PrefetchScalarGridSpec(
            num_scalar_prefetch=1,
            grid=(1,),
            in_specs=[pl.BlockSpec(meta.shape, lambda i, st: (0, 0))],
            out_specs=pl.BlockSpec(meta.shape, lambda i, st: (0, 0))),
        out_shape=jax.ShapeDtypeStruct(meta.shape, jnp.int32),
        compiler_params=_cparams("arbitrary"),
        name="destrows",
    )(blk_start, meta)


def _row_slice(row):
    return pl.ds(pl.multiple_of(row * ROW_TILE, ROW_TILE), ROW_TILE)


def _row_copy(src_ref, src_row, dst_ref, dst_row, sem):
    return pltpu.make_async_copy(src_ref.at[_row_slice(src_row)], dst_ref.at[_row_slice(dst_row)], sem)


def _bf16_bits(x):
    return lax.bitcast_convert_type(x.astype(BF16).astype(F32), U32)


def _rows_2d(ref, n_rows, dtype):
    hi, lo = [], []
    for cblk in range(ROW_TILE):
        w = ref[pl.ds(cblk, n_rows, stride=ROW_TILE), :]
        hi.append(lax.bitcast_convert_type(w & jnp.uint32(0xFFFF0000), F32).astype(dtype))
        lo.append(lax.bitcast_convert_type(w << 16, F32).astype(dtype))
    return jnp.concatenate(hi + lo, axis=1)


def _store_rows(ref, val):
    for cblk in range(ROW_TILE):
        hi = _bf16_bits(val[:, cblk * 128:(cblk + 1) * 128])
        lo = _bf16_bits(val[:, (cblk + ROW_TILE) * 128:(cblk + ROW_TILE + 1) * 128])
        ref[pl.ds(cblk, val.shape[0], stride=ROW_TILE), :] = hi | (lo >> 16)


def _dispatch_kernel(dest_ref, h_ref, zeros_ref, xs_ref, sem):
    del zeros_ref
    n = 2 * TD_DISPATCH

    def issue(r, c):
        _row_copy(h_ref, r, xs_ref, dest_ref[0, 0, 2 * r], sem).start(priority=0)
        _row_copy(h_ref, r, xs_ref, dest_ref[0, 0, 2 * r + 1], sem).start(priority=1)
        return c

    lax.fori_loop(0, TD_DISPATCH, issue, 0, unroll=DMA_UNROLL // 2)

    def drain(a, c):
        _row_copy(h_ref, 0, xs_ref, 0, sem).wait()
        return c

    lax.fori_loop(0, n, drain, 0, unroll=DMA_UNROLL)


def _dispatch(dest3, h3, xs_zeros):
    t = h3.shape[0] // ROW_TILE
    return pl.pallas_call(
        _dispatch_kernel,
        grid=(t // TD_DISPATCH,),
        in_specs=[pl.BlockSpec((1, 1, 2 * TD_DISPATCH), lambda i: (i, 0, 0),
                               memory_space=pltpu.SMEM),
                  pl.BlockSpec((TD_DISPATCH * ROW_TILE, 128), lambda i: (i, 0)),
                  pl.BlockSpec(memory_space=pl.ANY)],
        out_specs=pl.BlockSpec(memory_space=pl.ANY),
        out_shape=jax.ShapeDtypeStruct(xs_zeros.shape, xs_zeros.dtype),
        scratch_shapes=[pltpu.SemaphoreType.DMA(())],
        input_output_aliases={2: 0},
        compiler_params=_cparams("arbitrary"),
        name="dispatch",
    )(dest3, h3, xs_zeros)


def _experts_kernel(be_ref, nu_ref, nxt_ref, xs_ref, wg_hbm, wu_hbm, wd_hbm, ys_ref,
                    wgb, wub, wdb, sg, su, sd, sem):
    j = pl.program_id(0)
    used = j < nu_ref[0]
    e = be_ref[j]
    new_expert = (j == 0) | (e != be_ref[jnp.maximum(j - 1, 0)])
    e_next = nxt_ref[e]

    def weight_copies(idx):
        return (pltpu.make_async_copy(wg_hbm.at[idx], sg, sem.at[0]),
                pltpu.make_async_copy(wu_hbm.at[idx], su, sem.at[1]),
                pltpu.make_async_copy(wd_hbm.at[idx], sd, sem.at[2]))

    @pl.when(j == 0)
    def _():
        for cp in weight_copies(e):
            cp.start(priority=1)

    @pl.when(used & new_expert)
    def _():
        for cp in weight_copies(e):
            cp.wait()
        wgb[...] = sg[...].astype(BF16)
        wub[...] = su[...].astype(BF16)
        wdb[...] = sd[...].astype(BF16)

    @pl.when(used & new_expert & (e_next != e))
    def _():
        for cp in weight_copies(e_next):
            cp.start(priority=1)

    @pl.when(used)
    def _():
        xb = _rows_2d(xs_ref, MOE_BLK, BF16)
        gate = jnp.dot(xb, wgb[...], preferred_element_type=F32)
        up = jnp.dot(xb, wub[...], preferred_element_type=F32)
        hid = (gate * jax.nn.sigmoid(gate) * up).astype(BF16)
        _store_rows(ys_ref, jnp.dot(hid, wdb[...], preferred_element_type=F32))

    @pl.when(pl.program_id(0) >= nu_ref[0])
    def _():
        ys_ref[...] = jnp.zeros_like(ys_ref)


def _experts(blk_expert, n_used, next_expert, xs, wg, wu, wd):
    rows = xs.shape[0] // ROW_TILE
    d = wg.shape[1]
    nb = rows // MOE_BLK
    row_blk = lambda j, be, nu, nx: (jnp.minimum(j, nu[0] - 1), 0)
    out_blk = lambda j, be, nu, nx: (j, 0)
    return pl.pallas_call(
        _experts_kernel,
        grid_spec=pltpu.PrefetchScalarGridSpec(
            num_scalar_prefetch=3,
            grid=(nb,),
            in_specs=[pl.BlockSpec((MOE_BLK * ROW_TILE, 128), row_blk),
                      pl.BlockSpec(memory_space=pl.ANY),
                      pl.BlockSpec(memory_space=pl.ANY),
                      pl.BlockSpec(memory_space=pl.ANY)],
            out_specs=pl.BlockSpec((MOE_BLK * ROW_TILE, 128), out_blk),
            scratch_shapes=[pltpu.VMEM((d, D_EXPERT), BF16), pltpu.VMEM((d, D_EXPERT), BF16),
                            pltpu.VMEM((D_EXPERT, d), BF16),
                            pltpu.VMEM((d, D_EXPERT), F32), pltpu.VMEM((d, D_EXPERT), F32),
                            pltpu.VMEM((D_EXPERT, d), F32),
                            pltpu.SemaphoreType.DMA((3,))]),
        out_shape=jax.ShapeDtypeStruct(xs.shape, xs.dtype),
        compiler_params=_cparams("arbitrary"),
        name="experts",
    )(blk_expert, n_used, next_expert, xs, wg, wu, wd)


def _combine_kernel(dest_ref, dest_next_ref, ys_ref, x1_ref, gate_ref, mod_ref, g_ref, o_ref,
                    ya, yb, sem):
    step = pl.program_id(0) * pl.num_programs(1) + pl.program_id(1)
    n_steps = pl.num_programs(0) * pl.num_programs(1)
    slot = step % 2

    def start_gathers(idx_ref, for_step, to_slot):
        base = (for_step % (TD_DISPATCH // TD)) * (2 * TD)

        def issue(r, c):
            _row_copy(ys_ref, idx_ref[0, 0, base + 2 * r], ya.at[to_slot], r,
                      sem.at[to_slot]).start(priority=0)
            _row_copy(ys_ref, idx_ref[0, 0, base + 2 * r + 1], yb.at[to_slot], r,
                      sem.at[to_slot]).start(priority=1)
            return c

        lax.fori_loop(0, TD, issue, 0, unroll=DMA_UNROLL)

    @pl.when(step == 0)
    def _():
        start_gathers(dest_ref, step, 0)

    @pl.when(step + 1 < n_steps)
    def _():
        start_gathers(dest_next_ref, step + 1, 1 - slot)

    def drain(r, c):
        _row_copy(ys_ref, 0, ya.at[slot], 0, sem.at[slot]).wait()
        _row_copy(ys_ref, 0, yb.at[slot], 0, sem.at[slot]).wait()
        return c

    lax.fori_loop(0, TD, drain, 0, unroll=DMA_UNROLL)
    gt = gate_ref[...]
    moe = (gt[:, 0:1] * _rows_2d(ya.at[slot], TD, F32)
           + gt[:, 1:2] * _rows_2d(yb.at[slot], TD, F32))
    x2 = x1_ref[0] + mod_ref[0][5:6] * moe
    ms = jnp.mean(x2 * x2, axis=-1, keepdims=True)
    o_ref[0] = x2 * lax.rsqrt(ms + EPS) * g_ref[...]


def _combine(dest3, ys, x1, gates, mods, g_final):
    b, s, d = x1.shape
    nt = s // TD
    ratio = TD_DISPATCH // TD
    assert ratio * TD == TD_DISPATCH
    return pl.pallas_call(
        _combine_kernel,
        grid=(b, nt),
        in_specs=[pl.BlockSpec((1, 1, 2 * TD_DISPATCH),
                               lambda bi, i: ((bi * nt + i) // ratio, 0, 0),
                               memory_space=pltpu.SMEM),
                  pl.BlockSpec((1, 1, 2 * TD_DISPATCH),
                               lambda bi, i: (jnp.minimum(bi * nt + i + 1, b * nt - 1) // ratio, 0, 0),
                               memory_space=pltpu.SMEM),
                  pl.BlockSpec(memory_space=pl.ANY),
                  pl.BlockSpec((1, TD, d), lambda bi, i: (bi, i, 0)),
                  pl.BlockSpec((TD, 128), lambda bi, i: (bi * nt + i, 0)),
                  pl.BlockSpec((1, N_MOD, d), lambda bi, i: (bi, 0, 0)),
                  pl.BlockSpec((1, d), lambda bi, i: (0, 0))],
        out_specs=pl.BlockSpec((1, TD, d), lambda bi, i: (bi, i, 0)),
        out_shape=jax.ShapeDtypeStruct((b, s, d), F32),
        scratch_shapes=[pltpu.VMEM((2, TD * ROW_TILE, 128), U32),
                        pltpu.VMEM((2, TD * ROW_TILE, 128), U32),
                        pltpu.SemaphoreType.DMA((2,))],
        compiler_params=_cparams("arbitrary", "arbitrary"),
        name="combine",
    )(dest3, dest3, ys, x1, gates, mods, g_final)


@functools.lru_cache(maxsize=None)
def _rope_tables(rows):
    r, col = np.meshgrid(np.arange(rows), np.arange(GRID_W), indexing='ij')
    pos = np.stack([r.reshape(-1), col.reshape(-1)], axis=-1).astype(np.float32)
    inv_freq = (np.float32(ROPE_THETA)
                ** (-np.arange(0, ROPE_AXIS, 2, dtype=np.float32) / np.float32(ROPE_AXIS))).astype(np.float32)
    ang = (pos[:, :, None] * inv_freq).astype(np.float32)
    ang = np.concatenate([ang, ang], axis=-1).astype(np.float64)
    n = ang.shape[0]
    cos = np.tile(np.cos(ang).reshape(n, HEAD_DIM), (1, 2)).astype(np.float32)
    sin = np.tile(np.sin(ang).reshape(n, HEAD_DIM), (1, 2)).astype(np.float32)
    upper = (np.arange(HEAD_W) % ROPE_AXIS) >= ROPE_HALF
    sa = np.where(upper, sin, np.float32(0.0))
    sb = np.where(upper, np.float32(0.0), -sin)
    return cos, sa, sb


@functools.lru_cache(maxsize=None)
def _dft_constants(n_pos):
    c = np.arange(FGROUP_DIM)
    ang_c = 2.0 * np.pi * ((c[:, None] * c[None, :]) % FGROUP_DIM) / FGROUP_DIM
    norm = 1.0 / math.sqrt(n_pos * FGROUP_DIM)
    cmat = (np.cos(ang_c) * norm).astype(np.float32)
    smat = (np.sin(ang_c) * norm).astype(np.float32)
    k1 = np.arange(DFT_N1)[None, :, None]
    n1 = np.arange(DFT_N1)[None, None, :]
    n2 = np.arange(DFT_N2)[:, None, None]
    ang_g = 2.0 * np.pi * ((k1 * (DFT_N2 * n1 + n2)) % n_pos) / n_pos
    gsmall = np.stack([np.cos(ang_g), np.sin(ang_g)], axis=1)
    gsmall = gsmall.reshape(DFT_N2 // F1_NB, F1_NB, 2, DFT_N1, DFT_N1)
    gmat = np.einsum('japkn,ab->jpkanb', gsmall, np.eye(F1_NB)).reshape(
        DFT_N2 // F1_NB, 2 * DFT_N1 * F1_NB, DFT_N1 * F1_NB).astype(BF16)
    k2 = np.arange(DFT_N2)
    ang_2 = 2.0 * np.pi * ((k2[:, None] * k2[None, :]) % DFT_N2) / DFT_N2
    c2 = np.cos(ang_2).astype(np.float32)
    s2 = np.sin(ang_2).astype(np.float32)
    return cmat, smat, gmat, c2, s2


def kernel(x, c, ctx, c_ctx, w_ada, b_ada, g_mix_norm, g_ffn_norm, w_in, lambda_q1, lambda_k1, lambda_q2, lambda_k2, g_subln, w_fourier, w_out, w_router_group, b_router_group, w_router_expert, b_router_expert, w_gate, w_up, w_down, g_final):
    b, s, d = x.shape
    t = b * s
    assert d == D_MODEL and s == DFT_N1 * DFT_N2 and s % GRID_W == 0 and b == 2

    cc = jnp.concatenate([c, c_ctx[None, :], jnp.zeros((8 - b - 1, d), F32)], axis=0)
    mods = _adaln(cc.T, w_ada[0], b_ada[0]).reshape(8, N_MOD, d)

    cmat, smat, gmat, c2, s2 = _dft_constants(s)
    wf = _wfold(jnp.asarray(cmat), jnp.asarray(smat), w_fourier[0])
    cos_t, sa_t, sb_t = _rope_tables(s // GRID_W)

    w_in_b = w_in[0].astype(BF16)
    g_mix = g_mix_norm[0].reshape(1, d)
    q, kl, vl, y = _inproj(x, mods, g_mix, w_in_b, wf, cos_t, sa_t, sb_t)
    kc, vc = _ctxproj(ctx, mods, g_mix, w_in_b)

    lam_p = jnp.stack([lambda_q1[0], lambda_k1[0], lambda_q2[0], lambda_k2[0]], axis=0)
    n_blocks = t * TOP_K // MOE_BLK + N_EXPERTS
    attn_o, xs_zeros = _attention(q, kc, vc, kl, vl, lam_p, g_subln[0].reshape(1, HEAD_W),
                                  n_blocks * MOE_BLK * ROW_TILE)

    br, bi = _dft1(y.reshape(b, DFT_N1, DFT_N2, 2 * F_W), jnp.asarray(gmat))
    four = _dft2(br, bi, jnp.asarray(c2).astype(BF16),
                 jnp.asarray(s2).astype(BF16)).reshape(b, s, F_W)

    n_r = N_GROUPS + N_EXPERTS
    w_r = jnp.concatenate([w_router_group[0], w_router_expert[0],
                           jnp.zeros((d, 128 - n_r), F32)], axis=1)
    b_r = jnp.concatenate([b_router_group[0], b_router_expert[0],
                           jnp.zeros((128 - n_r,), F32)]).reshape(1, 128)
    w_r_hi = w_r.astype(BF16)
    w_r_lo = (w_r - w_r_hi.astype(F32)).astype(BF16)
    x1, h3, meta, gates, cnt = _outproj(attn_o, four, x, mods, g_ffn_norm[0].reshape(1, d),
                                        w_out[0].astype(BF16),
                                        jnp.concatenate([w_r_hi, w_r_lo], axis=1), b_r)

    counts = cnt[0, N_GROUPS:N_GROUPS + N_EXPERTS].astype(jnp.int32)
    nblk = (counts + MOE_BLK - 1) // MOE_BLK
    blk_end = jnp.cumsum(nblk)
    blk_start = (blk_end - nblk).astype(jnp.int32)
    drows = _destrows(blk_start, meta)
    dest = jnp.stack([drows[0], drows[1]], axis=1)
    blk_ids = jnp.arange(n_blocks, dtype=jnp.int32)
    blk_expert = jnp.minimum(
        jnp.sum((blk_end[None, :] <= blk_ids[:, None]).astype(jnp.int32), axis=1),
        N_EXPERTS - 1).astype(jnp.int32)
    n_used = blk_end[-1:].astype(jnp.int32)
    dest3 = dest.reshape(t // TD_DISPATCH, 1, 2 * TD_DISPATCH)

    xs = _dispatch(dest3, h3, xs_zeros)
    e_ids = jnp.arange(N_EXPERTS, dtype=jnp.int32)
    later = (e_ids[None, :] > e_ids[:, None]) & (nblk[None, :] > 0)
    next_expert = jnp.min(jnp.where(later, e_ids[None, :], N_EXPERTS), axis=1)
    next_expert = jnp.where(next_expert == N_EXPERTS, e_ids, next_expert).astype(jnp.int32)
    ys = _experts(blk_expert, n_used, next_expert, xs, w_gate[0], w_up[0], w_down[0])
    return _combine(dest3, ys, x1, gates, mods, g_final.reshape(1, d))
```

```python
import functools
import math

import numpy as np
import jax
import jax.numpy as jnp
from jax import lax
from jax.experimental import pallas as pl
from jax.experimental.pallas import tpu as pltpu

F32 = jnp.float32
BF16 = jnp.bfloat16

D_MODEL = 1024
GRID_W = 64
N_HEADS = 4
HEAD_DIM = 64
HEAD_W = 2 * HEAD_DIM
QK_W = N_HEADS * HEAD_W
N_FGROUPS = 4
FGROUP_DIM = 128
F_W = N_FGROUPS * FGROUP_DIM
ROPE_THETA = 10000.0
ROPE_AXIS = HEAD_DIM // 2
ROPE_HALF = ROPE_AXIS // 2
N_GROUPS = 4
EXPERTS_PER_GROUP = 8
N_EXPERTS = N_GROUPS * EXPERTS_PER_GROUP
TOP_K = 2
D_EXPERT = 512
N_MOD = 6
N_COND = 3
EPS = 1e-6
LAMBDA_INIT = 0.8 - 0.6 * math.exp(-0.3 * 0)
LOG2_E = 1.4426950408889634

DFT_N1 = 64
DFT_N2 = 128

TM_PROJ = 512
TM_INPROJ = 1024
TQ = 1024
TK = 2048
F1_NB = 8
F2_KB = 16
MOE_BLK = 512
ROW_TILE = D_MODEL // 256
U32 = jnp.uint32
TD = 512
TD_DISPATCH = 1024
DMA_UNROLL = 16
VMEM_LIMIT = 48 * 1024 * 1024


def _cparams(*sem):
    return pltpu.CompilerParams(dimension_semantics=sem, vmem_limit_bytes=VMEM_LIMIT)


def _adaln_kernel(c_ref, w_ref, b_ref, o_ref):
    cc = c_ref[...]
    s = cc * jax.nn.sigmoid(cc)
    w = w_ref[...]
    o_ref[...] = jnp.zeros(o_ref.shape, F32)
    for r in range(N_COND):
        o_ref[r:r + 1, :] = jnp.sum(w * s[:, r:r + 1], axis=0, keepdims=True) + b_ref[...]


def _adaln(cc, w_ada, b_ada):
    n = w_ada.shape[1]
    tn = 1536
    return pl.pallas_call(
        _adaln_kernel,
        grid=(n // tn,),
        in_specs=[pl.BlockSpec((D_MODEL, 8), lambda j: (0, 0)),
                  pl.BlockSpec((D_MODEL, tn), lambda j: (0, j)),
                  pl.BlockSpec((1, tn), lambda j: (0, j))],
        out_specs=pl.BlockSpec((8, tn), lambda j: (0, j)),
        out_shape=jax.ShapeDtypeStruct((8, n), F32),
        compiler_params=_cparams("arbitrary"),
        name="adaln",
    )(cc, w_ada, b_ada.reshape(1, n))


def _wfold_kernel(c_ref, s_ref, w_ref, o_ref):
    w = w_ref[0]
    o_ref[0, :, :FGROUP_DIM] = jnp.dot(c_ref[...], w, preferred_element_type=F32,
                                       precision=lax.Precision.HIGHEST).astype(BF16)
    o_ref[0, :, FGROUP_DIM:] = jnp.dot(s_ref[...], w, preferred_element_type=F32,
                                       precision=lax.Precision.HIGHEST).astype(BF16)


def _wfold(cmat, smat, w_fourier):
    return pl.pallas_call(
        _wfold_kernel,
        grid=(N_FGROUPS,),
        in_specs=[pl.BlockSpec((FGROUP_DIM, FGROUP_DIM), lambda g: (0, 0)),
                  pl.BlockSpec((FGROUP_DIM, FGROUP_DIM), lambda g: (0, 0)),
                  pl.BlockSpec((1, FGROUP_DIM, FGROUP_DIM), lambda g: (g, 0, 0))],
        out_specs=pl.BlockSpec((1, FGROUP_DIM, 2 * FGROUP_DIM), lambda g: (g, 0, 0)),
        out_shape=jax.ShapeDtypeStruct((N_FGROUPS, FGROUP_DIM, 2 * FGROUP_DIM), BF16),
        compiler_params=_cparams("arbitrary"),
        name="wfold",
    )(cmat, smat, w_fourier)


def _norm_mod(x, g, shift, scale):
    ms = jnp.mean(x * x, axis=-1, keepdims=True)
    y = x * lax.rsqrt(ms + EPS) * g
    return y * (1.0 + scale) + shift


def _rope_slab(p, cos, sa, sb):
    return (p * cos + pltpu.roll(p, ROPE_HALF, 1) * sa
            + pltpu.roll(p, HEAD_W - ROPE_HALF, 1) * sb)


def _inproj_kernel(x_ref, mod_ref, g_ref, w_ref, cos_ref, sa_ref, sb_ref,
                   q_ref, k_ref, v_ref, f_ref):
    m = mod_ref[0]
    h = _norm_mod(x_ref[0], g_ref[...], m[0:1], m[1:2]).astype(BF16)
    cos, sa, sb = cos_ref[...], sa_ref[...], sb_ref[...]
    scale = HEAD_DIM ** -0.5 * LOG2_E
    pq = jnp.dot(h, w_ref[:, 0:QK_W], preferred_element_type=F32)
    for hh in range(N_HEADS):
        sl = slice(hh * HEAD_W, (hh + 1) * HEAD_W)
        q_ref[0, :, sl] = (_rope_slab(pq[:, sl], cos, sa, sb) * scale).astype(BF16)
    pk = jnp.dot(h, w_ref[:, QK_W:2 * QK_W], preferred_element_type=F32)
    for hh in range(N_HEADS):
        sl = slice(hh * HEAD_W, (hh + 1) * HEAD_W)
        k_ref[0, :, sl] = _rope_slab(pk[:, sl], cos, sa, sb).astype(BF16)
    v_ref[0] = jnp.dot(h, w_ref[:, 2 * QK_W:3 * QK_W], preferred_element_type=F32).astype(BF16)
    f_ref[0] = jnp.dot(h, w_ref[:, 3 * QK_W:], preferred_element_type=F32)


def _inproj(x, mods, g_mix, w_in, cos_t, sa_t, sb_t):
    b, s, d = x.shape
    tm = TM_INPROJ
    tok = lambda bi, i: (bi, i, 0)
    return pl.pallas_call(
        _inproj_kernel,
        grid=(b, s // tm),
        in_specs=[pl.BlockSpec((1, tm, d), tok),
                  pl.BlockSpec((1, N_MOD, d), lambda bi, i: (bi, 0, 0)),
                  pl.BlockSpec((1, d), lambda bi, i: (0, 0)),
                  pl.BlockSpec(w_in.shape, lambda bi, i: (0, 0)),
                  pl.BlockSpec((tm, HEAD_W), lambda bi, i: (i, 0)),
                  pl.BlockSpec((tm, HEAD_W), lambda bi, i: (i, 0)),
                  pl.BlockSpec((tm, HEAD_W), lambda bi, i: (i, 0))],
        out_specs=[pl.BlockSpec((1, tm, QK_W), tok),
                   pl.BlockSpec((1, tm, QK_W), tok),
                   pl.BlockSpec((1, tm, QK_W), tok),
                   pl.BlockSpec((1, tm, F_W), tok)],
        out_shape=[jax.ShapeDtypeStruct((b, s, QK_W), BF16),
                   jax.ShapeDtypeStruct((b, s, QK_W), BF16),
                   jax.ShapeDtypeStruct((b, s, QK_W), BF16),
                   jax.ShapeDtypeStruct((b, s, F_W), F32)],
        compiler_params=_cparams("parallel", "arbitrary"),
        name="inproj",
    )(x, mods, g_mix, w_in, cos_t, sa_t, sb_t)


def _ctxproj_kernel(x_ref, mod_ref, g_ref, w_ref, k_ref, v_ref):
    m = mod_ref[0]
    h = _norm_mod(x_ref[0], g_ref[...], m[0:1], m[1:2]).astype(BF16)
    k_ref[0] = jnp.dot(h, w_ref[:, QK_W:2 * QK_W], preferred_element_type=F32).astype(BF16)
    v_ref[0] = jnp.dot(h, w_ref[:, 2 * QK_W:3 * QK_W], preferred_element_type=F32).astype(BF16)


def _ctxproj(ctx, mods, g_mix, w_in):
    b, n, d = ctx.shape
    return pl.pallas_call(
        _ctxproj_kernel,
        grid=(b,),
        in_specs=[pl.BlockSpec((1, n, d), lambda bi: (bi, 0, 0)),
                  pl.BlockSpec((1, N_MOD, d), lambda bi: (2, 0, 0)),
                  pl.BlockSpec((1, d), lambda bi: (0, 0)),
                  pl.BlockSpec(w_in.shape, lambda bi: (0, 0))],
        out_specs=[pl.BlockSpec((1, n, QK_W), lambda bi: (bi, 0, 0)),
                   pl.BlockSpec((1, n, QK_W), lambda bi: (bi, 0, 0))],
        out_shape=[jax.ShapeDtypeStruct((b, n, QK_W), BF16),
                   jax.ShapeDtypeStruct((b, n, QK_W), BF16)],
        compiler_params=_cparams("arbitrary"),
        name="ctxproj",
    )(ctx, mods, g_mix, w_in)


def _attn_kernel(q_ref, kc_ref, vc_ref, kl_ref, vl_ref, lam_ref, g_ref, o_ref, z_ref,
                 m_ref, l_ref, acc_ref):
    q = q_ref[0]
    lane = lax.broadcasted_iota(jnp.int32, q.shape, 1)
    zero = jnp.zeros_like(q)
    q0 = jnp.where(lane < HEAD_DIM, q, zero)
    q1 = jnp.where(lane >= HEAD_DIM, q, zero)
    qs = (q0, q1)
    contract_last = (((1,), (1,)), ((), ()))
    m_ref[...] = jnp.full(m_ref.shape, -1e30, F32)
    l_ref[...] = jnp.zeros(l_ref.shape, F32)
    acc_ref[...] = jnp.zeros(acc_ref.shape, F32)

    def step(kb, vb):
        nk = kb.shape[0] // HEAD_W
        for mi in range(2):
            s = lax.dot_general(qs[mi], kb, contract_last, preferred_element_type=F32)
            m_old = m_ref[mi]
            m_new = jnp.maximum(m_old, jnp.max(s, axis=-1, keepdims=True))
            alpha = jnp.exp2(m_old - m_new)
            p = jnp.exp2(s - jnp.concatenate([m_new] * nk, axis=1))
            psum = p[:, 0:HEAD_W]
            for cblk in range(1, nk):
                psum = psum + p[:, cblk * HEAD_W:(cblk + 1) * HEAD_W]
            l_ref[mi] = alpha * l_ref[mi] + psum
            acc_ref[mi] = alpha * acc_ref[mi] + jnp.dot(p.astype(BF16), vb,
                                                       preferred_element_type=F32)
            m_ref[mi] = m_new

    step(kc_ref[0], vc_ref[0])

    def body(i, c):
        off = pl.multiple_of(i * TK, TK)
        step(kl_ref[0, pl.ds(off, TK), :], vl_ref[0, pl.ds(off, TK), :])
        return c

    lax.fori_loop(0, kl_ref.shape[1] // TK, body, 0)

    lp = lam_ref[...]
    t1 = jnp.sum(lp[0:1] * lp[1:2], axis=-1, keepdims=True)
    t2 = jnp.sum(lp[2:3] * lp[3:4], axis=-1, keepdims=True)
    lam = jnp.exp(t1) - jnp.exp(t2) + LAMBDA_INIT
    l0 = jnp.sum(l_ref[0], axis=-1, keepdims=True)
    l1 = jnp.sum(l_ref[1], axis=-1, keepdims=True)
    o = acc_ref[0] / l0 - lam * (acc_ref[1] / l1)
    ms = jnp.mean(o * o, axis=-1, keepdims=True)
    o = o * lax.rsqrt(ms + EPS) * g_ref[...] * (1.0 - LAMBDA_INIT)
    o_ref[0] = o.astype(BF16)
    z_ref[...] = jnp.zeros(z_ref.shape, U32)


def _attention(q, kc, vc, kl, vl, lam_p, g_subln, zero_rows):
    b, s, _ = q.shape
    n_ctx = kc.shape[1]
    nq = s // TQ
    n_steps = b * N_HEADS * nq
    zrows = zero_rows // n_steps
    assert zrows * n_steps == zero_rows and zrows % 8 == 0
    return pl.pallas_call(
        _attn_kernel,
        grid=(b, N_HEADS, nq),
        in_specs=[pl.BlockSpec((1, TQ, HEAD_W), lambda bi, h, i: (bi, i, h)),
                  pl.BlockSpec((1, n_ctx, HEAD_W), lambda bi, h, i: (bi, 0, h)),
                  pl.BlockSpec((1, n_ctx, HEAD_W), lambda bi, h, i: (bi, 0, h)),
                  pl.BlockSpec((1, s, HEAD_W), lambda bi, h, i: (bi, 0, h)),
                  pl.BlockSpec((1, s, HEAD_W), lambda bi, h, i: (bi, 0, h)),
                  pl.BlockSpec((4, HEAD_DIM), lambda bi, h, i: (0, 0)),
                  pl.BlockSpec((1, HEAD_W), lambda bi, h, i: (0, 0))],
        out_specs=[pl.BlockSpec((1, TQ, HEAD_W), lambda bi, h, i: (bi, i, h)),
                   pl.BlockSpec((zrows, 128), lambda bi, h, i: ((bi * N_HEADS + h) * nq + i, 0))],
        out_shape=[jax.ShapeDtypeStruct((b, s, QK_W), BF16),
                   jax.ShapeDtypeStruct((zero_rows, 128), U32)],
        scratch_shapes=[pltpu.VMEM((2, TQ, HEAD_W), F32)] * 3,
        compiler_params=_cparams("parallel", "parallel", "arbitrary"),
        name="diffattn",
    )(q, kc, vc, kl, vl, lam_p, g_subln)


def _dft1_kernel(f_ref, wf_ref, g_ref, br_ref, bi_ref):
    rows = DFT_N1 * F1_NB
    fb = f_ref[0].reshape(rows, F_W).astype(BF16)
    y1, y2 = [], []
    for g in range(N_FGROUPS):
        yy = jnp.dot(fb[:, g * FGROUP_DIM:(g + 1) * FGROUP_DIM], wf_ref[g],
                     preferred_element_type=F32)
        y1.append(yy[:, :FGROUP_DIM])
        y2.append(yy[:, FGROUP_DIM:])
    yb = jnp.concatenate(y1 + y2, axis=1).astype(BF16)
    p = jnp.dot(g_ref[0], yb, preferred_element_type=F32)
    top, bot = p[:rows], p[rows:]
    br_ref[0] = (top[:, :F_W] - bot[:, F_W:]).reshape(DFT_N1, F1_NB, F_W)
    bi_ref[0] = (-top[:, F_W:] - bot[:, :F_W]).reshape(DFT_N1, F1_NB, F_W)


def _dft1(f4, wf, gmat):
    b = f4.shape[0]
    rows = DFT_N1 * F1_NB
    return pl.pallas_call(
        _dft1_kernel,
        grid=(DFT_N2 // F1_NB, b),
        in_specs=[pl.BlockSpec((1, DFT_N1, F1_NB, F_W), lambda j, bi: (bi, 0, j, 0)),
                  pl.BlockSpec(wf.shape, lambda j, bi: (0, 0, 0)),
                  pl.BlockSpec((1, 2 * rows, rows), lambda j, bi: (j, 0, 0))],
        out_specs=[pl.BlockSpec((1, DFT_N1, F1_NB, F_W), lambda j, bi: (bi, 0, j, 0)),
                   pl.BlockSpec((1, DFT_N1, F1_NB, F_W), lambda j, bi: (bi, 0, j, 0))],
        out_shape=[jax.ShapeDtypeStruct((b, DFT_N1, DFT_N2, F_W), F32),
                   jax.ShapeDtypeStruct((b, DFT_N1, DFT_N2, F_W), F32)],
        compiler_params=_cparams("arbitrary", "arbitrary"),
        name="dft1",
    )(f4, wf, gmat)


def _dft2_kernel(br_ref, bi_ref, c_ref, s_ref, o_ref):
    for j in range(F2_KB):
        r = (jnp.dot(c_ref[...], br_ref[0, j].astype(BF16), preferred_element_type=F32)
             + jnp.dot(s_ref[...], bi_ref[0, j].astype(BF16), preferred_element_type=F32))
        o_ref[0, :, j, :] = r


def _dft2(br4, bi4, c2, s2):
    b = br4.shape[0]
    blk = (1, F2_KB, DFT_N2, F_W)
    return pl.pallas_call(
        _dft2_kernel,
        grid=(b, DFT_N1 // F2_KB),
        in_specs=[pl.BlockSpec(blk, lambda bi, j: (bi, j, 0, 0)),
                  pl.BlockSpec(blk, lambda bi, j: (bi, j, 0, 0)),
                  pl.BlockSpec((DFT_N2, DFT_N2), lambda bi, j: (0, 0)),
                  pl.BlockSpec((DFT_N2, DFT_N2), lambda bi, j: (0, 0))],
        out_specs=pl.BlockSpec((1, DFT_N2, F2_KB, F_W), lambda bi, j: (bi, 0, j, 0)),
        out_shape=jax.ShapeDtypeStruct((b, DFT_N2, DFT_N1, F_W), F32),
        compiler_params=_cparams("parallel", "arbitrary"),
        name="dft2",
    )(br4, bi4, c2, s2)


def _outproj_kernel(o_ref, f_ref, x_ref, mod_ref, g_ref, wo_ref, wr_ref, br_ref, tri_ref,
                    x1_ref, h3_ref, meta_ref, gate_ref, cnt_out_ref, cnt_ref):
    first = (pl.program_id(0) == 0) & (pl.program_id(1) == 0)

    @pl.when(first)
    def _():
        cnt_ref[...] = jnp.zeros(cnt_ref.shape, F32)

    m = mod_ref[0]
    mix = (jnp.dot(o_ref[0], wo_ref[0:QK_W, :], preferred_element_type=F32)
           + jnp.dot(f_ref[0].astype(BF16), wo_ref[QK_W:, :], preferred_element_type=F32))
    x1 = x_ref[0] + m[2:3] * mix
    x1_ref[0] = x1
    h2 = _norm_mod(x1, g_ref[...], m[3:4], m[4:5])
    _store_rows(h3_ref, h2)
    h_hi = h2.astype(BF16)
    h_lo = (h2 - h_hi.astype(F32)).astype(BF16)
    hw = jnp.dot(h_hi, wr_ref[...], preferred_element_type=F32)
    lg = (hw[:, :128] + hw[:, 128:]
          + jnp.dot(h_lo, wr_ref[:, 0:128], preferred_element_type=F32)
          + br_ref[...])
    lane = lax.broadcasted_iota(jnp.int32, lg.shape, 1)
    lanef = lane.astype(F32)
    ninf = jnp.float32(-jnp.inf)
    big = jnp.float32(lg.shape[1])
    gl = jnp.where(lane < N_GROUPS, lg, ninf)
    gmax = jnp.max(gl, axis=-1, keepdims=True)
    grp = jnp.min(jnp.where(gl == gmax, lanef, big), axis=-1, keepdims=True)
    pg = 1.0 / jnp.sum(jnp.exp(gl - gmax), axis=-1, keepdims=True)
    e_lane = lane - N_GROUPS
    lane_grp = (e_lane >> 3).astype(F32)
    emask = (e_lane >= 0) & (e_lane < N_EXPERTS) & (lane_grp == grp)
    el = jnp.where(emask, lg, ninf)
    t1 = jnp.max(el, axis=-1, keepdims=True)
    i1 = jnp.min(jnp.where(el == t1, lanef, big), axis=-1, keepdims=True)
    el2 = jnp.where(lanef == i1, ninf, el)
    t2 = jnp.max(el2, axis=-1, keepdims=True)
    i2 = jnp.min(jnp.where(el2 == t2, lanef, big), axis=-1, keepdims=True)
    dd = jnp.exp(t2 - t1)
    w1 = pg / (1.0 + dd)
    w2 = pg * dd / (1.0 + dd)
    gate_ref[...] = jnp.where(lane == 0, w1, jnp.where(lane == 1, w2, 0.0))
    hit1 = lanef == i1
    hit2 = lanef == i2
    oh = jnp.where(hit1 | hit2, 1.0, 0.0)
    before = jnp.dot(tri_ref[...], oh.astype(BF16), preferred_element_type=F32) + cnt_ref[0:1, :]
    r1 = jnp.sum(jnp.where(hit1, before, 0.0), axis=-1, keepdims=True)
    r2 = jnp.sum(jnp.where(hit2, before, 0.0), axis=-1, keepdims=True)
    cnt_ref[0:1, :] = cnt_ref[0:1, :] + jnp.sum(oh, axis=0, keepdims=True)
    cnt_out_ref[...] = cnt_ref[...]
    meta = jnp.where(lane == 0, i1 - N_GROUPS,
                     jnp.where(lane == 1, i2 - N_GROUPS,
                               jnp.where(lane == 2, r1, jnp.where(lane == 3, r2, 0.0))))
    meta_ref[...] = jnp.transpose(meta)[0:8, :].astype(jnp.int32)


def _outproj(attn_o, four, x, mods, g_ffn, w_out, w_r, b_r):
    b, s, d = x.shape
    tm = TM_PROJ
    nt = s // tm
    tok = lambda bi, i: (bi, i, 0)
    flat = lambda bi, i: (bi * nt + i, 0)
    tri = jnp.asarray(np.tril(np.ones((tm, tm), np.float32), -1)).astype(BF16)
    return pl.pallas_call(
        _outproj_kernel,
        grid=(b, nt),
        in_specs=[pl.BlockSpec((1, tm, QK_W), tok),
                  pl.BlockSpec((1, tm, F_W), tok),
                  pl.BlockSpec((1, tm, d), tok),
                  pl.BlockSpec((1, N_MOD, d), lambda bi, i: (bi, 0, 0)),
                  pl.BlockSpec((1, d), lambda bi, i: (0, 0)),
                  pl.BlockSpec(w_out.shape, lambda bi, i: (0, 0)),
                  pl.BlockSpec(w_r.shape, lambda bi, i: (0, 0)),
                  pl.BlockSpec(b_r.shape, lambda bi, i: (0, 0)),
                  pl.BlockSpec((tm, tm), lambda bi, i: (0, 0))],
        out_specs=[pl.BlockSpec((1, tm, d), tok),
                   pl.BlockSpec((tm * ROW_TILE, 128), flat),
                   pl.BlockSpec((8, tm), lambda bi, i: (0, bi * nt + i)),
                   pl.BlockSpec((tm, 128), flat),
                   pl.BlockSpec((8, 128), lambda bi, i: (0, 0))],
        out_shape=[jax.ShapeDtypeStruct((b, s, d), F32),
                   jax.ShapeDtypeStruct((b * s * ROW_TILE, 128), U32),
                   jax.ShapeDtypeStruct((8, b * s), jnp.int32),
                   jax.ShapeDtypeStruct((b * s, 128), F32),
                   jax.ShapeDtypeStruct((8, 128), F32)],
        scratch_shapes=[pltpu.VMEM((8, 128), F32)],
        compiler_params=_cparams("arbitrary", "arbitrary"),
        name="outproj",
    )(attn_o, four, x, mods, g_ffn, w_out, w_r, b_r, tri)


def _destrows_kernel(start_ref, meta_ref, o_ref):
    meta = meta_ref[...]
    row = lax.broadcasted_iota(jnp.int32, meta.shape, 0)
    base = jnp.zeros_like(meta)
    for e in range(N_EXPERTS):
        base = jnp.where(meta == e, start_ref[e] * MOE_BLK, base)
    ranks = pltpu.roll(meta, meta.shape[0] - TOP_K, 0)
    o_ref[...] = jnp.where(row < TOP_K, base + ranks, 0)


def _destrows(blk_start, meta):
    return pl.pallas_call(
        _destrows_kernel,
        grid_spec=pltpu.PrefetchScalarGridSpec(
            num_scalar_prefetch=1,
            grid=(1,),
            in_specs=[pl.BlockSpec(meta.shape, lambda i, st: (0, 0))],
            out_specs=pl.BlockSpec(meta.shape, lambda i, st: (0, 0))),
        out_shape=jax.ShapeDtypeStruct(meta.shape, jnp.int32),
        compiler_params=_cparams("arbitrary"),
        name="destrows",
    )(blk_start, meta)


def _row_slice(row):
    return pl.ds(pl.multiple_of(row * ROW_TILE, ROW_TILE), ROW_TILE)


def _row_copy(src_ref, src_row, dst_ref, dst_row, sem):
    return pltpu.make_async_copy(src_ref.at[_row_slice(src_row)], dst_ref.at[_row_slice(dst_row)], sem)


def _bf16_bits(x):
    return lax.bitcast_convert_type(x.astype(BF16).astype(F32), U32)


def _rows_2d(ref, n_rows, dtype):
    hi, lo = [], []
    for cblk in range(ROW_TILE):
        w = ref[pl.ds(cblk, n_rows, stride=ROW_TILE), :]
        hi.append(lax.bitcast_convert_type(w & jnp.uint32(0xFFFF0000), F32).astype(dtype))
        lo.append(lax.bitcast_convert_type(w << 16, F32).astype(dtype))
    return jnp.concatenate(hi + lo, axis=1)


def _store_rows(ref, val):
    for cblk in range(ROW_TILE):
        hi = _bf16_bits(val[:, cblk * 128:(cblk + 1) * 128])
        lo = _bf16_bits(val[:, (cblk + ROW_TILE) * 128:(cblk + ROW_TILE + 1) * 128])
        ref[pl.ds(cblk, val.shape[0], stride=ROW_TILE), :] = hi | (lo >> 16)


def _dispatch_kernel(dest_ref, h_ref, zeros_ref, xs_ref, sem):
    del zeros_ref
    n = 2 * TD_DISPATCH

    def issue(r, c):
        _row_copy(h_ref, r, xs_ref, dest_ref[0, 0, 2 * r], sem).start(priority=0)
        _row_copy(h_ref, r, xs_ref, dest_ref[0, 0, 2 * r + 1], sem).start(priority=1)
        return c

    lax.fori_loop(0, TD_DISPATCH, issue, 0, unroll=DMA_UNROLL // 2)

    def drain(a, c):
        _row_copy(h_ref, 0, xs_ref, 0, sem).wait()
        return c

    lax.fori_loop(0, n, drain, 0, unroll=DMA_UNROLL)


def _dispatch(dest3, h3, xs_zeros):
    t = h3.shape[0] // ROW_TILE
    return pl.pallas_call(
        _dispatch_kernel,
        grid=(t // TD_DISPATCH,),
        in_specs=[pl.BlockSpec((1, 1, 2 * TD_DISPATCH), lambda i: (i, 0, 0),
                               memory_space=pltpu.SMEM),
                  pl.BlockSpec((TD_DISPATCH * ROW_TILE, 128), lambda i: (i, 0)),
                  pl.BlockSpec(memory_space=pl.ANY)],
        out_specs=pl.BlockSpec(memory_space=pl.ANY),
        out_shape=jax.ShapeDtypeStruct(xs_zeros.shape, xs_zeros.dtype),
        scratch_shapes=[pltpu.SemaphoreType.DMA(())],
        input_output_aliases={2: 0},
        compiler_params=_cparams("arbitrary"),
        name="dispatch",
    )(dest3, h3, xs_zeros)


def _experts_kernel(be_ref, nu_ref, nxt_ref, xs_ref, wg_hbm, wu_hbm, wd_hbm, ys_ref,
                    wgb, wub, wdb, sg, su, sd, sem):
    j = pl.program_id(0)
    used = j < nu_ref[0]
    e = be_ref[j]
    new_expert = (j == 0) | (e != be_ref[jnp.maximum(j - 1, 0)])
    e_next = nxt_ref[e]

    def weight_copies(idx):
        return (pltpu.make_async_copy(wg_hbm.at[idx], sg, sem.at[0]),
                pltpu.make_async_copy(wu_hbm.at[idx], su, sem.at[1]),
                pltpu.make_async_copy(wd_hbm.at[idx], sd, sem.at[2]))

    @pl.when(j == 0)
    def _():
        for cp in weight_copies(e):
            cp.start(priority=1)

    @pl.when(used & new_expert)
    def _():
        for cp in weight_copies(e):
            cp.wait()
        wgb[...] = sg[...].astype(BF16)
        wub[...] = su[...].astype(BF16)
        wdb[...] = sd[...].astype(BF16)

    @pl.when(used & new_expert & (e_next != e))
    def _():
        for cp in weight_copies(e_next):
            cp.start(priority=1)

    @pl.when(used)
    def _():
        xb = _rows_2d(xs_ref, MOE_BLK, BF16)
        gate = jnp.dot(xb, wgb[...], preferred_element_type=F32)
        up = jnp.dot(xb, wub[...], preferred_element_type=F32)
        hid = (gate * jax.nn.sigmoid(gate) * up).astype(BF16)
        _store_rows(ys_ref, jnp.dot(hid, wdb[...], preferred_element_type=F32))

    @pl.when(pl.program_id(0) >= nu_ref[0])
    def _():
        ys_ref[...] = jnp.zeros_like(ys_ref)


def _experts(blk_expert, n_used, next_expert, xs, wg, wu, wd):
    rows = xs.shape[0] // ROW_TILE
    d = wg.shape[1]
    nb = rows // MOE_BLK
    row_blk = lambda j, be, nu, nx: (jnp.minimum(j, nu[0] - 1), 0)
    out_blk = lambda j, be, nu, nx: (j, 0)
    return pl.pallas_call(
        _experts_kernel,
        grid_spec=pltpu.PrefetchScalarGridSpec(
            num_scalar_prefetch=3,
            grid=(nb,),
            in_specs=[pl.BlockSpec((MOE_BLK * ROW_TILE, 128), row_blk),
                      pl.BlockSpec(memory_space=pl.ANY),
                      pl.BlockSpec(memory_space=pl.ANY),
                      pl.BlockSpec(memory_space=pl.ANY)],
            out_specs=pl.BlockSpec((MOE_BLK * ROW_TILE, 128), out_blk),
            scratch_shapes=[pltpu.VMEM((d, D_EXPERT), BF16), pltpu.VMEM((d, D_EXPERT), BF16),
                            pltpu.VMEM((D_EXPERT, d), BF16),
                            pltpu.VMEM((d, D_EXPERT), F32), pltpu.VMEM((d, D_EXPERT), F32),
                            pltpu.VMEM((D_EXPERT, d), F32),
                            pltpu.SemaphoreType.DMA((3,))]),
        out_shape=jax.ShapeDtypeStruct(xs.shape, xs.dtype),
        compiler_params=_cparams("arbitrary"),
        name="experts",
    )(blk_expert, n_used, next_expert, xs, wg, wu, wd)


def _combine_kernel(dest_ref, dest_next_ref, ys_ref, x1_ref, gate_ref, mod_ref, g_ref, o_ref,
                    ya, yb, sem):
    step = pl.program_id(0) * pl.num_programs(1) + pl.program_id(1)
    n_steps = pl.num_programs(0) * pl.num_programs(1)
    slot = step % 2

    def start_gathers(idx_ref, for_step, to_slot):
        base = (for_step % (TD_DISPATCH // TD)) * (2 * TD)

        def issue(r, c):
            _row_copy(ys_ref, idx_ref[0, 0, base + 2 * r], ya.at[to_slot], r,
                      sem.at[to_slot]).start(priority=0)
            _row_copy(ys_ref, idx_ref[0, 0, base + 2 * r + 1], yb.at[to_slot], r,
                      sem.at[to_slot]).start(priority=1)
            return c

        lax.fori_loop(0, TD, issue, 0, unroll=DMA_UNROLL)

    @pl.when(step == 0)
    def _():
        start_gathers(dest_ref, step, 0)

    @pl.when(step + 1 < n_steps)
    def _():
        start_gathers(dest_next_ref, step + 1, 1 - slot)

    def drain(r, c):
        _row_copy(ys_ref, 0, ya.at[slot], 0, sem.at[slot]).wait()
        _row_copy(ys_ref, 0, yb.at[slot], 0, sem.at[slot]).wait()
        return c

    lax.fori_loop(0, TD, drain, 0, unroll=DMA_UNROLL)
    gt = gate_ref[...]
    moe = (gt[:, 0:1] * _rows_2d(ya.at[slot], TD, F32)
           + gt[:, 1:2] * _rows_2d(yb.at[slot], TD, F32))
    x2 = x1_ref[0] + mod_ref[0][5:6] * moe
    ms = jnp.mean(x2 * x2, axis=-1, keepdims=True)
    o_ref[0] = x2 * lax.rsqrt(ms + EPS) * g_ref[...]


def _combine(dest3, ys, x1, gates, mods, g_final):
    b, s, d = x1.shape
    nt = s // TD
    ratio = TD_DISPATCH // TD
    assert ratio * TD == TD_DISPATCH
    return pl.pallas_call(
        _combine_kernel,
        grid=(b, nt),
        in_specs=[pl.BlockSpec((1, 1, 2 * TD_DISPATCH),
                               lambda bi, i: ((bi * nt + i) // ratio, 0, 0),
                               memory_space=pltpu.SMEM),
                  pl.BlockSpec((1, 1, 2 * TD_DISPATCH),
                               lambda bi, i: (jnp.minimum(bi * nt + i + 1, b * nt - 1) // ratio, 0, 0),
                               memory_space=pltpu.SMEM),
                  pl.BlockSpec(memory_space=pl.ANY),
                  pl.BlockSpec((1, TD, d), lambda bi, i: (bi, i, 0)),
                  pl.BlockSpec((TD, 128), lambda bi, i: (bi * nt + i, 0)),
                  pl.BlockSpec((1, N_MOD, d), lambda bi, i: (bi, 0, 0)),
                  pl.BlockSpec((1, d), lambda bi, i: (0, 0))],
        out_specs=pl.BlockSpec((1, TD, d), lambda bi, i: (bi, i, 0)),
        out_shape=jax.ShapeDtypeStruct((b, s, d), F32),
        scratch_shapes=[pltpu.VMEM((2, TD * ROW_TILE, 128), U32),
                        pltpu.VMEM((2, TD * ROW_TILE, 128), U32),
                        pltpu.SemaphoreType.DMA((2,))],
        compiler_params=_cparams("arbitrary", "arbitrary"),
        name="combine",
    )(dest3, dest3, ys, x1, gates, mods, g_final)


@functools.lru_cache(maxsize=None)
def _rope_tables(rows):
    r, col = np.meshgrid(np.arange(rows), np.arange(GRID_W), indexing='ij')
    pos = np.stack([r.reshape(-1), col.reshape(-1)], axis=-1).astype(np.float32)
    inv_freq = (np.float32(ROPE_THETA)
                ** (-np.arange(0, ROPE_AXIS, 2, dtype=np.float32) / np.float32(ROPE_AXIS))).astype(np.float32)
    ang = (pos[:, :, None] * inv_freq).astype(np.float32)
    ang = np.concatenate([ang, ang], axis=-1).astype(np.float64)
    n = ang.shape[0]
    cos = np.tile(np.cos(ang).reshape(n, HEAD_DIM), (1, 2)).astype(np.float32)
    sin = np.tile(np.sin(ang).reshape(n, HEAD_DIM), (1, 2)).astype(np.float32)
    upper = (np.arange(HEAD_W) % ROPE_AXIS) >= ROPE_HALF
    sa = np.where(upper, sin, np.float32(0.0))
    sb = np.where(upper, np.float32(0.0), -sin)
    return cos, sa, sb


@functools.lru_cache(maxsize=None)
def _dft_constants(n_pos):
    c = np.arange(FGROUP_DIM)
    ang_c = 2.0 * np.pi * ((c[:, None] * c[None, :]) % FGROUP_DIM) / FGROUP_DIM
    norm = 1.0 / math.sqrt(n_pos * FGROUP_DIM)
    cmat = (np.cos(ang_c) * norm).astype(np.float32)
    smat = (np.sin(ang_c) * norm).astype(np.float32)
    k1 = np.arange(DFT_N1)[None, :, None]
    n1 = np.arange(DFT_N1)[None, None, :]
    n2 = np.arange(DFT_N2)[:, None, None]
    ang_g = 2.0 * np.pi * ((k1 * (DFT_N2 * n1 + n2)) % n_pos) / n_pos
    gsmall = np.stack([np.cos(ang_g), np.sin(ang_g)], axis=1)
    gsmall = gsmall.reshape(DFT_N2 // F1_NB, F1_NB, 2, DFT_N1, DFT_N1)
    gmat = np.einsum('japkn,ab->jpkanb', gsmall, np.eye(F1_NB)).reshape(
        DFT_N2 // F1_NB, 2 * DFT_N1 * F1_NB, DFT_N1 * F1_NB).astype(BF16)
    k2 = np.arange(DFT_N2)
    ang_2 = 2.0 * np.pi * ((k2[:, None] * k2[None, :]) % DFT_N2) / DFT_N2
    c2 = np.cos(ang_2).astype(np.float32)
    s2 = np.sin(ang_2).astype(np.float32)
    return cmat, smat, gmat, c2, s2


def kernel(x, c, ctx, c_ctx, w_ada, b_ada, g_mix_norm, g_ffn_norm, w_in, lambda_q1, lambda_k1, lambda_q2, lambda_k2, g_subln, w_fourier, w_out, w_router_group, b_router_group, w_router_expert, b_router_expert, w_gate, w_up, w_down, g_final):
    b, s, d = x.shape
    t = b * s
    assert d == D_MODEL and s == DFT_N1 * DFT_N2 and s % GRID_W == 0 and b == 2

    cc = jnp.concatenate([c, c_ctx[None, :], jnp.zeros((8 - b - 1, d), F32)], axis=0)
    mods = _adaln(cc.T, w_ada[0], b_ada[0]).reshape(8, N_MOD, d)

    cmat, smat, gmat, c2, s2 = _dft_constants(s)
    wf = _wfold(jnp.asarray(cmat), jnp.asarray(smat), w_fourier[0])
    cos_t, sa_t, sb_t = _rope_tables(s // GRID_W)

    w_in_b = w_in[0].astype(BF16)
    g_mix = g_mix_norm[0].reshape(1, d)
    q, kl, vl, f = _inproj(x, mods, g_mix, w_in_b, cos_t, sa_t, sb_t)
    kc, vc = _ctxproj(ctx, mods, g_mix, w_in_b)

    lam_p = jnp.stack([lambda_q1[0], lambda_k1[0], lambda_q2[0], lambda_k2[0]], axis=0)
    n_blocks = t * TOP_K // MOE_BLK + N_EXPERTS
    attn_o, xs_zeros = _attention(q, kc, vc, kl, vl, lam_p, g_subln[0].reshape(1, HEAD_W),
                                  n_blocks * MOE_BLK * ROW_TILE)

    br, bi = _dft1(f.reshape(b, DFT_N1, DFT_N2, F_W), wf, jnp.asarray(gmat))
    four = _dft2(br, bi, jnp.asarray(c2).astype(BF16),
                 jnp.asarray(s2).astype(BF16)).reshape(b, s, F_W)

    n_r = N_GROUPS + N_EXPERTS
    w_r = jnp.concatenate([w_router_group[0], w_router_expert[0],
                           jnp.zeros((d, 128 - n_r), F32)], axis=1)
    b_r = jnp.concatenate([b_router_group[0], b_router_expert[0],
                           jnp.zeros((128 - n_r,), F32)]).reshape(1, 128)
    w_r_hi = w_r.astype(BF16)
    w_r_lo = (w_r - w_r_hi.astype(F32)).astype(BF16)
    x1, h3, meta, gates, cnt = _outproj(attn_o, four, x, mods, g_ffn_norm[0].reshape(1, d),
                                        w_out[0].astype(BF16),
                                        jnp.concatenate([w_r_hi, w_r_lo], axis=1), b_r)

    counts = cnt[0, N_GROUPS:N_GROUPS + N_EXPERTS].astype(jnp.int32)
    nblk = (counts + MOE_BLK - 1) // MOE_BLK
    blk_end = jnp.cumsum(nblk)
    blk_start = (blk_end - nblk).astype(jnp.int32)
    drows = _destrows(blk_start, meta)
    dest = jnp.stack([drows[0], drows[1]], axis=1)
    blk_ids = jnp.arange(n_blocks, dtype=jnp.int32)
    blk_expert = jnp.minimum(
        jnp.sum((blk_end[None, :] <= blk_ids[:, None]).astype(jnp.int32), axis=1),
        N_EXPERTS - 1).astype(jnp.int32)
    n_used = blk_end[-1:].astype(jnp.int32)
    dest3 = dest.reshape(t // TD_DISPATCH, 1, 2 * TD_DISPATCH)

    xs = _dispatch(dest3, h3, xs_zeros)
    e_ids = jnp.arange(N_EXPERTS, dtype=jnp.int32)
    later = (e_ids[None, :] > e_ids[:, None]) & (nblk[None, :] > 0)
    next_expert = jnp.min(jnp.where(later, e_ids[None, :], N_EXPERTS), axis=1)
    next_expert = jnp.where(next_expert == N_EXPERTS, e_ids, next_expert).astype(jnp.int32)
    ys = _experts(blk_expert, n_used, next_expert, xs, w_gate[0], w_up[0], w_down[0])
    return _combine(dest3, ys, x1, gates, mods, g_final.reshape(1, d))
```

```python
import functools
import math

import numpy as np
import jax
import jax.numpy as jnp
from jax import lax
from jax.experimental import pallas as pl
from jax.experimental.pallas import tpu as pltpu

F32 = jnp.float32
BF16 = jnp.bfloat16

D_MODEL = 1024
GRID_W = 64
N_HEADS = 4
HEAD_DIM = 64
HEAD_W = 2 * HEAD_DIM
QK_W = N_HEADS * HEAD_W
N_FGROUPS = 4
FGROUP_DIM = 128
F_W = N_FGROUPS * FGROUP_DIM
ROPE_THETA = 10000.0
ROPE_AXIS = HEAD_DIM // 2
ROPE_HALF = ROPE_AXIS // 2
N_GROUPS = 4
EXPERTS_PER_GROUP = 8
N_EXPERTS = N_GROUPS * EXPERTS_PER_GROUP
TOP_K = 2
D_EXPERT = 512
N_MOD = 6
N_COND = 3
EPS = 1e-6
LAMBDA_INIT = 0.8 - 0.6 * math.exp(-0.3 * 0)
LOG2_E = 1.4426950408889634

DFT_N1 = 64
DFT_N2 = 128

TM_PROJ = 512
TM_INPROJ = 1024
TQ = 1024
TK = 2048
F1_NB = 8
F2_KB = 16
MOE_BLK = 512
ROW_TILE = D_MODEL // 256
U32 = jnp.uint32
TD = 512
TD_DISPATCH = 1024
DMA_UNROLL = 16
VMEM_LIMIT = 48 * 1024 * 1024


def _cparams(*sem):
    return pltpu.CompilerParams(dimension_semantics=sem, vmem_limit_bytes=VMEM_LIMIT)


def _adaln_kernel(c_ref, w_ref, b_ref, o_ref):
    cc = c_ref[...]
    s = cc * jax.nn.sigmoid(cc)
    w = w_ref[...]
    o_ref[...] = jnp.zeros(o_ref.shape, F32)
    for r in range(N_COND):
        o_ref[r:r + 1, :] = jnp.sum(w * s[:, r:r + 1], axis=0, keepdims=True) + b_ref[...]


def _adaln(cc, w_ada, b_ada):
    n = w_ada.shape[1]
    tn = 1536
    return pl.pallas_call(
        _adaln_kernel,
        grid=(n // tn,),
        in_specs=[pl.BlockSpec((D_MODEL, 8), lambda j: (0, 0)),
                  pl.BlockSpec((D_MODEL, tn), lambda j: (0, j)),
                  pl.BlockSpec((1, tn), lambda j: (0, j))],
        out_specs=pl.BlockSpec((8, tn), lambda j: (0, j)),
        out_shape=jax.ShapeDtypeStruct((8, n), F32),
        compiler_params=_cparams("arbitrary"),
        name="adaln",
    )(cc, w_ada, b_ada.reshape(1, n))


def _wfold_kernel(c_ref, s_ref, w_ref, o_ref):
    w = w_ref[0]
    o_ref[0, :, :FGROUP_DIM] = jnp.dot(c_ref[...], w, preferred_element_type=F32,
                                       precision=lax.Precision.HIGHEST).astype(BF16)
    o_ref[0, :, FGROUP_DIM:] = jnp.dot(s_ref[...], w, preferred_element_type=F32,
                                       precision=lax.Precision.HIGHEST).astype(BF16)


def _wfold(cmat, smat, w_fourier):
    return pl.pallas_call(
        _wfold_kernel,
        grid=(N_FGROUPS,),
        in_specs=[pl.BlockSpec((FGROUP_DIM, FGROUP_DIM), lambda g: (0, 0)),
                  pl.BlockSpec((FGROUP_DIM, FGROUP_DIM), lambda g: (0, 0)),
                  pl.BlockSpec((1, FGROUP_DIM, FGROUP_DIM), lambda g: (g, 0, 0))],
        out_specs=pl.BlockSpec((1, FGROUP_DIM, 2 * FGROUP_DIM), lambda g: (g, 0, 0)),
        out_shape=jax.ShapeDtypeStruct((N_FGROUPS, FGROUP_DIM, 2 * FGROUP_DIM), BF16),
        compiler_params=_cparams("arbitrary"),
        name="wfold",
    )(cmat, smat, w_fourier)


def _norm_mod(x, g, shift, scale):
    ms = jnp.mean(x * x, axis=-1, keepdims=True)
    y = x * lax.rsqrt(ms + EPS) * g
    return y * (1.0 + scale) + shift


def _rope_slab(p, cos, sa, sb):
    return (p * cos + pltpu.roll(p, ROPE_HALF, 1) * sa
            + pltpu.roll(p, HEAD_W - ROPE_HALF, 1) * sb)


def _inproj_kernel(x_ref, mod_ref, g_ref, w_ref, cos_ref, sa_ref, sb_ref,
                   q_ref, k_ref, v_ref, f_ref):
    m = mod_ref[0]
    h = _norm_mod(x_ref[0], g_ref[...], m[0:1], m[1:2]).astype(BF16)
    cos, sa, sb = cos_ref[...], sa_ref[...], sb_ref[...]
    scale = HEAD_DIM ** -0.5 * LOG2_E
    pq = jnp.dot(h, w_ref[:, 0:QK_W], preferred_element_type=F32)
    for hh in range(N_HEADS):
        sl = slice(hh * HEAD_W, (hh + 1) * HEAD_W)
        q_ref[0, :, sl] = (_rope_slab(pq[:, sl], cos, sa, sb) * scale).astype(BF16)
    pk = jnp.dot(h, w_ref[:, QK_W:2 * QK_W], preferred_element_type=F32)
    for hh in range(N_HEADS):
        sl = slice(hh * HEAD_W, (hh + 1) * HEAD_W)
        k_ref[0, :, sl] = _rope_slab(pk[:, sl], cos, sa, sb).astype(BF16)
    v_ref[0] = jnp.dot(h, w_ref[:, 2 * QK_W:3 * QK_W], preferred_element_type=F32).astype(BF16)
    f_ref[0] = jnp.dot(h, w_ref[:, 3 * QK_W:], preferred_element_type=F32)


def _inproj(x, mods, g_mix, w_in, cos_t, sa_t, sb_t):
    b, s, d = x.shape
    tm = TM_INPROJ
    tok = lambda bi, i: (bi, i, 0)
    return pl.pallas_call(
        _inproj_kernel,
        grid=(b, s // tm),
        in_specs=[pl.BlockSpec((1, tm, d), tok),
                  pl.BlockSpec((1, N_MOD, d), lambda bi, i: (bi, 0, 0)),
                  pl.BlockSpec((1, d), lambda bi, i: (0, 0)),
                  pl.BlockSpec(w_in.shape, lambda bi, i: (0, 0)),
                  pl.BlockSpec((tm, HEAD_W), lambda bi, i: (i, 0)),
                  pl.BlockSpec((tm, HEAD_W), lambda bi, i: (i, 0)),
                  pl.BlockSpec((tm, HEAD_W), lambda bi, i: (i, 0))],
        out_specs=[pl.BlockSpec((1, tm, QK_W), tok),
                   pl.BlockSpec((1, tm, QK_W), tok),
                   pl.BlockSpec((1, tm, QK_W), tok),
                   pl.BlockSpec((1, tm, F_W), tok)],
        out_shape=[jax.ShapeDtypeStruct((b, s, QK_W), BF16),
                   jax.ShapeDtypeStruct((b, s, QK_W), BF16),
                   jax.ShapeDtypeStruct((b, s, QK_W), BF16),
                   jax.ShapeDtypeStruct((b, s, F_W), F32)],
        compiler_params=_cparams("parallel", "arbitrary"),
        name="inproj",
    )(x, mods, g_mix, w_in, cos_t, sa_t, sb_t)


def _ctxproj_kernel(x_ref, mod_ref, g_ref, w_ref, k_ref, v_ref):
    m = mod_ref[0]
    h = _norm_mod(x_ref[0], g_ref[...], m[0:1], m[1:2]).astype(BF16)
    k_ref[0] = jnp.dot(h, w_ref[:, QK_W:2 * QK_W], preferred_element_type=F32).astype(BF16)
    v_ref[0] = jnp.dot(h, w_ref[:, 2 * QK_W:3 * QK_W], preferred_element_type=F32).astype(BF16)


def _ctxproj(ctx, mods, g_mix, w_in):
    b, n, d = ctx.shape
    return pl.pallas_call(
        _ctxproj_kernel,
        grid=(b,),
        in_specs=[pl.BlockSpec((1, n, d), lambda bi: (bi, 0, 0)),
                  pl.BlockSpec((1, N_MOD, d), lambda bi: (2, 0, 0)),
                  pl.BlockSpec((1, d), lambda bi: (0, 0)),
                  pl.BlockSpec(w_in.shape, lambda bi: (0, 0))],
        out_specs=[pl.BlockSpec((1, n, QK_W), lambda bi: (bi, 0, 0)),
                   pl.BlockSpec((1, n, QK_W), lambda bi: (bi, 0, 0))],
        out_shape=[jax.ShapeDtypeStruct((b, n, QK_W), BF16),
                   jax.ShapeDtypeStruct((b, n, QK_W), BF16)],
        compiler_params=_cparams("arbitrary"),
        name="ctxproj",
    )(ctx, mods, g_mix, w_in)


def _attn_kernel(q_ref, kc_ref, vc_ref, kl_ref, vl_ref, lam_ref, g_ref, o_ref, z_ref,
                 m_ref, l_ref, acc_ref):
    q = q_ref[0]
    lane = lax.broadcasted_iota(jnp.int32, q.shape, 1)
    zero = jnp.zeros_like(q)
    q0 = jnp.where(lane < HEAD_DIM, q, zero)
    q1 = jnp.where(lane >= HEAD_DIM, q, zero)
    qs = (q0, q1)
    contract_last = (((1,), (1,)), ((), ()))
    m_ref[...] = jnp.full(m_ref.shape, -1e30, F32)
    l_ref[...] = jnp.zeros(l_ref.shape, F32)
    acc_ref[...] = jnp.zeros(acc_ref.shape, F32)

    def step(kb, vb):
        nk = kb.shape[0] // HEAD_W
        for mi in range(2):
            s = lax.dot_general(qs[mi], kb, contract_last, preferred_element_type=F32)
            m_old = m_ref[mi]
            m_new = jnp.maximum(m_old, jnp.max(s, axis=-1, keepdims=True))
            alpha = jnp.exp2(m_old - m_new)
            p = jnp.exp2(s - jnp.concatenate([m_new] * nk, axis=1))
            psum = p[:, 0:HEAD_W]
            for cblk in range(1, nk):
                psum = psum + p[:, cblk * HEAD_W:(cblk + 1) * HEAD_W]
            l_ref[mi] = alpha * l_ref[mi] + psum
            acc_ref[mi] = alpha * acc_ref[mi] + jnp.dot(p.astype(BF16), vb,
                                                       preferred_element_type=F32)
            m_ref[mi] = m_new

    step(kc_ref[0], vc_ref[0])

    def body(i, c):
        off = pl.multiple_of(i * TK, TK)
        step(kl_ref[0, pl.ds(off, TK), :], vl_ref[0, pl.ds(off, TK), :])
        return c

    lax.fori_loop(0, kl_ref.shape[1] // TK, body, 0, unroll=2)

    lp = lam_ref[...]
    t1 = jnp.sum(lp[0:1] * lp[1:2], axis=-1, keepdims=True)
    t2 = jnp.sum(lp[2:3] * lp[3:4], axis=-1, keepdims=True)
    lam = jnp.exp(t1) - jnp.exp(t2) + LAMBDA_INIT
    l0 = jnp.sum(l_ref[0], axis=-1, keepdims=True)
    l1 = jnp.sum(l_ref[1], axis=-1, keepdims=True)
    o = acc_ref[0] / l0 - lam * (acc_ref[1] / l1)
    ms = jnp.mean(o * o, axis=-1, keepdims=True)
    o = o * lax.rsqrt(ms + EPS) * g_ref[...] * (1.0 - LAMBDA_INIT)
    o_ref[0] = o.astype(BF16)
    z_ref[...] = jnp.zeros(z_ref.shape, U32)


def _attention(q, kc, vc, kl, vl, lam_p, g_subln, zero_rows):
    b, s, _ = q.shape
    n_ctx = kc.shape[1]
    nq = s // TQ
    n_steps = b * N_HEADS * nq
    zrows = zero_rows // n_steps
    assert zrows * n_steps == zero_rows and zrows % 8 == 0
    return pl.pallas_call(
        _attn_kernel,
        grid=(b, N_HEADS, nq),
        in_specs=[pl.BlockSpec((1, TQ, HEAD_W), lambda bi, h, i: (bi, i, h)),
                  pl.BlockSpec((1, n_ctx, HEAD_W), lambda bi, h, i: (bi, 0, h)),
                  pl.BlockSpec((1, n_ctx, HEAD_W), lambda bi, h, i: (bi, 0, h)),
                  pl.BlockSpec((1, s, HEAD_W), lambda bi, h, i: (bi, 0, h)),
                  pl.BlockSpec((1, s, HEAD_W), lambda bi, h, i: (bi, 0, h)),
                  pl.BlockSpec((4, HEAD_DIM), lambda bi, h, i: (0, 0)),
                  pl.BlockSpec((1, HEAD_W), lambda bi, h, i: (0, 0))],
        out_specs=[pl.BlockSpec((1, TQ, HEAD_W), lambda bi, h, i: (bi, i, h)),
                   pl.BlockSpec((zrows, 128), lambda bi, h, i: ((bi * N_HEADS + h) * nq + i, 0))],
        out_shape=[jax.ShapeDtypeStruct((b, s, QK_W), BF16),
                   jax.ShapeDtypeStruct((zero_rows, 128), U32)],
        scratch_shapes=[pltpu.VMEM((2, TQ, HEAD_W), F32)] * 3,
        compiler_params=_cparams("parallel", "parallel", "arbitrary"),
        name="diffattn",
    )(q, kc, vc, kl, vl, lam_p, g_subln)


def _dft1_kernel(f_ref, wf_ref, g_ref, br_ref, bi_ref):
    rows = DFT_N1 * F1_NB
    fb = f_ref[0].reshape(rows, F_W).astype(BF16)
    y1, y2 = [], []
    for g in range(N_FGROUPS):
        yy = jnp.dot(fb[:, g * FGROUP_DIM:(g + 1) * FGROUP_DIM], wf_ref[g],
                     preferred_element_type=F32)
        y1.append(yy[:, :FGROUP_DIM])
        y2.append(yy[:, FGROUP_DIM:])
    yb = jnp.concatenate(y1 + y2, axis=1).astype(BF16)
    p = jnp.dot(g_ref[0], yb, preferred_element_type=F32)
    top, bot = p[:rows], p[rows:]
    br_ref[0] = (top[:, :F_W] - bot[:, F_W:]).reshape(DFT_N1, F1_NB, F_W)
    bi_ref[0] = (-top[:, F_W:] - bot[:, :F_W]).reshape(DFT_N1, F1_NB, F_W)


def _dft1(f4, wf, gmat):
    b = f4.shape[0]
    rows = DFT_N1 * F1_NB
    return pl.pallas_call(
        _dft1_kernel,
        grid=(DFT_N2 // F1_NB, b),
        in_specs=[pl.BlockSpec((1, DFT_N1, F1_NB, F_W), lambda j, bi: (bi, 0, j, 0)),
                  pl.BlockSpec(wf.shape, lambda j, bi: (0, 0, 0)),
                  pl.BlockSpec((1, 2 * rows, rows), lambda j, bi: (j, 0, 0))],
        out_specs=[pl.BlockSpec((1, DFT_N1, F1_NB, F_W), lambda j, bi: (bi, 0, j, 0)),
                   pl.BlockSpec((1, DFT_N1, F1_NB, F_W), lambda j, bi: (bi, 0, j, 0))],
        out_shape=[jax.ShapeDtypeStruct((b, DFT_N1, DFT_N2, F_W), F32),
                   jax.ShapeDtypeStruct((b, DFT_N1, DFT_N2, F_W), F32)],
        compiler_params=_cparams("arbitrary", "arbitrary"),
        name="dft1",
    )(f4, wf, gmat)


def _dft2_kernel(br_ref, bi_ref, c_ref, s_ref, o_ref):
    for j in range(F2_KB):
        r = (jnp.dot(c_ref[...], br_ref[0, j].astype(BF16), preferred_element_type=F32)
             + jnp.dot(s_ref[...], bi_ref[0, j].astype(BF16), preferred_element_type=F32))
        o_ref[0, :, j, :] = r


def _dft2(br4, bi4, c2, s2):
    b = br4.shape[0]
    blk = (1, F2_KB, DFT_N2, F_W)
    return pl.pallas_call(
        _dft2_kernel,
        grid=(b, DFT_N1 // F2_KB),
        in_specs=[pl.BlockSpec(blk, lambda bi, j: (bi, j, 0, 0)),
                  pl.BlockSpec(blk, lambda bi, j: (bi, j, 0, 0)),
                  pl.BlockSpec((DFT_N2, DFT_N2), lambda bi, j: (0, 0)),
                  pl.BlockSpec((DFT_N2, DFT_N2), lambda bi, j: (0, 0))],
        out_specs=pl.BlockSpec((1, DFT_N2, F2_KB, F_W), lambda bi, j: (bi, 0, j, 0)),
        out_shape=jax.ShapeDtypeStruct((b, DFT_N2, DFT_N1, F_W), F32),
        compiler_params=_cparams("parallel", "arbitrary"),
        name="dft2",
    )(br4, bi4, c2, s2)


def _outproj_kernel(o_ref, f_ref, x_ref, mod_ref, g_ref, wo_ref, wr_ref, br_ref, tri_ref,
                    x1_ref, h3_ref, meta_ref, gate_ref, cnt_out_ref, cnt_ref):
    first = (pl.program_id(0) == 0) & (pl.program_id(1) == 0)

    @pl.when(first)
    def _():
        cnt_ref[...] = jnp.zeros(cnt_ref.shape, F32)

    m = mod_ref[0]
    mix = (jnp.dot(o_ref[0], wo_ref[0:QK_W, :], preferred_element_type=F32)
           + jnp.dot(f_ref[0].astype(BF16), wo_ref[QK_W:, :], preferred_element_type=F32))
    x1 = x_ref[0] + m[2:3] * mix
    x1_ref[0] = x1
    h2 = _norm_mod(x1, g_ref[...], m[3:4], m[4:5])
    _store_rows(h3_ref, h2)
    h_hi = h2.astype(BF16)
    h_lo = (h2 - h_hi.astype(F32)).astype(BF16)
    hw = jnp.dot(h_hi, wr_ref[...], preferred_element_type=F32)
    lg = (hw[:, :128] + hw[:, 128:]
          + jnp.dot(h_lo, wr_ref[:, 0:128], preferred_element_type=F32)
          + br_ref[...])
    lane = lax.broadcasted_iota(jnp.int32, lg.shape, 1)
    lanef = lane.astype(F32)
    ninf = jnp.float32(-jnp.inf)
    big = jnp.float32(lg.shape[1])
    gl = jnp.where(lane < N_GROUPS, lg, ninf)
    gmax = jnp.max(gl, axis=-1, keepdims=True)
    grp = jnp.min(jnp.where(gl == gmax, lanef, big), axis=-1, keepdims=True)
    pg = 1.0 / jnp.sum(jnp.exp(gl - gmax), axis=-1, keepdims=True)
    e_lane = lane - N_GROUPS
    lane_grp = (e_lane >> 3).astype(F32)
    emask = (e_lane >= 0) & (e_lane < N_EXPERTS) & (lane_grp == grp)
    el = jnp.where(emask, lg, ninf)
    t1 = jnp.max(el, axis=-1, keepdims=True)
    i1 = jnp.min(jnp.where(el == t1, lanef, big), axis=-1, keepdims=True)
    el2 = jnp.where(lanef == i1, ninf, el)
    t2 = jnp.max(el2, axis=-1, keepdims=True)
    i2 = jnp.min(jnp.where(el2 == t2, lanef, big), axis=-1, keepdims=True)
    dd = jnp.exp(t2 - t1)
    w1 = pg / (1.0 + dd)
    w2 = pg * dd / (1.0 + dd)
    gate_ref[...] = jnp.where(lane == 0, w1, jnp.where(lane == 1, w2, 0.0))
    hit1 = lanef == i1
    hit2 = lanef == i2
    oh = jnp.where(hit1 | hit2, 1.0, 0.0)
    before = jnp.dot(tri_ref[...], oh.astype(BF16), preferred_element_type=F32) + cnt_ref[0:1, :]
    r1 = jnp.sum(jnp.where(hit1, before, 0.0), axis=-1, keepdims=True)
    r2 = jnp.sum(jnp.where(hit2, before, 0.0), axis=-1, keepdims=True)
    cnt_ref[0:1, :] = cnt_ref[0:1, :] + jnp.sum(oh, axis=0, keepdims=True)
    cnt_out_ref[...] = cnt_ref[...]
    meta = jnp.where(lane == 0, i1 - N_GROUPS,
                     jnp.where(lane == 1, i2 - N_GROUPS,
                               jnp.where(lane == 2, r1, jnp.where(lane == 3, r2, 0.0))))
    meta_ref[...] = jnp.transpose(meta)[0:8, :].astype(jnp.int32)


def _outproj(attn_o, four, x, mods, g_ffn, w_out, w_r, b_r):
    b, s, d = x.shape
    tm = TM_PROJ
    nt = s // tm
    tok = lambda bi, i: (bi, i, 0)
    flat = lambda bi, i: (bi * nt + i, 0)
    tri = jnp.asarray(np.tril(np.ones((tm, tm), np.float32), -1)).astype(BF16)
    return pl.pallas_call(
        _outproj_kernel,
        grid=(b, nt),
        in_specs=[pl.BlockSpec((1, tm, QK_W), tok),
                  pl.BlockSpec((1, tm, F_W), tok),
                  pl.BlockSpec((1, tm, d), tok),
                  pl.BlockSpec((1, N_MOD, d), lambda bi, i: (bi, 0, 0)),
                  pl.BlockSpec((1, d), lambda bi, i: (0, 0)),
                  pl.BlockSpec(w_out.shape, lambda bi, i: (0, 0)),
                  pl.BlockSpec(w_r.shape, lambda bi, i: (0, 0)),
                  pl.BlockSpec(b_r.shape, lambda bi, i: (0, 0)),
                  pl.BlockSpec((tm, tm), lambda bi, i: (0, 0))],
        out_specs=[pl.BlockSpec((1, tm, d), tok),
                   pl.BlockSpec((tm * ROW_TILE, 128), flat),
                   pl.BlockSpec((8, tm), lambda bi, i: (0, bi * nt + i)),
                   pl.BlockSpec((tm, 128), flat),
                   pl.BlockSpec((8, 128), lambda bi, i: (0, 0))],
        out_shape=[jax.ShapeDtypeStruct((b, s, d), F32),
                   jax.ShapeDtypeStruct((b * s * ROW_TILE, 128), U32),
                   jax.ShapeDtypeStruct((8, b * s), jnp.int32),
                   jax.ShapeDtypeStruct((b * s, 128), F32),
                   jax.ShapeDtypeStruct((8, 128), F32)],
        scratch_shapes=[pltpu.VMEM((8, 128), F32)],
        compiler_params=_cparams("arbitrary", "arbitrary"),
        name="outproj",
    )(attn_o, four, x, mods, g_ffn, w_out, w_r, b_r, tri)


def _destrows_kernel(start_ref, meta_ref, o_ref):
    meta = meta_ref[...]
    row = lax.broadcasted_iota(jnp.int32, meta.shape, 0)
    base = jnp.zeros_like(meta)
    for e in range(N_EXPERTS):
        base = jnp.where(meta == e, start_ref[e] * MOE_BLK, base)
    ranks = pltpu.roll(meta, meta.shape[0] - TOP_K, 0)
    o_ref[...] = jnp.where(row < TOP_K, base + ranks, 0)


def _destrows(blk_start, meta):
    return pl.pallas_call(
        _destrows_kernel,
        grid_spec=pltpu.PrefetchScalarGridSpec(
            num_scalar_prefetch=1,
            grid=(1,),
            in_specs=[pl.BlockSpec(meta.shape, lambda i, st: (0, 0))],
            out_specs=pl.BlockSpec(meta.shape, lambda i, st: (0, 0))),
        out_shape=jax.ShapeDtypeStruct(meta.shape, jnp.int32),
        compiler_params=_cparams("arbitrary"),
        name="destrows",
    )(blk_start, meta)


def _row_slice(row):
    return pl.ds(pl.multiple_of(row * ROW_TILE, ROW_TILE), ROW_TILE)


def _row_copy(src_ref, src_row, dst_ref, dst_row, sem):
    return pltpu.make_async_copy(src_ref.at[_row_slice(src_row)], dst_ref.at[_row_slice(dst_row)], sem)


def _bf16_bits(x):
    return lax.bitcast_convert_type(x.astype(BF16).astype(F32), U32)


def _rows_2d(ref, n_rows, dtype):
    hi, lo = [], []
    for cblk in range(ROW_TILE):
        w = ref[pl.ds(cblk, n_rows, stride=ROW_TILE), :]
        hi.append(lax.bitcast_convert_type(w & jnp.uint32(0xFFFF0000), F32).astype(dtype))
        lo.append(lax.bitcast_convert_type(w << 16, F32).astype(dtype))
    return jnp.concatenate(hi + lo, axis=1)


def _store_rows(ref, val):
    for cblk in range(ROW_TILE):
        hi = _bf16_bits(val[:, cblk * 128:(cblk + 1) * 128])
        lo = _bf16_bits(val[:, (cblk + ROW_TILE) * 128:(cblk + ROW_TILE + 1) * 128])
        ref[pl.ds(cblk, val.shape[0], stride=ROW_TILE), :] = hi | (lo >> 16)


def _dispatch_kernel(dest_ref, h_ref, zeros_ref, xs_ref, sem):
    del zeros_ref
    n = 2 * TD_DISPATCH

    def issue(r, c):
        _row_copy(h_ref, r, xs_ref, dest_ref[0, 0, 2 * r], sem).start(priority=0)
        _row_copy(h_ref, r, xs_ref, dest_ref[0, 0, 2 * r + 1], sem).start(priority=1)
        return c

    lax.fori_loop(0, TD_DISPATCH, issue, 0, unroll=DMA_UNROLL // 2)

    def drain(a, c):
        _row_copy(h_ref, 0, xs_ref, 0, sem).wait()
        return c

    lax.fori_loop(0, n, drain, 0, unroll=DMA_UNROLL)


def _dispatch(dest3, h3, xs_zeros):
    t = h3.shape[0] // ROW_TILE
    return pl.pallas_call(
        _dispatch_kernel,
        grid=(t // TD_DISPATCH,),
        in_specs=[pl.BlockSpec((1, 1, 2 * TD_DISPATCH), lambda i: (i, 0, 0),
                               memory_space=pltpu.SMEM),
                  pl.BlockSpec((TD_DISPATCH * ROW_TILE, 128), lambda i: (i, 0)),
                  pl.BlockSpec(memory_space=pl.ANY)],
        out_specs=pl.BlockSpec(memory_space=pl.ANY),
        out_shape=jax.ShapeDtypeStruct(xs_zeros.shape, xs_zeros.dtype),
        scratch_shapes=[pltpu.SemaphoreType.DMA(())],
        input_output_aliases={2: 0},
        compiler_params=_cparams("arbitrary"),
        name="dispatch",
    )(dest3, h3, xs_zeros)


def _experts_kernel(be_ref, nu_ref, nxt_ref, xs_ref, wg_hbm, wu_hbm, wd_hbm, ys_ref,
                    wgb, wub, wdb, sg, su, sd, sem):
    j = pl.program_id(0)
    used = j < nu_ref[0]
    e = be_ref[j]
    new_expert = (j == 0) | (e != be_ref[jnp.maximum(j - 1, 0)])
    e_next = nxt_ref[e]

    def weight_copies(idx):
        return (pltpu.make_async_copy(wg_hbm.at[idx], sg, sem.at[0]),
                pltpu.make_async_copy(wu_hbm.at[idx], su, sem.at[1]),
                pltpu.make_async_copy(wd_hbm.at[idx], sd, sem.at[2]))

    @pl.when(j == 0)
    def _():
        for cp in weight_copies(e):
            cp.start(priority=1)

    @pl.when(used & new_expert)
    def _():
        for cp in weight_copies(e):
            cp.wait()
        wgb[...] = sg[...].astype(BF16)
        wub[...] = su[...].astype(BF16)
        wdb[...] = sd[...].astype(BF16)

    @pl.when(used & new_expert & (e_next != e))
    def _():
        for cp in weight_copies(e_next):
            cp.start(priority=1)

    @pl.when(used)
    def _():
        xb = _rows_2d(xs_ref, MOE_BLK, BF16)
        gate = jnp.dot(xb, wgb[...], preferred_element_type=F32)
        up = jnp.dot(xb, wub[...], preferred_element_type=F32)
        hid = (gate * jax.nn.sigmoid(gate) * up).astype(BF16)
        _store_rows(ys_ref, jnp.dot(hid, wdb[...], preferred_element_type=F32))

    @pl.when(pl.program_id(0) >= nu_ref[0])
    def _():
        ys_ref[...] = jnp.zeros_like(ys_ref)


def _experts(blk_expert, n_used, next_expert, xs, wg, wu, wd):
    rows = xs.shape[0] // ROW_TILE
    d = wg.shape[1]
    nb = rows // MOE_BLK
    row_blk = lambda j, be, nu, nx: (jnp.minimum(j, nu[0] - 1), 0)
    out_blk = lambda j, be, nu, nx: (j, 0)
    return pl.pallas_call(
        _experts_kernel,
        grid_spec=pltpu.PrefetchScalarGridSpec(
            num_scalar_prefetch=3,
            grid=(nb,),
            in_specs=[pl.BlockSpec((MOE_BLK * ROW_TILE, 128), row_blk),
                      pl.BlockSpec(memory_space=pl.ANY),
                      pl.BlockSpec(memory_space=pl.ANY),
                      pl.BlockSpec(memory_space=pl.ANY)],
            out_specs=pl.BlockSpec((MOE_BLK * ROW_TILE, 128), out_blk),
            scratch_shapes=[pltpu.VMEM((d, D_EXPERT), BF16), pltpu.VMEM((d, D_EXPERT), BF16),
                            pltpu.VMEM((D_EXPERT, d), BF16),
                            pltpu.VMEM((d, D_EXPERT), F32), pltpu.VMEM((d, D_EXPERT), F32),
                            pltpu.VMEM((D_EXPERT, d), F32),
                            pltpu.SemaphoreType.DMA((3,))]),
        out_shape=jax.ShapeDtypeStruct(xs.shape, xs.dtype),
        compiler_params=_cparams("arbitrary"),
        name="experts",
    )(blk_expert, n_used, next_expert, xs, wg, wu, wd)


def _combine_kernel(dest_ref, dest_next_ref, ys_ref, x1_ref, gate_ref, mod_ref, g_ref, o_ref,
                    ya, yb, sem):
    step = pl.program_id(0) * pl.num_programs(1) + pl.program_id(1)
    n_steps = pl.num_programs(0) * pl.num_programs(1)
    slot = step % 2

    def start_gathers(idx_ref, for_step, to_slot):
        base = (for_step % (TD_DISPATCH // TD)) * (2 * TD)

        def issue(r, c):
            _row_copy(ys_ref, idx_ref[0, 0, base + 2 * r], ya.at[to_slot], r,
                      sem.at[to_slot]).start(priority=0)
            _row_copy(ys_ref, idx_ref[0, 0, base + 2 * r + 1], yb.at[to_slot], r,
                      sem.at[to_slot]).start(priority=1)
            return c

        lax.fori_loop(0, TD, issue, 0, unroll=DMA_UNROLL)

    @pl.when(step == 0)
    def _():
        start_gathers(dest_ref, step, 0)

    @pl.when(step + 1 < n_steps)
    def _():
        start_gathers(dest_next_ref, step + 1, 1 - slot)

    def drain(r, c):
        _row_copy(ys_ref, 0, ya.at[slot], 0, sem.at[slot]).wait()
        _row_copy(ys_ref, 0, yb.at[slot], 0, sem.at[slot]).wait()
        return c

    lax.fori_loop(0, TD, drain, 0, unroll=DMA_UNROLL)
    gt = gate_ref[...]
    moe = (gt[:, 0:1] * _rows_2d(ya.at[slot], TD, F32)
           + gt[:, 1:2] * _rows_2d(yb.at[slot], TD, F32))
    x2 = x1_ref[0] + mod_ref[0][5:6] * moe
    ms = jnp.mean(x2 * x2, axis=-1, keepdims=True)
    o_ref[0] = x2 * lax.rsqrt(ms + EPS) * g_ref[...]


def _combine(dest3, ys, x1, gates, mods, g_final):
    b, s, d = x1.shape
    nt = s // TD
    ratio = TD_DISPATCH // TD
    assert ratio * TD == TD_DISPATCH
    return pl.pallas_call(
        _combine_kernel,
        grid=(b, nt),
        in_specs=[pl.BlockSpec((1, 1, 2 * TD_DISPATCH),
                               lambda bi, i: ((bi * nt + i) // ratio, 0, 0),
                               memory_space=pltpu.SMEM),
                  pl.BlockSpec((1, 1, 2 * TD_DISPATCH),
                               lambda bi, i: (jnp.minimum(bi * nt + i + 1, b * nt - 1) // ratio, 0, 0),
                               memory_space=pltpu.SMEM),
                  pl.BlockSpec(memory_space=pl.ANY),
                  pl.BlockSpec((1, TD, d), lambda bi, i: (bi, i, 0)),
                  pl.BlockSpec((TD, 128), lambda bi, i: (bi * nt + i, 0)),
                  pl.BlockSpec((1, N_MOD, d), lambda bi, i: (bi, 0, 0)),
                  pl.BlockSpec((1, d), lambda bi, i: (0, 0))],
        out_specs=pl.BlockSpec((1, TD, d), lambda bi, i: (bi, i, 0)),
        out_shape=jax.ShapeDtypeStruct((b, s, d), F32),
        scratch_shapes=[pltpu.VMEM((2, TD * ROW_TILE, 128), U32),
                        pltpu.VMEM((2, TD * ROW_TILE, 128), U32),
                        pltpu.SemaphoreType.DMA((2,))],
        compiler_params=_cparams("arbitrary", "arbitrary"),
        name="combine",
    )(dest3, dest3, ys, x1, gates, mods, g_final)


@functools.lru_cache(maxsize=None)
def _rope_tables(rows):
    r, col = np.meshgrid(np.arange(rows), np.arange(GRID_W), indexing='ij')
    pos = np.stack([r.reshape(-1), col.reshape(-1)], axis=-1).astype(np.float32)
    inv_freq = (np.float32(ROPE_THETA)
                ** (-np.arange(0, ROPE_AXIS, 2, dtype=np.float32) / np.float32(ROPE_AXIS))).astype(np.float32)
    ang = (pos[:, :, None] * inv_freq).astype(np.float32)
    ang = np.concatenate([ang, ang], axis=-1).astype(np.float64)
    n = ang.shape[0]
    cos = np.tile(np.cos(ang).reshape(n, HEAD_DIM), (1, 2)).astype(np.float32)
    sin = np.tile(np.sin(ang).reshape(n, HEAD_DIM), (1, 2)).astype(np.float32)
    upper = (np.arange(HEAD_W) % ROPE_AXIS) >= ROPE_HALF
    sa = np.where(upper, sin, np.float32(0.0))
    sb = np.where(upper, np.float32(0.0), -sin)
    return cos, sa, sb


@functools.lru_cache(maxsize=None)
def _dft_constants(n_pos):
    c = np.arange(FGROUP_DIM)
    ang_c = 2.0 * np.pi * ((c[:, None] * c[None, :]) % FGROUP_DIM) / FGROUP_DIM
    norm = 1.0 / math.sqrt(n_pos * FGROUP_DIM)
    cmat = (np.cos(ang_c) * norm).astype(np.float32)
    smat = (np.sin(ang_c) * norm).astype(np.float32)
    k1 = np.arange(DFT_N1)[None, :, None]
    n1 = np.arange(DFT_N1)[None, None, :]
    n2 = np.arange(DFT_N2)[:, None, None]
    ang_g = 2.0 * np.pi * ((k1 * (DFT_N2 * n1 + n2)) % n_pos) / n_pos
    gsmall = np.stack([np.cos(ang_g), np.sin(ang_g)], axis=1)
    gsmall = gsmall.reshape(DFT_N2 // F1_NB, F1_NB, 2, DFT_N1, DFT_N1)
    gmat = np.einsum('japkn,ab->jpkanb', gsmall, np.eye(F1_NB)).reshape(
        DFT_N2 // F1_NB, 2 * DFT_N1 * F1_NB, DFT_N1 * F1_NB).astype(BF16)
    k2 = np.arange(DFT_N2)
    ang_2 = 2.0 * np.pi * ((k2[:, None] * k2[None, :]) % DFT_N2) / DFT_N2
    c2 = np.cos(ang_2).astype(np.float32)
    s2 = np.sin(ang_2).astype(np.float32)
    return cmat, smat, gmat, c2, s2


def kernel(x, c, ctx, c_ctx, w_ada, b_ada, g_mix_norm, g_ffn_norm, w_in, lambda_q1, lambda_k1, lambda_q2, lambda_k2, g_subln, w_fourier, w_out, w_router_group, b_router_group, w_router_expert, b_router_expert, w_gate, w_up, w_down, g_final):
    b, s, d = x.shape
    t = b * s
    assert d == D_MODEL and s == DFT_N1 * DFT_N2 and s % GRID_W == 0 and b == 2

    cc = jnp.concatenate([c, c_ctx[None, :], jnp.zeros((8 - b - 1, d), F32)], axis=0)
    mods = _adaln(cc.T, w_ada[0], b_ada[0]).reshape(8, N_MOD, d)

    cmat, smat, gmat, c2, s2 = _dft_constants(s)
    wf = _wfold(jnp.asarray(cmat), jnp.asarray(smat), w_fourier[0])
    cos_t, sa_t, sb_t = _rope_tables(s // GRID_W)

    w_in_b = w_in[0].astype(BF16)
    g_mix = g_mix_norm[0].reshape(1, d)
    q, kl, vl, f = _inproj(x, mods, g_mix, w_in_b, cos_t, sa_t, sb_t)
    kc, vc = _ctxproj(ctx, mods, g_mix, w_in_b)

    lam_p = jnp.stack([lambda_q1[0], lambda_k1[0], lambda_q2[0], lambda_k2[0]], axis=0)
    n_blocks = t * TOP_K // MOE_BLK + N_EXPERTS
    attn_o, xs_zeros = _attention(q, kc, vc, kl, vl, lam_p, g_subln[0].reshape(1, HEAD_W),
                                  n_blocks * MOE_BLK * ROW_TILE)

    br, bi = _dft1(f.reshape(b, DFT_N1, DFT_N2, F_W), wf, jnp.asarray(gmat))
    four = _dft2(br, bi, jnp.asarray(c2).astype(BF16),
                 jnp.asarray(s2).astype(BF16)).reshape(b, s, F_W)

    n_r = N_GROUPS + N_EXPERTS
    w_r = jnp.concatenate([w_router_group[0], w_router_expert[0],
                           jnp.zeros((d, 128 - n_r), F32)], axis=1)
    b_r = jnp.concatenate([b_router_group[0], b_router_expert[0],
                           jnp.zeros((128 - n_r,), F32)]).reshape(1, 128)
    w_r_hi = w_r.astype(BF16)
    w_r_lo = (w_r - w_r_hi.astype(F32)).astype(BF16)
    x1, h3, meta, gates, cnt = _outproj(attn_o, four, x, mods, g_ffn_norm[0].reshape(1, d),
                                        w_out[0].astype(BF16),
                                        jnp.concatenate([w_r_hi, w_r_lo], axis=1), b_r)

    counts = cnt[0, N_GROUPS:N_GROUPS + N_EXPERTS].astype(jnp.int32)
    nblk = (counts + MOE_BLK - 1) // MOE_BLK
    blk_end = jnp.cumsum(nblk)
    blk_start = (blk_end - nblk).astype(jnp.int32)
    drows = _destrows(blk_start, meta)
    dest = jnp.stack([drows[0], drows[1]], axis=1)
    blk_ids = jnp.arange(n_blocks, dtype=jnp.int32)
    blk_expert = jnp.minimum(
        jnp.sum((blk_end[None, :] <= blk_ids[:, None]).astype(jnp.int32), axis=1),
        N_EXPERTS - 1).astype(jnp.int32)
    n_used = blk_end[-1:].astype(jnp.int32)
    dest3 = dest.reshape(t // TD_DISPATCH, 1, 2 * TD_DISPATCH)

    xs = _dispatch(dest3, h3, xs_zeros)
    e_ids = jnp.arange(N_EXPERTS, dtype=jnp.int32)
    later = (e_ids[None, :] > e_ids[:, None]) & (nblk[None, :] > 0)
    next_expert = jnp.min(jnp.where(later, e_ids[None, :], N_EXPERTS), axis=1)
    next_expert = jnp.where(next_expert == N_EXPERTS, e_ids, next_expert).astype(jnp.int32)
    ys = _experts(blk_expert, n_used, next_expert, xs, w_gate[0], w_up[0], w_down[0])
    return _combine(dest3, ys, x1, gates, mods, g_final.reshape(1, d))
```
